```python
import jax, jax.numpy as jnp
from jax import lax
import numpy as np

D_MODEL = 2048
BATCH = 8
SEQ = 4096
DEPTH = 4

CHUNK = 64
N_A_LAYERS = DEPTH // 2
N_B_LAYERS = DEPTH - N_A_LAYERS
GMLP_BLOCK = 128
GMLP_HALF = 3 * D_MODEL
GMLP_GROUPS = 8
GMLP_GROUP_DIM = GMLP_HALF // GMLP_GROUPS
N_HEADS = 16
HEAD_DIM = D_MODEL // N_HEADS
Q_BLOCK = 128
D_FF = 4 * D_MODEL
N_MOD = 6
EPS = 1e-6

kernel_name = "yoco_gmlp_forgetting_attn_adaln_trunk"


def rms_norm(x, g):
    xf = x.astype(jnp.float32)
    y = xf * lax.rsqrt(jnp.mean(xf * xf, axis=-1, keepdims=True) + EPS)
    return y.astype(x.dtype) * g


def layer_norm(x, g, b):
    xf = x.astype(jnp.float32)
    mu = jnp.mean(xf, axis=-1, keepdims=True)
    var = jnp.mean(jnp.square(xf - mu), axis=-1, keepdims=True)
    y = (xf - mu) * lax.rsqrt(var + EPS)
    return y.astype(x.dtype) * g + b


def modulate(h, shift, scale):
    return h * (1.0 + scale[:, None, :]) + shift[:, None, :]


def split_heads(t):
    b, s, _ = t.shape
    return t.reshape(b, s, N_HEADS, HEAD_DIM)


def gmlp_block_mask():
    idx = np.arange(GMLP_BLOCK) // CHUNK
    return jnp.asarray(idx[None, :] <= idx[:, None])


def gmlp_mixer(h, w_in, ln_g, ln_b, ws, bs, w_out):
    b, s, _ = h.shape
    z = jax.nn.gelu(h @ w_in, approximate=False)
    u, v = jnp.split(z, 2, axis=-1)
    v = layer_norm(v, ln_g, ln_b)
    n_blk = s // GMLP_BLOCK
    v = v.reshape(b, n_blk, GMLP_BLOCK, GMLP_GROUPS, GMLP_GROUP_DIM)
    w = jnp.where(gmlp_block_mask()[None], ws, jnp.zeros((), ws.dtype))
    sv = jnp.einsum('gij,bnjgc->bnigc', w, v) + bs.T[:, :, None]
    sv = sv.reshape(b, s, GMLP_HALF)
    return (u * sv) @ w_out


def squared_relu_mlp(h, w1, w2):
    return jnp.square(jax.nn.relu(h @ w1)) @ w2


def shared_kv(x, sc, kv_norm_g, kv_ada_w, kv_ada_b, w_kv, k_norm_g, w_f, b_f):
    shift, scale = jnp.split(sc @ kv_ada_w + kv_ada_b, 2, axis=-1)
    h = modulate(rms_norm(x, kv_norm_g), shift, scale)
    k, v = jnp.split(h @ w_kv, 2, axis=-1)
    k = rms_norm(split_heads(k), k_norm_g).transpose(0, 2, 1, 3)
    v = split_heads(v).transpose(0, 2, 1, 3)
    logf = jax.nn.log_sigmoid((h @ w_f).astype(jnp.float32) + b_f.astype(jnp.float32))
    fcum = jnp.cumsum(logf, axis=1).transpose(0, 2, 1)
    return k, v, fcum


def forgetting_attention(q, k, v, fcum):
    b, nh, s, dh = q.shape
    n_blk = s // Q_BLOCK
    qb = q.reshape(b, nh, n_blk, Q_BLOCK, dh).transpose(2, 0, 1, 3, 4)
    fq = fcum.reshape(b, nh, n_blk, Q_BLOCK).transpose(2, 0, 1, 3)
    key_pos = jnp.arange(s)
    inv_sqrt = 1.0 / float(np.sqrt(dh))

    def one_block(args):
        i, q_i, f_i = args
        logits = jnp.einsum('bhqd,bhkd->bhqk', q_i, k,
                            preferred_element_type=jnp.float32) * inv_sqrt
        logits = logits + (f_i[..., :, None] - fcum[..., None, :])
        q_pos = i * Q_BLOCK + jnp.arange(Q_BLOCK)
        logits = jnp.where(key_pos[None, :] <= q_pos[:, None], logits, -jnp.inf)
        p = jax.nn.softmax(logits, axis=-1)
        return jnp.einsum('bhqk,bhkd->bhqd', p.astype(v.dtype), v)

    out = lax.map(one_block, (jnp.arange(n_blk), qb, fq))
    return out.transpose(1, 2, 0, 3, 4).reshape(b, nh, s, dh)


def attention_mixer(h, k, v, fcum, wq, q_norm_g, wo):
    b, s, _ = h.shape
    q = rms_norm(split_heads(h @ wq), q_norm_g).transpose(0, 2, 1, 3)
    o = forgetting_attention(q, k, v, fcum)
    return o.transpose(0, 2, 1, 3).reshape(b, s, D_MODEL) @ wo


def _fwd_setup_inputs(seed: int = 0) -> dict:
    key = jax.random.key(seed)
    ks = jax.random.split(key, 24)
    f32 = jnp.float32
    nrm = lambda k, shape, s: jax.random.normal(k, shape, f32) * s
    d = D_MODEL
    return {
        "x": nrm(ks[0], (BATCH, SEQ, d), 1.0),
        "c": nrm(ks[1], (BATCH, d), 1.0),
        "ada_w": nrm(ks[2], (DEPTH, d, N_MOD * d), d ** -0.5),
        "ada_b": nrm(ks[3], (DEPTH, N_MOD * d), 0.02),
        "norm_g": 1.0 + nrm(ks[4], (DEPTH, 2, d), 0.02),
        "mlp_w1": nrm(ks[5], (DEPTH, d, D_FF), d ** -0.5),
        "mlp_w2": nrm(ks[6], (DEPTH, D_FF, d), D_FF ** -0.5),
        "gmlp_w_in": nrm(ks[7], (N_A_LAYERS, d, 2 * GMLP_HALF), d ** -0.5),
        "gmlp_ln_g": 1.0 + nrm(ks[8], (N_A_LAYERS, GMLP_HALF), 0.02),
        "gmlp_ln_b": nrm(ks[9], (N_A_LAYERS, GMLP_HALF), 0.02),
        "gmlp_ws": nrm(ks[10], (N_A_LAYERS, GMLP_GROUPS, GMLP_BLOCK, GMLP_BLOCK), GMLP_BLOCK ** -0.5),
        "gmlp_bs": 1.0 + nrm(ks[11], (N_A_LAYERS, GMLP_GROUPS, GMLP_BLOCK), 0.1),
        "gmlp_w_out": nrm(ks[12], (N_A_LAYERS, GMLP_HALF, d), GMLP_HALF ** -0.5),
        "kv_norm_g": 1.0 + nrm(ks[13], (d,), 0.02),
        "kv_ada_w": nrm(ks[14], (d, 2 * d), d ** -0.5),
        "kv_ada_b": nrm(ks[15], (2 * d,), 0.02),
        "w_kv": nrm(ks[16], (d, 2 * d), d ** -0.5),
        "k_norm_g": 1.0 + nrm(ks[17], (HEAD_DIM,), 0.02),
        "w_f": nrm(ks[18], (d, N_HEADS), 0.1 * d ** -0.5),
        "b_f": jax.random.uniform(ks[19], (N_HEADS,), f32, 0.5, 5.0),
        "attn_wq": nrm(ks[20], (N_B_LAYERS, d, d), d ** -0.5),
        "q_norm_g": 1.0 + nrm(ks[21], (N_B_LAYERS, HEAD_DIM), 0.02),
        "attn_wo": nrm(ks[22], (N_B_LAYERS, d, d), d ** -0.5),
    }


def _fwd_reference(x, c, ada_w, ada_b, norm_g, mlp_w1, mlp_w2, gmlp_w_in, gmlp_ln_g,
              gmlp_ln_b, gmlp_ws, gmlp_bs, gmlp_w_out, kv_norm_g, kv_ada_w, kv_ada_b,
              w_kv, k_norm_g, w_f, b_f, attn_wq, q_norm_g, attn_wo):
    sc = jax.nn.silu(c)
    k = v = fcum = None
    for layer in range(DEPTH):
        mod = sc @ ada_w[layer] + ada_b[layer]
        sh1, sc1, g1, sh2, sc2, g2 = jnp.split(mod, N_MOD, axis=-1)
        h = modulate(rms_norm(x, norm_g[layer, 0]), sh1, sc1)
        if layer < N_A_LAYERS:
            a = layer
            y = gmlp_mixer(h, gmlp_w_in[a], gmlp_ln_g[a], gmlp_ln_b[a],
                           gmlp_ws[a], gmlp_bs[a], gmlp_w_out[a])
        else:
            if layer == N_A_LAYERS:
                k, v, fcum = shared_kv(x, sc, kv_norm_g, kv_ada_w, kv_ada_b,
                                       w_kv, k_norm_g, w_f, b_f)
            bl = layer - N_A_LAYERS
            y = attention_mixer(h, k, v, fcum, attn_wq[bl], q_norm_g[bl], attn_wo[bl])
        x = x + g1[:, None, :] * y
        h = modulate(rms_norm(x, norm_g[layer, 1]), sh2, sc2)
        x = x + g2[:, None, :] * squared_relu_mlp(h, mlp_w1[layer], mlp_w2[layer])
    return x


import jax as _jax
import jax.numpy as _jnp

TWIN_FORMAT = 'train_step'
FWD_PARAMS = ['x', 'c', 'ada_w', 'ada_b', 'norm_g', 'mlp_w1', 'mlp_w2', 'gmlp_w_in', 'gmlp_ln_g', 'gmlp_ln_b', 'gmlp_ws', 'gmlp_bs', 'gmlp_w_out', 'kv_norm_g', 'kv_ada_w', 'kv_ada_b', 'w_kv', 'k_norm_g', 'w_f', 'b_f', 'attn_wq', 'q_norm_g', 'attn_wo']
TWIN_WEIGHTS = ['ada_w', 'ada_b', 'norm_g', 'mlp_w1', 'mlp_w2', 'gmlp_w_in', 'gmlp_ln_g', 'gmlp_ln_b', 'gmlp_ws', 'gmlp_bs', 'gmlp_w_out', 'kv_norm_g', 'kv_ada_w', 'kv_ada_b', 'w_kv', 'k_norm_g', 'w_f', 'b_f', 'attn_wq', 'q_norm_g', 'attn_wo']
TWIN_DIFF_INPUT = 'x'
TWIN_INPUTS = ['x', 'c', 'ada_w', 'ada_b', 'norm_g', 'mlp_w1', 'mlp_w2', 'gmlp_w_in', 'gmlp_ln_g', 'gmlp_ln_b', 'gmlp_ws', 'gmlp_bs', 'gmlp_w_out', 'kv_norm_g', 'kv_ada_w', 'kv_ada_b', 'w_kv', 'k_norm_g', 'w_f', 'b_f', 'attn_wq', 'q_norm_g', 'attn_wo', 'loss_target', 'm_ada_w', 'm_ada_b', 'm_norm_g', 'm_mlp_w1', 'm_mlp_w2', 'm_gmlp_w_in', 'm_gmlp_ln_g', 'm_gmlp_ln_b', 'm_gmlp_ws', 'm_gmlp_bs', 'm_gmlp_w_out', 'm_kv_norm_g', 'm_kv_ada_w', 'm_kv_ada_b', 'm_w_kv', 'm_k_norm_g', 'm_w_f', 'm_b_f', 'm_attn_wq', 'm_q_norm_g', 'm_attn_wo', 'v_ada_w', 'v_ada_b', 'v_norm_g', 'v_mlp_w1', 'v_mlp_w2', 'v_gmlp_w_in', 'v_gmlp_ln_g', 'v_gmlp_ln_b', 'v_gmlp_ws', 'v_gmlp_bs', 'v_gmlp_w_out', 'v_kv_norm_g', 'v_kv_ada_w', 'v_kv_ada_b', 'v_w_kv', 'v_k_norm_g', 'v_w_f', 'v_b_f', 'v_attn_wq', 'v_q_norm_g', 'v_attn_wo']
TWIN_OUTPUTS = ['loss', 'grad_x', 'grad_ada_w', 'grad_ada_b', 'grad_norm_g', 'grad_mlp_w1', 'grad_mlp_w2', 'grad_gmlp_w_in', 'grad_gmlp_ln_g', 'grad_gmlp_ln_b', 'grad_gmlp_ws', 'grad_gmlp_bs', 'grad_gmlp_w_out', 'grad_kv_norm_g', 'grad_kv_ada_w', 'grad_kv_ada_b', 'grad_w_kv', 'grad_k_norm_g', 'grad_w_f', 'grad_b_f', 'grad_attn_wq', 'grad_q_norm_g', 'grad_attn_wo', 'delta_ada_w', 'delta_ada_b', 'delta_norm_g', 'delta_mlp_w1', 'delta_mlp_w2', 'delta_gmlp_w_in', 'delta_gmlp_ln_g', 'delta_gmlp_ln_b', 'delta_gmlp_ws', 'delta_gmlp_bs', 'delta_gmlp_w_out', 'delta_kv_norm_g', 'delta_kv_ada_w', 'delta_kv_ada_b', 'delta_w_kv', 'delta_k_norm_g', 'delta_w_f', 'delta_b_f', 'delta_attn_wq', 'delta_q_norm_g', 'delta_attn_wo', 'new_m_ada_w', 'new_m_ada_b', 'new_m_norm_g', 'new_m_mlp_w1', 'new_m_mlp_w2', 'new_m_gmlp_w_in', 'new_m_gmlp_ln_g', 'new_m_gmlp_ln_b', 'new_m_gmlp_ws', 'new_m_gmlp_bs', 'new_m_gmlp_w_out', 'new_m_kv_norm_g', 'new_m_kv_ada_w', 'new_m_kv_ada_b', 'new_m_w_kv', 'new_m_k_norm_g', 'new_m_w_f', 'new_m_b_f', 'new_m_attn_wq', 'new_m_q_norm_g', 'new_m_attn_wo', 'new_v_ada_w', 'new_v_ada_b', 'new_v_norm_g', 'new_v_mlp_w1', 'new_v_mlp_w2', 'new_v_gmlp_w_in', 'new_v_gmlp_ln_g', 'new_v_gmlp_ln_b', 'new_v_gmlp_ws', 'new_v_gmlp_bs', 'new_v_gmlp_w_out', 'new_v_kv_norm_g', 'new_v_kv_ada_w', 'new_v_kv_ada_b', 'new_v_w_kv', 'new_v_k_norm_g', 'new_v_w_f', 'new_v_b_f', 'new_v_attn_wq', 'new_v_q_norm_g', 'new_v_attn_wo']
TWIN_LEAF_KINDS = {'loss': 'loss', 'grad_x': 'grad_x', 'grad_ada_w': 'grad_w', 'grad_ada_b': 'grad_w', 'grad_norm_g': 'grad_w', 'grad_mlp_w1': 'grad_w', 'grad_mlp_w2': 'grad_w', 'grad_gmlp_w_in': 'grad_w', 'grad_gmlp_ln_g': 'grad_w', 'grad_gmlp_ln_b': 'grad_w', 'grad_gmlp_ws': 'grad_w', 'grad_gmlp_bs': 'grad_w', 'grad_gmlp_w_out': 'grad_w', 'grad_kv_norm_g': 'grad_w', 'grad_kv_ada_w': 'grad_w', 'grad_kv_ada_b': 'grad_w', 'grad_w_kv': 'grad_w', 'grad_k_norm_g': 'grad_w', 'grad_w_f': 'grad_w', 'grad_b_f': 'grad_w', 'grad_attn_wq': 'grad_w', 'grad_q_norm_g': 'grad_w', 'grad_attn_wo': 'grad_w', 'delta_ada_w': 'delta_w', 'delta_ada_b': 'delta_w', 'delta_norm_g': 'delta_w', 'delta_mlp_w1': 'delta_w', 'delta_mlp_w2': 'delta_w', 'delta_gmlp_w_in': 'delta_w', 'delta_gmlp_ln_g': 'delta_w', 'delta_gmlp_ln_b': 'delta_w', 'delta_gmlp_ws': 'delta_w', 'delta_gmlp_bs': 'delta_w', 'delta_gmlp_w_out': 'delta_w', 'delta_kv_norm_g': 'delta_w', 'delta_kv_ada_w': 'delta_w', 'delta_kv_ada_b': 'delta_w', 'delta_w_kv': 'delta_w', 'delta_k_norm_g': 'delta_w', 'delta_w_f': 'delta_w', 'delta_b_f': 'delta_w', 'delta_attn_wq': 'delta_w', 'delta_q_norm_g': 'delta_w', 'delta_attn_wo': 'delta_w', 'new_m_ada_w': 'new_m', 'new_m_ada_b': 'new_m', 'new_m_norm_g': 'new_m', 'new_m_mlp_w1': 'new_m', 'new_m_mlp_w2': 'new_m', 'new_m_gmlp_w_in': 'new_m', 'new_m_gmlp_ln_g': 'new_m', 'new_m_gmlp_ln_b': 'new_m', 'new_m_gmlp_ws': 'new_m', 'new_m_gmlp_bs': 'new_m', 'new_m_gmlp_w_out': 'new_m', 'new_m_kv_norm_g': 'new_m', 'new_m_kv_ada_w': 'new_m', 'new_m_kv_ada_b': 'new_m', 'new_m_w_kv': 'new_m', 'new_m_k_norm_g': 'new_m', 'new_m_w_f': 'new_m', 'new_m_b_f': 'new_m', 'new_m_attn_wq': 'new_m', 'new_m_q_norm_g': 'new_m', 'new_m_attn_wo': 'new_m', 'new_v_ada_w': 'new_v', 'new_v_ada_b': 'new_v', 'new_v_norm_g': 'new_v', 'new_v_mlp_w1': 'new_v', 'new_v_mlp_w2': 'new_v', 'new_v_gmlp_w_in': 'new_v', 'new_v_gmlp_ln_g': 'new_v', 'new_v_gmlp_ln_b': 'new_v', 'new_v_gmlp_ws': 'new_v', 'new_v_gmlp_bs': 'new_v', 'new_v_gmlp_w_out': 'new_v', 'new_v_kv_norm_g': 'new_v', 'new_v_kv_ada_w': 'new_v', 'new_v_kv_ada_b': 'new_v', 'new_v_w_kv': 'new_v', 'new_v_k_norm_g': 'new_v', 'new_v_w_f': 'new_v', 'new_v_b_f': 'new_v', 'new_v_attn_wq': 'new_v', 'new_v_q_norm_g': 'new_v', 'new_v_attn_wo': 'new_v'}


def _forward(args):
    return _fwd_reference(*[args[k] for k in FWD_PARAMS])


def _output_shape():
    out = _jax.eval_shape(lambda: _forward(_fwd_setup_inputs(0)))
    return out.shape, out.dtype

N_MICROBATCH = 1
ADAM_LR = 0.001
ADAM_B1 = 0.9
ADAM_B2 = 0.999
ADAM_EPS = 1e-08
ADAM_WD = 0.01
ADAM_STEP = 10
PER_EXAMPLE_BATCH_AXIS = {'x': 0, 'c': 0, 'loss_target': 0}
SHARED_INPUTS = []
_WEIGHT_DTYPES = {'ada_w': _jnp.float32, 'ada_b': _jnp.float32, 'norm_g': _jnp.float32, 'mlp_w1': _jnp.float32, 'mlp_w2': _jnp.float32, 'gmlp_w_in': _jnp.float32, 'gmlp_ln_g': _jnp.float32, 'gmlp_ln_b': _jnp.float32, 'gmlp_ws': _jnp.float32, 'gmlp_bs': _jnp.float32, 'gmlp_w_out': _jnp.float32, 'kv_norm_g': _jnp.float32, 'kv_ada_w': _jnp.float32, 'kv_ada_b': _jnp.float32, 'w_kv': _jnp.float32, 'k_norm_g': _jnp.float32, 'w_f': _jnp.float32, 'b_f': _jnp.float32, 'attn_wq': _jnp.float32, 'q_norm_g': _jnp.float32, 'attn_wo': _jnp.float32}
MOMENT_SCALE = {'ada_w': 1.533115e+01, 'ada_b': 2.863917e+01, 'norm_g': 3.949717e+01, 'mlp_w1': 9.720008e+00, 'mlp_w2': 2.115375e+01, 'gmlp_w_in': 3.306174e+00, 'gmlp_ln_g': 1.120709e+00, 'gmlp_ln_b': 4.838342e-01, 'gmlp_ws': 1.282194e+00, 'gmlp_bs': 9.074177e+00, 'gmlp_w_out': 1.224775e+01, 'kv_norm_g': 1.609257e+01, 'kv_ada_w': 9.685568e+00, 'kv_ada_b': 1.631443e+01, 'w_kv': 1.169683e+01, 'k_norm_g': 3.498337e+00, 'w_f': 2.692788e+01, 'b_f': 5.544281e+01, 'attn_wq': 1.818539e-01, 'q_norm_g': 1.775978e+00, 'attn_wo': 1.132840e+01}


def _to_microbatches(a, axis):
    t = _jnp.moveaxis(a, axis, 0)
    t = t.reshape((N_MICROBATCH, t.shape[0] // N_MICROBATCH) + t.shape[1:])
    return _jnp.moveaxis(t, 1, axis + 1)


def setup_inputs(seed: int = 0) -> dict:
    inp = _fwd_setup_inputs(seed)
    key = _jax.random.fold_in(_jax.random.key(seed), 7919)
    shape, _ = _output_shape()
    out = dict(inp)
    out["loss_target"] = _jax.random.normal(_jax.random.fold_in(key, 0), shape, _jnp.float32)
    for i, name in enumerate(TWIN_WEIGHTS):
        w = inp[name].astype(_jnp.float32)
        if MOMENT_SCALE is None:
            s = _jnp.sqrt(_jnp.mean(_jnp.square(w)) + 1e-30)
        else:
            s = MOMENT_SCALE[name]
        km, kv = _jax.random.split(_jax.random.fold_in(key, i + 1))
        out[name] = w
        out["m_" + name] = s * _jax.random.normal(km, w.shape, _jnp.float32)
        out["v_" + name] = (s * s) * _jax.random.uniform(kv, w.shape, _jnp.float32, 0.5, 1.5)
    if N_MICROBATCH > 1:
        for name, axis in PER_EXAMPLE_BATCH_AXIS.items():
            out[name] = _to_microbatches(out[name], axis)
    return {'x': out['x'], 'c': out['c'], 'ada_w': out['ada_w'], 'ada_b': out['ada_b'], 'norm_g': out['norm_g'], 'mlp_w1': out['mlp_w1'], 'mlp_w2': out['mlp_w2'], 'gmlp_w_in': out['gmlp_w_in'], 'gmlp_ln_g': out['gmlp_ln_g'], 'gmlp_ln_b': out['gmlp_ln_b'], 'gmlp_ws': out['gmlp_ws'], 'gmlp_bs': out['gmlp_bs'], 'gmlp_w_out': out['gmlp_w_out'], 'kv_norm_g': out['kv_norm_g'], 'kv_ada_w': out['kv_ada_w'], 'kv_ada_b': out['kv_ada_b'], 'w_kv': out['w_kv'], 'k_norm_g': out['k_norm_g'], 'w_f': out['w_f'], 'b_f': out['b_f'], 'attn_wq': out['attn_wq'], 'q_norm_g': out['q_norm_g'], 'attn_wo': out['attn_wo'], 'loss_target': out['loss_target'], 'm_ada_w': out['m_ada_w'], 'm_ada_b': out['m_ada_b'], 'm_norm_g': out['m_norm_g'], 'm_mlp_w1': out['m_mlp_w1'], 'm_mlp_w2': out['m_mlp_w2'], 'm_gmlp_w_in': out['m_gmlp_w_in'], 'm_gmlp_ln_g': out['m_gmlp_ln_g'], 'm_gmlp_ln_b': out['m_gmlp_ln_b'], 'm_gmlp_ws': out['m_gmlp_ws'], 'm_gmlp_bs': out['m_gmlp_bs'], 'm_gmlp_w_out': out['m_gmlp_w_out'], 'm_kv_norm_g': out['m_kv_norm_g'], 'm_kv_ada_w': out['m_kv_ada_w'], 'm_kv_ada_b': out['m_kv_ada_b'], 'm_w_kv': out['m_w_kv'], 'm_k_norm_g': out['m_k_norm_g'], 'm_w_f': out['m_w_f'], 'm_b_f': out['m_b_f'], 'm_attn_wq': out['m_attn_wq'], 'm_q_norm_g': out['m_q_norm_g'], 'm_attn_wo': out['m_attn_wo'], 'v_ada_w': out['v_ada_w'], 'v_ada_b': out['v_ada_b'], 'v_norm_g': out['v_norm_g'], 'v_mlp_w1': out['v_mlp_w1'], 'v_mlp_w2': out['v_mlp_w2'], 'v_gmlp_w_in': out['v_gmlp_w_in'], 'v_gmlp_ln_g': out['v_gmlp_ln_g'], 'v_gmlp_ln_b': out['v_gmlp_ln_b'], 'v_gmlp_ws': out['v_gmlp_ws'], 'v_gmlp_bs': out['v_gmlp_bs'], 'v_gmlp_w_out': out['v_gmlp_w_out'], 'v_kv_norm_g': out['v_kv_norm_g'], 'v_kv_ada_w': out['v_kv_ada_w'], 'v_kv_ada_b': out['v_kv_ada_b'], 'v_w_kv': out['v_w_kv'], 'v_k_norm_g': out['v_k_norm_g'], 'v_w_f': out['v_w_f'], 'v_b_f': out['v_b_f'], 'v_attn_wq': out['v_attn_wq'], 'v_q_norm_g': out['v_q_norm_g'], 'v_attn_wo': out['v_attn_wo']}


def _loss(weights, diff, rest, loss_target):
    with _jax.named_scope("forward"):
        args = {**rest, TWIN_DIFF_INPUT: diff, **{k: w.astype(_WEIGHT_DTYPES[k]) for k, w in weights.items()}}
        y = _forward(args)
    with _jax.named_scope("loss_head"):
        err = _jnp.square(y.astype(_jnp.float32) - loss_target)
        return 0.5 * _jnp.sum(_jnp.mean(err, axis=-1)) if err.ndim else 0.5 * err


def _adamw(w, g, m, v):
    m = ADAM_B1 * m + (1.0 - ADAM_B1) * g
    v = ADAM_B2 * v + (1.0 - ADAM_B2) * _jnp.square(g)
    m_hat = m / (1.0 - ADAM_B1 ** ADAM_STEP)
    v_hat = v / (1.0 - ADAM_B2 ** ADAM_STEP)
    delta = -ADAM_LR * (m_hat / (_jnp.sqrt(v_hat) + ADAM_EPS) + ADAM_WD * w)
    return delta, m, v


def reference(x, c, ada_w, ada_b, norm_g, mlp_w1, mlp_w2, gmlp_w_in, gmlp_ln_g, gmlp_ln_b, gmlp_ws, gmlp_bs, gmlp_w_out, kv_norm_g, kv_ada_w, kv_ada_b, w_kv, k_norm_g, w_f, b_f, attn_wq, q_norm_g, attn_wo, loss_target, m_ada_w, m_ada_b, m_norm_g, m_mlp_w1, m_mlp_w2, m_gmlp_w_in, m_gmlp_ln_g, m_gmlp_ln_b, m_gmlp_ws, m_gmlp_bs, m_gmlp_w_out, m_kv_norm_g, m_kv_ada_w, m_kv_ada_b, m_w_kv, m_k_norm_g, m_w_f, m_b_f, m_attn_wq, m_q_norm_g, m_attn_wo, v_ada_w, v_ada_b, v_norm_g, v_mlp_w1, v_mlp_w2, v_gmlp_w_in, v_gmlp_ln_g, v_gmlp_ln_b, v_gmlp_ws, v_gmlp_bs, v_gmlp_w_out, v_kv_norm_g, v_kv_ada_w, v_kv_ada_b, v_w_kv, v_k_norm_g, v_w_f, v_b_f, v_attn_wq, v_q_norm_g, v_attn_wo):
    given = dict(x=x, c=c, ada_w=ada_w, ada_b=ada_b, norm_g=norm_g, mlp_w1=mlp_w1, mlp_w2=mlp_w2, gmlp_w_in=gmlp_w_in, gmlp_ln_g=gmlp_ln_g, gmlp_ln_b=gmlp_ln_b, gmlp_ws=gmlp_ws, gmlp_bs=gmlp_bs, gmlp_w_out=gmlp_w_out, kv_norm_g=kv_norm_g, kv_ada_w=kv_ada_w, kv_ada_b=kv_ada_b, w_kv=w_kv, k_norm_g=k_norm_g, w_f=w_f, b_f=b_f, attn_wq=attn_wq, q_norm_g=q_norm_g, attn_wo=attn_wo, loss_target=loss_target, m_ada_w=m_ada_w, m_ada_b=m_ada_b, m_norm_g=m_norm_g, m_mlp_w1=m_mlp_w1, m_mlp_w2=m_mlp_w2, m_gmlp_w_in=m_gmlp_w_in, m_gmlp_ln_g=m_gmlp_ln_g, m_gmlp_ln_b=m_gmlp_ln_b, m_gmlp_ws=m_gmlp_ws, m_gmlp_bs=m_gmlp_bs, m_gmlp_w_out=m_gmlp_w_out, m_kv_norm_g=m_kv_norm_g, m_kv_ada_w=m_kv_ada_w, m_kv_ada_b=m_kv_ada_b, m_w_kv=m_w_kv, m_k_norm_g=m_k_norm_g, m_w_f=m_w_f, m_b_f=m_b_f, m_attn_wq=m_attn_wq, m_q_norm_g=m_q_norm_g, m_attn_wo=m_attn_wo, v_ada_w=v_ada_w, v_ada_b=v_ada_b, v_norm_g=v_norm_g, v_mlp_w1=v_mlp_w1, v_mlp_w2=v_mlp_w2, v_gmlp_w_in=v_gmlp_w_in, v_gmlp_ln_g=v_gmlp_ln_g, v_gmlp_ln_b=v_gmlp_ln_b, v_gmlp_ws=v_gmlp_ws, v_gmlp_bs=v_gmlp_bs, v_gmlp_w_out=v_gmlp_w_out, v_kv_norm_g=v_kv_norm_g, v_kv_ada_w=v_kv_ada_w, v_kv_ada_b=v_kv_ada_b, v_w_kv=v_w_kv, v_k_norm_g=v_k_norm_g, v_w_f=v_w_f, v_b_f=v_b_f, v_attn_wq=v_attn_wq, v_q_norm_g=v_q_norm_g, v_attn_wo=v_attn_wo)
    weights = {n: given[n] for n in TWIN_WEIGHTS}
    shared = {n: given[n] for n in SHARED_INPUTS}
    per_example = {n: given[n] for n in ['x', 'c']}
    grad_fn = _jax.value_and_grad(_loss, argnums=(0, 1))

    def one_microbatch(ex, loss_target):
        ex = dict(ex)
        diff = ex.pop(TWIN_DIFF_INPUT)
        return grad_fn(weights, diff, {**shared, **ex}, loss_target)

    if N_MICROBATCH == 1:
        loss, (grad_w, grad_x) = one_microbatch(per_example, given["loss_target"])
    else:
        def body(carry, xs):
            loss_sum, grad_sum = carry
            l_k, (gw_k, gx_k) = one_microbatch(xs[0], xs[1])
            with _jax.named_scope("update"):
                return (loss_sum + l_k, _jax.tree.map(_jnp.add, grad_sum, gw_k)), gx_k

        init = (_jnp.zeros((), _jnp.float32), _jax.tree.map(_jnp.zeros_like, weights))
        (loss, grad_w), grad_x = _jax.lax.scan(body, init, (per_example, given["loss_target"]))
    with _jax.named_scope("update"):
        delta_w, new_m, new_v = {}, {}, {}
        for n in TWIN_WEIGHTS:
            delta_w[n], new_m[n], new_v[n] = _adamw(weights[n], grad_w[n], given["m_" + n], given["v_" + n])
    return (loss, grad_x, *[grad_w[n] for n in TWIN_WEIGHTS], *[delta_w[n] for n in TWIN_WEIGHTS],
            *[new_m[n] for n in TWIN_WEIGHTS], *[new_v[n] for n in TWIN_WEIGHTS])
```

```python
import functools
import math

import jax
import jax.numpy as jnp
from jax import lax
from jax.experimental import pallas as pl
from jax.experimental.pallas import tpu as pltpu

F32 = jnp.float32
BF16 = jnp.bfloat16
N_DEV = 8
EPS = 1e-6
CHUNK = 64
BLK = 128
N_MOD = 6
ADAM_LR = 0.001
ADAM_B1 = 0.9
ADAM_B2 = 0.999
ADAM_EPS = 1e-08
ADAM_WD = 0.01
ADAM_STEP = 10
VMEM_LIMIT_BYTES = 56 * 2 ** 20
NEG_BIG = -1e30
WEIGHT_NAMES = ['ada_w', 'ada_b', 'norm_g', 'mlp_w1', 'mlp_w2', 'gmlp_w_in', 'gmlp_ln_g', 'gmlp_ln_b', 'gmlp_ws',
                'gmlp_bs', 'gmlp_w_out', 'kv_norm_g', 'kv_ada_w', 'kv_ada_b', 'w_kv', 'k_norm_g', 'w_f', 'b_f',
                'attn_wq', 'q_norm_g', 'attn_wo']
MESH = pl.DeviceIdType.MESH


def _params(sem):
    return pltpu.CompilerParams(dimension_semantics=sem, vmem_limit_bytes=VMEM_LIMIT_BYTES)


def _tile(n, cap, unit=128):
    if n <= cap:
        return n
    t = (cap // unit) * unit
    while t > unit and n % t:
        t -= unit
    assert n % t == 0, (n, cap, unit)
    return t


def _my_index():
    return 4 * lax.axis_index("x") + 2 * lax.axis_index("y") + lax.axis_index("c")


def _all_to_all(x, name, bcast=False):
    slab = x.shape if bcast else x.shape[1:]

    def body(x_ref, o_ref, send_sems, recv_sems, local_sem):
        me = _my_index()

        def src(j):
            return x_ref if bcast else x_ref.at[j]

        mine = pltpu.make_async_copy(src(me), o_ref.at[me], local_sem)
        mine.start()
        sends = []
        for d in range(1, N_DEV):
            peer = (me + d) % N_DEV
            cp = pltpu.make_async_remote_copy(
                src_ref=src(peer), dst_ref=o_ref.at[me],
                send_sem=send_sems.at[d - 1], recv_sem=recv_sems.at[d - 1],
                device_id=(peer // 4, (peer // 2) % 2, peer % 2), device_id_type=MESH)
            cp.start()
            sends.append(cp)
        for d in range(1, N_DEV):
            frm = (me + N_DEV - d) % N_DEV
            pltpu.make_async_remote_copy(
                src_ref=src(frm), dst_ref=o_ref.at[frm],
                send_sem=send_sems.at[d - 1], recv_sem=recv_sems.at[d - 1],
                device_id=(frm // 4, (frm // 2) % 2, frm % 2), device_id_type=MESH).wait_recv()
        for cp in sends:
            cp.wait_send()
        mine.wait()

    return pl.pallas_call(
        body, name=name,
        out_shape=jax.ShapeDtypeStruct((N_DEV,) + tuple(slab), x.dtype),
        in_specs=[pl.BlockSpec(memory_space=pl.ANY)],
        out_specs=pl.BlockSpec(memory_space=pl.ANY),
        scratch_shapes=[pltpu.SemaphoreType.DMA((N_DEV - 1,)), pltpu.SemaphoreType.DMA((N_DEV - 1,)),
                        pltpu.SemaphoreType.DMA],
        compiler_params=pltpu.CompilerParams(has_side_effects=True),
    )(x)


def _all_gather(x, name):
    def body(x_ref, o_ref, send_sems, recv_sems, local_sem):
        cx, cy, cc = lax.axis_index("x"), lax.axis_index("y"), lax.axis_index("c")
        me, sibling = (cx, cy, cc), (cx, cy, 1 - cc)
        chips = [(1 - cx, cy), (cx, 1 - cy), (1 - cx, 1 - cy)]

        def slab(px, py, pc):
            return o_ref.at[4 * px + 2 * py + pc]

        def copy(k, block, to, src=None):
            return pltpu.make_async_remote_copy(
                src_ref=slab(*block) if src is None else src, dst_ref=slab(*block),
                send_sem=send_sems.at[k], recv_sem=recv_sems.at[k], device_id=to, device_id_type=MESH)

        mine = pltpu.make_async_copy(x_ref, slab(*me), local_sem)
        mine.start()
        first = [copy(0, me, sibling, src=x_ref)]
        first += [copy(1 + j, me, (*chip, cc), src=x_ref) for j, chip in enumerate(chips)]
        for cp in first:
            cp.start()
        passed = [copy(4 + j, (*chip, cc), sibling) for j, chip in enumerate(chips)]
        for j, chip in enumerate(chips):
            copy(1 + j, (*chip, cc), me).wait_recv()
            passed[j].start()
        copy(0, sibling, me).wait_recv()
        for j, chip in enumerate(chips):
            copy(4 + j, (*chip, 1 - cc), me).wait_recv()
        for cp in first + passed:
            cp.wait_send()
        mine.wait()

    return pl.pallas_call(
        body, name=name,
        out_shape=jax.ShapeDtypeStruct((N_DEV,) + tuple(x.shape), x.dtype),
        in_specs=[pl.BlockSpec(memory_space=pl.ANY)],
        out_specs=pl.BlockSpec(memory_space=pl.ANY),
        scratch_shapes=[pltpu.SemaphoreType.DMA((7,)), pltpu.SemaphoreType.DMA((7,)), pltpu.SemaphoreType.DMA],
        compiler_params=pltpu.CompilerParams(has_side_effects=True),
    )(x)


def _mm(a, b, *, name, ta=False, tb=False, bmode="plain", layer=0, out_mode="plain", out_dtypes=(F32,),
        epilogue=None, extra=(), caps=(1024, 1024, 1024)):
    if ta:
        K, M = a.shape
    else:
        M, K = a.shape
    n_unit = k_unit = None
    if bmode == "plain":
        N, Kb = (b.shape if tb else b.shape[::-1])
    elif bmode == "col":
        _, _, Kw, Ns = b.shape
        if tb:
            N, Kb, k_unit = Kw, N_DEV * Ns, Ns
        else:
            N, Kb, n_unit = N_DEV * Ns, Kw, Ns
    else:
        _, _, Ks, Nw = b.shape
        if tb:
            N, Kb, n_unit = N_DEV * Ks, Nw, Ks
        else:
            N, Kb, k_unit = Nw, N_DEV * Ks, Ks
    assert K == Kb, (name, a.shape, b.shape)
    if out_mode == "col":
        assert n_unit is None
        n_unit = N // N_DEV
    tm = _tile(M, caps[0])
    tn = _tile(n_unit or N, caps[1])
    tk = _tile(k_unit or K, caps[2])
    nk = K // tk
    npb = (n_unit // tn) if n_unit else None
    kpb = (k_unit // tk) if k_unit else None
    grid = (M // tm, N // tn, nk)

    a_spec = pl.BlockSpec((tk, tm), lambda i, j, k: (k, i)) if ta else pl.BlockSpec((tm, tk), lambda i, j, k: (i, k))
    if bmode == "plain":
        b_spec = (pl.BlockSpec((tn, tk), lambda i, j, k: (j, k)) if tb
                  else pl.BlockSpec((tk, tn), lambda i, j, k: (k, j)))
    elif bmode == "col":
        b_spec = (pl.BlockSpec((None, None, tn, tk), lambda i, j, k: (k // kpb, layer, j, k % kpb)) if tb
                  else pl.BlockSpec((None, None, tk, tn), lambda i, j, k: (j // npb, layer, k, j % npb)))
    else:
        b_spec = (pl.BlockSpec((None, None, tn, tk), lambda i, j, k: (j // npb, layer, j % npb, k)) if tb
                  else pl.BlockSpec((None, None, tk, tn), lambda i, j, k: (k // kpb, layer, k % kpb, j)))
    mn_spec = pl.BlockSpec((tm, tn), lambda i, j, k: (i, j))
    if out_mode == "col":
        o_specs = [pl.BlockSpec((None, tm, tn), lambda i, j, k: (j // npb, i, j % npb))]
        o_shapes = [jax.ShapeDtypeStruct((N_DEV, M, N // N_DEV), out_dtypes[0])]
    else:
        o_specs = [mn_spec for _ in out_dtypes]
        o_shapes = [jax.ShapeDtypeStruct((M, N), dt) for dt in out_dtypes]
    dims = (((0 if ta else 1,), (1 if tb else 0,)), ((), ()))
    n_extra, n_out = len(extra), len(out_dtypes)

    def body(a_ref, b_ref, *rest):
        extra_refs, out_refs, acc_ref = rest[:n_extra], rest[n_extra:n_extra + n_out], rest[-1]
        k = pl.program_id(2)

        @pl.when(k == 0)
        def _():
            acc_ref[...] = jnp.zeros_like(acc_ref)

        acc_ref[...] += lax.dot_general(a_ref[...].astype(BF16), b_ref[...].astype(BF16), dims,
                                        preferred_element_type=F32)

        @pl.when(k == nk - 1)
        def _():
            acc = acc_ref[...]
            outs = (acc,) if epilogue is None else epilogue(acc, *[r[...] for r in extra_refs])
            for o_ref, val in zip(out_refs, outs):
                o_ref[...] = val.astype(o_ref.dtype)

    outs = pl.pallas_call(
        body, name=name, grid=grid,
        in_specs=[a_spec, b_spec] + [mn_spec for _ in extra],
        out_specs=o_specs, out_shape=o_shapes,
        scratch_shapes=[pltpu.VMEM((tm, tn), F32)],
        compiler_params=_params(("parallel", "parallel", "arbitrary")),
    )(a, b, *extra)
    return outs[0] if n_out == 1 else outs


def _relu2_epilogue(acc):
    r = jnp.maximum(acc, 0.0)
    return acc, r * r


def _relu2_bwd_epilogue(acc, a_pre):
    return (acc * (2.0 * jnp.maximum(a_pre.astype(F32), 0.0)),)


def _rowcall(body, *, name, tr, row_ins, full_ins=(), row_outs=(), acc_outs=(), scratch=(), reverse=False):
    T = row_ins[0].shape[0]
    nb = T // tr
    rmap = (lambda i: (nb - 1 - i, 0)) if reverse else (lambda i: (i, 0))

    def full_spec(shape):
        nd = len(shape)
        return pl.BlockSpec(tuple(shape), lambda i: (0,) * nd)

    in_specs = [pl.BlockSpec((tr, a.shape[1]), rmap) for a in row_ins] + [full_spec(a.shape) for a in full_ins]
    out_specs = [pl.BlockSpec((tr, s.shape[1]), rmap) for s in row_outs] + [full_spec(s.shape) for s in acc_outs]
    outs = pl.pallas_call(
        body, name=name, grid=(nb,), in_specs=in_specs, out_specs=out_specs,
        out_shape=list(row_outs) + list(acc_outs), scratch_shapes=list(scratch),
        compiler_params=_params(("arbitrary",)),
    )(*row_ins, *full_ins)
    return outs


def _sds(shape, dtype):
    return jax.ShapeDtypeStruct(tuple(shape), dtype)


def _row_tile(T, C, elems=512 * 1024):
    t = max(8, min(T, elems // C))
    p = 8
    while p * 2 <= t and T % (p * 2) == 0:
        p *= 2
    return p


def _norm_mod(x, ng, sc, sh, name):
    T, D = x.shape

    def body(x_ref, ng_ref, sc_ref, sh_ref, h_ref):
        xv = x_ref[...]
        r = lax.rsqrt(jnp.mean(xv * xv, axis=-1, keepdims=True) + EPS)
        h_ref[...] = (((xv * r) * ng_ref[...]) * (1.0 + sc_ref[...]) + sh_ref[...]).astype(BF16)

    return _rowcall(body, name=name, tr=_row_tile(T, D), row_ins=[x], full_ins=[ng, sc, sh],
                    row_outs=[_sds((T, D), BF16)])[0]


def _res_norm_mod(x, y, gate, ng, sc, sh, name):
    T, D = x.shape

    def body(x_ref, y_ref, g_ref, ng_ref, sc_ref, sh_ref, x2_ref, h_ref):
        xv = x_ref[...] + g_ref[...] * y_ref[...]
        x2_ref[...] = xv
        r = lax.rsqrt(jnp.mean(xv * xv, axis=-1, keepdims=True) + EPS)
        h_ref[...] = (((xv * r) * ng_ref[...]) * (1.0 + sc_ref[...]) + sh_ref[...]).astype(BF16)

    return _rowcall(body, name=name, tr=_row_tile(T, D, 256 * 1024), row_ins=[x, y], full_ins=[gate, ng, sc, sh],
                    row_outs=[_sds((T, D), F32), _sds((T, D), BF16)])


def _res_add(x, y, gate, name):
    T, D = x.shape

    def body(x_ref, y_ref, g_ref, x2_ref):
        x2_ref[...] = x_ref[...] + g_ref[...] * y_ref[...]

    return _rowcall(body, name=name, tr=_row_tile(T, D), row_ins=[x, y], full_ins=[gate],
                    row_outs=[_sds((T, D), F32)])[0]


def _res_loss(x, y, gate, target, name):
    T, D = x.shape

    def body(x_ref, y_ref, t_ref, g_ref, dy_ref, loss_ref):
        @pl.when(pl.program_id(0) == 0)
        def _():
            loss_ref[...] = jnp.zeros_like(loss_ref)

        diff = x_ref[...] + g_ref[...] * y_ref[...] - t_ref[...]
        dy_ref[...] = diff * (1.0 / D)
        loss_ref[...] += jnp.sum(diff * diff) * (0.5 / D)

    return _rowcall(body, name=name, tr=_row_tile(T, D, 256 * 1024), row_ins=[x, y, target], full_ins=[gate],
                    row_outs=[_sds((T, D), F32)], acc_outs=[_sds((1, BLK), F32)])


def _gate_bwd(dx, y, gate, name):
    T, D = dx.shape

    def body(dx_ref, y_ref, g_ref, dy_ref, dg_ref):
        @pl.when(pl.program_id(0) == 0)
        def _():
            dg_ref[...] = jnp.zeros_like(dg_ref)

        dxv = dx_ref[...]
        dy_ref[...] = (dxv * g_ref[...]).astype(BF16)
        dg_ref[...] += jnp.sum(dxv * y_ref[...], axis=0, keepdims=True)

    return _rowcall(body, name=name, tr=_row_tile(T, D), row_ins=[dx, y], full_ins=[gate],
                    row_outs=[_sds((T, D), BF16)], acc_outs=[_sds((1, D), F32)])


def _norm_mod_bwd(x, dhs, dres, ng, sc, name):
    T, D = x.shape
    n_dh = len(dhs)

    def body(*refs):
        x_ref, dh_refs, dres_ref = refs[0], refs[1:1 + n_dh], refs[1 + n_dh]
        ng_ref, sc_ref, dx_ref, sums_ref = refs[2 + n_dh:]

        @pl.when(pl.program_id(0) == 0)
        def _():
            sums_ref[...] = jnp.zeros_like(sums_ref)

        xv = x_ref[...]
        dh = dh_refs[0][...].astype(F32)
        for r_ in dh_refs[1:]:
            dh = dh + r_[...].astype(F32)
        r = lax.rsqrt(jnp.mean(xv * xv, axis=-1, keepdims=True) + EPS)
        n = xv * r
        ngv, scale1 = ng_ref[...], 1.0 + sc_ref[...]
        dn = dh * (ngv * scale1)
        dx_ref[...] = dres_ref[...] + r * (dn - n * jnp.mean(dn * n, axis=-1, keepdims=True))
        dhn = dh * n
        sums_ref[0:1, :] += jnp.sum(dh, axis=0, keepdims=True)
        sums_ref[1:2, :] += jnp.sum(dhn * ngv, axis=0, keepdims=True)
        sums_ref[2:3, :] += jnp.sum(dhn * scale1, axis=0, keepdims=True)

    return _rowcall(body, name=name, tr=_row_tile(T, D, 256 * 1024), row_ins=[x, *dhs, dres], full_ins=[ng, sc],
                    row_outs=[_sds((T, D), F32)], acc_outs=[_sds((8, D), F32)])


def _head_norm(x, g, n_heads, name, tail=False):
    T = x.shape[0]
    D = n_heads * BLK
    W = x.shape[1] if tail else D

    def body(x_ref, g_ref, o_ref, *tail_ref):
        for h in range(n_heads):
            xv = x_ref[:, h * BLK:(h + 1) * BLK]
            r = lax.rsqrt(jnp.mean(xv * xv, axis=-1, keepdims=True) + EPS)
            o_ref[:, h * BLK:(h + 1) * BLK] = ((xv * r) * g_ref[...]).astype(BF16)
        if tail:
            tail_ref[0][...] = x_ref[:, D:2 * D].astype(BF16)

    tr = _row_tile(T, x.shape[1])
    o_spec = pl.BlockSpec((tr, D), lambda i: (i, 0))
    return pl.pallas_call(
        body, name=name, grid=(T // tr,),
        in_specs=[pl.BlockSpec((tr, W), lambda i: (i, 0)), pl.BlockSpec((1, BLK), lambda i: (0, 0))],
        out_specs=[o_spec] * (2 if tail else 1), out_shape=[_sds((T, D), BF16)] * (2 if tail else 1),
        compiler_params=_params(("parallel",)),
    )(x, g)


def _head_norm_bwd(x, dys, g, n_heads, name, tails=()):
    T = x.shape[0]
    D = n_heads * BLK
    n_dy, n_tail = len(dys), len(tails)
    W = 2 * D if tails else D

    def body(*refs):
        x_ref, dy_refs, tail_refs = refs[0], refs[1:1 + n_dy], refs[1 + n_dy:1 + n_dy + n_tail]
        g_ref, dx_ref, dg_ref = refs[1 + n_dy + n_tail:]

        @pl.when(pl.program_id(0) == 0)
        def _():
            dg_ref[...] = jnp.zeros_like(dg_ref)

        tot = jnp.zeros((1, BLK), F32)
        for h in range(n_heads):
            cols = slice(h * BLK, (h + 1) * BLK)
            xv = x_ref[:, cols]
            dyv = dy_refs[0][:, cols]
            for r_ in dy_refs[1:]:
                dyv = dyv + r_[:, cols]
            r = lax.rsqrt(jnp.mean(xv * xv, axis=-1, keepdims=True) + EPS)
            n = xv * r
            dn = dyv * g_ref[...]
            dx_ref[:, cols] = (r * (dn - n * jnp.mean(dn * n, axis=-1, keepdims=True))).astype(BF16)
            tot = tot + jnp.sum(dyv * n, axis=0, keepdims=True)
        dg_ref[0:1, :] += tot
        if n_tail:
            tv = tail_refs[0][...]
            for r_ in tail_refs[1:]:
                tv = tv + r_[...]
            dx_ref[:, D:] = tv.astype(BF16)

    tr = _row_tile(T, 2 * D, 256 * 1024)
    d_spec = pl.BlockSpec((tr, D), lambda i: (i, 0))
    return pl.pallas_call(
        body, name=name, grid=(T // tr,),
        in_specs=[d_spec] * (1 + n_dy + n_tail) + [pl.BlockSpec((1, BLK), lambda i: (0, 0))],
        out_specs=[pl.BlockSpec((tr, W), lambda i: (i, 0)), pl.BlockSpec((8, BLK), lambda i: (0, 0))],
        out_shape=[_sds((T, W), BF16), _sds((8, BLK), F32)],
        compiler_params=_params(("arbitrary",)),
    )(x, *dys, *tails, g)


def _fcum_fwd(fl, bf, name):
    T = fl.shape[0]

    def body(fl_ref, b_ref, o_ref, carry_ref):
        @pl.when(pl.program_id(0) == 0)
        def _():
            carry_ref[...] = jnp.zeros_like(carry_ref)

        z = fl_ref[...] + b_ref[...]
        logf = jnp.minimum(z, 0.0) - jnp.log(1.0 + jnp.exp(-jnp.abs(z)))
        row = lax.broadcasted_iota(jnp.int32, (BLK, BLK), 0)
        col = lax.broadcasted_iota(jnp.int32, (BLK, BLK), 1)
        tri = (col <= row).astype(F32)
        run = jnp.dot(tri, logf, preferred_element_type=F32, precision=lax.Precision.HIGHEST) + carry_ref[0:1, :]
        o_ref[...] = run
        carry_ref[0:1, :] = run[BLK - 1:BLK, :]

    return _rowcall(body, name=name, tr=BLK, row_ins=[fl], full_ins=[bf], row_outs=[_sds((T, BLK), F32)],
                    scratch=[pltpu.VMEM((8, BLK), F32)])[0]


def _fcum_bwd(dfs, fl, bf, name):
    T = fl.shape[0]
    n_df = len(dfs)

    def body(*refs):
        df_refs = refs[:n_df]
        fl_ref, b_ref, dfl_ref, dbias_ref, carry_ref = refs[n_df:]

        @pl.when(pl.program_id(0) == 0)
        def _():
            carry_ref[...] = jnp.zeros_like(carry_ref)
            dbias_ref[...] = jnp.zeros_like(dbias_ref)

        dfc = df_refs[0][...]
        for r_ in df_refs[1:]:
            dfc = dfc + r_[...]
        row = lax.broadcasted_iota(jnp.int32, (BLK, BLK), 0)
        col = lax.broadcasted_iota(jnp.int32, (BLK, BLK), 1)
        tri = (col >= row).astype(F32)
        suffix = jnp.dot(tri, dfc, preferred_element_type=F32, precision=lax.Precision.HIGHEST) + carry_ref[0:1, :]
        carry_ref[0:1, :] = suffix[0:1, :]
        z = fl_ref[...] + b_ref[...]
        dfl = suffix / (1.0 + jnp.exp(z))
        dfl_ref[...] = dfl.astype(BF16)
        dbias_ref[0:1, :] += jnp.sum(dfl, axis=0, keepdims=True)

    return _rowcall(body, name=name, tr=BLK, row_ins=[*dfs, fl], full_ins=[bf], reverse=True,
                    row_outs=[_sds((T, BLK), BF16)], acc_outs=[_sds((8, BLK), F32)],
                    scratch=[pltpu.VMEM((8, BLK), F32)])


def _attn_tile(T):
    return min(T, 512)


def _lane_pick(block, h):
    lane = lax.broadcasted_iota(jnp.int32, block.shape, 1)
    return jnp.sum(jnp.where(lane == h, block, 0.0), axis=1, keepdims=True)


def _attn_fwd(q, k, v, fq, fk, n_heads, name):
    T = q.shape[0]
    tq = tk = _attn_tile(T)
    nkb = T // tk
    inv_sqrt = 1.0 / float(math.sqrt(BLK))

    def body(q_ref, k_ref, v_ref, fq_ref, fk_ref, o_ref, o32_ref, lse_ref):
        h, i = pl.program_id(0), pl.program_id(1)
        qv = q_ref[...]
        fqv = _lane_pick(fq_ref[...], h)
        qpos = i * tq + lax.broadcasted_iota(jnp.int32, (tq, tk), 0)
        kloc = lax.broadcasted_iota(jnp.int32, (tq, tk), 1)

        def step(j, carry):
            m, l, acc = carry
            rows = pl.ds(pl.multiple_of(j * tk, tk), tk)
            kj, vj = k_ref[rows, :], v_ref[rows, :]
            s = lax.dot_general(qv, kj, (((1,), (1,)), ((), ())), preferred_element_type=F32) * inv_sqrt
            s = s + (fqv - fk_ref[j])
            s = jnp.where(j * tk + kloc <= qpos, s, NEG_BIG)
            m_new = jnp.maximum(m, jnp.max(s, axis=-1, keepdims=True))
            alpha = jnp.exp(m - m_new)
            p = jnp.exp(s - m_new)
            l = alpha * l + jnp.sum(p, axis=-1, keepdims=True)
            acc = alpha * acc + jnp.dot(p.astype(BF16), vj, preferred_element_type=F32)
            return m_new, l, acc

        init = (jnp.full((tq, 1), NEG_BIG, F32), jnp.zeros((tq, 1), F32), jnp.zeros((tq, BLK), F32))
        m, l, acc = lax.fori_loop(0, i + 1, step, init)
        out = acc / l
        o_ref[...] = out.astype(BF16)
        o32_ref[...] = out
        lse_ref[...] = m + jnp.log(l)

    return pl.pallas_call(
        body, name=name, grid=(n_heads, T // tq),
        in_specs=[pl.BlockSpec((tq, BLK), lambda h, i: (i, h)),
                  pl.BlockSpec((T, BLK), lambda h, i: (0, h)),
                  pl.BlockSpec((T, BLK), lambda h, i: (0, h)),
                  pl.BlockSpec((tq, BLK), lambda h, i: (i, 0)),
                  pl.BlockSpec((None, nkb, 1, tk), lambda h, i: (h, 0, 0, 0))],
        out_specs=[pl.BlockSpec((tq, BLK), lambda h, i: (i, h)),
                   pl.BlockSpec((tq, BLK), lambda h, i: (i, h)),
                   pl.BlockSpec((None, tq, 1), lambda h, i: (h, i, 0))],
        out_shape=[_sds((T, n_heads * BLK), BF16), _sds((T, n_heads * BLK), F32), _sds((n_heads, T, 1), F32)],
        compiler_params=_params(("parallel", "arbitrary")),
    )(q, k, v, fq, fk)


def _attn_bwd(q, k, v, o, do, lse, fq, fk, n_heads, name):
    T = q.shape[0]
    tq = tk = _attn_tile(T)
    nkb = T // tk
    nq = T // tq
    inv_sqrt = 1.0 / float(math.sqrt(BLK))
    tn_dims = (((0,), (0,)), ((), ()))
    nt_dims = (((1,), (1,)), ((), ()))

    def body(q_ref, k_ref, v_ref, o_ref, do_ref, lse_ref, fq_ref, fk_ref, dq_ref, dk_ref, dv_ref, dfq_ref, dfk_ref,
             delta_ref):
        h, j = pl.program_id(0), pl.program_id(1)

        @pl.when(j == 0)
        def _():
            delta_ref[...] = jnp.sum(do_ref[...].astype(F32) * o_ref[...], axis=1, keepdims=True)
            dq_ref[...] = jnp.zeros_like(dq_ref)
            dfq_ref[...] = jnp.zeros_like(dfq_ref)

        kj, vj, fkv = k_ref[...], v_ref[...], fk_ref[...]
        kpos = j * tk + lax.broadcasted_iota(jnp.int32, (tq, tk), 1)
        qloc = lax.broadcasted_iota(jnp.int32, (tq, tk), 0)

        def step(i, carry):
            dk, dv, dfk = carry
            rows = pl.ds(pl.multiple_of(i * tq, tq), tq)
            qi, doi = q_ref[rows, :], do_ref[rows, :]
            fqv = _lane_pick(fq_ref[rows, :], h)
            s = lax.dot_general(qi, kj, nt_dims, preferred_element_type=F32) * inv_sqrt + (fqv - fkv)
            s = jnp.where(kpos <= i * tq + qloc, s, NEG_BIG)
            p = jnp.exp(s - lse_ref[rows, :])
            dv = dv + lax.dot_general(p.astype(BF16), doi, tn_dims, preferred_element_type=F32)
            dp = lax.dot_general(doi, vj, nt_dims, preferred_element_type=F32)
            ds = p * (dp - delta_ref[rows, :])
            dsb = ds.astype(BF16)
            dq_ref[rows, :] += jnp.dot(dsb, kj, preferred_element_type=F32) * inv_sqrt
            dk = dk + lax.dot_general(dsb, qi, tn_dims, preferred_element_type=F32)
            dfq_ref[rows, :] += jnp.sum(ds, axis=1, keepdims=True)
            dfk = dfk - jnp.sum(ds, axis=0, keepdims=True)
            return dk, dv, dfk

        init = (jnp.zeros((tk, BLK), F32), jnp.zeros((tk, BLK), F32), jnp.zeros((1, tk), F32))
        dk, dv, dfk = lax.fori_loop(j, nq, step, init)
        dk_ref[...] = dk * inv_sqrt
        dv_ref[...] = dv
        dfk_ref[...] = dfk

    head_col = lambda h, j: (0, h)
    return pl.pallas_call(
        body, name=name, grid=(n_heads, nkb),
        in_specs=[pl.BlockSpec((T, BLK), head_col),
                  pl.BlockSpec((tk, BLK), lambda h, j: (j, h)),
                  pl.BlockSpec((tk, BLK), lambda h, j: (j, h)),
                  pl.BlockSpec((T, BLK), head_col),
                  pl.BlockSpec((T, BLK), head_col),
                  pl.BlockSpec((None, T, 1), lambda h, j: (h, 0, 0)),
                  pl.BlockSpec((T, BLK), lambda h, j: (0, 0)),
                  pl.BlockSpec((None, None, 1, tk), lambda h, j: (h, j, 0, 0))],
        out_specs=[pl.BlockSpec((T, BLK), head_col),
                   pl.BlockSpec((tk, BLK), lambda h, j: (j, h)),
                   pl.BlockSpec((tk, BLK), lambda h, j: (j, h)),
                   pl.BlockSpec((None, T, 1), lambda h, j: (h, 0, 0)),
                   pl.BlockSpec((None, None, 1, tk), lambda h, j: (h, j, 0, 0))],
        out_shape=[_sds((T, n_heads * BLK), F32), _sds((T, n_heads * BLK), F32), _sds((T, n_heads * BLK), F32),
                   _sds((n_heads, T, 1), F32), _sds((n_heads, nkb, 1, tk), F32)],
        scratch_shapes=[pltpu.VMEM((T, 1), F32)],
        compiler_params=_params(("parallel", "arbitrary")),
    )(q, k, v, o, do, lse, fq, fk)


_INV_SQRT2 = 1.0 / math.sqrt(2.0)
_INV_SQRT_2PI = 1.0 / math.sqrt(2.0 * math.pi)


def _gelu_parts(z):
    cdf = 0.5 * (1.0 + lax.erf(z * _INV_SQRT2))
    return cdf, z * cdf


def _mix_mask(transposed):
    row = lax.broadcasted_iota(jnp.int32, (BLK, BLK), 0) // CHUNK
    col = lax.broadcasted_iota(jnp.int32, (BLK, BLK), 1) // CHUNK
    return (row <= col) if transposed else (col <= row)


def _gmlp_mid_fwd(zpre, ln_g, ln_b, ws, bs_t, name):
    T, two_h = zpre.shape
    Hh = two_h // 2
    G = ws.shape[0]
    gd = Hh // G

    def body(z_ref, lg_ref, lb_ref, ws_ref, bs_ref, p_ref):
        _, zg = _gelu_parts(z_ref[...].astype(F32))
        u, v = zg[:, :Hh], zg[:, Hh:]
        mu = jnp.mean(v, axis=-1, keepdims=True)
        vc = v - mu
        rstd = lax.rsqrt(jnp.mean(vc * vc, axis=-1, keepdims=True) + EPS)
        vn = ((vc * rstd) * lg_ref[...] + lb_ref[...]).astype(BF16)
        mask = _mix_mask(False)
        for g in range(G):
            wm = jnp.where(mask, ws_ref[g], 0.0).astype(BF16)
            sv = jnp.dot(wm, vn[:, g * gd:(g + 1) * gd], preferred_element_type=F32) + bs_ref[:, g:g + 1]
            p_ref[:, g * gd:(g + 1) * gd] = (u[:, g * gd:(g + 1) * gd] * sv).astype(BF16)

    return _rowcall(body, name=name, tr=BLK, row_ins=[zpre], full_ins=[ln_g, ln_b, ws, bs_t],
                    row_outs=[_sds((T, Hh), BF16)])[0]


def _gmlp_mid_bwd(zpre, dp, ln_g, ln_b, ws, ws_t, bs_t, name):
    T, two_h = zpre.shape
    Hh = two_h // 2
    G = ws.shape[0]
    gd = Hh // G
    nt_dims = (((1,), (1,)), ((), ()))

    def body(z_ref, dp_ref, lg_ref, lb_ref, ws_ref, wst_ref, bs_ref, dz_ref, dws_ref, dbs_ref, dlg_ref, dlb_ref,
             dvn_ref):
        @pl.when(pl.program_id(0) == 0)
        def _():
            dws_ref[...] = jnp.zeros_like(dws_ref)
            dbs_ref[...] = jnp.zeros_like(dbs_ref)
            dlg_ref[...] = jnp.zeros_like(dlg_ref)
            dlb_ref[...] = jnp.zeros_like(dlb_ref)

        z = z_ref[...].astype(F32)
        cdf, zg = _gelu_parts(z)
        dgelu = cdf + z * (jnp.exp(-0.5 * z * z) * _INV_SQRT_2PI)
        u, v = zg[:, :Hh], zg[:, Hh:]
        mu = jnp.mean(v, axis=-1, keepdims=True)
        vc = v - mu
        rstd = lax.rsqrt(jnp.mean(vc * vc, axis=-1, keepdims=True) + EPS)
        vhat = vc * rstd
        vn = (vhat * lg_ref[...] + lb_ref[...]).astype(BF16)
        mask, mask_t = _mix_mask(False), _mix_mask(True)
        lane = lax.broadcasted_iota(jnp.int32, (BLK, BLK), 1)
        dbs = jnp.zeros((BLK, BLK), F32)
        for g in range(G):
            cols = slice(g * gd, (g + 1) * gd)
            wm = jnp.where(mask, ws_ref[g], 0.0).astype(BF16)
            wm_t = jnp.where(mask_t, wst_ref[g], 0.0).astype(BF16)
            vn_g = vn[:, cols]
            sv = jnp.dot(wm, vn_g, preferred_element_type=F32) + bs_ref[:, g:g + 1]
            dp_g = dp_ref[:, cols].astype(F32)
            dz_ref[:, cols] = ((dp_g * sv) * dgelu[:, cols]).astype(BF16)
            dsv = dp_g * u[:, cols]
            dsv_b = dsv.astype(BF16)
            dbs = dbs + jnp.where(lane == g, jnp.sum(dsv, axis=1, keepdims=True), 0.0)
            dws_ref[g] += jnp.where(mask, lax.dot_general(dsv_b, vn_g, nt_dims, preferred_element_type=F32), 0.0)
            dvn_ref[:, cols] = jnp.dot(wm_t, dsv_b, preferred_element_type=F32)
        dbs_ref[...] += dbs
        dvn = dvn_ref[...]
        dlg_ref[0:1, :] += jnp.sum(dvn * vhat, axis=0, keepdims=True)
        dlb_ref[0:1, :] += jnp.sum(dvn, axis=0, keepdims=True)
        dvh = dvn * lg_ref[...]
        dv = rstd * (dvh - jnp.mean(dvh, axis=-1, keepdims=True) - vhat * jnp.mean(dvh * vhat, axis=-1, keepdims=True))
        dz_ref[:, Hh:] = (dv * dgelu[:, Hh:]).astype(BF16)

    return _rowcall(body, name=name, tr=BLK, row_ins=[zpre, dp], full_ins=[ln_g, ln_b, ws, ws_t, bs_t],
                    row_outs=[_sds((T, two_h), BF16)],
                    acc_outs=[_sds((G, BLK, BLK), F32), _sds((BLK, BLK), F32), _sds((8, Hh), F32), _sds((8, Hh), F32)],
                    scratch=[pltpu.VMEM((BLK, Hh), F32)])


def _mods(c_all, w, layer, bias, name):
    nb, K = c_all.shape
    N = w.shape[-1]
    tn = _tile(N, 512)

    def body(c_ref, w_ref, b_ref, o_ref):
        cv = c_ref[...]
        sc = cv / (1.0 + jnp.exp(-cv))
        o_ref[...] = jnp.dot(sc, w_ref[...], preferred_element_type=F32, precision=lax.Precision.HIGHEST) + b_ref[...]

    return pl.pallas_call(
        body, name=name, grid=(N // tn,),
        in_specs=[pl.BlockSpec((nb, K), lambda j: (0, 0)),
                  pl.BlockSpec((None, K, tn), lambda j: (layer, 0, j)),
                  pl.BlockSpec((1, tn), lambda j: (0, j))],
        out_specs=pl.BlockSpec((nb, tn), lambda j: (0, j)), out_shape=_sds((nb, N), F32),
        compiler_params=_params(("parallel",)),
    )(c_all, w, bias)


def _sum_slabs(x, name):
    _, R, C = x.shape
    tr = _row_tile(R, C * N_DEV)

    def body(x_ref, o_ref):
        acc = x_ref[0]
        for s in range(1, N_DEV):
            acc = acc + x_ref[s]
        o_ref[...] = acc

    return pl.pallas_call(
        body, name=name, grid=(R // tr,),
        in_specs=[pl.BlockSpec((N_DEV, tr, C), lambda i: (0, i, 0))],
        out_specs=pl.BlockSpec((tr, C), lambda i: (i, 0)), out_shape=_sds((R, C), F32),
        compiler_params=_params(("parallel",)),
    )(x)


def _adamw_math(w, g, m, v):
    m = ADAM_B1 * m + (1.0 - ADAM_B1) * g
    v = ADAM_B2 * v + (1.0 - ADAM_B2) * (g * g)
    m_hat = m / (1.0 - ADAM_B1 ** ADAM_STEP)
    v_hat = v / (1.0 - ADAM_B2 ** ADAM_STEP)
    delta = -ADAM_LR * (m_hat / (jnp.sqrt(v_hat) + ADAM_EPS) + ADAM_WD * w)
    return delta, m, v


def _adamw_layers(grad_sources, make_grad, w, m, v, name):
    L, R, C = w.shape
    tr = _row_tile(R, C, 128 * 1024)
    outs = None
    for l in range(L):
        srcs, src_specs = grad_sources(l, tr)
        n_src = len(srcs)
        wspec = pl.BlockSpec((None, tr, C), lambda i, l=l: (l, i, 0))

        def body(*refs, n_src=n_src):
            src_refs = refs[:n_src]
            w_ref, m_ref, v_ref = refs[n_src:n_src + 3]
            g_ref, d_ref, m2_ref, v2_ref = refs[-4:]
            g = make_grad(*src_refs)
            delta, m2, v2 = _adamw_math(w_ref[...], g, m_ref[...], v_ref[...])
            g_ref[...] = g
            d_ref[...] = delta
            m2_ref[...] = m2
            v2_ref[...] = v2

        prev = [] if outs is None else list(outs)
        aliases = {} if outs is None else {n_src + 3 + t: t for t in range(4)}
        outs = pl.pallas_call(
            body, name=f"{name}_l{l}", grid=(R // tr,),
            in_specs=list(src_specs) + [wspec] * 3 + [pl.BlockSpec(memory_space=pl.ANY)] * len(prev),
            out_specs=[wspec] * 4, out_shape=[_sds((L, R, C), F32)] * 4,
            input_output_aliases=aliases,
            compiler_params=_params(("parallel",)),
        )(*srcs, w, m, v, *prev)
    return outs


def _adamw_from_parts(parts_per_layer, w, m, v, name):
    _, R, C = w.shape

    def sources(l, tr):
        return [parts_per_layer[l]], [pl.BlockSpec((N_DEV, tr, C), lambda i: (0, i, 0))]

    def make_grad(p_ref):
        g = p_ref[0].astype(F32)
        for s in range(1, N_DEV):
            g = g + p_ref[s].astype(F32)
        return g

    return _adamw_layers(sources, make_grad, w, m, v, name)


def _adamw_from_outer(sc_t, dmod_per_layer, w, m, v, name):
    _, R, C = w.shape

    def sources(l, tr):
        return ([sc_t, dmod_per_layer[l]],
                [pl.BlockSpec((tr, N_DEV), lambda i: (i, 0)), pl.BlockSpec((N_DEV, C), lambda i: (0, 0))])

    def make_grad(s_ref, d_ref):
        g = s_ref[:, 0:1] * d_ref[0:1, :]
        for b in range(1, N_DEV):
            g = g + s_ref[:, b:b + 1] * d_ref[b:b + 1, :]
        return g

    return _adamw_layers(sources, make_grad, w, m, v, name)


def _adamw_flat(w, g, m, v, name):
    R, C = w.shape
    tr = _row_tile(R, C, 128 * 1024)

    def body(w_ref, g_ref, m_ref, v_ref, d_ref, m2_ref, v2_ref):
        delta, m2, v2 = _adamw_math(w_ref[...], g_ref[...], m_ref[...], v_ref[...])
        d_ref[...] = delta
        m2_ref[...] = m2
        v2_ref[...] = v2

    spec = pl.BlockSpec((tr, C), lambda i: (i, 0))
    return pl.pallas_call(
        body, name=name, grid=(R // tr,), in_specs=[spec] * 4, out_specs=[spec] * 3,
        out_shape=[_sds((R, C), F32)] * 3, compiler_params=_params(("parallel",)),
    )(w, g, m, v)


def _pack(arrays):
    flat = jnp.concatenate([a.reshape(-1).astype(F32) for a in arrays])
    pad = (-flat.shape[0]) % (8 * BLK)
    if pad:
        flat = jnp.concatenate([flat, jnp.zeros((pad,), F32)])
    return flat.reshape(-1, BLK)


def _unpack(buf, shapes, lead=()):
    flat = buf.reshape(tuple(lead) + (-1,))
    out, off = [], 0
    for s in shapes:
        n = int(math.prod(s))
        out.append(flat[..., off:off + n].reshape(tuple(lead) + tuple(s)))
        off += n
    return out


def _row(vec):
    return vec.reshape(1, -1)


def _shard_of(full, axis, me, size):
    return lax.dynamic_slice_in_dim(full, me * size, size, axis=axis)


def kernel(x, c, ada_w, ada_b, norm_g, mlp_w1, mlp_w2, gmlp_w_in, gmlp_ln_g, gmlp_ln_b, gmlp_ws, gmlp_bs, gmlp_w_out, kv_norm_g, kv_ada_w, kv_ada_b, w_kv, k_norm_g, w_f, b_f, attn_wq, q_norm_g, attn_wo, loss_target, m_ada_w, m_ada_b, m_norm_g, m_mlp_w1, m_mlp_w2, m_gmlp_w_in, m_gmlp_ln_g, m_gmlp_ln_b, m_gmlp_ws, m_gmlp_bs, m_gmlp_w_out, m_kv_norm_g, m_kv_ada_w, m_kv_ada_b, m_w_kv, m_k_norm_g, m_w_f, m_b_f, m_attn_wq, m_q_norm_g, m_attn_wo, v_ada_w, v_ada_b, v_norm_g, v_mlp_w1, v_mlp_w2, v_gmlp_w_in, v_gmlp_ln_g, v_gmlp_ln_b, v_gmlp_ws, v_gmlp_bs, v_gmlp_w_out, v_kv_norm_g, v_kv_ada_w, v_kv_ada_b, v_w_kv, v_k_norm_g, v_w_f, v_b_f, v_attn_wq, v_q_norm_g, v_attn_wo):
    given = dict(locals())
    weights = {n: given[n] for n in WEIGHT_NAMES}
    mom_m = {n: given["m_" + n] for n in WEIGHT_NAMES}
    mom_v = {n: given["v_" + n] for n in WEIGHT_NAMES}

    me = _my_index()
    T, D = x.shape[1], x.shape[2]
    depth = ada_w.shape[0]
    n_a = gmlp_w_in.shape[0]
    n_heads = b_f.shape[0]
    G = gmlp_ws.shape[1]
    Hh = gmlp_ln_g.shape[1] * N_DEV
    mod_cols = ada_w.shape[2]
    kv_cols = kv_ada_w.shape[1]
    x0 = x.reshape(T, D)
    target = loss_target.reshape(T, D)

    small_in = [c, norm_g, gmlp_ln_g, gmlp_ln_b, w_f]
    small_shapes = [a.shape for a in small_in]
    got = _all_to_all(_pack(small_in), "gather_small_inputs", bcast=True)
    c_all, norm_g_sh, ln_g_sh, ln_b_sh, w_f_sh = _unpack(got, small_shapes, lead=(N_DEV,))
    c_all = c_all.reshape(N_DEV, D)
    norm_g_full = jnp.moveaxis(norm_g_sh, 0, 2).reshape(depth, 2, D)
    ln_g_full = jnp.moveaxis(ln_g_sh, 0, 1).reshape(n_a, Hh)
    ln_b_full = jnp.moveaxis(ln_b_sh, 0, 1).reshape(n_a, Hh)
    w_f_full = w_f_sh.reshape(D, n_heads)
    w_f_pad = jnp.pad(w_f_full, ((0, 0), (0, BLK - n_heads))).astype(BF16)
    b_f_pad = jnp.pad(b_f, (0, BLK - n_heads)).reshape(1, BLK)

    mod_parts = []
    for l in range(depth):
        bias = _shard_of(ada_b[l], 0, me, mod_cols).reshape(1, mod_cols)
        mod_parts.append(_mods(c_all, ada_w, l, bias, f"mods_l{l}"))
    kv_bias = _shard_of(kv_ada_b, 0, me, kv_cols).reshape(1, kv_cols)
    mod_parts.append(_mods(c_all, kv_ada_w.reshape(1, D, kv_cols), 0, kv_bias, "mods_kv"))
    mods_mine = jnp.concatenate(mod_parts, axis=1)
    mod_width = mods_mine.shape[1]
    mods_pack = jnp.pad(mods_mine, ((0, 0), (0, (-mod_width) % (8 * BLK)))).reshape(N_DEV, -1, BLK)
    mods_got = _all_to_all(mods_pack, "exchange_mods").reshape(N_DEV, -1)[:, :mod_width]
    mods = []
    for l in range(depth):
        mods.append(mods_got[:, l * mod_cols:(l + 1) * mod_cols].reshape(N_MOD, D))
    kv_mod = mods_got[:, depth * mod_cols:].reshape(2, D)
    silu_all = c_all / (1.0 + jnp.exp(-c_all))

    gw1 = _all_gather(mlp_w1.astype(BF16), "gather_mlp_w1")
    gw2 = _all_gather(mlp_w2.astype(BF16), "gather_mlp_w2")
    gwin = _all_gather(gmlp_w_in.astype(BF16), "gather_gmlp_w_in")
    gwout = _all_gather(gmlp_w_out.astype(BF16), "gather_gmlp_w_out")
    gwkv = _all_gather(w_kv.astype(BF16).reshape((1,) + w_kv.shape), "gather_w_kv")
    gwq = _all_gather(attn_wq.astype(BF16), "gather_attn_wq")
    gwo = _all_gather(attn_wo.astype(BF16), "gather_attn_wo")

    tkk = _attn_tile(T)

    saved = []
    xs = x0
    pending = None
    kv = None
    for l in range(depth):
        sh1, sc1, g1, sh2, sc2, g2 = [_row(mods[l][t]) for t in range(N_MOD)]
        ng1, ng2 = _row(norm_g_full[l, 0]), _row(norm_g_full[l, 1])
        st = dict(sc1=sc1, g1=g1, sc2=sc2, g2=g2, ng1=ng1, ng2=ng2)
        if pending is None:
            h1 = _norm_mod(xs, ng1, sc1, sh1, f"norm1_l{l}")
        else:
            xs, h1 = _res_norm_mod(xs, pending[0], pending[1], ng1, sc1, sh1, f"res_norm1_l{l}")
        st["x_in"], st["h1"] = xs, h1
        if l < n_a:
            a = l
            zpre = _mm(h1, gwin, bmode="col", layer=a, out_dtypes=(BF16,), name=f"gmlp_in_l{l}")
            bs_t = gmlp_bs[a].T
            p = _gmlp_mid_fwd(zpre, _row(ln_g_full[a]), _row(ln_b_full[a]), gmlp_ws[a], bs_t, f"gmlp_mid_l{l}")
            y = _mm(p, gwout, bmode="row", layer=a, name=f"gmlp_out_l{l}")
            st.update(zpre=zpre, p=p)
        else:
            if kv is None:
                kv_ng, kv_sh, kv_sc = _row(kv_norm_g), _row(kv_mod[0]), _row(kv_mod[1])
                hkv = _norm_mod(xs, kv_ng, kv_sc, kv_sh, "norm_kv")
                kvp = _mm(hkv, gwkv, bmode="col", layer=0, name="kv_proj")
                kk, vv = _head_norm(kvp, _row(k_norm_g), n_heads, "k_norm", tail=True)
                fl = _mm(hkv, w_f_pad, name="gate_logits")
                fcum = _fcum_fwd(fl, b_f_pad, "fcum")
                fk = fcum[:, :n_heads].T.reshape(n_heads, T // tkk, 1, tkk)
                kv = dict(x=xs, hkv=hkv, kvp=kvp, k=kk, v=vv, fl=fl, fcum=fcum, fk=fk, ng=kv_ng, sc=kv_sc)
            bl = l - n_a
            qp = _mm(h1, gwq, bmode="row", layer=bl, name=f"q_proj_l{l}")
            q = _head_norm(qp, _row(q_norm_g[bl]), n_heads, f"q_norm_l{l}")[0]
            o, o32, lse = _attn_fwd(q, kv["k"], kv["v"], kv["fcum"], kv["fk"], n_heads, f"attn_l{l}")
            y = _mm(o, gwo, bmode="row", layer=bl, name=f"attn_out_l{l}")
            st.update(qp=qp, q=q, o=o, o32=o32, lse=lse)
        xs, h2 = _res_norm_mod(xs, y, g1, ng2, sc2, sh2, f"res_norm2_l{l}")
        a_pre, s_act = _mm(h2, gw1, bmode="col", layer=l, out_dtypes=(BF16, BF16), epilogue=_relu2_epilogue,
                           name=f"mlp_up_l{l}")
        mo = _mm(s_act, gw2, bmode="row", layer=l, name=f"mlp_down_l{l}")
        st.update(y=y, x_mid=xs, h2=h2, a_pre=a_pre, s=s_act, m=mo)
        saved.append(st)
        pending = (mo, g2)

    dx, loss_row = _res_loss(xs, pending[0], pending[1], target, "loss")
    loss = lax.psum(loss_row[0, 0], ("x", "y", "c"))

    def scatter(dw_slabs, name):
        return _all_to_all(dw_slabs, name)

    parts = {n: [None] * weights[n].shape[0] for n in ("mlp_w1", "mlp_w2", "gmlp_w_in", "gmlp_w_out", "attn_wq", "attn_wo")}
    d_mod = [None] * depth
    d_norm_g = [None] * depth
    d_ln_g, d_ln_b, d_ws, d_bs = [None] * n_a, [None] * n_a, [None] * n_a, [None] * n_a
    d_qg = [None] * (depth - n_a)
    dk_list, dv_list, dfk_list = [], [], []
    small = {}

    for l in reversed(range(depth)):
        st = saved[l]
        dm, dg2 = _gate_bwd(dx, st["m"], st["g2"], f"gate2_bwd_l{l}")
        da = _mm(dm, gw2, tb=True, bmode="row", layer=l, out_dtypes=(BF16,), epilogue=_relu2_bwd_epilogue,
                 extra=(st["a_pre"],), name=f"mlp_down_dx_l{l}")
        dw2 = _mm(st["s"], dm, ta=True, out_dtypes=(BF16,), name=f"mlp_down_dw_l{l}")
        parts["mlp_w2"][l] = scatter(dw2.reshape(N_DEV, -1, D), f"scatter_mlp_w2_l{l}")
        dh2 = _mm(da, gw1, tb=True, bmode="col", layer=l, out_dtypes=(BF16,), name=f"mlp_up_dx_l{l}")
        dw1 = _mm(st["h2"], da, ta=True, out_mode="col", out_dtypes=(BF16,), name=f"mlp_up_dw_l{l}")
        parts["mlp_w1"][l] = scatter(dw1, f"scatter_mlp_w1_l{l}")
        dx, sums2 = _norm_mod_bwd(st["x_mid"], [dh2], dx, st["ng2"], st["sc2"], f"norm2_bwd_l{l}")
        dy, dg1 = _gate_bwd(dx, st["y"], st["g1"], f"gate1_bwd_l{l}")
        if l < n_a:
            a = l
            dwo = _mm(st["p"], dy, ta=True, out_dtypes=(BF16,), name=f"gmlp_out_dw_l{l}")
            parts["gmlp_w_out"][a] = scatter(dwo.reshape(N_DEV, -1, D), f"scatter_gmlp_w_out_l{l}")
            dp = _mm(dy, gwout, tb=True, bmode="row", layer=a, out_dtypes=(BF16,), name=f"gmlp_out_dx_l{l}")
            dz, d_ws[a], dbs_t, dlg, dlb = _gmlp_mid_bwd(
                st["zpre"], dp, _row(ln_g_full[a]), _row(ln_b_full[a]), gmlp_ws[a],
                jnp.swapaxes(gmlp_ws[a], 1, 2), gmlp_bs[a].T, f"gmlp_mid_bwd_l{l}")
            d_bs[a], d_ln_g[a], d_ln_b[a] = dbs_t[:, :G].T, dlg[0], dlb[0]
            dwi = _mm(st["h1"], dz, ta=True, out_mode="col", out_dtypes=(BF16,), name=f"gmlp_in_dw_l{l}")
            parts["gmlp_w_in"][a] = scatter(dwi, f"scatter_gmlp_w_in_l{l}")
            dh1s = [_mm(dz, gwin, tb=True, bmode="col", layer=a, out_dtypes=(BF16,), name=f"gmlp_in_dx_l{l}")]
        else:
            bl = l - n_a
            dwo = _mm(st["o"], dy, ta=True, out_dtypes=(BF16,), name=f"attn_out_dw_l{l}")
            parts["attn_wo"][bl] = scatter(dwo.reshape(N_DEV, -1, D), f"scatter_attn_wo_l{l}")
            do = _mm(dy, gwo, tb=True, bmode="row", layer=bl, out_dtypes=(BF16,), name=f"attn_out_dx_l{l}")
            dq, dk, dv, dfq, dfk = _attn_bwd(st["q"], kv["k"], kv["v"], st["o32"], do, st["lse"], kv["fcum"],
                                             kv["fk"], n_heads, f"attn_bwd_l{l}")
            dfk_list += [dfq, dfk]
            dk_list.append(dk)
            dv_list.append(dv)
            dqp, dqg = _head_norm_bwd(st["qp"], [dq], _row(q_norm_g[bl]), n_heads, f"q_norm_bwd_l{l}")
            d_qg[bl] = dqg[0]
            dwq = _mm(st["h1"], dqp, ta=True, out_dtypes=(BF16,), name=f"q_proj_dw_l{l}")
            parts["attn_wq"][bl] = scatter(dwq.reshape(N_DEV, -1, D), f"scatter_attn_wq_l{l}")
            dh1s = [_mm(dqp, gwq, tb=True, bmode="row", layer=bl, out_dtypes=(BF16,), name=f"q_proj_dx_l{l}")]
        dx, sums1 = _norm_mod_bwd(st["x_in"], dh1s, dx, st["ng1"], st["sc1"], f"norm1_bwd_l{l}")
        d_mod[l] = jnp.stack([sums1[0], sums1[1], dg1[0], sums2[0], sums2[1], dg2[0]])
        d_norm_g[l] = jnp.stack([sums1[2], sums2[2]])
        if l == n_a:
            dkvp, dkg = _head_norm_bwd(kv["kvp"], dk_list, _row(k_norm_g), n_heads, "k_norm_bwd", tails=dv_list)
            dfc = [jnp.pad(d.reshape(n_heads, T).T, ((0, 0), (0, BLK - n_heads))) for d in dfk_list]
            dfl, dbf = _fcum_bwd(dfc, kv["fl"], b_f_pad, "fcum_bwd")
            dwkv = _mm(kv["hkv"], dkvp, ta=True, out_mode="col", out_dtypes=(BF16,), name="kv_proj_dw")
            small_parts_wkv = scatter(dwkv, "scatter_w_kv")
            dwf = _mm(kv["hkv"], dfl, ta=True, name="gate_logits_dw")
            dh_a = _mm(dkvp, gwkv, tb=True, bmode="col", layer=0, out_dtypes=(BF16,), name="kv_proj_dx")
            dh_b = _mm(dfl, w_f_pad, tb=True, out_dtypes=(BF16,), name="gate_logits_dx")
            dx, sums_kv = _norm_mod_bwd(kv["x"], [dh_a, dh_b], dx, kv["ng"], kv["sc"], "norm_kv_bwd")
            small.update(d_kv_mod=jnp.stack([sums_kv[0], sums_kv[1]]), d_kv_norm_g=sums_kv[2], d_k_norm_g=dkg[0],
                         d_w_f=dwf[:, :n_heads], d_b_f=dbf[0, :n_heads])

    grad_x = dx.reshape(x.shape)

    contrib = [jnp.stack(d_mod).reshape(depth, N_MOD * D), small["d_kv_mod"].reshape(-1),
               jnp.stack(d_norm_g), jnp.stack(d_ln_g), jnp.stack(d_ln_b), jnp.stack(d_ws), jnp.stack(d_bs),
               small["d_kv_norm_g"], small["d_k_norm_g"], small["d_w_f"], small["d_b_f"], jnp.stack(d_qg)]
    contrib_shapes = [a.shape for a in contrib]
    all_contrib = _all_to_all(_pack(contrib), "gather_small_grads", bcast=True)
    summed = _unpack(_sum_slabs(all_contrib, "sum_small_grads"), contrib_shapes)
    (g_ada_b, g_kv_ada_b, g_norm_g_full, g_ln_g_full, g_ln_b_full, g_ws, g_bs, g_kv_norm_g, g_k_norm_g, g_w_f_full,
     g_b_f, g_q_norm_g) = summed
    dmod_all, dkvmod_all = _unpack(all_contrib, contrib_shapes[:2], lead=(N_DEV,))

    grads = {
        "ada_b": g_ada_b, "kv_ada_b": g_kv_ada_b.reshape(kv_ada_b.shape),
        "norm_g": _shard_of(g_norm_g_full, 2, me, norm_g.shape[2]),
        "gmlp_ln_g": _shard_of(g_ln_g_full, 1, me, gmlp_ln_g.shape[1]),
        "gmlp_ln_b": _shard_of(g_ln_b_full, 1, me, gmlp_ln_b.shape[1]),
        "gmlp_ws": g_ws, "gmlp_bs": g_bs, "kv_norm_g": g_kv_norm_g, "k_norm_g": g_k_norm_g,
        "w_f": _shard_of(g_w_f_full, 0, me, w_f.shape[0]), "b_f": g_b_f, "q_norm_g": g_q_norm_g,
    }
    small_names = list(grads)
    small_w_shapes = [weights[n].shape for n in small_names]
    d_pack, m_pack, v_pack = _adamw_flat(_pack([weights[n] for n in small_names]), _pack([grads[n] for n in small_names]),
                                         _pack([mom_m[n] for n in small_names]), _pack([mom_v[n] for n in small_names]),
                                         "adamw_small")
    deltas = dict(zip(small_names, _unpack(d_pack, small_w_shapes)))
    new_m = dict(zip(small_names, _unpack(m_pack, small_w_shapes)))
    new_v = dict(zip(small_names, _unpack(v_pack, small_w_shapes)))

    for n in ("mlp_w1", "mlp_w2", "gmlp_w_in", "gmlp_w_out", "attn_wq", "attn_wo"):
        grads[n], deltas[n], new_m[n], new_v[n] = _adamw_from_parts(parts[n], weights[n], mom_m[n], mom_v[n], f"adamw_{n}")
    r3 = lambda a: a.reshape((1,) + a.shape)
    res = _adamw_from_parts([small_parts_wkv], r3(w_kv), r3(m_w_kv), r3(v_w_kv), "adamw_w_kv")
    grads["w_kv"], deltas["w_kv"], new_m["w_kv"], new_v["w_kv"] = [a.reshape(w_kv.shape) for a in res]

    sc_t = silu_all.T
    dmod_cols = [_shard_of(dmod_all[:, l], 1, me, mod_cols) for l in range(depth)]
    grads["ada_w"], deltas["ada_w"], new_m["ada_w"], new_v["ada_w"] = _adamw_from_outer(
        sc_t, dmod_cols, ada_w, m_ada_w, v_ada_w, "adamw_ada_w")
    res = _adamw_from_outer(sc_t, [_shard_of(dkvmod_all, 1, me, kv_cols)], r3(kv_ada_w), r3(m_kv_ada_w),
                            r3(v_kv_ada_w), "adamw_kv_ada_w")
    grads["kv_ada_w"], deltas["kv_ada_w"], new_m["kv_ada_w"], new_v["kv_ada_w"] = [a.reshape(kv_ada_w.shape) for a in res]

    return (loss, grad_x, *[grads[n] for n in WEIGHT_NAMES], *[deltas[n] for n in WEIGHT_NAMES],
            *[new_m[n] for n in WEIGHT_NAMES], *[new_v[n] for n in WEIGHT_NAMES])
```

```python
import functools
import math

import jax
import jax.numpy as jnp
from jax import lax
from jax.experimental import pallas as pl
from jax.experimental.pallas import tpu as pltpu

F32 = jnp.float32
BF16 = jnp.bfloat16
N_DEV = 8
EPS = 1e-6
CHUNK = 64
BLK = 128
N_MOD = 6
ADAM_LR = 0.001
ADAM_B1 = 0.9
ADAM_B2 = 0.999
ADAM_EPS = 1e-08
ADAM_WD = 0.01
ADAM_STEP = 10
VMEM_LIMIT_BYTES = 56 * 2 ** 20
NEG_BIG = -1e30
WEIGHT_NAMES = ['ada_w', 'ada_b', 'norm_g', 'mlp_w1', 'mlp_w2', 'gmlp_w_in', 'gmlp_ln_g', 'gmlp_ln_b', 'gmlp_ws',
                'gmlp_bs', 'gmlp_w_out', 'kv_norm_g', 'kv_ada_w', 'kv_ada_b', 'w_kv', 'k_norm_g', 'w_f', 'b_f',
                'attn_wq', 'q_norm_g', 'attn_wo']
MESH = pl.DeviceIdType.MESH


def _params(sem):
    return pltpu.CompilerParams(dimension_semantics=sem, vmem_limit_bytes=VMEM_LIMIT_BYTES)


def _tile(n, cap, unit=128):
    if n <= cap:
        return n
    t = (cap // unit) * unit
    while t > unit and n % t:
        t -= unit
    assert n % t == 0, (n, cap, unit)
    return t


def _my_index():
    return 4 * lax.axis_index("x") + 2 * lax.axis_index("y") + lax.axis_index("c")


def _all_to_all(x, name, bcast=False):
    slab = x.shape if bcast else x.shape[1:]

    def body(x_ref, o_ref, send_sems, recv_sems, local_sem):
        me = _my_index()

        def src(j):
            return x_ref if bcast else x_ref.at[j]

        mine = pltpu.make_async_copy(src(me), o_ref.at[me], local_sem)
        mine.start()
        sends = []
        for d in range(1, N_DEV):
            peer = (me + d) % N_DEV
            cp = pltpu.make_async_remote_copy(
                src_ref=src(peer), dst_ref=o_ref.at[me],
                send_sem=send_sems.at[d - 1], recv_sem=recv_sems.at[d - 1],
                device_id=(peer // 4, (peer // 2) % 2, peer % 2), device_id_type=MESH)
            cp.start()
            sends.append(cp)
        for d in range(1, N_DEV):
            frm = (me + N_DEV - d) % N_DEV
            pltpu.make_async_remote_copy(
                src_ref=src(frm), dst_ref=o_ref.at[frm],
                send_sem=send_sems.at[d - 1], recv_sem=recv_sems.at[d - 1],
                device_id=(frm // 4, (frm // 2) % 2, frm % 2), device_id_type=MESH).wait_recv()
        for cp in sends:
            cp.wait_send()
        mine.wait()

    return pl.pallas_call(
        body, name=name,
        out_shape=jax.ShapeDtypeStruct((N_DEV,) + tuple(slab), x.dtype),
        in_specs=[pl.BlockSpec(memory_space=pl.ANY)],
        out_specs=pl.BlockSpec(memory_space=pl.ANY),
        scratch_shapes=[pltpu.SemaphoreType.DMA((N_DEV - 1,)), pltpu.SemaphoreType.DMA((N_DEV - 1,)),
                        pltpu.SemaphoreType.DMA],
        compiler_params=pltpu.CompilerParams(has_side_effects=True),
    )(x)


_HBM = pl.BlockSpec(memory_space=pltpu.HBM)
_SEM = pl.BlockSpec(memory_space=pltpu.SEMAPHORE)
_ANY = pl.BlockSpec(memory_space=pl.ANY)
_DATAFLOW = pltpu.SideEffectType.DATAFLOW_SIDE_EFFECTING


def _a2a_peer_copy(x_ref, land_ref, send_sems, recv_sems, d, me, incoming):
    peer = (me + N_DEV - d) % N_DEV if incoming else (me + d) % N_DEV
    return pltpu.make_async_remote_copy(
        src_ref=x_ref.at[peer], dst_ref=land_ref.at[peer if incoming else me],
        send_sem=send_sems.at[d - 1], recv_sem=recv_sems.at[d - 1],
        device_id=(peer // 4, (peer // 2) % 2, peer % 2), device_id_type=MESH)


def _a2a_start(x, name):
    def body(x_ref, land_ref, send_sems, recv_sems, x_thru, land_thru, token):
        me = _my_index()
        for d in range(1, N_DEV):
            _a2a_peer_copy(x_ref, land_ref, send_sems, recv_sems, d, me, False).start()
        token[...] = jnp.zeros_like(token)

    return pl.pallas_call(
        body, name=name,
        out_shape=(pltpu.SemaphoreType.DMA((N_DEV - 1,)), pltpu.SemaphoreType.DMA((N_DEV - 1,)),
                   pltpu.HBM(x.shape, x.dtype), pltpu.HBM(x.shape, x.dtype), jax.ShapeDtypeStruct((8, BLK), F32)),
        in_specs=(_HBM, _HBM), out_specs=(_SEM, _SEM, _HBM, _HBM, pl.BlockSpec(memory_space=pltpu.VMEM)),
        input_output_aliases={0: 2, 1: 3},
        compiler_params=pltpu.CompilerParams(has_side_effects=_DATAFLOW),
    )(pltpu.with_memory_space_constraint(x, pltpu.HBM),
      pltpu.with_memory_space_constraint(lax.empty(x.shape, x.dtype), pltpu.HBM))


def _a2a_wait(started, after, name):
    send_sems, recv_sems, x_thru, land_thru, _ = started

    def body(x_ref, land_ref, send_sems, recv_sems, after_ref, x_dead, land_out):
        me = _my_index()
        for d in range(1, N_DEV):
            _a2a_peer_copy(x_ref, land_ref, send_sems, recv_sems, d, me, False).wait_send()
        for d in range(1, N_DEV):
            _a2a_peer_copy(x_ref, land_ref, send_sems, recv_sems, d, me, True).wait_recv()

    return pl.pallas_call(
        body, name=name,
        out_shape=(pltpu.HBM(x_thru.shape, x_thru.dtype), pltpu.HBM(x_thru.shape, x_thru.dtype)),
        in_specs=(_HBM, _HBM, _SEM, _SEM, _ANY), out_specs=(_HBM, _HBM),
        input_output_aliases={0: 0, 1: 1},
        compiler_params=pltpu.CompilerParams(has_side_effects=_DATAFLOW),
    )(x_thru, land_thru, send_sems, recv_sems, after)


def _all_gather(x, name):
    def body(x_ref, o_ref, send_sems, recv_sems, local_sem):
        cx, cy, cc = lax.axis_index("x"), lax.axis_index("y"), lax.axis_index("c")
        me, sibling = (cx, cy, cc), (cx, cy, 1 - cc)
        chips = [(1 - cx, cy), (cx, 1 - cy), (1 - cx, 1 - cy)]

        def slab(px, py, pc):
            return o_ref.at[4 * px + 2 * py + pc]

        def copy(k, block, to, src=None):
            return pltpu.make_async_remote_copy(
                src_ref=slab(*block) if src is None else src, dst_ref=slab(*block),
                send_sem=send_sems.at[k], recv_sem=recv_sems.at[k], device_id=to, device_id_type=MESH)

        mine = pltpu.make_async_copy(x_ref, slab(*me), local_sem)
        mine.start()
        first = [copy(0, me, sibling, src=x_ref)]
        first += [copy(1 + j, me, (*chip, cc), src=x_ref) for j, chip in enumerate(chips)]
        for cp in first:
            cp.start()
        passed = [copy(4 + j, (*chip, cc), sibling) for j, chip in enumerate(chips)]
        for j, chip in enumerate(chips):
            copy(1 + j, (*chip, cc), me).wait_recv()
            passed[j].start()
        copy(0, sibling, me).wait_recv()
        for j, chip in enumerate(chips):
            copy(4 + j, (*chip, 1 - cc), me).wait_recv()
        for cp in first + passed:
            cp.wait_send()
        mine.wait()

    return pl.pallas_call(
        body, name=name,
        out_shape=jax.ShapeDtypeStruct((N_DEV,) + tuple(x.shape), x.dtype),
        in_specs=[pl.BlockSpec(memory_space=pl.ANY)],
        out_specs=pl.BlockSpec(memory_space=pl.ANY),
        scratch_shapes=[pltpu.SemaphoreType.DMA((7,)), pltpu.SemaphoreType.DMA((7,)), pltpu.SemaphoreType.DMA],
        compiler_params=pltpu.CompilerParams(has_side_effects=True),
    )(x)


def _mm(a, b, *, name, ta=False, tb=False, bmode="plain", layer=0, out_mode="plain", out_dtypes=(F32,),
        epilogue=None, extra=(), caps=(1024, 1024, 1024), deps=()):
    if ta:
        K, M = a.shape
    else:
        M, K = a.shape
    n_unit = k_unit = None
    if bmode == "plain":
        N, Kb = (b.shape if tb else b.shape[::-1])
    elif bmode == "col":
        _, _, Kw, Ns = b.shape
        if tb:
            N, Kb, k_unit = Kw, N_DEV * Ns, Ns
        else:
            N, Kb, n_unit = N_DEV * Ns, Kw, Ns
    else:
        _, _, Ks, Nw = b.shape
        if tb:
            N, Kb, n_unit = N_DEV * Ks, Nw, Ks
        else:
            N, Kb, k_unit = Nw, N_DEV * Ks, Ks
    assert K == Kb, (name, a.shape, b.shape)
    if out_mode == "col":
        assert n_unit is None
        n_unit = N // N_DEV
    tm = _tile(M, caps[0])
    tn = _tile(n_unit or N, caps[1])
    tk = _tile(k_unit or K, caps[2])
    nk = K // tk
    npb = (n_unit // tn) if n_unit else None
    kpb = (k_unit // tk) if k_unit else None
    grid = (M // tm, N // tn, nk)

    a_spec = pl.BlockSpec((tk, tm), lambda i, j, k: (k, i)) if ta else pl.BlockSpec((tm, tk), lambda i, j, k: (i, k))
    if bmode == "plain":
        b_spec = (pl.BlockSpec((tn, tk), lambda i, j, k: (j, k)) if tb
                  else pl.BlockSpec((tk, tn), lambda i, j, k: (k, j)))
    elif bmode == "col":
        b_spec = (pl.BlockSpec((None, None, tn, tk), lambda i, j, k: (k // kpb, layer, j, k % kpb)) if tb
                  else pl.BlockSpec((None, None, tk, tn), lambda i, j, k: (j // npb, layer, k, j % npb)))
    else:
        b_spec = (pl.BlockSpec((None, None, tn, tk), lambda i, j, k: (j // npb, layer, j % npb, k)) if tb
                  else pl.BlockSpec((None, None, tk, tn), lambda i, j, k: (k // kpb, layer, k % kpb, j)))
    mn_spec = pl.BlockSpec((tm, tn), lambda i, j, k: (i, j))
    if out_mode == "col":
        o_specs = [pl.BlockSpec((None, tm, tn), lambda i, j, k: (j // npb, i, j % npb))]
        o_shapes = [jax.ShapeDtypeStruct((N_DEV, M, N // N_DEV), out_dtypes[0])]
    else:
        o_specs = [mn_spec for _ in out_dtypes]
        o_shapes = [jax.ShapeDtypeStruct((M, N), dt) for dt in out_dtypes]
    dims = (((0 if ta else 1,), (1 if tb else 0,)), ((), ()))
    n_extra, n_out, n_dep = len(extra), len(out_dtypes), len(deps)

    def body(a_ref, b_ref, *rest):
        extra_refs, out_refs, acc_ref = rest[:n_extra], rest[n_extra + n_dep:n_extra + n_dep + n_out], rest[-1]
        k = pl.program_id(2)

        @pl.when(k == 0)
        def _():
            acc_ref[...] = jnp.zeros_like(acc_ref)

        acc_ref[...] += lax.dot_general(a_ref[...].astype(BF16), b_ref[...].astype(BF16), dims,
                                        preferred_element_type=F32)

        @pl.when(k == nk - 1)
        def _():
            acc = acc_ref[...]
            outs = (acc,) if epilogue is None else epilogue(acc, *[r[...] for r in extra_refs])
            for o_ref, val in zip(out_refs, outs):
                o_ref[...] = val.astype(o_ref.dtype)

    outs = pl.pallas_call(
        body, name=name, grid=grid,
        in_specs=[a_spec, b_spec] + [mn_spec for _ in extra] + [_ANY for _ in deps],
        out_specs=o_specs, out_shape=o_shapes,
        scratch_shapes=[pltpu.VMEM((tm, tn), F32)],
        compiler_params=_params(("parallel", "parallel", "arbitrary")),
    )(a, b, *extra, *deps)
    return outs[0] if n_out == 1 else outs


def _relu2_epilogue(acc):
    r = jnp.maximum(acc, 0.0)
    return acc, r * r


def _relu2_bwd_epilogue(acc, a_pre):
    return (acc * (2.0 * jnp.maximum(a_pre.astype(F32), 0.0)),)


def _rowcall(body, *, name, tr, row_ins, full_ins=(), row_outs=(), acc_outs=(), scratch=(), reverse=False):
    T = row_ins[0].shape[0]
    nb = T // tr
    rmap = (lambda i: (nb - 1 - i, 0)) if reverse else (lambda i: (i, 0))

    def full_spec(shape):
        nd = len(shape)
        return pl.BlockSpec(tuple(shape), lambda i: (0,) * nd)

    in_specs = [pl.BlockSpec((tr, a.shape[1]), rmap) for a in row_ins] + [full_spec(a.shape) for a in full_ins]
    out_specs = [pl.BlockSpec((tr, s.shape[1]), rmap) for s in row_outs] + [full_spec(s.shape) for s in acc_outs]
    outs = pl.pallas_call(
        body, name=name, grid=(nb,), in_specs=in_specs, out_specs=out_specs,
        out_shape=list(row_outs) + list(acc_outs), scratch_shapes=list(scratch),
        compiler_params=_params(("arbitrary",)),
    )(*row_ins, *full_ins)
    return outs


def _sds(shape, dtype):
    return jax.ShapeDtypeStruct(tuple(shape), dtype)


def _row_tile(T, C, elems=512 * 1024):
    t = max(8, min(T, elems // C))
    p = 8
    while p * 2 <= t and T % (p * 2) == 0:
        p *= 2
    return p


def _norm_mod(x, ng, sc, sh, name):
    T, D = x.shape

    def body(x_ref, ng_ref, sc_ref, sh_ref, h_ref):
        xv = x_ref[...]
        r = lax.rsqrt(jnp.mean(xv * xv, axis=-1, keepdims=True) + EPS)
        h_ref[...] = (((xv * r) * ng_ref[...]) * (1.0 + sc_ref[...]) + sh_ref[...]).astype(BF16)

    return _rowcall(body, name=name, tr=_row_tile(T, D), row_ins=[x], full_ins=[ng, sc, sh],
                    row_outs=[_sds((T, D), BF16)])[0]


def _res_norm_mod(x, y, gate, ng, sc, sh, name):
    T, D = x.shape

    def body(x_ref, y_ref, g_ref, ng_ref, sc_ref, sh_ref, x2_ref, h_ref):
        xv = x_ref[...] + g_ref[...] * y_ref[...]
        x2_ref[...] = xv
        r = lax.rsqrt(jnp.mean(xv * xv, axis=-1, keepdims=True) + EPS)
        h_ref[...] = (((xv * r) * ng_ref[...]) * (1.0 + sc_ref[...]) + sh_ref[...]).astype(BF16)

    return _rowcall(body, name=name, tr=_row_tile(T, D, 256 * 1024), row_ins=[x, y], full_ins=[gate, ng, sc, sh],
                    row_outs=[_sds((T, D), F32), _sds((T, D), BF16)])


def _res_add(x, y, gate, name):
    T, D = x.shape

    def body(x_ref, y_ref, g_ref, x2_ref):
        x2_ref[...] = x_ref[...] + g_ref[...] * y_ref[...]

    return _rowcall(body, name=name, tr=_row_tile(T, D), row_ins=[x, y], full_ins=[gate],
                    row_outs=[_sds((T, D), F32)])[0]


def _res_loss(x, y, gate, target, name):
    T, D = x.shape

    def body(x_ref, y_ref, t_ref, g_ref, dy_ref, loss_ref):
        @pl.when(pl.program_id(0) == 0)
        def _():
            loss_ref[...] = jnp.zeros_like(loss_ref)

        diff = x_ref[...] + g_ref[...] * y_ref[...] - t_ref[...]
        dy_ref[...] = diff * (1.0 / D)
        loss_ref[...] += jnp.sum(diff * diff) * (0.5 / D)

    return _rowcall(body, name=name, tr=_row_tile(T, D, 256 * 1024), row_ins=[x, y, target], full_ins=[gate],
                    row_outs=[_sds((T, D), F32)], acc_outs=[_sds((1, BLK), F32)])


def _gate_bwd(dx, y, gate, name):
    T, D = dx.shape

    def body(dx_ref, y_ref, g_ref, dy_ref, dg_ref):
        @pl.when(pl.program_id(0) == 0)
        def _():
            dg_ref[...] = jnp.zeros_like(dg_ref)

        dxv = dx_ref[...]
        dy_ref[...] = (dxv * g_ref[...]).astype(BF16)
        dg_ref[...] += jnp.sum(dxv * y_ref[...], axis=0, keepdims=True)

    return _rowcall(body, name=name, tr=_row_tile(T, D), row_ins=[dx, y], full_ins=[gate],
                    row_outs=[_sds((T, D), BF16)], acc_outs=[_sds((1, D), F32)])


def _norm_mod_bwd(x, dhs, dres, ng, sc, name):
    T, D = x.shape
    n_dh = len(dhs)

    def body(*refs):
        x_ref, dh_refs, dres_ref = refs[0], refs[1:1 + n_dh], refs[1 + n_dh]
        ng_ref, sc_ref, dx_ref, sums_ref = refs[2 + n_dh:]

        @pl.when(pl.program_id(0) == 0)
        def _():
            sums_ref[...] = jnp.zeros_like(sums_ref)

        xv = x_ref[...]
        dh = dh_refs[0][...].astype(F32)
        for r_ in dh_refs[1:]:
            dh = dh + r_[...].astype(F32)
        r = lax.rsqrt(jnp.mean(xv * xv, axis=-1, keepdims=True) + EPS)
        n = xv * r
        ngv, scale1 = ng_ref[...], 1.0 + sc_ref[...]
        dn = dh * (ngv * scale1)
        dx_ref[...] = dres_ref[...] + r * (dn - n * jnp.mean(dn * n, axis=-1, keepdims=True))
        dhn = dh * n
        sums_ref[0:1, :] += jnp.sum(dh, axis=0, keepdims=True)
        sums_ref[1:2, :] += jnp.sum(dhn * ngv, axis=0, keepdims=True)
        sums_ref[2:3, :] += jnp.sum(dhn * scale1, axis=0, keepdims=True)

    return _rowcall(body, name=name, tr=_row_tile(T, D, 256 * 1024), row_ins=[x, *dhs, dres], full_ins=[ng, sc],
                    row_outs=[_sds((T, D), F32)], acc_outs=[_sds((8, D), F32)])


def _head_norm(x, g, n_heads, name, tail=False):
    T = x.shape[0]
    D = n_heads * BLK
    W = x.shape[1] if tail else D

    def body(x_ref, g_ref, o_ref, *tail_ref):
        for h in range(n_heads):
            xv = x_ref[:, h * BLK:(h + 1) * BLK]
            r = lax.rsqrt(jnp.mean(xv * xv, axis=-1, keepdims=True) + EPS)
            o_ref[:, h * BLK:(h + 1) * BLK] = ((xv * r) * g_ref[...]).astype(BF16)
        if tail:
            tail_ref[0][...] = x_ref[:, D:2 * D].astype(BF16)

    tr = _row_tile(T, x.shape[1])
    o_spec = pl.BlockSpec((tr, D), lambda i: (i, 0))
    return pl.pallas_call(
        body, name=name, grid=(T // tr,),
        in_specs=[pl.BlockSpec((tr, W), lambda i: (i, 0)), pl.BlockSpec((1, BLK), lambda i: (0, 0))],
        out_specs=[o_spec] * (2 if tail else 1), out_shape=[_sds((T, D), BF16)] * (2 if tail else 1),
        compiler_params=_params(("parallel",)),
    )(x, g)


def _head_norm_bwd(x, dys, g, n_heads, name, tails=()):
    T = x.shape[0]
    D = n_heads * BLK
    n_dy, n_tail = len(dys), len(tails)
    W = 2 * D if tails else D

    def body(*refs):
        x_ref, dy_refs, tail_refs = refs[0], refs[1:1 + n_dy], refs[1 + n_dy:1 + n_dy + n_tail]
        g_ref, dx_ref, dg_ref = refs[1 + n_dy + n_tail:]

        @pl.when(pl.program_id(0) == 0)
        def _():
            dg_ref[...] = jnp.zeros_like(dg_ref)

        tot = jnp.zeros((1, BLK), F32)
        for h in range(n_heads):
            cols = slice(h * BLK, (h + 1) * BLK)
            xv = x_ref[:, cols]
            dyv = dy_refs[0][:, cols]
            for r_ in dy_refs[1:]:
                dyv = dyv + r_[:, cols]
            r = lax.rsqrt(jnp.mean(xv * xv, axis=-1, keepdims=True) + EPS)
            n = xv * r
            dn = dyv * g_ref[...]
            dx_ref[:, cols] = (r * (dn - n * jnp.mean(dn * n, axis=-1, keepdims=True))).astype(BF16)
            tot = tot + jnp.sum(dyv * n, axis=0, keepdims=True)
        dg_ref[0:1, :] += tot
        if n_tail:
            tv = tail_refs[0][...]
            for r_ in tail_refs[1:]:
                tv = tv + r_[...]
            dx_ref[:, D:] = tv.astype(BF16)

    tr = _row_tile(T, 2 * D, 256 * 1024)
    d_spec = pl.BlockSpec((tr, D), lambda i: (i, 0))
    return pl.pallas_call(
        body, name=name, grid=(T // tr,),
        in_specs=[d_spec] * (1 + n_dy + n_tail) + [pl.BlockSpec((1, BLK), lambda i: (0, 0))],
        out_specs=[pl.BlockSpec((tr, W), lambda i: (i, 0)), pl.BlockSpec((8, BLK), lambda i: (0, 0))],
        out_shape=[_sds((T, W), BF16), _sds((8, BLK), F32)],
        compiler_params=_params(("arbitrary",)),
    )(x, *dys, *tails, g)


def _fcum_fwd(fl, bf, name):
    T = fl.shape[0]

    def body(fl_ref, b_ref, o_ref, carry_ref):
        @pl.when(pl.program_id(0) == 0)
        def _():
            carry_ref[...] = jnp.zeros_like(carry_ref)

        z = fl_ref[...] + b_ref[...]
        logf = jnp.minimum(z, 0.0) - jnp.log(1.0 + jnp.exp(-jnp.abs(z)))
        row = lax.broadcasted_iota(jnp.int32, (BLK, BLK), 0)
        col = lax.broadcasted_iota(jnp.int32, (BLK, BLK), 1)
        tri = (col <= row).astype(F32)
        run = jnp.dot(tri, logf, preferred_element_type=F32, precision=lax.Precision.HIGHEST) + carry_ref[0:1, :]
        o_ref[...] = run
        carry_ref[0:1, :] = run[BLK - 1:BLK, :]

    return _rowcall(body, name=name, tr=BLK, row_ins=[fl], full_ins=[bf], row_outs=[_sds((T, BLK), F32)],
                    scratch=[pltpu.VMEM((8, BLK), F32)])[0]


def _fcum_bwd(dfs, fl, bf, name):
    T = fl.shape[0]
    n_df = len(dfs)

    def body(*refs):
        df_refs = refs[:n_df]
        fl_ref, b_ref, dfl_ref, dbias_ref, carry_ref = refs[n_df:]

        @pl.when(pl.program_id(0) == 0)
        def _():
            carry_ref[...] = jnp.zeros_like(carry_ref)
            dbias_ref[...] = jnp.zeros_like(dbias_ref)

        dfc = df_refs[0][...]
        for r_ in df_refs[1:]:
            dfc = dfc + r_[...]
        row = lax.broadcasted_iota(jnp.int32, (BLK, BLK), 0)
        col = lax.broadcasted_iota(jnp.int32, (BLK, BLK), 1)
        tri = (col >= row).astype(F32)
        suffix = jnp.dot(tri, dfc, preferred_element_type=F32, precision=lax.Precision.HIGHEST) + carry_ref[0:1, :]
        carry_ref[0:1, :] = suffix[0:1, :]
        z = fl_ref[...] + b_ref[...]
        dfl = suffix / (1.0 + jnp.exp(z))
        dfl_ref[...] = dfl.astype(BF16)
        dbias_ref[0:1, :] += jnp.sum(dfl, axis=0, keepdims=True)

    return _rowcall(body, name=name, tr=BLK, row_ins=[*dfs, fl], full_ins=[bf], reverse=True,
                    row_outs=[_sds((T, BLK), BF16)], acc_outs=[_sds((8, BLK), F32)],
                    scratch=[pltpu.VMEM((8, BLK), F32)])


def _attn_tile(T):
    return min(T, 512)


def _lane_pick(block, h):
    lane = lax.broadcasted_iota(jnp.int32, block.shape, 1)
    return jnp.sum(jnp.where(lane == h, block, 0.0), axis=1, keepdims=True)


def _attn_fwd(q, k, v, fq, fk, n_heads, name):
    T = q.shape[0]
    tq = tk = _attn_tile(T)
    nkb = T // tk
    inv_sqrt = 1.0 / float(math.sqrt(BLK))

    def body(q_ref, k_ref, v_ref, fq_ref, fk_ref, o_ref, o32_ref, lse_ref):
        h, i = pl.program_id(0), pl.program_id(1)
        qv = q_ref[...]
        fqv = _lane_pick(fq_ref[...], h)
        qpos = i * tq + lax.broadcasted_iota(jnp.int32, (tq, tk), 0)
        kloc = lax.broadcasted_iota(jnp.int32, (tq, tk), 1)

        def step(j, carry):
            m, l, acc = carry
            rows = pl.ds(pl.multiple_of(j * tk, tk), tk)
            kj, vj = k_ref[rows, :], v_ref[rows, :]
            s = lax.dot_general(qv, kj, (((1,), (1,)), ((), ())), preferred_element_type=F32) * inv_sqrt
            s = s + (fqv - fk_ref[j])
            s = jnp.where(j * tk + kloc <= qpos, s, NEG_BIG)
            m_new = jnp.maximum(m, jnp.max(s, axis=-1, keepdims=True))
            alpha = jnp.exp(m - m_new)
            p = jnp.exp(s - m_new)
            l = alpha * l + jnp.sum(p, axis=-1, keepdims=True)
            acc = alpha * acc + jnp.dot(p.astype(BF16), vj, preferred_element_type=F32)
            return m_new, l, acc

        init = (jnp.full((tq, 1), NEG_BIG, F32), jnp.zeros((tq, 1), F32), jnp.zeros((tq, BLK), F32))
        m, l, acc = lax.fori_loop(0, i + 1, step, init)
        out = acc / l
        o_ref[...] = out.astype(BF16)
        o32_ref[...] = out
        lse_ref[...] = m + jnp.log(l)

    return pl.pallas_call(
        body, name=name, grid=(n_heads, T // tq),
        in_specs=[pl.BlockSpec((tq, BLK), lambda h, i: (i, h)),
                  pl.BlockSpec((T, BLK), lambda h, i: (0, h)),
                  pl.BlockSpec((T, BLK), lambda h, i: (0, h)),
                  pl.BlockSpec((tq, BLK), lambda h, i: (i, 0)),
                  pl.BlockSpec((None, nkb, 1, tk), lambda h, i: (h, 0, 0, 0))],
        out_specs=[pl.BlockSpec((tq, BLK), lambda h, i: (i, h)),
                   pl.BlockSpec((tq, BLK), lambda h, i: (i, h)),
                   pl.BlockSpec((None, tq, 1), lambda h, i: (h, i, 0))],
        out_shape=[_sds((T, n_heads * BLK), BF16), _sds((T, n_heads * BLK), F32), _sds((n_heads, T, 1), F32)],
        compiler_params=_params(("parallel", "arbitrary")),
    )(q, k, v, fq, fk)


def _attn_bwd(q, k, v, o, do, lse, fq, fk, n_heads, name):
    T = q.shape[0]
    tq = tk = _attn_tile(T)
    nkb = T // tk
    nq = T // tq
    inv_sqrt = 1.0 / float(math.sqrt(BLK))
    tn_dims = (((0,), (0,)), ((), ()))
    nt_dims = (((1,), (1,)), ((), ()))

    def body(q_ref, k_ref, v_ref, o_ref, do_ref, lse_ref, fq_ref, fk_ref, dq_ref, dk_ref, dv_ref, dfq_ref, dfk_ref,
             delta_ref):
        h, j = pl.program_id(0), pl.program_id(1)

        @pl.when(j == 0)
        def _():
            delta_ref[...] = jnp.sum(do_ref[...].astype(F32) * o_ref[...], axis=1, keepdims=True)
            dq_ref[...] = jnp.zeros_like(dq_ref)
            dfq_ref[...] = jnp.zeros_like(dfq_ref)

        kj, vj, fkv = k_ref[...], v_ref[...], fk_ref[...]
        kpos = j * tk + lax.broadcasted_iota(jnp.int32, (tq, tk), 1)
        qloc = lax.broadcasted_iota(jnp.int32, (tq, tk), 0)

        def step(i, carry):
            dk, dv, dfk = carry
            rows = pl.ds(pl.multiple_of(i * tq, tq), tq)
            qi, doi = q_ref[rows, :], do_ref[rows, :]
            fqv = _lane_pick(fq_ref[rows, :], h)
            s = lax.dot_general(qi, kj, nt_dims, preferred_element_type=F32) * inv_sqrt + (fqv - fkv)
            s = jnp.where(kpos <= i * tq + qloc, s, NEG_BIG)
            p = jnp.exp(s - lse_ref[rows, :])
            dv = dv + lax.dot_general(p.astype(BF16), doi, tn_dims, preferred_element_type=F32)
            dp = lax.dot_general(doi, vj, nt_dims, preferred_element_type=F32)
            ds = p * (dp - delta_ref[rows, :])
            dsb = ds.astype(BF16)
            dq_ref[rows, :] += jnp.dot(dsb, kj, preferred_element_type=F32) * inv_sqrt
            dk = dk + lax.dot_general(dsb, qi, tn_dims, preferred_element_type=F32)
            dfq_ref[rows, :] += jnp.sum(ds, axis=1, keepdims=True)
            dfk = dfk - jnp.sum(ds, axis=0, keepdims=True)
            return dk, dv, dfk

        init = (jnp.zeros((tk, BLK), F32), jnp.zeros((tk, BLK), F32), jnp.zeros((1, tk), F32))
        dk, dv, dfk = lax.fori_loop(j, nq, step, init)
        dk_ref[...] = dk * inv_sqrt
        dv_ref[...] = dv
        dfk_ref[...] = dfk

    head_col = lambda h, j: (0, h)
    return pl.pallas_call(
        body, name=name, grid=(n_heads, nkb),
        in_specs=[pl.BlockSpec((T, BLK), head_col),
                  pl.BlockSpec((tk, BLK), lambda h, j: (j, h)),
                  pl.BlockSpec((tk, BLK), lambda h, j: (j, h)),
                  pl.BlockSpec((T, BLK), head_col),
                  pl.BlockSpec((T, BLK), head_col),
                  pl.BlockSpec((None, T, 1), lambda h, j: (h, 0, 0)),
                  pl.BlockSpec((T, BLK), lambda h, j: (0, 0)),
                  pl.BlockSpec((None, None, 1, tk), lambda h, j: (h, j, 0, 0))],
        out_specs=[pl.BlockSpec((T, BLK), head_col),
                   pl.BlockSpec((tk, BLK), lambda h, j: (j, h)),
                   pl.BlockSpec((tk, BLK), lambda h, j: (j, h)),
                   pl.BlockSpec((None, T, 1), lambda h, j: (h, 0, 0)),
                   pl.BlockSpec((None, None, 1, tk), lambda h, j: (h, j, 0, 0))],
        out_shape=[_sds((T, n_heads * BLK), F32), _sds((T, n_heads * BLK), F32), _sds((T, n_heads * BLK), F32),
                   _sds((n_heads, T, 1), F32), _sds((n_heads, nkb, 1, tk), F32)],
        scratch_shapes=[pltpu.VMEM((T, 1), F32)],
        compiler_params=_params(("parallel", "arbitrary")),
    )(q, k, v, o, do, lse, fq, fk)


_INV_SQRT2 = 1.0 / math.sqrt(2.0)
_INV_SQRT_2PI = 1.0 / math.sqrt(2.0 * math.pi)


def _gelu_parts(z):
    cdf = 0.5 * (1.0 + lax.erf(z * _INV_SQRT2))
    return cdf, z * cdf


def _mix_mask(transposed):
    row = lax.broadcasted_iota(jnp.int32, (BLK, BLK), 0) // CHUNK
    col = lax.broadcasted_iota(jnp.int32, (BLK, BLK), 1) // CHUNK
    return (row <= col) if transposed else (col <= row)


def _gmlp_mid_fwd(zpre, ln_g, ln_b, ws, bs_t, name):
    T, two_h = zpre.shape
    Hh = two_h // 2
    G = ws.shape[0]
    gd = Hh // G

    def body(z_ref, lg_ref, lb_ref, ws_ref, bs_ref, p_ref):
        _, zg = _gelu_parts(z_ref[...].astype(F32))
        u, v = zg[:, :Hh], zg[:, Hh:]
        mu = jnp.mean(v, axis=-1, keepdims=True)
        vc = v - mu
        rstd = lax.rsqrt(jnp.mean(vc * vc, axis=-1, keepdims=True) + EPS)
        vn = ((vc * rstd) * lg_ref[...] + lb_ref[...]).astype(BF16)
        mask = _mix_mask(False)
        for g in range(G):
            wm = jnp.where(mask, ws_ref[g], 0.0).astype(BF16)
            sv = jnp.dot(wm, vn[:, g * gd:(g + 1) * gd], preferred_element_type=F32) + bs_ref[:, g:g + 1]
            p_ref[:, g * gd:(g + 1) * gd] = (u[:, g * gd:(g + 1) * gd] * sv).astype(BF16)

    return _rowcall(body, name=name, tr=BLK, row_ins=[zpre], full_ins=[ln_g, ln_b, ws, bs_t],
                    row_outs=[_sds((T, Hh), BF16)])[0]


def _gmlp_mid_bwd(zpre, dp, ln_g, ln_b, ws, ws_t, bs_t, name):
    T, two_h = zpre.shape
    Hh = two_h // 2
    G = ws.shape[0]
    gd = Hh // G
    nt_dims = (((1,), (1,)), ((), ()))

    def body(z_ref, dp_ref, lg_ref, lb_ref, ws_ref, wst_ref, bs_ref, dz_ref, dws_ref, dbs_ref, dlg_ref, dlb_ref,
             dvn_ref):
        @pl.when(pl.program_id(0) == 0)
        def _():
            dws_ref[...] = jnp.zeros_like(dws_ref)
            dbs_ref[...] = jnp.zeros_like(dbs_ref)
            dlg_ref[...] = jnp.zeros_like(dlg_ref)
            dlb_ref[...] = jnp.zeros_like(dlb_ref)

        z = z_ref[...].astype(F32)
        cdf, zg = _gelu_parts(z)
        dgelu = cdf + z * (jnp.exp(-0.5 * z * z) * _INV_SQRT_2PI)
        u, v = zg[:, :Hh], zg[:, Hh:]
        mu = jnp.mean(v, axis=-1, keepdims=True)
        vc = v - mu
        rstd = lax.rsqrt(jnp.mean(vc * vc, axis=-1, keepdims=True) + EPS)
        vhat = vc * rstd
        vn = (vhat * lg_ref[...] + lb_ref[...]).astype(BF16)
        mask, mask_t = _mix_mask(False), _mix_mask(True)
        lane = lax.broadcasted_iota(jnp.int32, (BLK, BLK), 1)
        dbs = jnp.zeros((BLK, BLK), F32)
        for g in range(G):
            cols = slice(g * gd, (g + 1) * gd)
            wm = jnp.where(mask, ws_ref[g], 0.0).astype(BF16)
            wm_t = jnp.where(mask_t, wst_ref[g], 0.0).astype(BF16)
            vn_g = vn[:, cols]
            sv = jnp.dot(wm, vn_g, preferred_element_type=F32) + bs_ref[:, g:g + 1]
            dp_g = dp_ref[:, cols].astype(F32)
            dz_ref[:, cols] = ((dp_g * sv) * dgelu[:, cols]).astype(BF16)
            dsv = dp_g * u[:, cols]
            dsv_b = dsv.astype(BF16)
            dbs = dbs + jnp.where(lane == g, jnp.sum(dsv, axis=1, keepdims=True), 0.0)
            dws_ref[g] += jnp.where(mask, lax.dot_general(dsv_b, vn_g, nt_dims, preferred_element_type=F32), 0.0)
            dvn_ref[:, cols] = jnp.dot(wm_t, dsv_b, preferred_element_type=F32)
        dbs_ref[...] += dbs
        dvn = dvn_ref[...]
        dlg_ref[0:1, :] += jnp.sum(dvn * vhat, axis=0, keepdims=True)
        dlb_ref[0:1, :] += jnp.sum(dvn, axis=0, keepdims=True)
        dvh = dvn * lg_ref[...]
        dv = rstd * (dvh - jnp.mean(dvh, axis=-1, keepdims=True) - vhat * jnp.mean(dvh * vhat, axis=-1, keepdims=True))
        dz_ref[:, Hh:] = (dv * dgelu[:, Hh:]).astype(BF16)

    return _rowcall(body, name=name, tr=BLK, row_ins=[zpre, dp], full_ins=[ln_g, ln_b, ws, ws_t, bs_t],
                    row_outs=[_sds((T, two_h), BF16)],
                    acc_outs=[_sds((G, BLK, BLK), F32), _sds((BLK, BLK), F32), _sds((8, Hh), F32), _sds((8, Hh), F32)],
                    scratch=[pltpu.VMEM((BLK, Hh), F32)])


def _mods(c_all, w, layer, bias, name):
    nb, K = c_all.shape
    N = w.shape[-1]
    tn = _tile(N, 512)

    def body(c_ref, w_ref, b_ref, o_ref):
        cv = c_ref[...]
        sc = cv / (1.0 + jnp.exp(-cv))
        o_ref[...] = jnp.dot(sc, w_ref[...], preferred_element_type=F32, precision=lax.Precision.HIGHEST) + b_ref[...]

    return pl.pallas_call(
        body, name=name, grid=(N // tn,),
        in_specs=[pl.BlockSpec((nb, K), lambda j: (0, 0)),
                  pl.BlockSpec((None, K, tn), lambda j: (layer, 0, j)),
                  pl.BlockSpec((1, tn), lambda j: (0, j))],
        out_specs=pl.BlockSpec((nb, tn), lambda j: (0, j)), out_shape=_sds((nb, N), F32),
        compiler_params=_params(("parallel",)),
    )(c_all, w, bias)


def _sum_slabs(x, name):
    _, R, C = x.shape
    tr = _row_tile(R, C * N_DEV)

    def body(x_ref, o_ref):
        acc = x_ref[0]
        for s in range(1, N_DEV):
            acc = acc + x_ref[s]
        o_ref[...] = acc

    return pl.pallas_call(
        body, name=name, grid=(R // tr,),
        in_specs=[pl.BlockSpec((N_DEV, tr, C), lambda i: (0, i, 0))],
        out_specs=pl.BlockSpec((tr, C), lambda i: (i, 0)), out_shape=_sds((R, C), F32),
        compiler_params=_params(("parallel",)),
    )(x)


def _adamw_math(w, g, m, v):
    m = ADAM_B1 * m + (1.0 - ADAM_B1) * g
    v = ADAM_B2 * v + (1.0 - ADAM_B2) * (g * g)
    m_hat = m / (1.0 - ADAM_B1 ** ADAM_STEP)
    v_hat = v / (1.0 - ADAM_B2 ** ADAM_STEP)
    delta = -ADAM_LR * (m_hat / (jnp.sqrt(v_hat) + ADAM_EPS) + ADAM_WD * w)
    return delta, m, v


class _AdamStack:
    def __init__(self, w, m, v, name):
        self.w, self.m, self.v, self.name = w, m, v, name
        self.L, self.R, self.C = w.shape
        self.tr = _row_tile(self.R, self.C, 128 * 1024)
        self.outs = None

    def _layer(self, l, who, srcs, src_specs, make_grad):
        n_src = len(srcs)
        L, R, C, tr = self.L, self.R, self.C, self.tr
        wspec = pl.BlockSpec((None, tr, C), lambda i, who_ref: (l, i, 0))

        def body(who_ref, *refs):
            src_refs = refs[:n_src]
            w_ref, m_ref, v_ref = refs[n_src:n_src + 3]
            g_ref, d_ref, m2_ref, v2_ref = refs[-4:]
            g = make_grad(who_ref[0], *src_refs)
            delta, m2, v2 = _adamw_math(w_ref[...], g, m_ref[...], v_ref[...])
            g_ref[...] = g
            d_ref[...] = delta
            m2_ref[...] = m2
            v2_ref[...] = v2

        prev = [] if self.outs is None else list(self.outs)
        aliases = {} if self.outs is None else {1 + n_src + 3 + t: t for t in range(4)}
        self.outs = pl.pallas_call(
            body, name=f"{self.name}_l{l}",
            grid_spec=pltpu.PrefetchScalarGridSpec(
                num_scalar_prefetch=1, grid=(R // tr,),
                in_specs=list(src_specs) + [wspec] * 3 + [_ANY] * len(prev), out_specs=[wspec] * 4),
            out_shape=[_sds((L, R, C), F32)] * 4,
            input_output_aliases=aliases,
            compiler_params=_params(("parallel",)),
        )(who, *srcs, self.w, self.m, self.v, *prev)

    def from_parts(self, l, who, landed, sent):
        tr, C = self.tr, self.C

        def make_grad(me, p_ref, own_ref):
            mine = own_ref[...].astype(F32)
            g = jnp.where(me == 0, mine, p_ref[0].astype(F32))
            for s in range(1, N_DEV):
                g = g + jnp.where(me == s, mine, p_ref[s].astype(F32))
            return g

        self._layer(l, who, [landed, sent],
                    [pl.BlockSpec((N_DEV, tr, C), lambda i, who_ref: (0, i, 0)),
                     pl.BlockSpec((None, tr, C), lambda i, who_ref: (who_ref[0], i, 0))], make_grad)

    def from_outer(self, l, who, sc_t, dmod):
        tr, C = self.tr, self.C

        def make_grad(me, s_ref, d_ref):
            g = s_ref[:, 0:1] * d_ref[0:1, :]
            for b in range(1, N_DEV):
                g = g + s_ref[:, b:b + 1] * d_ref[b:b + 1, :]
            return g

        self._layer(l, who, [sc_t, dmod], [pl.BlockSpec((tr, N_DEV), lambda i, who_ref: (i, 0)),
                                           pl.BlockSpec((N_DEV, C), lambda i, who_ref: (0, 0))], make_grad)


def _adamw_flat(w, g, m, v, name):
    R, C = w.shape
    tr = _row_tile(R, C, 128 * 1024)

    def body(w_ref, g_ref, m_ref, v_ref, d_ref, m2_ref, v2_ref):
        delta, m2, v2 = _adamw_math(w_ref[...], g_ref[...], m_ref[...], v_ref[...])
        d_ref[...] = delta
        m2_ref[...] = m2
        v2_ref[...] = v2

    spec = pl.BlockSpec((tr, C), lambda i: (i, 0))
    return pl.pallas_call(
        body, name=name, grid=(R // tr,), in_specs=[spec] * 4, out_specs=[spec] * 3,
        out_shape=[_sds((R, C), F32)] * 3, compiler_params=_params(("parallel",)),
    )(w, g, m, v)


def _pack(arrays):
    flat = jnp.concatenate([a.reshape(-1).astype(F32) for a in arrays])
    pad = (-flat.shape[0]) % (8 * BLK)
    if pad:
        flat = jnp.concatenate([flat, jnp.zeros((pad,), F32)])
    return flat.reshape(-1, BLK)


def _unpack(buf, shapes, lead=()):
    sizes = [int(math.prod(s)) for s in shapes]
    out, off = [], 0
    if all(n % BLK == 0 for n in sizes):
        for s, n in zip(shapes, sizes):
            out.append(buf[..., off // BLK:(off + n) // BLK, :].reshape(tuple(lead) + tuple(s)))
            off += n
        return out
    flat = buf.reshape(tuple(lead) + (-1,))
    for s, n in zip(shapes, sizes):
        out.append(flat[..., off:off + n].reshape(tuple(lead) + tuple(s)))
        off += n
    return out


def _row(vec):
    return vec.reshape(1, -1)


def _shard_of(full, axis, me, size):
    return lax.dynamic_slice_in_dim(full, me * size, size, axis=axis)


def kernel(x, c, ada_w, ada_b, norm_g, mlp_w1, mlp_w2, gmlp_w_in, gmlp_ln_g, gmlp_ln_b, gmlp_ws, gmlp_bs, gmlp_w_out, kv_norm_g, kv_ada_w, kv_ada_b, w_kv, k_norm_g, w_f, b_f, attn_wq, q_norm_g, attn_wo, loss_target, m_ada_w, m_ada_b, m_norm_g, m_mlp_w1, m_mlp_w2, m_gmlp_w_in, m_gmlp_ln_g, m_gmlp_ln_b, m_gmlp_ws, m_gmlp_bs, m_gmlp_w_out, m_kv_norm_g, m_kv_ada_w, m_kv_ada_b, m_w_kv, m_k_norm_g, m_w_f, m_b_f, m_attn_wq, m_q_norm_g, m_attn_wo, v_ada_w, v_ada_b, v_norm_g, v_mlp_w1, v_mlp_w2, v_gmlp_w_in, v_gmlp_ln_g, v_gmlp_ln_b, v_gmlp_ws, v_gmlp_bs, v_gmlp_w_out, v_kv_norm_g, v_kv_ada_w, v_kv_ada_b, v_w_kv, v_k_norm_g, v_w_f, v_b_f, v_attn_wq, v_q_norm_g, v_attn_wo):
    given = dict(locals())
    weights = {n: given[n] for n in WEIGHT_NAMES}
    mom_m = {n: given["m_" + n] for n in WEIGHT_NAMES}
    mom_v = {n: given["v_" + n] for n in WEIGHT_NAMES}

    me = _my_index()
    T, D = x.shape[1], x.shape[2]
    depth = ada_w.shape[0]
    n_a = gmlp_w_in.shape[0]
    n_heads = b_f.shape[0]
    G = gmlp_ws.shape[1]
    Hh = gmlp_ln_g.shape[1] * N_DEV
    mod_cols = ada_w.shape[2]
    kv_cols = kv_ada_w.shape[1]
    x0 = x.reshape(T, D)
    target = loss_target.reshape(T, D)

    small_in = [c, norm_g, gmlp_ln_g, gmlp_ln_b, w_f]
    small_shapes = [a.shape for a in small_in]
    got = _all_to_all(_pack(small_in), "gather_small_inputs", bcast=True)
    c_all, norm_g_sh, ln_g_sh, ln_b_sh, w_f_sh = _unpack(got, small_shapes, lead=(N_DEV,))
    c_all = c_all.reshape(N_DEV, D)
    norm_g_full = jnp.moveaxis(norm_g_sh, 0, 2).reshape(depth, 2, D)
    ln_g_full = jnp.moveaxis(ln_g_sh, 0, 1).reshape(n_a, Hh)
    ln_b_full = jnp.moveaxis(ln_b_sh, 0, 1).reshape(n_a, Hh)
    w_f_full = w_f_sh.reshape(D, n_heads)
    w_f_pad = jnp.pad(w_f_full, ((0, 0), (0, BLK - n_heads))).astype(BF16)
    b_f_pad = jnp.pad(b_f, (0, BLK - n_heads)).reshape(1, BLK)

    mod_parts = []
    for l in range(depth):
        bias = _shard_of(ada_b[l], 0, me, mod_cols).reshape(1, mod_cols)
        mod_parts.append(_mods(c_all, ada_w, l, bias, f"mods_l{l}"))
    kv_bias = _shard_of(kv_ada_b, 0, me, kv_cols).reshape(1, kv_cols)
    mod_parts.append(_mods(c_all, kv_ada_w.reshape(1, D, kv_cols), 0, kv_bias, "mods_kv"))
    mods_mine = jnp.concatenate(mod_parts, axis=1)
    mod_width = mods_mine.shape[1]
    mods_pack = jnp.pad(mods_mine, ((0, 0), (0, (-mod_width) % (8 * BLK)))).reshape(N_DEV, -1, BLK)
    mods_got = _all_to_all(mods_pack, "exchange_mods").reshape(N_DEV, -1)[:, :mod_width]
    mods = []
    for l in range(depth):
        mods.append(mods_got[:, l * mod_cols:(l + 1) * mod_cols].reshape(N_MOD, D))
    kv_mod = mods_got[:, depth * mod_cols:].reshape(2, D)
    silu_all = c_all / (1.0 + jnp.exp(-c_all))

    gw1 = _all_gather(mlp_w1.astype(BF16), "gather_mlp_w1")
    gw2 = _all_gather(mlp_w2.astype(BF16), "gather_mlp_w2")
    gwin = _all_gather(gmlp_w_in.astype(BF16), "gather_gmlp_w_in")
    gwout = _all_gather(gmlp_w_out.astype(BF16), "gather_gmlp_w_out")
    gwkv = _all_gather(w_kv.astype(BF16).reshape((1,) + w_kv.shape), "gather_w_kv")
    gwq = _all_gather(attn_wq.astype(BF16), "gather_attn_wq")
    gwo = _all_gather(attn_wo.astype(BF16), "gather_attn_wo")

    tkk = _attn_tile(T)

    saved = []
    xs = x0
    pending = None
    kv = None
    for l in range(depth):
        sh1, sc1, g1, sh2, sc2, g2 = [_row(mods[l][t]) for t in range(N_MOD)]
        ng1, ng2 = _row(norm_g_full[l, 0]), _row(norm_g_full[l, 1])
        st = dict(sc1=sc1, g1=g1, sc2=sc2, g2=g2, ng1=ng1, ng2=ng2)
        if pending is None:
            h1 = _norm_mod(xs, ng1, sc1, sh1, f"norm1_l{l}")
        else:
            xs, h1 = _res_norm_mod(xs, pending[0], pending[1], ng1, sc1, sh1, f"res_norm1_l{l}")
        st["x_in"], st["h1"] = xs, h1
        if l < n_a:
            a = l
            zpre = _mm(h1, gwin, bmode="col", layer=a, out_dtypes=(BF16,), name=f"gmlp_in_l{l}")
            bs_t = gmlp_bs[a].T
            p = _gmlp_mid_fwd(zpre, _row(ln_g_full[a]), _row(ln_b_full[a]), gmlp_ws[a], bs_t, f"gmlp_mid_l{l}")
            y = _mm(p, gwout, bmode="row", layer=a, name=f"gmlp_out_l{l}")
            st.update(zpre=zpre, p=p)
        else:
            if kv is None:
                kv_ng, kv_sh, kv_sc = _row(kv_norm_g), _row(kv_mod[0]), _row(kv_mod[1])
                hkv = _norm_mod(xs, kv_ng, kv_sc, kv_sh, "norm_kv")
                kvp = _mm(hkv, gwkv, bmode="col", layer=0, name="kv_proj")
                kk, vv = _head_norm(kvp, _row(k_norm_g), n_heads, "k_norm", tail=True)
                fl = _mm(hkv, w_f_pad, name="gate_logits")
                fcum = _fcum_fwd(fl, b_f_pad, "fcum")
                fk = fcum[:, :n_heads].T.reshape(n_heads, T // tkk, 1, tkk)
                kv = dict(x=xs, hkv=hkv, kvp=kvp, k=kk, v=vv, fl=fl, fcum=fcum, fk=fk, ng=kv_ng, sc=kv_sc)
            bl = l - n_a
            qp = _mm(h1, gwq, bmode="row", layer=bl, name=f"q_proj_l{l}")
            q = _head_norm(qp, _row(q_norm_g[bl]), n_heads, f"q_norm_l{l}")[0]
            o, o32, lse = _attn_fwd(q, kv["k"], kv["v"], kv["fcum"], kv["fk"], n_heads, f"attn_l{l}")
            y = _mm(o, gwo, bmode="row", layer=bl, name=f"attn_out_l{l}")
            st.update(qp=qp, q=q, o=o, o32=o32, lse=lse)
        xs, h2 = _res_norm_mod(xs, y, g1, ng2, sc2, sh2, f"res_norm2_l{l}")
        a_pre, s_act = _mm(h2, gw1, bmode="col", layer=l, out_dtypes=(BF16, BF16), epilogue=_relu2_epilogue,
                           name=f"mlp_up_l{l}")
        mo = _mm(s_act, gw2, bmode="row", layer=l, name=f"mlp_down_l{l}")
        st.update(y=y, x_mid=xs, h2=h2, a_pre=a_pre, s=s_act, m=mo)
        saved.append(st)
        pending = (mo, g2)

    dx, loss_row = _res_loss(xs, pending[0], pending[1], target, "loss")
    loss = lax.psum(loss_row[0, 0], ("x", "y", "c"))

    started = {}
    tokens = []

    def scatter(dw_slabs, key, idx):
        started[(key, idx)] = _a2a_start(dw_slabs, f"scatter_{key}_l{idx}_start")
        tokens.append(started[(key, idx)][4])

    def behind_starts():
        out = tuple(tokens)
        tokens.clear()
        return out

    d_mod = [None] * depth
    d_norm_g = [None] * depth
    d_ln_g, d_ln_b, d_ws, d_bs = [None] * n_a, [None] * n_a, [None] * n_a, [None] * n_a
    d_qg = [None] * (depth - n_a)
    dk_list, dv_list, dfk_list = [], [], []
    small = {}

    for l in reversed(range(depth)):
        st = saved[l]
        dm, dg2 = _gate_bwd(dx, st["m"], st["g2"], f"gate2_bwd_l{l}")
        da = _mm(dm, gw2, tb=True, bmode="row", layer=l, out_dtypes=(BF16,), epilogue=_relu2_bwd_epilogue,
                 extra=(st["a_pre"],), name=f"mlp_down_dx_l{l}")
        dw2 = _mm(st["s"], dm, ta=True, out_dtypes=(BF16,), name=f"mlp_down_dw_l{l}", deps=behind_starts())
        scatter(dw2.reshape(N_DEV, -1, D), "mlp_w2", l)
        dw1 = _mm(st["h2"], da, ta=True, out_mode="col", out_dtypes=(BF16,), name=f"mlp_up_dw_l{l}",
                  deps=behind_starts())
        scatter(dw1, "mlp_w1", l)
        dh2 = _mm(da, gw1, tb=True, bmode="col", layer=l, out_dtypes=(BF16,), name=f"mlp_up_dx_l{l}",
                  deps=behind_starts())
        dx, sums2 = _norm_mod_bwd(st["x_mid"], [dh2], dx, st["ng2"], st["sc2"], f"norm2_bwd_l{l}")
        dy, dg1 = _gate_bwd(dx, st["y"], st["g1"], f"gate1_bwd_l{l}")
        if l < n_a:
            a = l
            dwo = _mm(st["p"], dy, ta=True, out_dtypes=(BF16,), name=f"gmlp_out_dw_l{l}", deps=behind_starts())
            scatter(dwo.reshape(N_DEV, -1, D), "gmlp_w_out", a)
            dp = _mm(dy, gwout, tb=True, bmode="row", layer=a, out_dtypes=(BF16,), name=f"gmlp_out_dx_l{l}",
                     deps=behind_starts())
            dz, d_ws[a], dbs_t, dlg, dlb = _gmlp_mid_bwd(
                st["zpre"], dp, _row(ln_g_full[a]), _row(ln_b_full[a]), gmlp_ws[a],
                jnp.swapaxes(gmlp_ws[a], 1, 2), gmlp_bs[a].T, f"gmlp_mid_bwd_l{l}")
            d_bs[a], d_ln_g[a], d_ln_b[a] = dbs_t[:, :G].T, dlg[0], dlb[0]
            dwi = _mm(st["h1"], dz, ta=True, out_mode="col", out_dtypes=(BF16,), name=f"gmlp_in_dw_l{l}",
                      deps=behind_starts())
            scatter(dwi, "gmlp_w_in", a)
            dh1s = [_mm(dz, gwin, tb=True, bmode="col", layer=a, out_dtypes=(BF16,), name=f"gmlp_in_dx_l{l}",
                        deps=behind_starts())]
        else:
            bl = l - n_a
            dwo = _mm(st["o"], dy, ta=True, out_dtypes=(BF16,), name=f"attn_out_dw_l{l}", deps=behind_starts())
            scatter(dwo.reshape(N_DEV, -1, D), "attn_wo", bl)
            do = _mm(dy, gwo, tb=True, bmode="row", layer=bl, out_dtypes=(BF16,), name=f"attn_out_dx_l{l}",
                     deps=behind_starts())
            dq, dk, dv, dfq, dfk = _attn_bwd(st["q"], kv["k"], kv["v"], st["o32"], do, st["lse"], kv["fcum"],
                                             kv["fk"], n_heads, f"attn_bwd_l{l}")
            dfk_list += [dfq, dfk]
            dk_list.append(dk)
            dv_list.append(dv)
            dqp, dqg = _head_norm_bwd(st["qp"], [dq], _row(q_norm_g[bl]), n_heads, f"q_norm_bwd_l{l}")
            d_qg[bl] = dqg[0]
            dwq = _mm(st["h1"], dqp, ta=True, out_dtypes=(BF16,), name=f"q_proj_dw_l{l}", deps=behind_starts())
            scatter(dwq.reshape(N_DEV, -1, D), "attn_wq", bl)
            dh1s = [_mm(dqp, gwq, tb=True, bmode="row", layer=bl, out_dtypes=(BF16,), name=f"q_proj_dx_l{l}",
                        deps=behind_starts())]
        dx, sums1 = _norm_mod_bwd(st["x_in"], dh1s, dx, st["ng1"], st["sc1"], f"norm1_bwd_l{l}")
        d_mod[l] = jnp.stack([sums1[0], sums1[1], dg1[0], sums2[0], sums2[1], dg2[0]])
        d_norm_g[l] = jnp.stack([sums1[2], sums2[2]])
        if l == n_a:
            dkvp, dkg = _head_norm_bwd(kv["kvp"], dk_list, _row(k_norm_g), n_heads, "k_norm_bwd", tails=dv_list)
            dfc = [jnp.pad(d.reshape(n_heads, T).T, ((0, 0), (0, BLK - n_heads))) for d in dfk_list]
            dfl, dbf = _fcum_bwd(dfc, kv["fl"], b_f_pad, "fcum_bwd")
            dwkv = _mm(kv["hkv"], dkvp, ta=True, out_mode="col", out_dtypes=(BF16,), name="kv_proj_dw",
                       deps=behind_starts())
            scatter(dwkv, "w_kv", 0)
            dwf = _mm(kv["hkv"], dfl, ta=True, name="gate_logits_dw", deps=behind_starts())
            dh_a = _mm(dkvp, gwkv, tb=True, bmode="col", layer=0, out_dtypes=(BF16,), name="kv_proj_dx")
            dh_b = _mm(dfl, w_f_pad, tb=True, out_dtypes=(BF16,), name="gate_logits_dx")
            dx, sums_kv = _norm_mod_bwd(kv["x"], [dh_a, dh_b], dx, kv["ng"], kv["sc"], "norm_kv_bwd")
            small.update(d_kv_mod=jnp.stack([sums_kv[0], sums_kv[1]]), d_kv_norm_g=sums_kv[2], d_k_norm_g=dkg[0],
                         d_w_f=dwf[:, :n_heads], d_b_f=dbf[0, :n_heads])

    grad_x = dx.reshape(x.shape)

    contrib = [jnp.stack(d_mod).reshape(depth, N_MOD * D), small["d_kv_mod"].reshape(-1),
               jnp.stack(d_norm_g), jnp.stack(d_ln_g), jnp.stack(d_ln_b), jnp.stack(d_ws), jnp.stack(d_bs),
               small["d_kv_norm_g"], small["d_k_norm_g"], small["d_w_f"], small["d_b_f"], jnp.stack(d_qg)]
    contrib_shapes = [a.shape for a in contrib]
    all_contrib = _all_to_all(_pack(contrib), "gather_small_grads", bcast=True)
    summed = _unpack(_sum_slabs(all_contrib, "sum_small_grads"), contrib_shapes)
    (g_ada_b, g_kv_ada_b, g_norm_g_full, g_ln_g_full, g_ln_b_full, g_ws, g_bs, g_kv_norm_g, g_k_norm_g, g_w_f_full,
     g_b_f, g_q_norm_g) = summed
    dmod_all, dkvmod_all = _unpack(all_contrib, contrib_shapes[:2], lead=(N_DEV,))

    grads = {
        "ada_b": g_ada_b, "kv_ada_b": g_kv_ada_b.reshape(kv_ada_b.shape),
        "norm_g": _shard_of(g_norm_g_full, 2, me, norm_g.shape[2]),
        "gmlp_ln_g": _shard_of(g_ln_g_full, 1, me, gmlp_ln_g.shape[1]),
        "gmlp_ln_b": _shard_of(g_ln_b_full, 1, me, gmlp_ln_b.shape[1]),
        "gmlp_ws": g_ws, "gmlp_bs": g_bs, "kv_norm_g": g_kv_norm_g, "k_norm_g": g_k_norm_g,
        "w_f": _shard_of(g_w_f_full, 0, me, w_f.shape[0]), "b_f": g_b_f, "q_norm_g": g_q_norm_g,
    }
    small_names = list(grads)
    small_w_shapes = [weights[n].shape for n in small_names]
    d_pack, m_pack, v_pack = _adamw_flat(_pack([weights[n] for n in small_names]), _pack([grads[n] for n in small_names]),
                                         _pack([mom_m[n] for n in small_names]), _pack([mom_v[n] for n in small_names]),
                                         "adamw_small")
    deltas = dict(zip(small_names, _unpack(d_pack, small_w_shapes)))
    new_m = dict(zip(small_names, _unpack(m_pack, small_w_shapes)))
    new_v = dict(zip(small_names, _unpack(v_pack, small_w_shapes)))

    def stack_of(n):
        lead = () if weights[n].ndim == 3 else (1,)
        return _AdamStack(*[a.reshape(lead + a.shape) for a in (weights[n], mom_m[n], mom_v[n])], f"adamw_{n}")

    def results_of(n, stack):
        grads[n], deltas[n], new_m[n], new_v[n] = [a.reshape(weights[n].shape) for a in stack.outs]

    sc_t = silu_all.T
    ada_stack, kv_ada_stack = stack_of("ada_w"), stack_of("kv_ada_w")
    who = me.astype(jnp.int32).reshape(1)
    for l in range(depth):
        ada_stack.from_outer(l, who, sc_t, _shard_of(dmod_all[:, l], 1, me, mod_cols))
    kv_ada_stack.from_outer(0, who, sc_t, _shard_of(dkvmod_all, 1, me, kv_cols))
    results_of("ada_w", ada_stack)
    results_of("kv_ada_w", kv_ada_stack)

    stacks = {n: stack_of(n) for n in ("mlp_w1", "mlp_w2", "gmlp_w_in", "gmlp_w_out", "attn_wq", "attn_wo", "w_kv")}
    after = ada_stack.outs[0]
    for l in reversed(range(depth)):
        keys = [("mlp_w2", l), ("mlp_w1", l)]
        keys += [("gmlp_w_out", l), ("gmlp_w_in", l)] if l < n_a else [("attn_wo", l - n_a), ("attn_wq", l - n_a)]
        keys += [("w_kv", 0)] if l == n_a else []
        landed = {}
        for key in keys:
            sent, land = _a2a_wait(started[key], after, f"scatter_{key[0]}_l{key[1]}_wait")
            landed[key] = (land, sent)
        for key in keys:
            stacks[key[0]].from_parts(key[1], who, *landed[key])
            after = stacks[key[0]].outs[0]
    for n, stack in stacks.items():
        results_of(n, stack)

    return (loss, grad_x, *[grads[n] for n in WEIGHT_NAMES], *[deltas[n] for n in WEIGHT_NAMES],
            *[new_m[n] for n in WEIGHT_NAMES], *[new_v[n] for n in WEIGHT_NAMES])
```

```python
import functools
import math

import jax
import jax.numpy as jnp
from jax import lax
from jax.experimental import pallas as pl
from jax.experimental.pallas import tpu as pltpu

F32 = jnp.float32
BF16 = jnp.bfloat16
N_DEV = 8
EPS = 1e-6
CHUNK = 64
BLK = 128
N_MOD = 6
ADAM_LR = 0.001
ADAM_B1 = 0.9
ADAM_B2 = 0.999
ADAM_EPS = 1e-08
ADAM_WD = 0.01
ADAM_STEP = 10
VMEM_LIMIT_BYTES = 56 * 2 ** 20
NEG_BIG = -1e30
WEIGHT_NAMES = ['ada_w', 'ada_b', 'norm_g', 'mlp_w1', 'mlp_w2', 'gmlp_w_in', 'gmlp_ln_g', 'gmlp_ln_b', 'gmlp_ws',
                'gmlp_bs', 'gmlp_w_out', 'kv_norm_g', 'kv_ada_w', 'kv_ada_b', 'w_kv', 'k_norm_g', 'w_f', 'b_f',
                'attn_wq', 'q_norm_g', 'attn_wo']
MESH = pl.DeviceIdType.MESH


def _params(sem):
    return pltpu.CompilerParams(dimension_semantics=sem, vmem_limit_bytes=VMEM_LIMIT_BYTES)


def _tile(n, cap, unit=128):
    if n <= cap:
        return n
    t = (cap // unit) * unit
    while t > unit and n % t:
        t -= unit
    assert n % t == 0, (n, cap, unit)
    return t


def _my_index():
    return 4 * lax.axis_index("x") + 2 * lax.axis_index("y") + lax.axis_index("c")


def _all_to_all(x, name, bcast=False):
    slab = x.shape if bcast else x.shape[1:]

    def body(x_ref, o_ref, send_sems, recv_sems, local_sem):
        me = _my_index()

        def src(j):
            return x_ref if bcast else x_ref.at[j]

        mine = pltpu.make_async_copy(src(me), o_ref.at[me], local_sem)
        mine.start()
        sends = []
        for d in range(1, N_DEV):
            peer = (me + d) % N_DEV
            cp = pltpu.make_async_remote_copy(
                src_ref=src(peer), dst_ref=o_ref.at[me],
                send_sem=send_sems.at[d - 1], recv_sem=recv_sems.at[d - 1],
                device_id=(peer // 4, (peer // 2) % 2, peer % 2), device_id_type=MESH)
            cp.start()
            sends.append(cp)
        for d in range(1, N_DEV):
            frm = (me + N_DEV - d) % N_DEV
            pltpu.make_async_remote_copy(
                src_ref=src(frm), dst_ref=o_ref.at[frm],
                send_sem=send_sems.at[d - 1], recv_sem=recv_sems.at[d - 1],
                device_id=(frm // 4, (frm // 2) % 2, frm % 2), device_id_type=MESH).wait_recv()
        for cp in sends:
            cp.wait_send()
        mine.wait()

    return pl.pallas_call(
        body, name=name,
        out_shape=jax.ShapeDtypeStruct((N_DEV,) + tuple(slab), x.dtype),
        in_specs=[pl.BlockSpec(memory_space=pl.ANY)],
        out_specs=pl.BlockSpec(memory_space=pl.ANY),
        scratch_shapes=[pltpu.SemaphoreType.DMA((N_DEV - 1,)), pltpu.SemaphoreType.DMA((N_DEV - 1,)),
                        pltpu.SemaphoreType.DMA],
        compiler_params=pltpu.CompilerParams(has_side_effects=True),
    )(x)


_HBM = pl.BlockSpec(memory_space=pltpu.HBM)
_SEM = pl.BlockSpec(memory_space=pltpu.SEMAPHORE)
_ANY = pl.BlockSpec(memory_space=pl.ANY)
_DATAFLOW = pltpu.SideEffectType.DATAFLOW_SIDE_EFFECTING


def _a2a_peer_copy(x_ref, land_ref, send_sems, recv_sems, d, me, incoming):
    peer = (me + N_DEV - d) % N_DEV if incoming else (me + d) % N_DEV
    return pltpu.make_async_remote_copy(
        src_ref=x_ref.at[peer], dst_ref=land_ref.at[peer if incoming else me],
        send_sem=send_sems.at[d - 1], recv_sem=recv_sems.at[d - 1],
        device_id=(peer // 4, (peer // 2) % 2, peer % 2), device_id_type=MESH)


def _a2a_start(x, name):
    def body(x_ref, land_ref, send_sems, recv_sems, x_thru, land_thru, token):
        me = _my_index()
        for d in range(1, N_DEV):
            _a2a_peer_copy(x_ref, land_ref, send_sems, recv_sems, d, me, False).start()
        token[...] = jnp.zeros_like(token)

    return pl.pallas_call(
        body, name=name,
        out_shape=(pltpu.SemaphoreType.DMA((N_DEV - 1,)), pltpu.SemaphoreType.DMA((N_DEV - 1,)),
                   pltpu.HBM(x.shape, x.dtype), pltpu.HBM(x.shape, x.dtype), jax.ShapeDtypeStruct((8, BLK), F32)),
        in_specs=(_HBM, _HBM), out_specs=(_SEM, _SEM, _HBM, _HBM, pl.BlockSpec(memory_space=pltpu.VMEM)),
        input_output_aliases={0: 2, 1: 3},
        compiler_params=pltpu.CompilerParams(has_side_effects=_DATAFLOW),
    )(pltpu.with_memory_space_constraint(x, pltpu.HBM),
      pltpu.with_memory_space_constraint(lax.empty(x.shape, x.dtype), pltpu.HBM))


def _a2a_wait(started, after, name):
    send_sems, recv_sems, x_thru, land_thru, _ = started

    def body(x_ref, land_ref, send_sems, recv_sems, after_ref, x_dead, land_out):
        me = _my_index()
        for d in range(1, N_DEV):
            _a2a_peer_copy(x_ref, land_ref, send_sems, recv_sems, d, me, False).wait_send()
        for d in range(1, N_DEV):
            _a2a_peer_copy(x_ref, land_ref, send_sems, recv_sems, d, me, True).wait_recv()

    return pl.pallas_call(
        body, name=name,
        out_shape=(pltpu.HBM(x_thru.shape, x_thru.dtype), pltpu.HBM(x_thru.shape, x_thru.dtype)),
        in_specs=(_HBM, _HBM, _SEM, _SEM, _ANY), out_specs=(_HBM, _HBM),
        input_output_aliases={0: 0, 1: 1},
        compiler_params=pltpu.CompilerParams(has_side_effects=_DATAFLOW),
    )(x_thru, land_thru, send_sems, recv_sems, after)


def _place_shard(w, layer, who, name):
    _, R, C = w.shape
    tr = _row_tile(R, C)

    def body(who_ref, w_ref, o_ref):
        o_ref[...] = w_ref[...].astype(BF16)

    return pl.pallas_call(
        body, name=name,
        grid_spec=pltpu.PrefetchScalarGridSpec(
            num_scalar_prefetch=1, grid=(R // tr,),
            in_specs=[pl.BlockSpec((None, tr, C), lambda i, who_ref: (layer, i, 0))],
            out_specs=pl.BlockSpec((None, None, tr, C), lambda i, who_ref: (who_ref[0], 0, i, 0))),
        out_shape=_sds((N_DEV, 1, R, C), BF16),
        compiler_params=_params(("parallel",)),
    )(who, w)


def _gather_copies(land_ref, send_sems, recv_sems, base, phase, incoming):
    cx, cy, cc = lax.axis_index("x"), lax.axis_index("y"), lax.axis_index("c")
    sibling = (cx, cy, 1 - cc)
    chips = [(1 - cx, cy), (cx, 1 - cy), (1 - cx, 1 - cy)]
    if phase == 1:
        out = [((cx, cy, cc), sibling)] + [((cx, cy, cc), (*chip, cc)) for chip in chips]
        inc = [(sibling, sibling)] + [((*chip, cc), (*chip, cc)) for chip in chips]
    else:
        out = [((*chip, cc), sibling) for chip in chips]
        inc = [((*chip, 1 - cc), sibling) for chip in chips]
    copies = []
    for k, (block, peer) in enumerate(inc if incoming else out):
        slab = land_ref.at[4 * block[0] + 2 * block[1] + block[2]]
        copies.append(pltpu.make_async_remote_copy(
            src_ref=slab, dst_ref=slab, send_sem=send_sems.at[base + k], recv_sem=recv_sems.at[base + k],
            device_id=peer, device_id_type=MESH))
    return copies


def _gather_start(lands, phase, name):
    n, per = len(lands), (4 if phase == 1 else 3)

    def body(*refs):
        land_refs, send_sems, recv_sems, token = refs[:n], refs[n], refs[n + 1], refs[-1]
        for a, land_ref in enumerate(land_refs):
            for cp in _gather_copies(land_ref, send_sems, recv_sems, a * per, phase, False):
                cp.start()
        token[...] = jnp.zeros_like(token)

    outs = pl.pallas_call(
        body, name=name,
        out_shape=(pltpu.SemaphoreType.DMA((n * per,)), pltpu.SemaphoreType.DMA((n * per,)),
                   *[pltpu.HBM(x.shape, x.dtype) for x in lands], jax.ShapeDtypeStruct((8, BLK), F32)),
        in_specs=tuple(_HBM for _ in lands),
        out_specs=(_SEM, _SEM, *[_HBM for _ in lands], pl.BlockSpec(memory_space=pltpu.VMEM)),
        input_output_aliases={a: 2 + a for a in range(n)},
        compiler_params=pltpu.CompilerParams(has_side_effects=_DATAFLOW),
    )(*[pltpu.with_memory_space_constraint(x, pltpu.HBM) for x in lands])
    return outs[0], outs[1], list(outs[2:2 + n]), outs[-1]


def _gather_wait(started, phase, after, name):
    send_sems, recv_sems, lands, _ = started
    n, per = len(lands), (4 if phase == 1 else 3)

    def body(*refs):
        land_refs, send_sems, recv_sems = refs[:n], refs[n], refs[n + 1]
        for a, land_ref in enumerate(land_refs):
            for cp in _gather_copies(land_ref, send_sems, recv_sems, a * per, phase, False):
                cp.wait_send()
            for cp in _gather_copies(land_ref, send_sems, recv_sems, a * per, phase, True):
                cp.wait_recv()

    outs = pl.pallas_call(
        body, name=name,
        out_shape=tuple(pltpu.HBM(x.shape, x.dtype) for x in lands),
        in_specs=(*[_HBM for _ in lands], _SEM, _SEM, _ANY), out_specs=tuple(_HBM for _ in lands),
        input_output_aliases={a: a for a in range(n)},
        compiler_params=pltpu.CompilerParams(has_side_effects=_DATAFLOW),
    )(*lands, send_sems, recv_sems, after)
    return list(outs)


def _mm(a, b, *, name, ta=False, tb=False, bmode="plain", layer=0, out_mode="plain", out_dtypes=(F32,),
        epilogue=None, extra=(), caps=(1024, 1024, 1024), deps=()):
    if ta:
        K, M = a.shape
    else:
        M, K = a.shape
    n_unit = k_unit = None
    if bmode == "plain":
        N, Kb = (b.shape if tb else b.shape[::-1])
    elif bmode == "col":
        _, _, Kw, Ns = b.shape
        if tb:
            N, Kb, k_unit = Kw, N_DEV * Ns, Ns
        else:
            N, Kb, n_unit = N_DEV * Ns, Kw, Ns
    else:
        _, _, Ks, Nw = b.shape
        if tb:
            N, Kb, n_unit = N_DEV * Ks, Nw, Ks
        else:
            N, Kb, k_unit = Nw, N_DEV * Ks, Ks
    assert K == Kb, (name, a.shape, b.shape)
    if out_mode == "col":
        assert n_unit is None
        n_unit = N // N_DEV
    tm = _tile(M, caps[0])
    tn = _tile(n_unit or N, caps[1])
    tk = _tile(k_unit or K, caps[2])
    nk = K // tk
    npb = (n_unit // tn) if n_unit else None
    kpb = (k_unit // tk) if k_unit else None
    grid = (M // tm, N // tn, nk)

    a_spec = pl.BlockSpec((tk, tm), lambda i, j, k: (k, i)) if ta else pl.BlockSpec((tm, tk), lambda i, j, k: (i, k))
    if bmode == "plain":
        b_spec = (pl.BlockSpec((tn, tk), lambda i, j, k: (j, k)) if tb
                  else pl.BlockSpec((tk, tn), lambda i, j, k: (k, j)))
    elif bmode == "col":
        b_spec = (pl.BlockSpec((None, None, tn, tk), lambda i, j, k: (k // kpb, layer, j, k % kpb)) if tb
                  else pl.BlockSpec((None, None, tk, tn), lambda i, j, k: (j // npb, layer, k, j % npb)))
    else:
        b_spec = (pl.BlockSpec((None, None, tn, tk), lambda i, j, k: (j // npb, layer, j % npb, k)) if tb
                  else pl.BlockSpec((None, None, tk, tn), lambda i, j, k: (k // kpb, layer, k % kpb, j)))
    mn_spec = pl.BlockSpec((tm, tn), lambda i, j, k: (i, j))
    if out_mode == "col":
        o_specs = [pl.BlockSpec((None, tm, tn), lambda i, j, k: (j // npb, i, j % npb))]
        o_shapes = [jax.ShapeDtypeStruct((N_DEV, M, N // N_DEV), out_dtypes[0])]
    else:
        o_specs = [mn_spec for _ in out_dtypes]
        o_shapes = [jax.ShapeDtypeStruct((M, N), dt) for dt in out_dtypes]
    dims = (((0 if ta else 1,), (1 if tb else 0,)), ((), ()))
    n_extra, n_out, n_dep = len(extra), len(out_dtypes), len(deps)

    def body(a_ref, b_ref, *rest):
        extra_refs, out_refs, acc_ref = rest[:n_extra], rest[n_extra + n_dep:n_extra + n_dep + n_out], rest[-1]
        k = pl.program_id(2)

        @pl.when(k == 0)
        def _():
            acc_ref[...] = jnp.zeros_like(acc_ref)

        acc_ref[...] += lax.dot_general(a_ref[...].astype(BF16), b_ref[...].astype(BF16), dims,
                                        preferred_element_type=F32)

        @pl.when(k == nk - 1)
        def _():
            acc = acc_ref[...]
            outs = (acc,) if epilogue is None else epilogue(acc, *[r[...] for r in extra_refs])
            for o_ref, val in zip(out_refs, outs):
                o_ref[...] = val.astype(o_ref.dtype)

    outs = pl.pallas_call(
        body, name=name, grid=grid,
        in_specs=[a_spec, b_spec] + [mn_spec for _ in extra] + [_ANY for _ in deps],
        out_specs=o_specs, out_shape=o_shapes,
        scratch_shapes=[pltpu.VMEM((tm, tn), F32)],
        compiler_params=_params(("parallel", "parallel", "arbitrary")),
    )(a, b, *extra, *deps)
    return outs[0] if n_out == 1 else outs


def _relu2_epilogue(acc):
    r = jnp.maximum(acc, 0.0)
    return acc, r * r


def _relu2_bwd_epilogue(acc, a_pre):
    return (acc * (2.0 * jnp.maximum(a_pre.astype(F32), 0.0)),)


def _rowcall(body, *, name, tr, row_ins, full_ins=(), row_outs=(), acc_outs=(), scratch=(), reverse=False):
    T = row_ins[0].shape[0]
    nb = T // tr
    rmap = (lambda i: (nb - 1 - i, 0)) if reverse else (lambda i: (i, 0))

    def full_spec(shape):
        nd = len(shape)
        return pl.BlockSpec(tuple(shape), lambda i: (0,) * nd)

    in_specs = [pl.BlockSpec((tr, a.shape[1]), rmap) for a in row_ins] + [full_spec(a.shape) for a in full_ins]
    out_specs = [pl.BlockSpec((tr, s.shape[1]), rmap) for s in row_outs] + [full_spec(s.shape) for s in acc_outs]
    outs = pl.pallas_call(
        body, name=name, grid=(nb,), in_specs=in_specs, out_specs=out_specs,
        out_shape=list(row_outs) + list(acc_outs), scratch_shapes=list(scratch),
        compiler_params=_params(("arbitrary",)),
    )(*row_ins, *full_ins)
    return outs


def _sds(shape, dtype):
    return jax.ShapeDtypeStruct(tuple(shape), dtype)


def _row_tile(T, C, elems=512 * 1024):
    t = max(8, min(T, elems // C))
    p = 8
    while p * 2 <= t and T % (p * 2) == 0:
        p *= 2
    return p


def _norm_mod(x, ng, sc, sh, name):
    T, D = x.shape

    def body(x_ref, ng_ref, sc_ref, sh_ref, h_ref):
        xv = x_ref[...]
        r = lax.rsqrt(jnp.mean(xv * xv, axis=-1, keepdims=True) + EPS)
        h_ref[...] = (((xv * r) * ng_ref[...]) * (1.0 + sc_ref[...]) + sh_ref[...]).astype(BF16)

    return _rowcall(body, name=name, tr=_row_tile(T, D), row_ins=[x], full_ins=[ng, sc, sh],
                    row_outs=[_sds((T, D), BF16)])[0]


def _res_norm_mod(x, y, gate, ng, sc, sh, name):
    T, D = x.shape

    def body(x_ref, y_ref, g_ref, ng_ref, sc_ref, sh_ref, x2_ref, h_ref):
        xv = x_ref[...] + g_ref[...] * y_ref[...]
        x2_ref[...] = xv
        r = lax.rsqrt(jnp.mean(xv * xv, axis=-1, keepdims=True) + EPS)
        h_ref[...] = (((xv * r) * ng_ref[...]) * (1.0 + sc_ref[...]) + sh_ref[...]).astype(BF16)

    return _rowcall(body, name=name, tr=_row_tile(T, D, 256 * 1024), row_ins=[x, y], full_ins=[gate, ng, sc, sh],
                    row_outs=[_sds((T, D), F32), _sds((T, D), BF16)])


def _res_add(x, y, gate, name):
    T, D = x.shape

    def body(x_ref, y_ref, g_ref, x2_ref):
        x2_ref[...] = x_ref[...] + g_ref[...] * y_ref[...]

    return _rowcall(body, name=name, tr=_row_tile(T, D), row_ins=[x, y], full_ins=[gate],
                    row_outs=[_sds((T, D), F32)])[0]


def _res_loss(x, y, gate, target, name):
    T, D = x.shape

    def body(x_ref, y_ref, t_ref, g_ref, dy_ref, loss_ref):
        @pl.when(pl.program_id(0) == 0)
        def _():
            loss_ref[...] = jnp.zeros_like(loss_ref)

        diff = x_ref[...] + g_ref[...] * y_ref[...] - t_ref[...]
        dy_ref[...] = diff * (1.0 / D)
        loss_ref[...] += jnp.sum(diff * diff) * (0.5 / D)

    return _rowcall(body, name=name, tr=_row_tile(T, D, 256 * 1024), row_ins=[x, y, target], full_ins=[gate],
                    row_outs=[_sds((T, D), F32)], acc_outs=[_sds((1, BLK), F32)])


def _gate_bwd(dx, y, gate, name):
    T, D = dx.shape

    def body(dx_ref, y_ref, g_ref, dy_ref, dg_ref):
        @pl.when(pl.program_id(0) == 0)
        def _():
            dg_ref[...] = jnp.zeros_like(dg_ref)

        dxv = dx_ref[...]
        dy_ref[...] = (dxv * g_ref[...]).astype(BF16)
        dg_ref[...] += jnp.sum(dxv * y_ref[...], axis=0, keepdims=True)

    return _rowcall(body, name=name, tr=_row_tile(T, D), row_ins=[dx, y], full_ins=[gate],
                    row_outs=[_sds((T, D), BF16)], acc_outs=[_sds((1, D), F32)])


def _norm_mod_bwd(x, dhs, dres, ng, sc, name):
    T, D = x.shape
    n_dh = len(dhs)

    def body(*refs):
        x_ref, dh_refs, dres_ref = refs[0], refs[1:1 + n_dh], refs[1 + n_dh]
        ng_ref, sc_ref, dx_ref, sums_ref = refs[2 + n_dh:]

        @pl.when(pl.program_id(0) == 0)
        def _():
            sums_ref[...] = jnp.zeros_like(sums_ref)

        xv = x_ref[...]
        dh = dh_refs[0][...].astype(F32)
        for r_ in dh_refs[1:]:
            dh = dh + r_[...].astype(F32)
        r = lax.rsqrt(jnp.mean(xv * xv, axis=-1, keepdims=True) + EPS)
        n = xv * r
        ngv, scale1 = ng_ref[...], 1.0 + sc_ref[...]
        dn = dh * (ngv * scale1)
        dx_ref[...] = dres_ref[...] + r * (dn - n * jnp.mean(dn * n, axis=-1, keepdims=True))
        dhn = dh * n
        sums_ref[0:1, :] += jnp.sum(dh, axis=0, keepdims=True)
        sums_ref[1:2, :] += jnp.sum(dhn * ngv, axis=0, keepdims=True)
        sums_ref[2:3, :] += jnp.sum(dhn * scale1, axis=0, keepdims=True)

    return _rowcall(body, name=name, tr=_row_tile(T, D, 256 * 1024), row_ins=[x, *dhs, dres], full_ins=[ng, sc],
                    row_outs=[_sds((T, D), F32)], acc_outs=[_sds((8, D), F32)])


def _head_norm(x, g, n_heads, name, tail=False):
    T = x.shape[0]
    D = n_heads * BLK
    W = x.shape[1] if tail else D

    def body(x_ref, g_ref, o_ref, *tail_ref):
        for h in range(n_heads):
            xv = x_ref[:, h * BLK:(h + 1) * BLK]
            r = lax.rsqrt(jnp.mean(xv * xv, axis=-1, keepdims=True) + EPS)
            o_ref[:, h * BLK:(h + 1) * BLK] = ((xv * r) * g_ref[...]).astype(BF16)
        if tail:
            tail_ref[0][...] = x_ref[:, D:2 * D].astype(BF16)

    tr = _row_tile(T, x.shape[1])
    o_spec = pl.BlockSpec((tr, D), lambda i: (i, 0))
    return pl.pallas_call(
        body, name=name, grid=(T // tr,),
        in_specs=[pl.BlockSpec((tr, W), lambda i: (i, 0)), pl.BlockSpec((1, BLK), lambda i: (0, 0))],
        out_specs=[o_spec] * (2 if tail else 1), out_shape=[_sds((T, D), BF16)] * (2 if tail else 1),
        compiler_params=_params(("parallel",)),
    )(x, g)


def _head_norm_bwd(x, dys, g, n_heads, name, tails=()):
    T = x.shape[0]
    D = n_heads * BLK
    n_dy, n_tail = len(dys), len(tails)
    W = 2 * D if tails else D

    def body(*refs):
        x_ref, dy_refs, tail_refs = refs[0], refs[1:1 + n_dy], refs[1 + n_dy:1 + n_dy + n_tail]
        g_ref, dx_ref, dg_ref = refs[1 + n_dy + n_tail:]

        @pl.when(pl.program_id(0) == 0)
        def _():
            dg_ref[...] = jnp.zeros_like(dg_ref)

        tot = jnp.zeros((1, BLK), F32)
        for h in range(n_heads):
            cols = slice(h * BLK, (h + 1) * BLK)
            xv = x_ref[:, cols]
            dyv = dy_refs[0][:, cols]
            for r_ in dy_refs[1:]:
                dyv = dyv + r_[:, cols]
            r = lax.rsqrt(jnp.mean(xv * xv, axis=-1, keepdims=True) + EPS)
            n = xv * r
            dn = dyv * g_ref[...]
            dx_ref[:, cols] = (r * (dn - n * jnp.mean(dn * n, axis=-1, keepdims=True))).astype(BF16)
            tot = tot + jnp.sum(dyv * n, axis=0, keepdims=True)
        dg_ref[0:1, :] += tot
        if n_tail:
            tv = tail_refs[0][...]
            for r_ in tail_refs[1:]:
                tv = tv + r_[...]
            dx_ref[:, D:] = tv.astype(BF16)

    tr = _row_tile(T, 2 * D, 256 * 1024)
    d_spec = pl.BlockSpec((tr, D), lambda i: (i, 0))
    return pl.pallas_call(
        body, name=name, grid=(T // tr,),
        in_specs=[d_spec] * (1 + n_dy + n_tail) + [pl.BlockSpec((1, BLK), lambda i: (0, 0))],
        out_specs=[pl.BlockSpec((tr, W), lambda i: (i, 0)), pl.BlockSpec((8, BLK), lambda i: (0, 0))],
        out_shape=[_sds((T, W), BF16), _sds((8, BLK), F32)],
        compiler_params=_params(("arbitrary",)),
    )(x, *dys, *tails, g)


def _fcum_fwd(fl, bf, name):
    T = fl.shape[0]

    def body(fl_ref, b_ref, o_ref, carry_ref):
        @pl.when(pl.program_id(0) == 0)
        def _():
            carry_ref[...] = jnp.zeros_like(carry_ref)

        z = fl_ref[...] + b_ref[...]
        logf = jnp.minimum(z, 0.0) - jnp.log(1.0 + jnp.exp(-jnp.abs(z)))
        row = lax.broadcasted_iota(jnp.int32, (BLK, BLK), 0)
        col = lax.broadcasted_iota(jnp.int32, (BLK, BLK), 1)
        tri = (col <= row).astype(F32)
        run = jnp.dot(tri, logf, preferred_element_type=F32, precision=lax.Precision.HIGHEST) + carry_ref[0:1, :]
        o_ref[...] = run
        carry_ref[0:1, :] = run[BLK - 1:BLK, :]

    return _rowcall(body, name=name, tr=BLK, row_ins=[fl], full_ins=[bf], row_outs=[_sds((T, BLK), F32)],
                    scratch=[pltpu.VMEM((8, BLK), F32)])[0]


def _fcum_bwd(dfs, fl, bf, name):
    T = fl.shape[0]
    n_df = len(dfs)

    def body(*refs):
        df_refs = refs[:n_df]
        fl_ref, b_ref, dfl_ref, dbias_ref, carry_ref = refs[n_df:]

        @pl.when(pl.program_id(0) == 0)
        def _():
            carry_ref[...] = jnp.zeros_like(carry_ref)
            dbias_ref[...] = jnp.zeros_like(dbias_ref)

        dfc = df_refs[0][...]
        for r_ in df_refs[1:]:
            dfc = dfc + r_[...]
        row = lax.broadcasted_iota(jnp.int32, (BLK, BLK), 0)
        col = lax.broadcasted_iota(jnp.int32, (BLK, BLK), 1)
        tri = (col >= row).astype(F32)
        suffix = jnp.dot(tri, dfc, preferred_element_type=F32, precision=lax.Precision.HIGHEST) + carry_ref[0:1, :]
        carry_ref[0:1, :] = suffix[0:1, :]
        z = fl_ref[...] + b_ref[...]
        dfl = suffix / (1.0 + jnp.exp(z))
        dfl_ref[...] = dfl.astype(BF16)
        dbias_ref[0:1, :] += jnp.sum(dfl, axis=0, keepdims=True)

    return _rowcall(body, name=name, tr=BLK, row_ins=[*dfs, fl], full_ins=[bf], reverse=True,
                    row_outs=[_sds((T, BLK), BF16)], acc_outs=[_sds((8, BLK), F32)],
                    scratch=[pltpu.VMEM((8, BLK), F32)])


def _attn_tile(T):
    return min(T, 512)


def _lane_pick(block, h):
    lane = lax.broadcasted_iota(jnp.int32, block.shape, 1)
    return jnp.sum(jnp.where(lane == h, block, 0.0), axis=1, keepdims=True)


def _attn_fwd(q, k, v, fq, fk, n_heads, name):
    T = q.shape[0]
    tq = tk = _attn_tile(T)
    nkb = T // tk
    inv_sqrt = 1.0 / float(math.sqrt(BLK))

    def body(q_ref, k_ref, v_ref, fq_ref, fk_ref, o_ref, o32_ref, lse_ref):
        h, i = pl.program_id(0), pl.program_id(1)
        qv = q_ref[...]
        fqv = _lane_pick(fq_ref[...], h)
        qpos = i * tq + lax.broadcasted_iota(jnp.int32, (tq, tk), 0)
        kloc = lax.broadcasted_iota(jnp.int32, (tq, tk), 1)

        def step(j, carry):
            m, l, acc = carry
            rows = pl.ds(pl.multiple_of(j * tk, tk), tk)
            kj, vj = k_ref[rows, :], v_ref[rows, :]
            s = lax.dot_general(qv, kj, (((1,), (1,)), ((), ())), preferred_element_type=F32) * inv_sqrt
            s = s + (fqv - fk_ref[j])
            s = jnp.where(j * tk + kloc <= qpos, s, NEG_BIG)
            m_new = jnp.maximum(m, jnp.max(s, axis=-1, keepdims=True))
            alpha = jnp.exp(m - m_new)
            p = jnp.exp(s - m_new)
            l = alpha * l + jnp.sum(p, axis=-1, keepdims=True)
            acc = alpha * acc + jnp.dot(p.astype(BF16), vj, preferred_element_type=F32)
            return m_new, l, acc

        init = (jnp.full((tq, 1), NEG_BIG, F32), jnp.zeros((tq, 1), F32), jnp.zeros((tq, BLK), F32))
        m, l, acc = lax.fori_loop(0, i + 1, step, init)
        out = acc / l
        o_ref[...] = out.astype(BF16)
        o32_ref[...] = out
        lse_ref[...] = m + jnp.log(l)

    return pl.pallas_call(
        body, name=name, grid=(n_heads, T // tq),
        in_specs=[pl.BlockSpec((tq, BLK), lambda h, i: (i, h)),
                  pl.BlockSpec((T, BLK), lambda h, i: (0, h)),
                  pl.BlockSpec((T, BLK), lambda h, i: (0, h)),
                  pl.BlockSpec((tq, BLK), lambda h, i: (i, 0)),
                  pl.BlockSpec((None, nkb, 1, tk), lambda h, i: (h, 0, 0, 0))],
        out_specs=[pl.BlockSpec((tq, BLK), lambda h, i: (i, h)),
                   pl.BlockSpec((tq, BLK), lambda h, i: (i, h)),
                   pl.BlockSpec((None, tq, 1), lambda h, i: (h, i, 0))],
        out_shape=[_sds((T, n_heads * BLK), BF16), _sds((T, n_heads * BLK), F32), _sds((n_heads, T, 1), F32)],
        compiler_params=_params(("parallel", "arbitrary")),
    )(q, k, v, fq, fk)


def _attn_bwd(q, k, v, o, do, lse, fq, fk, n_heads, name):
    T = q.shape[0]
    tq = tk = _attn_tile(T)
    nkb = T // tk
    nq = T // tq
    inv_sqrt = 1.0 / float(math.sqrt(BLK))
    tn_dims = (((0,), (0,)), ((), ()))
    nt_dims = (((1,), (1,)), ((), ()))

    def body(q_ref, k_ref, v_ref, o_ref, do_ref, lse_ref, fq_ref, fk_ref, dq_ref, dk_ref, dv_ref, dfq_ref, dfk_ref,
             delta_ref):
        h, j = pl.program_id(0), pl.program_id(1)

        @pl.when(j == 0)
        def _():
            delta_ref[...] = jnp.sum(do_ref[...].astype(F32) * o_ref[...], axis=1, keepdims=True)
            dq_ref[...] = jnp.zeros_like(dq_ref)
            dfq_ref[...] = jnp.zeros_like(dfq_ref)

        kj, vj, fkv = k_ref[...], v_ref[...], fk_ref[...]
        kpos = j * tk + lax.broadcasted_iota(jnp.int32, (tq, tk), 1)
        qloc = lax.broadcasted_iota(jnp.int32, (tq, tk), 0)

        def step(i, carry):
            dk, dv, dfk = carry
            rows = pl.ds(pl.multiple_of(i * tq, tq), tq)
            qi, doi = q_ref[rows, :], do_ref[rows, :]
            fqv = _lane_pick(fq_ref[rows, :], h)
            s = lax.dot_general(qi, kj, nt_dims, preferred_element_type=F32) * inv_sqrt + (fqv - fkv)
            s = jnp.where(kpos <= i * tq + qloc, s, NEG_BIG)
            p = jnp.exp(s - lse_ref[rows, :])
            dv = dv + lax.dot_general(p.astype(BF16), doi, tn_dims, preferred_element_type=F32)
            dp = lax.dot_general(doi, vj, nt_dims, preferred_element_type=F32)
            ds = p * (dp - delta_ref[rows, :])
            dsb = ds.astype(BF16)
            dq_ref[rows, :] += jnp.dot(dsb, kj, preferred_element_type=F32) * inv_sqrt
            dk = dk + lax.dot_general(dsb, qi, tn_dims, preferred_element_type=F32)
            dfq_ref[rows, :] += jnp.sum(ds, axis=1, keepdims=True)
            dfk = dfk - jnp.sum(ds, axis=0, keepdims=True)
            return dk, dv, dfk

        init = (jnp.zeros((tk, BLK), F32), jnp.zeros((tk, BLK), F32), jnp.zeros((1, tk), F32))
        dk, dv, dfk = lax.fori_loop(j, nq, step, init)
        dk_ref[...] = dk * inv_sqrt
        dv_ref[...] = dv
        dfk_ref[...] = dfk

    head_col = lambda h, j: (0, h)
    return pl.pallas_call(
        body, name=name, grid=(n_heads, nkb),
        in_specs=[pl.BlockSpec((T, BLK), head_col),
                  pl.BlockSpec((tk, BLK), lambda h, j: (j, h)),
                  pl.BlockSpec((tk, BLK), lambda h, j: (j, h)),
                  pl.BlockSpec((T, BLK), head_col),
                  pl.BlockSpec((T, BLK), head_col),
                  pl.BlockSpec((None, T, 1), lambda h, j: (h, 0, 0)),
                  pl.BlockSpec((T, BLK), lambda h, j: (0, 0)),
                  pl.BlockSpec((None, None, 1, tk), lambda h, j: (h, j, 0, 0))],
        out_specs=[pl.BlockSpec((T, BLK), head_col),
                   pl.BlockSpec((tk, BLK), lambda h, j: (j, h)),
                   pl.BlockSpec((tk, BLK), lambda h, j: (j, h)),
                   pl.BlockSpec((None, T, 1), lambda h, j: (h, 0, 0)),
                   pl.BlockSpec((None, None, 1, tk), lambda h, j: (h, j, 0, 0))],
        out_shape=[_sds((T, n_heads * BLK), F32), _sds((T, n_heads * BLK), F32), _sds((T, n_heads * BLK), F32),
                   _sds((n_heads, T, 1), F32), _sds((n_heads, nkb, 1, tk), F32)],
        scratch_shapes=[pltpu.VMEM((T, 1), F32)],
        compiler_params=_params(("parallel", "arbitrary")),
    )(q, k, v, o, do, lse, fq, fk)


_INV_SQRT2 = 1.0 / math.sqrt(2.0)
_INV_SQRT_2PI = 1.0 / math.sqrt(2.0 * math.pi)


def _gelu_parts(z):
    cdf = 0.5 * (1.0 + lax.erf(z * _INV_SQRT2))
    return cdf, z * cdf


def _mix_mask(transposed):
    row = lax.broadcasted_iota(jnp.int32, (BLK, BLK), 0) // CHUNK
    col = lax.broadcasted_iota(jnp.int32, (BLK, BLK), 1) // CHUNK
    return (row <= col) if transposed else (col <= row)


def _gmlp_mid_fwd(zpre, ln_g, ln_b, ws, bs_t, name):
    T, two_h = zpre.shape
    Hh = two_h // 2
    G = ws.shape[0]
    gd = Hh // G

    def body(z_ref, lg_ref, lb_ref, ws_ref, bs_ref, p_ref):
        _, zg = _gelu_parts(z_ref[...].astype(F32))
        u, v = zg[:, :Hh], zg[:, Hh:]
        mu = jnp.mean(v, axis=-1, keepdims=True)
        vc = v - mu
        rstd = lax.rsqrt(jnp.mean(vc * vc, axis=-1, keepdims=True) + EPS)
        vn = ((vc * rstd) * lg_ref[...] + lb_ref[...]).astype(BF16)
        mask = _mix_mask(False)
        for g in range(G):
            wm = jnp.where(mask, ws_ref[g], 0.0).astype(BF16)
            sv = jnp.dot(wm, vn[:, g * gd:(g + 1) * gd], preferred_element_type=F32) + bs_ref[:, g:g + 1]
            p_ref[:, g * gd:(g + 1) * gd] = (u[:, g * gd:(g + 1) * gd] * sv).astype(BF16)

    return _rowcall(body, name=name, tr=BLK, row_ins=[zpre], full_ins=[ln_g, ln_b, ws, bs_t],
                    row_outs=[_sds((T, Hh), BF16)])[0]


def _gmlp_mid_bwd(zpre, dp, ln_g, ln_b, ws, ws_t, bs_t, name):
    T, two_h = zpre.shape
    Hh = two_h // 2
    G = ws.shape[0]
    gd = Hh // G
    nt_dims = (((1,), (1,)), ((), ()))

    def body(z_ref, dp_ref, lg_ref, lb_ref, ws_ref, wst_ref, bs_ref, dz_ref, dws_ref, dbs_ref, dlg_ref, dlb_ref,
             dvn_ref):
        @pl.when(pl.program_id(0) == 0)
        def _():
            dws_ref[...] = jnp.zeros_like(dws_ref)
            dbs_ref[...] = jnp.zeros_like(dbs_ref)
            dlg_ref[...] = jnp.zeros_like(dlg_ref)
            dlb_ref[...] = jnp.zeros_like(dlb_ref)

        z = z_ref[...].astype(F32)
        cdf, zg = _gelu_parts(z)
        dgelu = cdf + z * (jnp.exp(-0.5 * z * z) * _INV_SQRT_2PI)
        u, v = zg[:, :Hh], zg[:, Hh:]
        mu = jnp.mean(v, axis=-1, keepdims=True)
        vc = v - mu
        rstd = lax.rsqrt(jnp.mean(vc * vc, axis=-1, keepdims=True) + EPS)
        vhat = vc * rstd
        vn = (vhat * lg_ref[...] + lb_ref[...]).astype(BF16)
        mask, mask_t = _mix_mask(False), _mix_mask(True)
        lane = lax.broadcasted_iota(jnp.int32, (BLK, BLK), 1)
        dbs = jnp.zeros((BLK, BLK), F32)
        for g in range(G):
            cols = slice(g * gd, (g + 1) * gd)
            wm = jnp.where(mask, ws_ref[g], 0.0).astype(BF16)
            wm_t = jnp.where(mask_t, wst_ref[g], 0.0).astype(BF16)
            vn_g = vn[:, cols]
            sv = jnp.dot(wm, vn_g, preferred_element_type=F32) + bs_ref[:, g:g + 1]
            dp_g = dp_ref[:, cols].astype(F32)
            dz_ref[:, cols] = ((dp_g * sv) * dgelu[:, cols]).astype(BF16)
            dsv = dp_g * u[:, cols]
            dsv_b = dsv.astype(BF16)
            dbs = dbs + jnp.where(lane == g, jnp.sum(dsv, axis=1, keepdims=True), 0.0)
            dws_ref[g] += jnp.where(mask, lax.dot_general(dsv_b, vn_g, nt_dims, preferred_element_type=F32), 0.0)
            dvn_ref[:, cols] = jnp.dot(wm_t, dsv_b, preferred_element_type=F32)
        dbs_ref[...] += dbs
        dvn = dvn_ref[...]
        dlg_ref[0:1, :] += jnp.sum(dvn * vhat, axis=0, keepdims=True)
        dlb_ref[0:1, :] += jnp.sum(dvn, axis=0, keepdims=True)
        dvh = dvn * lg_ref[...]
        dv = rstd * (dvh - jnp.mean(dvh, axis=-1, keepdims=True) - vhat * jnp.mean(dvh * vhat, axis=-1, keepdims=True))
        dz_ref[:, Hh:] = (dv * dgelu[:, Hh:]).astype(BF16)

    return _rowcall(body, name=name, tr=BLK, row_ins=[zpre, dp], full_ins=[ln_g, ln_b, ws, ws_t, bs_t],
                    row_outs=[_sds((T, two_h), BF16)],
                    acc_outs=[_sds((G, BLK, BLK), F32), _sds((BLK, BLK), F32), _sds((8, Hh), F32), _sds((8, Hh), F32)],
                    scratch=[pltpu.VMEM((BLK, Hh), F32)])


def _mods(c_all, w, layer, bias, name):
    nb, K = c_all.shape
    N = w.shape[-1]
    tn = _tile(N, 512)

    def body(c_ref, w_ref, b_ref, o_ref):
        cv = c_ref[...]
        sc = cv / (1.0 + jnp.exp(-cv))
        o_ref[...] = jnp.dot(sc, w_ref[...], preferred_element_type=F32, precision=lax.Precision.HIGHEST) + b_ref[...]

    return pl.pallas_call(
        body, name=name, grid=(N // tn,),
        in_specs=[pl.BlockSpec((nb, K), lambda j: (0, 0)),
                  pl.BlockSpec((None, K, tn), lambda j: (layer, 0, j)),
                  pl.BlockSpec((1, tn), lambda j: (0, j))],
        out_specs=pl.BlockSpec((nb, tn), lambda j: (0, j)), out_shape=_sds((nb, N), F32),
        compiler_params=_params(("parallel",)),
    )(c_all, w, bias)


def _sum_slabs(x, name):
    _, R, C = x.shape
    tr = _row_tile(R, C * N_DEV)

    def body(x_ref, o_ref):
        acc = x_ref[0]
        for s in range(1, N_DEV):
            acc = acc + x_ref[s]
        o_ref[...] = acc

    return pl.pallas_call(
        body, name=name, grid=(R // tr,),
        in_specs=[pl.BlockSpec((N_DEV, tr, C), lambda i: (0, i, 0))],
        out_specs=pl.BlockSpec((tr, C), lambda i: (i, 0)), out_shape=_sds((R, C), F32),
        compiler_params=_params(("parallel",)),
    )(x)


def _adamw_math(w, g, m, v):
    m = ADAM_B1 * m + (1.0 - ADAM_B1) * g
    v = ADAM_B2 * v + (1.0 - ADAM_B2) * (g * g)
    m_hat = m / (1.0 - ADAM_B1 ** ADAM_STEP)
    v_hat = v / (1.0 - ADAM_B2 ** ADAM_STEP)
    delta = -ADAM_LR * (m_hat / (jnp.sqrt(v_hat) + ADAM_EPS) + ADAM_WD * w)
    return delta, m, v


class _AdamStack:
    def __init__(self, w, m, v, name):
        self.w, self.m, self.v, self.name = w, m, v, name
        self.L, self.R, self.C = w.shape
        self.tr = _row_tile(self.R, self.C, 128 * 1024)
        self.outs = None

    def _layer(self, l, who, srcs, src_specs, make_grad):
        n_src = len(srcs)
        L, R, C, tr = self.L, self.R, self.C, self.tr
        wspec = pl.BlockSpec((None, tr, C), lambda i, who_ref: (l, i, 0))

        def body(who_ref, *refs):
            src_refs = refs[:n_src]
            w_ref, m_ref, v_ref = refs[n_src:n_src + 3]
            g_ref, d_ref, m2_ref, v2_ref = refs[-4:]
            g = make_grad(who_ref[0], *src_refs)
            delta, m2, v2 = _adamw_math(w_ref[...], g, m_ref[...], v_ref[...])
            g_ref[...] = g
            d_ref[...] = delta
            m2_ref[...] = m2
            v2_ref[...] = v2

        prev = [] if self.outs is None else list(self.outs)
        aliases = {} if self.outs is None else {1 + n_src + 3 + t: t for t in range(4)}
        self.outs = pl.pallas_call(
            body, name=f"{self.name}_l{l}",
            grid_spec=pltpu.PrefetchScalarGridSpec(
                num_scalar_prefetch=1, grid=(R // tr,),
                in_specs=list(src_specs) + [wspec] * 3 + [_ANY] * len(prev), out_specs=[wspec] * 4),
            out_shape=[_sds((L, R, C), F32)] * 4,
            input_output_aliases=aliases,
            compiler_params=_params(("parallel",)),
        )(who, *srcs, self.w, self.m, self.v, *prev)

    def from_parts(self, l, who, landed, sent):
        tr, C = self.tr, self.C

        def make_grad(me, p_ref, own_ref):
            mine = own_ref[...].astype(F32)
            g = jnp.where(me == 0, mine, p_ref[0].astype(F32))
            for s in range(1, N_DEV):
                g = g + jnp.where(me == s, mine, p_ref[s].astype(F32))
            return g

        self._layer(l, who, [landed, sent],
                    [pl.BlockSpec((N_DEV, tr, C), lambda i, who_ref: (0, i, 0)),
                     pl.BlockSpec((None, tr, C), lambda i, who_ref: (who_ref[0], i, 0))], make_grad)

    def from_outer(self, l, who, sc_t, dmod):
        tr, C = self.tr, self.C

        def make_grad(me, s_ref, d_ref):
            g = s_ref[:, 0:1] * d_ref[0:1, :]
            for b in range(1, N_DEV):
                g = g + s_ref[:, b:b + 1] * d_ref[b:b + 1, :]
            return g

        self._layer(l, who, [sc_t, dmod], [pl.BlockSpec((tr, N_DEV), lambda i, who_ref: (i, 0)),
                                           pl.BlockSpec((N_DEV, C), lambda i, who_ref: (0, 0))], make_grad)


def _adamw_flat(w, g, m, v, name):
    R, C = w.shape
    tr = _row_tile(R, C, 128 * 1024)

    def body(w_ref, g_ref, m_ref, v_ref, d_ref, m2_ref, v2_ref):
        delta, m2, v2 = _adamw_math(w_ref[...], g_ref[...], m_ref[...], v_ref[...])
        d_ref[...] = delta
        m2_ref[...] = m2
        v2_ref[...] = v2

    spec = pl.BlockSpec((tr, C), lambda i: (i, 0))
    return pl.pallas_call(
        body, name=name, grid=(R // tr,), in_specs=[spec] * 4, out_specs=[spec] * 3,
        out_shape=[_sds((R, C), F32)] * 3, compiler_params=_params(("parallel",)),
    )(w, g, m, v)


def _pack(arrays):
    flat = jnp.concatenate([a.reshape(-1).astype(F32) for a in arrays])
    pad = (-flat.shape[0]) % (8 * BLK)
    if pad:
        flat = jnp.concatenate([flat, jnp.zeros((pad,), F32)])
    return flat.reshape(-1, BLK)


def _unpack(buf, shapes, lead=()):
    sizes = [int(math.prod(s)) for s in shapes]
    out, off = [], 0
    if all(n % BLK == 0 for n in sizes):
        for s, n in zip(shapes, sizes):
            out.append(buf[..., off // BLK:(off + n) // BLK, :].reshape(tuple(lead) + tuple(s)))
            off += n
        return out
    flat = buf.reshape(tuple(lead) + (-1,))
    for s, n in zip(shapes, sizes):
        out.append(flat[..., off:off + n].reshape(tuple(lead) + tuple(s)))
        off += n
    return out


def _row(vec):
    return vec.reshape(1, -1)


def _shard_of(full, axis, me, size):
    return lax.dynamic_slice_in_dim(full, me * size, size, axis=axis)


def kernel(x, c, ada_w, ada_b, norm_g, mlp_w1, mlp_w2, gmlp_w_in, gmlp_ln_g, gmlp_ln_b, gmlp_ws, gmlp_bs, gmlp_w_out, kv_norm_g, kv_ada_w, kv_ada_b, w_kv, k_norm_g, w_f, b_f, attn_wq, q_norm_g, attn_wo, loss_target, m_ada_w, m_ada_b, m_norm_g, m_mlp_w1, m_mlp_w2, m_gmlp_w_in, m_gmlp_ln_g, m_gmlp_ln_b, m_gmlp_ws, m_gmlp_bs, m_gmlp_w_out, m_kv_norm_g, m_kv_ada_w, m_kv_ada_b, m_w_kv, m_k_norm_g, m_w_f, m_b_f, m_attn_wq, m_q_norm_g, m_attn_wo, v_ada_w, v_ada_b, v_norm_g, v_mlp_w1, v_mlp_w2, v_gmlp_w_in, v_gmlp_ln_g, v_gmlp_ln_b, v_gmlp_ws, v_gmlp_bs, v_gmlp_w_out, v_kv_norm_g, v_kv_ada_w, v_kv_ada_b, v_w_kv, v_k_norm_g, v_w_f, v_b_f, v_attn_wq, v_q_norm_g, v_attn_wo):
    given = dict(locals())
    weights = {n: given[n] for n in WEIGHT_NAMES}
    mom_m = {n: given["m_" + n] for n in WEIGHT_NAMES}
    mom_v = {n: given["v_" + n] for n in WEIGHT_NAMES}

    me = _my_index()
    T, D = x.shape[1], x.shape[2]
    depth = ada_w.shape[0]
    n_a = gmlp_w_in.shape[0]
    n_heads = b_f.shape[0]
    G = gmlp_ws.shape[1]
    Hh = gmlp_ln_g.shape[1] * N_DEV
    mod_cols = ada_w.shape[2]
    kv_cols = kv_ada_w.shape[1]
    x0 = x.reshape(T, D)
    target = loss_target.reshape(T, D)

    small_in = [c, norm_g, gmlp_ln_g, gmlp_ln_b, w_f]
    small_shapes = [a.shape for a in small_in]
    got = _all_to_all(_pack(small_in), "gather_small_inputs", bcast=True)
    c_all, norm_g_sh, ln_g_sh, ln_b_sh, w_f_sh = _unpack(got, small_shapes, lead=(N_DEV,))
    c_all = c_all.reshape(N_DEV, D)
    norm_g_full = jnp.moveaxis(norm_g_sh, 0, 2).reshape(depth, 2, D)
    ln_g_full = jnp.moveaxis(ln_g_sh, 0, 1).reshape(n_a, Hh)
    ln_b_full = jnp.moveaxis(ln_b_sh, 0, 1).reshape(n_a, Hh)
    w_f_full = w_f_sh.reshape(D, n_heads)
    w_f_pad = jnp.pad(w_f_full, ((0, 0), (0, BLK - n_heads))).astype(BF16)
    b_f_pad = jnp.pad(b_f, (0, BLK - n_heads)).reshape(1, BLK)

    mod_parts = []
    for l in range(depth):
        bias = _shard_of(ada_b[l], 0, me, mod_cols).reshape(1, mod_cols)
        mod_parts.append(_mods(c_all, ada_w, l, bias, f"mods_l{l}"))
    kv_bias = _shard_of(kv_ada_b, 0, me, kv_cols).reshape(1, kv_cols)
    mod_parts.append(_mods(c_all, kv_ada_w.reshape(1, D, kv_cols), 0, kv_bias, "mods_kv"))
    mods_mine = jnp.concatenate(mod_parts, axis=1)
    mod_width = mods_mine.shape[1]
    mods_pack = jnp.pad(mods_mine, ((0, 0), (0, (-mod_width) % (8 * BLK)))).reshape(N_DEV, -1, BLK)
    mods_got = _all_to_all(mods_pack, "exchange_mods").reshape(N_DEV, -1)[:, :mod_width]
    mods = []
    for l in range(depth):
        mods.append(mods_got[:, l * mod_cols:(l + 1) * mod_cols].reshape(N_MOD, D))
    kv_mod = mods_got[:, depth * mod_cols:].reshape(2, D)
    silu_all = c_all / (1.0 + jnp.exp(-c_all))

    assert 1 <= n_a < depth
    who = me.astype(jnp.int32).reshape(1)
    big = {"mlp_w1": mlp_w1, "mlp_w2": mlp_w2, "gmlp_w_in": gmlp_w_in, "gmlp_w_out": gmlp_w_out,
           "w_kv": w_kv.reshape((1,) + w_kv.shape), "attn_wq": attn_wq, "attn_wo": attn_wo}
    groups = [[("gmlp_w_in", 0)], [("gmlp_w_out", 0), ("mlp_w1", 0), ("mlp_w2", 0)]]
    for l in range(1, depth):
        if l < n_a:
            groups.append([("gmlp_w_in", l), ("gmlp_w_out", l), ("mlp_w1", l), ("mlp_w2", l)])
        else:
            first = [("w_kv", 0)] if l == n_a else []
            groups.append(first + [("attn_wq", l - n_a), ("attn_wo", l - n_a), ("mlp_w1", l), ("mlp_w2", l)])
    tokens = []

    def behind_starts():
        out = tuple(tokens)
        tokens.clear()
        return out

    over_ici = []
    for gi, grp in enumerate(groups):
        placed = [_place_shard(big[n], l, who, f"place_{n}_l{l}") for n, l in grp]
        over_ici.append(_gather_start(placed, 1, f"gather_g{gi}_ici_start"))
        tokens.append(over_ici[gi][3])
    to_sibling = {}
    gw = {}

    def gather_forward(gi, after):
        landed = _gather_wait(over_ici[gi], 1, after, f"gather_g{gi}_ici_wait")
        to_sibling[gi] = _gather_start(landed, 2, f"gather_g{gi}_d2d_start")
        tokens.append(to_sibling[gi][3])

    def gather_finish(gi, after):
        for key, arr in zip(groups[gi], _gather_wait(to_sibling[gi], 2, after, f"gather_g{gi}_d2d_wait")):
            gw[key] = arr

    tkk = _attn_tile(T)

    saved = []
    xs = x0
    pending = None
    kv = None
    for l in range(depth):
        sh1, sc1, g1, sh2, sc2, g2 = [_row(mods[l][t]) for t in range(N_MOD)]
        ng1, ng2 = _row(norm_g_full[l, 0]), _row(norm_g_full[l, 1])
        st = dict(sc1=sc1, g1=g1, sc2=sc2, g2=g2, ng1=ng1, ng2=ng2)
        if pending is None:
            h1 = _norm_mod(xs, ng1, sc1, sh1, f"norm1_l{l}")
            gather_forward(0, h1)
            gather_finish(0, h1)
        else:
            gather_finish(l + 1, pending[0])
            xs, h1 = _res_norm_mod(xs, pending[0], pending[1], ng1, sc1, sh1, f"res_norm1_l{l}")
        st["x_in"], st["h1"] = xs, h1
        if l < n_a:
            a = l
            zpre = _mm(h1, gw["gmlp_w_in", a], bmode="col", out_dtypes=(BF16,), name=f"gmlp_in_l{l}",
                       deps=behind_starts())
            if l == 0:
                gather_forward(1, zpre)
            bs_t = gmlp_bs[a].T
            p = _gmlp_mid_fwd(zpre, _row(ln_g_full[a]), _row(ln_b_full[a]), gmlp_ws[a], bs_t, f"gmlp_mid_l{l}")
            if l == 0:
                gather_finish(1, p)
            y = _mm(p, gw["gmlp_w_out", a], bmode="row", name=f"gmlp_out_l{l}", deps=behind_starts())
            st.update(zpre=zpre, p=p)
        else:
            if kv is None:
                kv_ng, kv_sh, kv_sc = _row(kv_norm_g), _row(kv_mod[0]), _row(kv_mod[1])
                hkv = _norm_mod(xs, kv_ng, kv_sc, kv_sh, "norm_kv")
                kvp = _mm(hkv, gw["w_kv", 0], bmode="col", name="kv_proj", deps=behind_starts())
                kk, vv = _head_norm(kvp, _row(k_norm_g), n_heads, "k_norm", tail=True)
                fl = _mm(hkv, w_f_pad, name="gate_logits")
                fcum = _fcum_fwd(fl, b_f_pad, "fcum")
                fk = fcum[:, :n_heads].T.reshape(n_heads, T // tkk, 1, tkk)
                kv = dict(x=xs, hkv=hkv, kvp=kvp, k=kk, v=vv, fl=fl, fcum=fcum, fk=fk, ng=kv_ng, sc=kv_sc)
            bl = l - n_a
            qp = _mm(h1, gw["attn_wq", bl], bmode="row", name=f"q_proj_l{l}", deps=behind_starts())
            q = _head_norm(qp, _row(q_norm_g[bl]), n_heads, f"q_norm_l{l}")[0]
            o, o32, lse = _attn_fwd(q, kv["k"], kv["v"], kv["fcum"], kv["fk"], n_heads, f"attn_l{l}")
            y = _mm(o, gw["attn_wo", bl], bmode="row", name=f"attn_out_l{l}")
            st.update(qp=qp, q=q, o=o, o32=o32, lse=lse)
        xs, h2 = _res_norm_mod(xs, y, g1, ng2, sc2, sh2, f"res_norm2_l{l}")
        a_pre, s_act = _mm(h2, gw["mlp_w1", l], bmode="col", out_dtypes=(BF16, BF16), epilogue=_relu2_epilogue,
                           name=f"mlp_up_l{l}")
        if l + 1 < depth:
            gather_forward(l + 2, a_pre)
        mo = _mm(s_act, gw["mlp_w2", l], bmode="row", name=f"mlp_down_l{l}", deps=behind_starts())
        st.update(y=y, x_mid=xs, h2=h2, a_pre=a_pre, s=s_act, m=mo)
        saved.append(st)
        pending = (mo, g2)

    dx, loss_row = _res_loss(xs, pending[0], pending[1], target, "loss")
    loss = lax.psum(loss_row[0, 0], ("x", "y", "c"))

    started = {}

    def scatter(dw_slabs, key, idx):
        started[(key, idx)] = _a2a_start(dw_slabs, f"scatter_{key}_l{idx}_start")
        tokens.append(started[(key, idx)][4])

    d_mod = [None] * depth
    d_norm_g = [None] * depth
    d_ln_g, d_ln_b, d_ws, d_bs = [None] * n_a, [None] * n_a, [None] * n_a, [None] * n_a
    d_qg = [None] * (depth - n_a)
    dk_list, dv_list, dfk_list = [], [], []
    small = {}

    for l in reversed(range(depth)):
        st = saved[l]
        dm, dg2 = _gate_bwd(dx, st["m"], st["g2"], f"gate2_bwd_l{l}")
        da = _mm(dm, gw["mlp_w2", l], tb=True, bmode="row", out_dtypes=(BF16,), epilogue=_relu2_bwd_epilogue,
                 extra=(st["a_pre"],), name=f"mlp_down_dx_l{l}")
        dw2 = _mm(st["s"], dm, ta=True, out_dtypes=(BF16,), name=f"mlp_down_dw_l{l}", deps=behind_starts())
        scatter(dw2.reshape(N_DEV, -1, D), "mlp_w2", l)
        dw1 = _mm(st["h2"], da, ta=True, out_mode="col", out_dtypes=(BF16,), name=f"mlp_up_dw_l{l}",
                  deps=behind_starts())
        scatter(dw1, "mlp_w1", l)
        dh2 = _mm(da, gw["mlp_w1", l], tb=True, bmode="col", out_dtypes=(BF16,), name=f"mlp_up_dx_l{l}",
                  deps=behind_starts())
        dx, sums2 = _norm_mod_bwd(st["x_mid"], [dh2], dx, st["ng2"], st["sc2"], f"norm2_bwd_l{l}")
        dy, dg1 = _gate_bwd(dx, st["y"], st["g1"], f"gate1_bwd_l{l}")
        if l < n_a:
            a = l
            dwo = _mm(st["p"], dy, ta=True, out_dtypes=(BF16,), name=f"gmlp_out_dw_l{l}", deps=behind_starts())
            scatter(dwo.reshape(N_DEV, -1, D), "gmlp_w_out", a)
            dp = _mm(dy, gw["gmlp_w_out", a], tb=True, bmode="row", out_dtypes=(BF16,), name=f"gmlp_out_dx_l{l}",
                     deps=behind_starts())
            dz, d_ws[a], dbs_t, dlg, dlb = _gmlp_mid_bwd(
                st["zpre"], dp, _row(ln_g_full[a]), _row(ln_b_full[a]), gmlp_ws[a],
                jnp.swapaxes(gmlp_ws[a], 1, 2), gmlp_bs[a].T, f"gmlp_mid_bwd_l{l}")
            d_bs[a], d_ln_g[a], d_ln_b[a] = dbs_t[:, :G].T, dlg[0], dlb[0]
            dwi = _mm(st["h1"], dz, ta=True, out_mode="col", out_dtypes=(BF16,), name=f"gmlp_in_dw_l{l}",
                      deps=behind_starts())
            scatter(dwi, "gmlp_w_in", a)
            dh1s = [_mm(dz, gw["gmlp_w_in", a], tb=True, bmode="col", out_dtypes=(BF16,), name=f"gmlp_in_dx_l{l}",
                        deps=behind_starts())]
        else:
            bl = l - n_a
            dwo = _mm(st["o"], dy, ta=True, out_dtypes=(BF16,), name=f"attn_out_dw_l{l}", deps=behind_starts())
            scatter(dwo.reshape(N_DEV, -1, D), "attn_wo", bl)
            do = _mm(dy, gw["attn_wo", bl], tb=True, bmode="row", out_dtypes=(BF16,), name=f"attn_out_dx_l{l}",
                     deps=behind_starts())
            dq, dk, dv, dfq, dfk = _attn_bwd(st["q"], kv["k"], kv["v"], st["o32"], do, st["lse"], kv["fcum"],
                                             kv["fk"], n_heads, f"attn_bwd_l{l}")
            dfk_list += [dfq, dfk]
            dk_list.append(dk)
            dv_list.append(dv)
            dqp, dqg = _head_norm_bwd(st["qp"], [dq], _row(q_norm_g[bl]), n_heads, f"q_norm_bwd_l{l}")
            d_qg[bl] = dqg[0]
            dwq = _mm(st["h1"], dqp, ta=True, out_dtypes=(BF16,), name=f"q_proj_dw_l{l}", deps=behind_starts())
            scatter(dwq.reshape(N_DEV, -1, D), "attn_wq", bl)
            dh1s = [_mm(dqp, gw["attn_wq", bl], tb=True, bmode="row", out_dtypes=(BF16,), name=f"q_proj_dx_l{l}",
                        deps=behind_starts())]
        dx, sums1 = _norm_mod_bwd(st["x_in"], dh1s, dx, st["ng1"], st["sc1"], f"norm1_bwd_l{l}")
        d_mod[l] = jnp.stack([sums1[0], sums1[1], dg1[0], sums2[0], sums2[1], dg2[0]])
        d_norm_g[l] = jnp.stack([sums1[2], sums2[2]])
        if l == n_a:
            dkvp, dkg = _head_norm_bwd(kv["kvp"], dk_list, _row(k_norm_g), n_heads, "k_norm_bwd", tails=dv_list)
            dfc = [jnp.pad(d.reshape(n_heads, T).T, ((0, 0), (0, BLK - n_heads))) for d in dfk_list]
            dfl, dbf = _fcum_bwd(dfc, kv["fl"], b_f_pad, "fcum_bwd")
            dwkv = _mm(kv["hkv"], dkvp, ta=True, out_mode="col", out_dtypes=(BF16,), name="kv_proj_dw",
                       deps=behind_starts())
            scatter(dwkv, "w_kv", 0)
            dwf = _mm(kv["hkv"], dfl, ta=True, name="gate_logits_dw", deps=behind_starts())
            dh_a = _mm(dkvp, gw["w_kv", 0], tb=True, bmode="col", out_dtypes=(BF16,), name="kv_proj_dx")
            dh_b = _mm(dfl, w_f_pad, tb=True, out_dtypes=(BF16,), name="gate_logits_dx")
            dx, sums_kv = _norm_mod_bwd(kv["x"], [dh_a, dh_b], dx, kv["ng"], kv["sc"], "norm_kv_bwd")
            small.update(d_kv_mod=jnp.stack([sums_kv[0], sums_kv[1]]), d_kv_norm_g=sums_kv[2], d_k_norm_g=dkg[0],
                         d_w_f=dwf[:, :n_heads], d_b_f=dbf[0, :n_heads])

    grad_x = dx.reshape(x.shape)

    contrib = [jnp.stack(d_mod).reshape(depth, N_MOD * D), small["d_kv_mod"].reshape(-1),
               jnp.stack(d_norm_g), jnp.stack(d_ln_g), jnp.stack(d_ln_b), jnp.stack(d_ws), jnp.stack(d_bs),
               small["d_kv_norm_g"], small["d_k_norm_g"], small["d_w_f"], small["d_b_f"], jnp.stack(d_qg)]
    contrib_shapes = [a.shape for a in contrib]
    all_contrib = _all_to_all(_pack(contrib), "gather_small_grads", bcast=True)
    summed = _unpack(_sum_slabs(all_contrib, "sum_small_grads"), contrib_shapes)
    (g_ada_b, g_kv_ada_b, g_norm_g_full, g_ln_g_full, g_ln_b_full, g_ws, g_bs, g_kv_norm_g, g_k_norm_g, g_w_f_full,
     g_b_f, g_q_norm_g) = summed
    dmod_all, dkvmod_all = _unpack(all_contrib, contrib_shapes[:2], lead=(N_DEV,))

    grads = {
        "ada_b": g_ada_b, "kv_ada_b": g_kv_ada_b.reshape(kv_ada_b.shape),
        "norm_g": _shard_of(g_norm_g_full, 2, me, norm_g.shape[2]),
        "gmlp_ln_g": _shard_of(g_ln_g_full, 1, me, gmlp_ln_g.shape[1]),
        "gmlp_ln_b": _shard_of(g_ln_b_full, 1, me, gmlp_ln_b.shape[1]),
        "gmlp_ws": g_ws, "gmlp_bs": g_bs, "kv_norm_g": g_kv_norm_g, "k_norm_g": g_k_norm_g,
        "w_f": _shard_of(g_w_f_full, 0, me, w_f.shape[0]), "b_f": g_b_f, "q_norm_g": g_q_norm_g,
    }
    small_names = list(grads)
    small_w_shapes = [weights[n].shape for n in small_names]
    d_pack, m_pack, v_pack = _adamw_flat(_pack([weights[n] for n in small_names]), _pack([grads[n] for n in small_names]),
                                         _pack([mom_m[n] for n in small_names]), _pack([mom_v[n] for n in small_names]),
                                         "adamw_small")
    deltas = dict(zip(small_names, _unpack(d_pack, small_w_shapes)))
    new_m = dict(zip(small_names, _unpack(m_pack, small_w_shapes)))
    new_v = dict(zip(small_names, _unpack(v_pack, small_w_shapes)))

    def stack_of(n):
        lead = () if weights[n].ndim == 3 else (1,)
        return _AdamStack(*[a.reshape(lead + a.shape) for a in (weights[n], mom_m[n], mom_v[n])], f"adamw_{n}")

    def results_of(n, stack):
        grads[n], deltas[n], new_m[n], new_v[n] = [a.reshape(weights[n].shape) for a in stack.outs]

    sc_t = silu_all.T
    ada_stack, kv_ada_stack = stack_of("ada_w"), stack_of("kv_ada_w")
    for l in range(depth):
        ada_stack.from_outer(l, who, sc_t, _shard_of(dmod_all[:, l], 1, me, mod_cols))
    kv_ada_stack.from_outer(0, who, sc_t, _shard_of(dkvmod_all, 1, me, kv_cols))
    results_of("ada_w", ada_stack)
    results_of("kv_ada_w", kv_ada_stack)

    stacks = {n: stack_of(n) for n in ("mlp_w1", "mlp_w2", "gmlp_w_in", "gmlp_w_out", "attn_wq", "attn_wo", "w_kv")}
    after = ada_stack.outs[0]
    for l in reversed(range(depth)):
        keys = [("mlp_w2", l), ("mlp_w1", l)]
        keys += [("gmlp_w_out", l), ("gmlp_w_in", l)] if l < n_a else [("attn_wo", l - n_a), ("attn_wq", l - n_a)]
        keys += [("w_kv", 0)] if l == n_a else []
        landed = {}
        for key in keys:
            sent, land = _a2a_wait(started[key], after, f"scatter_{key[0]}_l{key[1]}_wait")
            landed[key] = (land, sent)
        for key in keys:
            stacks[key[0]].from_parts(key[1], who, *landed[key])
            after = stacks[key[0]].outs[0]
    for n, stack in stacks.items():
        results_of(n, stack)

    return (loss, grad_x, *[grads[n] for n in WEIGHT_NAMES], *[deltas[n] for n in WEIGHT_NAMES],
            *[new_m[n] for n in WEIGHT_NAMES], *[new_v[n] for n in WEIGHT_NAMES])
```

```python
import functools
import math

import jax
import jax.numpy as jnp
from jax import lax
from jax.experimental import pallas as pl
from jax.experimental.pallas import tpu as pltpu

F32 = jnp.float32
BF16 = jnp.bfloat16
N_DEV = 8
EPS = 1e-6
CHUNK = 64
BLK = 128
N_MOD = 6
ADAM_LR = 0.001
ADAM_B1 = 0.9
ADAM_B2 = 0.999
ADAM_EPS = 1e-08
ADAM_WD = 0.01
ADAM_STEP = 10
VMEM_LIMIT_BYTES = 56 * 2 ** 20
NEG_BIG = -1e30
WEIGHT_NAMES = ['ada_w', 'ada_b', 'norm_g', 'mlp_w1', 'mlp_w2', 'gmlp_w_in', 'gmlp_ln_g', 'gmlp_ln_b', 'gmlp_ws',
                'gmlp_bs', 'gmlp_w_out', 'kv_norm_g', 'kv_ada_w', 'kv_ada_b', 'w_kv', 'k_norm_g', 'w_f', 'b_f',
                'attn_wq', 'q_norm_g', 'attn_wo']
MESH = pl.DeviceIdType.MESH


def _params(sem):
    return pltpu.CompilerParams(dimension_semantics=sem, vmem_limit_bytes=VMEM_LIMIT_BYTES)


def _tile(n, cap, unit=128):
    if n <= cap:
        return n
    t = (cap // unit) * unit
    while t > unit and n % t:
        t -= unit
    assert n % t == 0, (n, cap, unit)
    return t


def _my_index():
    return 4 * lax.axis_index("x") + 2 * lax.axis_index("y") + lax.axis_index("c")


def _all_to_all(x, name, bcast=False):
    slab = x.shape if bcast else x.shape[1:]

    def body(x_ref, o_ref, send_sems, recv_sems, local_sem):
        me = _my_index()

        def src(j):
            return x_ref if bcast else x_ref.at[j]

        mine = pltpu.make_async_copy(src(me), o_ref.at[me], local_sem)
        mine.start()
        sends = []
        for d in range(1, N_DEV):
            peer = (me + d) % N_DEV
            cp = pltpu.make_async_remote_copy(
                src_ref=src(peer), dst_ref=o_ref.at[me],
                send_sem=send_sems.at[d - 1], recv_sem=recv_sems.at[d - 1],
                device_id=(peer // 4, (peer // 2) % 2, peer % 2), device_id_type=MESH)
            cp.start()
            sends.append(cp)
        for d in range(1, N_DEV):
            frm = (me + N_DEV - d) % N_DEV
            pltpu.make_async_remote_copy(
                src_ref=src(frm), dst_ref=o_ref.at[frm],
                send_sem=send_sems.at[d - 1], recv_sem=recv_sems.at[d - 1],
                device_id=(frm // 4, (frm // 2) % 2, frm % 2), device_id_type=MESH).wait_recv()
        for cp in sends:
            cp.wait_send()
        mine.wait()

    return pl.pallas_call(
        body, name=name,
        out_shape=jax.ShapeDtypeStruct((N_DEV,) + tuple(slab), x.dtype),
        in_specs=[pl.BlockSpec(memory_space=pl.ANY)],
        out_specs=pl.BlockSpec(memory_space=pl.ANY),
        scratch_shapes=[pltpu.SemaphoreType.DMA((N_DEV - 1,)), pltpu.SemaphoreType.DMA((N_DEV - 1,)),
                        pltpu.SemaphoreType.DMA],
        compiler_params=pltpu.CompilerParams(has_side_effects=True),
    )(x)


_HBM = pl.BlockSpec(memory_space=pltpu.HBM)
_SEM = pl.BlockSpec(memory_space=pltpu.SEMAPHORE)
_ANY = pl.BlockSpec(memory_space=pl.ANY)
_DATAFLOW = pltpu.SideEffectType.DATAFLOW_SIDE_EFFECTING


def _a2a_peer_copy(x_ref, land_ref, send_sems, recv_sems, d, me, incoming):
    peer = (me + N_DEV - d) % N_DEV if incoming else (me + d) % N_DEV
    return pltpu.make_async_remote_copy(
        src_ref=x_ref.at[peer], dst_ref=land_ref.at[peer if incoming else me],
        send_sem=send_sems.at[d - 1], recv_sem=recv_sems.at[d - 1],
        device_id=(peer // 4, (peer // 2) % 2, peer % 2), device_id_type=MESH)


def _a2a_start(x, name):
    def body(x_ref, land_ref, send_sems, recv_sems, x_thru, land_thru, token):
        me = _my_index()
        for d in range(1, N_DEV):
            _a2a_peer_copy(x_ref, land_ref, send_sems, recv_sems, d, me, False).start()
        token[...] = jnp.zeros_like(token)

    return pl.pallas_call(
        body, name=name,
        out_shape=(pltpu.SemaphoreType.DMA((N_DEV - 1,)), pltpu.SemaphoreType.DMA((N_DEV - 1,)),
                   pltpu.HBM(x.shape, x.dtype), pltpu.HBM(x.shape, x.dtype), jax.ShapeDtypeStruct((8, BLK), F32)),
        in_specs=(_HBM, _HBM), out_specs=(_SEM, _SEM, _HBM, _HBM, pl.BlockSpec(memory_space=pltpu.VMEM)),
        input_output_aliases={0: 2, 1: 3},
        compiler_params=pltpu.CompilerParams(has_side_effects=_DATAFLOW),
    )(pltpu.with_memory_space_constraint(x, pltpu.HBM),
      pltpu.with_memory_space_constraint(lax.empty(x.shape, x.dtype), pltpu.HBM))


def _a2a_wait(started, after, name):
    send_sems, recv_sems, x_thru, land_thru, _ = started

    def body(x_ref, land_ref, send_sems, recv_sems, after_ref, x_dead, land_out):
        me = _my_index()
        for d in range(1, N_DEV):
            _a2a_peer_copy(x_ref, land_ref, send_sems, recv_sems, d, me, False).wait_send()
        for d in range(1, N_DEV):
            _a2a_peer_copy(x_ref, land_ref, send_sems, recv_sems, d, me, True).wait_recv()

    return pl.pallas_call(
        body, name=name,
        out_shape=(pltpu.HBM(x_thru.shape, x_thru.dtype), pltpu.HBM(x_thru.shape, x_thru.dtype)),
        in_specs=(_HBM, _HBM, _SEM, _SEM, _ANY), out_specs=(_HBM, _HBM),
        input_output_aliases={0: 0, 1: 1},
        compiler_params=pltpu.CompilerParams(has_side_effects=_DATAFLOW),
    )(x_thru, land_thru, send_sems, recv_sems, after)


def _place_shard(w, layer, who, name):
    _, R, C = w.shape
    tr = _row_tile(R, C)

    def body(who_ref, w_ref, o_ref):
        o_ref[...] = w_ref[...].astype(BF16)

    return pl.pallas_call(
        body, name=name,
        grid_spec=pltpu.PrefetchScalarGridSpec(
            num_scalar_prefetch=1, grid=(R // tr,),
            in_specs=[pl.BlockSpec((None, tr, C), lambda i, who_ref: (layer, i, 0))],
            out_specs=pl.BlockSpec((None, None, tr, C), lambda i, who_ref: (who_ref[0], 0, i, 0))),
        out_shape=_sds((N_DEV, 1, R, C), BF16),
        compiler_params=_params(("parallel",)),
    )(who, w)


def _gather_copies(land_ref, send_sems, recv_sems, base, phase, incoming):
    cx, cy, cc = lax.axis_index("x"), lax.axis_index("y"), lax.axis_index("c")
    sibling = (cx, cy, 1 - cc)
    chips = [(1 - cx, cy), (cx, 1 - cy), (1 - cx, 1 - cy)]
    if phase == 1:
        out = [((cx, cy, cc), sibling)] + [((cx, cy, cc), (*chip, cc)) for chip in chips]
        inc = [(sibling, sibling)] + [((*chip, cc), (*chip, cc)) for chip in chips]
    else:
        out = [((*chip, cc), sibling) for chip in chips]
        inc = [((*chip, 1 - cc), sibling) for chip in chips]
    copies = []
    for k, (block, peer) in enumerate(inc if incoming else out):
        slab = land_ref.at[4 * block[0] + 2 * block[1] + block[2]]
        copies.append(pltpu.make_async_remote_copy(
            src_ref=slab, dst_ref=slab, send_sem=send_sems.at[base + k], recv_sem=recv_sems.at[base + k],
            device_id=peer, device_id_type=MESH))
    return copies


def _gather_start(lands, phase, name):
    n, per = len(lands), (4 if phase == 1 else 3)

    def body(*refs):
        land_refs, send_sems, recv_sems, token = refs[:n], refs[n], refs[n + 1], refs[-1]
        for a, land_ref in enumerate(land_refs):
            for cp in _gather_copies(land_ref, send_sems, recv_sems, a * per, phase, False):
                cp.start()
        token[...] = jnp.zeros_like(token)

    outs = pl.pallas_call(
        body, name=name,
        out_shape=(pltpu.SemaphoreType.DMA((n * per,)), pltpu.SemaphoreType.DMA((n * per,)),
                   *[pltpu.HBM(x.shape, x.dtype) for x in lands], jax.ShapeDtypeStruct((8, BLK), F32)),
        in_specs=tuple(_HBM for _ in lands),
        out_specs=(_SEM, _SEM, *[_HBM for _ in lands], pl.BlockSpec(memory_space=pltpu.VMEM)),
        input_output_aliases={a: 2 + a for a in range(n)},
        compiler_params=pltpu.CompilerParams(has_side_effects=_DATAFLOW),
    )(*[pltpu.with_memory_space_constraint(x, pltpu.HBM) for x in lands])
    return outs[0], outs[1], list(outs[2:2 + n]), outs[-1]


def _gather_wait(started, phase, after, name):
    send_sems, recv_sems, lands, _ = started
    n, per = len(lands), (4 if phase == 1 else 3)

    def body(*refs):
        land_refs, send_sems, recv_sems = refs[:n], refs[n], refs[n + 1]
        for a, land_ref in enumerate(land_refs):
            for cp in _gather_copies(land_ref, send_sems, recv_sems, a * per, phase, False):
                cp.wait_send()
            for cp in _gather_copies(land_ref, send_sems, recv_sems, a * per, phase, True):
                cp.wait_recv()

    outs = pl.pallas_call(
        body, name=name,
        out_shape=tuple(pltpu.HBM(x.shape, x.dtype) for x in lands),
        in_specs=(*[_HBM for _ in lands], _SEM, _SEM, _ANY), out_specs=tuple(_HBM for _ in lands),
        input_output_aliases={a: a for a in range(n)},
        compiler_params=pltpu.CompilerParams(has_side_effects=_DATAFLOW),
    )(*lands, send_sems, recv_sems, after)
    return list(outs)


def _mm(a, b, *, name, ta=False, tb=False, bmode="plain", layer=0, out_mode="plain", out_dtypes=(F32,),
        epilogue=None, extra=(), caps=(1024, 1024, 2048), deps=()):
    if ta:
        K, M = a.shape
    else:
        M, K = a.shape
    n_unit = k_unit = None
    if bmode == "plain":
        N, Kb = (b.shape if tb else b.shape[::-1])
    elif bmode == "col":
        _, _, Kw, Ns = b.shape
        if tb:
            N, Kb, k_unit = Kw, N_DEV * Ns, Ns
        else:
            N, Kb, n_unit = N_DEV * Ns, Kw, Ns
    else:
        _, _, Ks, Nw = b.shape
        if tb:
            N, Kb, n_unit = N_DEV * Ks, Nw, Ks
        else:
            N, Kb, k_unit = Nw, N_DEV * Ks, Ks
    assert K == Kb, (name, a.shape, b.shape)
    if out_mode == "col":
        assert n_unit is None
        n_unit = N // N_DEV
    tm = _tile(M, caps[0])
    tn = _tile(n_unit or N, caps[1])
    span = 1
    if k_unit and 2 * k_unit <= caps[2]:
        while 2 * span * k_unit <= caps[2] and N_DEV % (2 * span) == 0:
            span *= 2
        tk = span * k_unit
    else:
        tk = _tile(k_unit or K, caps[2])
    nk = K // tk
    npb = (n_unit // tn) if n_unit else None
    kpb = (k_unit // tk) if (k_unit and span == 1) else None
    grid = (M // tm, N // tn, nk)

    a_spec = pl.BlockSpec((tk, tm), lambda i, j, k: (k, i)) if ta else pl.BlockSpec((tm, tk), lambda i, j, k: (i, k))
    if bmode == "plain":
        b_spec = (pl.BlockSpec((tn, tk), lambda i, j, k: (j, k)) if tb
                  else pl.BlockSpec((tk, tn), lambda i, j, k: (k, j)))
    elif bmode == "col":
        if tb and span > 1:
            b_spec = pl.BlockSpec((span, None, tn, k_unit), lambda i, j, k: (k, layer, j, 0))
        elif tb:
            b_spec = pl.BlockSpec((None, None, tn, tk), lambda i, j, k: (k // kpb, layer, j, k % kpb))
        else:
            b_spec = pl.BlockSpec((None, None, tk, tn), lambda i, j, k: (j // npb, layer, k, j % npb))
    else:
        if tb:
            b_spec = pl.BlockSpec((None, None, tn, tk), lambda i, j, k: (j // npb, layer, j % npb, k))
        elif span > 1:
            b_spec = pl.BlockSpec((span, None, k_unit, tn), lambda i, j, k: (k, layer, 0, j))
        else:
            b_spec = pl.BlockSpec((None, None, tk, tn), lambda i, j, k: (k // kpb, layer, k % kpb, j))
    mn_spec = pl.BlockSpec((tm, tn), lambda i, j, k: (i, j))
    if out_mode == "col":
        o_specs = [pl.BlockSpec((None, tm, tn), lambda i, j, k: (j // npb, i, j % npb))]
        o_shapes = [jax.ShapeDtypeStruct((N_DEV, M, N // N_DEV), out_dtypes[0])]
    else:
        o_specs = [mn_spec for _ in out_dtypes]
        o_shapes = [jax.ShapeDtypeStruct((M, N), dt) for dt in out_dtypes]
    dims = (((0 if ta else 1,), (1 if tb else 0,)), ((), ()))
    n_extra, n_out, n_dep = len(extra), len(out_dtypes), len(deps)

    def body(a_ref, b_ref, *rest):
        extra_refs, out_refs = rest[:n_extra], rest[n_extra + n_dep:n_extra + n_dep + n_out]
        k = pl.program_id(2)

        def product():
            if span == 1:
                return lax.dot_general(a_ref[...].astype(BF16), b_ref[...].astype(BF16), dims,
                                       preferred_element_type=F32)
            if not tb:
                return lax.dot_general(a_ref[...].astype(BF16), b_ref[...].reshape(tk, tn).astype(BF16), dims,
                                       preferred_element_type=F32)
            out = None
            for s in range(span):
                part = lax.dot_general(a_ref[:, s * k_unit:(s + 1) * k_unit].astype(BF16), b_ref[s].astype(BF16),
                                       dims, preferred_element_type=F32)
                out = part if out is None else out + part
            return out

        def finish(acc):
            outs = (acc,) if epilogue is None else epilogue(acc, *[r[...] for r in extra_refs])
            for o_ref, val in zip(out_refs, outs):
                o_ref[...] = val.astype(o_ref.dtype)

        if nk == 1:
            finish(product())
            return
        acc_ref = rest[-1]

        @pl.when(k == 0)
        def _():
            acc_ref[...] = product()

        if nk > 2:
            @pl.when(jnp.logical_and(k > 0, k < nk - 1))
            def _():
                acc_ref[...] += product()

        @pl.when(k == nk - 1)
        def _():
            finish(acc_ref[...] + product())

    outs = pl.pallas_call(
        body, name=name, grid=grid,
        in_specs=[a_spec, b_spec] + [mn_spec for _ in extra] + [_ANY for _ in deps],
        out_specs=o_specs, out_shape=o_shapes,
        scratch_shapes=[pltpu.VMEM((tm, tn), F32)] if nk > 1 else [],
        compiler_params=_params(("parallel", "parallel", "arbitrary")),
    )(a, b, *extra, *deps)
    return outs[0] if n_out == 1 else outs


def _relu2_epilogue(acc):
    r = jnp.maximum(acc, 0.0)
    return acc, r * r


def _relu2_bwd_epilogue(acc, a_pre):
    return (acc * (2.0 * jnp.maximum(a_pre.astype(F32), 0.0)),)


def _rowcall(body, *, name, tr, row_ins, full_ins=(), row_outs=(), acc_outs=(), scratch=(), reverse=False):
    T = row_ins[0].shape[0]
    nb = T // tr
    rmap = (lambda i: (nb - 1 - i, 0)) if reverse else (lambda i: (i, 0))

    def full_spec(shape):
        nd = len(shape)
        return pl.BlockSpec(tuple(shape), lambda i: (0,) * nd)

    in_specs = [pl.BlockSpec((tr, a.shape[1]), rmap) for a in row_ins] + [full_spec(a.shape) for a in full_ins]
    out_specs = [pl.BlockSpec((tr, s.shape[1]), rmap) for s in row_outs] + [full_spec(s.shape) for s in acc_outs]
    outs = pl.pallas_call(
        body, name=name, grid=(nb,), in_specs=in_specs, out_specs=out_specs,
        out_shape=list(row_outs) + list(acc_outs), scratch_shapes=list(scratch),
        compiler_params=_params(("arbitrary",)),
    )(*row_ins, *full_ins)
    return outs


def _sds(shape, dtype):
    return jax.ShapeDtypeStruct(tuple(shape), dtype)


def _row_tile(T, C, elems=512 * 1024):
    t = max(8, min(T, elems // C))
    p = 8
    while p * 2 <= t and T % (p * 2) == 0:
        p *= 2
    return p


def _norm_mod(x, ng, sc, sh, name):
    T, D = x.shape

    def body(x_ref, ng_ref, sc_ref, sh_ref, h_ref):
        xv = x_ref[...]
        r = lax.rsqrt(jnp.mean(xv * xv, axis=-1, keepdims=True) + EPS)
        h_ref[...] = (((xv * r) * ng_ref[...]) * (1.0 + sc_ref[...]) + sh_ref[...]).astype(BF16)

    return _rowcall(body, name=name, tr=_row_tile(T, D), row_ins=[x], full_ins=[ng, sc, sh],
                    row_outs=[_sds((T, D), BF16)])[0]


def _res_norm_mod(x, y, gate, ng, sc, sh, name):
    T, D = x.shape

    def body(x_ref, y_ref, g_ref, ng_ref, sc_ref, sh_ref, x2_ref, h_ref):
        xv = x_ref[...] + g_ref[...] * y_ref[...]
        x2_ref[...] = xv
        r = lax.rsqrt(jnp.mean(xv * xv, axis=-1, keepdims=True) + EPS)
        h_ref[...] = (((xv * r) * ng_ref[...]) * (1.0 + sc_ref[...]) + sh_ref[...]).astype(BF16)

    return _rowcall(body, name=name, tr=_row_tile(T, D, 256 * 1024), row_ins=[x, y], full_ins=[gate, ng, sc, sh],
                    row_outs=[_sds((T, D), F32), _sds((T, D), BF16)])


def _res_add(x, y, gate, name):
    T, D = x.shape

    def body(x_ref, y_ref, g_ref, x2_ref):
        x2_ref[...] = x_ref[...] + g_ref[...] * y_ref[...]

    return _rowcall(body, name=name, tr=_row_tile(T, D), row_ins=[x, y], full_ins=[gate],
                    row_outs=[_sds((T, D), F32)])[0]


def _res_loss(x, y, gate, target, name):
    T, D = x.shape

    def body(x_ref, y_ref, t_ref, g_ref, dy_ref, loss_ref):
        @pl.when(pl.program_id(0) == 0)
        def _():
            loss_ref[...] = jnp.zeros_like(loss_ref)

        diff = x_ref[...] + g_ref[...] * y_ref[...] - t_ref[...]
        dy_ref[...] = diff * (1.0 / D)
        loss_ref[...] += jnp.sum(diff * diff) * (0.5 / D)

    return _rowcall(body, name=name, tr=_row_tile(T, D, 256 * 1024), row_ins=[x, y, target], full_ins=[gate],
                    row_outs=[_sds((T, D), F32)], acc_outs=[_sds((1, BLK), F32)])


def _gate_bwd(dx, y, gate, name):
    T, D = dx.shape

    def body(dx_ref, y_ref, g_ref, dy_ref, dg_ref):
        @pl.when(pl.program_id(0) == 0)
        def _():
            dg_ref[...] = jnp.zeros_like(dg_ref)

        dxv = dx_ref[...]
        dy_ref[...] = (dxv * g_ref[...]).astype(BF16)
        dg_ref[...] += jnp.sum(dxv * y_ref[...], axis=0, keepdims=True)

    return _rowcall(body, name=name, tr=_row_tile(T, D), row_ins=[dx, y], full_ins=[gate],
                    row_outs=[_sds((T, D), BF16)], acc_outs=[_sds((1, D), F32)])


def _norm_mod_bwd(x, dhs, dres, ng, sc, name):
    T, D = x.shape
    n_dh = len(dhs)

    def body(*refs):
        x_ref, dh_refs, dres_ref = refs[0], refs[1:1 + n_dh], refs[1 + n_dh]
        ng_ref, sc_ref, dx_ref, sums_ref = refs[2 + n_dh:]

        @pl.when(pl.program_id(0) == 0)
        def _():
            sums_ref[...] = jnp.zeros_like(sums_ref)

        xv = x_ref[...]
        dh = dh_refs[0][...].astype(F32)
        for r_ in dh_refs[1:]:
            dh = dh + r_[...].astype(F32)
        r = lax.rsqrt(jnp.mean(xv * xv, axis=-1, keepdims=True) + EPS)
        n = xv * r
        ngv, scale1 = ng_ref[...], 1.0 + sc_ref[...]
        dn = dh * (ngv * scale1)
        dx_ref[...] = dres_ref[...] + r * (dn - n * jnp.mean(dn * n, axis=-1, keepdims=True))
        dhn = dh * n
        sums_ref[0:1, :] += jnp.sum(dh, axis=0, keepdims=True)
        sums_ref[1:2, :] += jnp.sum(dhn * ngv, axis=0, keepdims=True)
        sums_ref[2:3, :] += jnp.sum(dhn * scale1, axis=0, keepdims=True)

    return _rowcall(body, name=name, tr=_row_tile(T, D, 256 * 1024), row_ins=[x, *dhs, dres], full_ins=[ng, sc],
                    row_outs=[_sds((T, D), F32)], acc_outs=[_sds((8, D), F32)])


def _head_norm(x, g, n_heads, name, tail=False):
    T = x.shape[0]
    D = n_heads * BLK
    W = x.shape[1] if tail else D

    def body(x_ref, g_ref, o_ref, *tail_ref):
        for h in range(n_heads):
            xv = x_ref[:, h * BLK:(h + 1) * BLK]
            r = lax.rsqrt(jnp.mean(xv * xv, axis=-1, keepdims=True) + EPS)
            o_ref[:, h * BLK:(h + 1) * BLK] = ((xv * r) * g_ref[...]).astype(BF16)
        if tail:
            tail_ref[0][...] = x_ref[:, D:2 * D].astype(BF16)

    tr = _row_tile(T, x.shape[1])
    o_spec = pl.BlockSpec((tr, D), lambda i: (i, 0))
    return pl.pallas_call(
        body, name=name, grid=(T // tr,),
        in_specs=[pl.BlockSpec((tr, W), lambda i: (i, 0)), pl.BlockSpec((1, BLK), lambda i: (0, 0))],
        out_specs=[o_spec] * (2 if tail else 1), out_shape=[_sds((T, D), BF16)] * (2 if tail else 1),
        compiler_params=_params(("parallel",)),
    )(x, g)


def _head_norm_bwd(x, dys, g, n_heads, name, tails=()):
    T = x.shape[0]
    D = n_heads * BLK
    n_dy, n_tail = len(dys), len(tails)
    W = 2 * D if tails else D

    def body(*refs):
        x_ref, dy_refs, tail_refs = refs[0], refs[1:1 + n_dy], refs[1 + n_dy:1 + n_dy + n_tail]
        g_ref, dx_ref, dg_ref = refs[1 + n_dy + n_tail:]

        @pl.when(pl.program_id(0) == 0)
        def _():
            dg_ref[...] = jnp.zeros_like(dg_ref)

        tot = jnp.zeros((1, BLK), F32)
        for h in range(n_heads):
            cols = slice(h * BLK, (h + 1) * BLK)
            xv = x_ref[:, cols]
            dyv = dy_refs[0][:, cols]
            for r_ in dy_refs[1:]:
                dyv = dyv + r_[:, cols]
            r = lax.rsqrt(jnp.mean(xv * xv, axis=-1, keepdims=True) + EPS)
            n = xv * r
            dn = dyv * g_ref[...]
            dx_ref[:, cols] = (r * (dn - n * jnp.mean(dn * n, axis=-1, keepdims=True))).astype(BF16)
            tot = tot + jnp.sum(dyv * n, axis=0, keepdims=True)
        dg_ref[0:1, :] += tot
        if n_tail:
            tv = tail_refs[0][...]
            for r_ in tail_refs[1:]:
                tv = tv + r_[...]
            dx_ref[:, D:] = tv.astype(BF16)

    tr = _row_tile(T, 2 * D, 256 * 1024)
    d_spec = pl.BlockSpec((tr, D), lambda i: (i, 0))
    return pl.pallas_call(
        body, name=name, grid=(T // tr,),
        in_specs=[d_spec] * (1 + n_dy + n_tail) + [pl.BlockSpec((1, BLK), lambda i: (0, 0))],
        out_specs=[pl.BlockSpec((tr, W), lambda i: (i, 0)), pl.BlockSpec((8, BLK), lambda i: (0, 0))],
        out_shape=[_sds((T, W), BF16), _sds((8, BLK), F32)],
        compiler_params=_params(("arbitrary",)),
    )(x, *dys, *tails, g)


def _fcum_fwd(fl, bf, name):
    T = fl.shape[0]

    def body(fl_ref, b_ref, o_ref, carry_ref):
        @pl.when(pl.program_id(0) == 0)
        def _():
            carry_ref[...] = jnp.zeros_like(carry_ref)

        z = fl_ref[...] + b_ref[...]
        logf = jnp.minimum(z, 0.0) - jnp.log(1.0 + jnp.exp(-jnp.abs(z)))
        row = lax.broadcasted_iota(jnp.int32, (BLK, BLK), 0)
        col = lax.broadcasted_iota(jnp.int32, (BLK, BLK), 1)
        tri = (col <= row).astype(F32)
        run = jnp.dot(tri, logf, preferred_element_type=F32, precision=lax.Precision.HIGHEST) + carry_ref[0:1, :]
        o_ref[...] = run
        carry_ref[0:1, :] = run[BLK - 1:BLK, :]

    return _rowcall(body, name=name, tr=BLK, row_ins=[fl], full_ins=[bf], row_outs=[_sds((T, BLK), F32)],
                    scratch=[pltpu.VMEM((8, BLK), F32)])[0]


def _fcum_bwd(dfs, fl, bf, name):
    T = fl.shape[0]
    n_df = len(dfs)

    def body(*refs):
        df_refs = refs[:n_df]
        fl_ref, b_ref, dfl_ref, dbias_ref, carry_ref = refs[n_df:]

        @pl.when(pl.program_id(0) == 0)
        def _():
            carry_ref[...] = jnp.zeros_like(carry_ref)
            dbias_ref[...] = jnp.zeros_like(dbias_ref)

        dfc = df_refs[0][...]
        for r_ in df_refs[1:]:
            dfc = dfc + r_[...]
        row = lax.broadcasted_iota(jnp.int32, (BLK, BLK), 0)
        col = lax.broadcasted_iota(jnp.int32, (BLK, BLK), 1)
        tri = (col >= row).astype(F32)
        suffix = jnp.dot(tri, dfc, preferred_element_type=F32, precision=lax.Precision.HIGHEST) + carry_ref[0:1, :]
        carry_ref[0:1, :] = suffix[0:1, :]
        z = fl_ref[...] + b_ref[...]
        dfl = suffix / (1.0 + jnp.exp(z))
        dfl_ref[...] = dfl.astype(BF16)
        dbias_ref[0:1, :] += jnp.sum(dfl, axis=0, keepdims=True)

    return _rowcall(body, name=name, tr=BLK, row_ins=[*dfs, fl], full_ins=[bf], reverse=True,
                    row_outs=[_sds((T, BLK), BF16)], acc_outs=[_sds((8, BLK), F32)],
                    scratch=[pltpu.VMEM((8, BLK), F32)])


def _attn_tile(T):
    return min(T, 512)


def _attn_fwd(q, k, v, fk, n_heads, name):
    T = q.shape[0]
    tq = tk = _attn_tile(T)
    nkb = T // tk
    inv_sqrt = 1.0 / float(math.sqrt(BLK))

    def body(q_ref, k_ref, v_ref, fk_ref, o_ref, o32_ref, lse_ref):
        i = pl.program_id(1)
        qv = q_ref[...]

        def block(j, carry, diagonal):
            m, l, acc = carry
            rows = pl.ds(pl.multiple_of(j * tk, tk), tk)
            kj, vj = k_ref[rows, :], v_ref[rows, :]
            s = lax.dot_general(qv, kj, (((1,), (1,)), ((), ())), preferred_element_type=F32) * inv_sqrt
            s = s - fk_ref[j]
            if diagonal:
                s = jnp.where(lax.broadcasted_iota(jnp.int32, (tq, tk), 1)
                              <= lax.broadcasted_iota(jnp.int32, (tq, tk), 0), s, NEG_BIG)
            m_new = jnp.maximum(m, jnp.max(s, axis=-1, keepdims=True))
            alpha = jnp.exp(m - m_new)
            p = jnp.exp(s - m_new)
            l = alpha * l + jnp.sum(p, axis=-1, keepdims=True)
            acc = alpha * acc + jnp.dot(p.astype(BF16), vj, preferred_element_type=F32)
            return m_new, l, acc

        init = (jnp.full((tq, 1), NEG_BIG, F32), jnp.zeros((tq, 1), F32), jnp.zeros((tq, BLK), F32))
        carry = lax.fori_loop(0, i, lambda j, c: block(j, c, False), init)
        m, l, acc = block(i, carry, True)
        out = acc / l
        o_ref[...] = out.astype(BF16)
        o32_ref[...] = out
        lse_ref[...] = m + jnp.log(l)

    return pl.pallas_call(
        body, name=name, grid=(n_heads, T // tq),
        in_specs=[pl.BlockSpec((tq, BLK), lambda h, i: (i, h)),
                  pl.BlockSpec((T, BLK), lambda h, i: (0, h)),
                  pl.BlockSpec((T, BLK), lambda h, i: (0, h)),
                  pl.BlockSpec((None, nkb, 1, tk), lambda h, i: (h, 0, 0, 0))],
        out_specs=[pl.BlockSpec((tq, BLK), lambda h, i: (i, h)),
                   pl.BlockSpec((tq, BLK), lambda h, i: (i, h)),
                   pl.BlockSpec((None, tq, 1), lambda h, i: (h, i, 0))],
        out_shape=[_sds((T, n_heads * BLK), BF16), _sds((T, n_heads * BLK), F32), _sds((n_heads, T, 1), F32)],
        compiler_params=_params(("parallel", "arbitrary")),
    )(q, k, v, fk)


def _attn_bwd(q, k, v, o, do, lse, fk, n_heads, name):
    T = q.shape[0]
    tq = tk = _attn_tile(T)
    nkb = T // tk
    nq = T // tq
    inv_sqrt = 1.0 / float(math.sqrt(BLK))
    tn_dims = (((0,), (0,)), ((), ()))
    nt_dims = (((1,), (1,)), ((), ()))

    def body(q_ref, k_ref, v_ref, o_ref, do_ref, lse_ref, fk_ref, dq_ref, dk_ref, dv_ref, dfq_ref, dfk_ref, delta_ref):
        j = pl.program_id(1)

        @pl.when(j == 0)
        def _():
            delta_ref[...] = jnp.sum(do_ref[...].astype(F32) * o_ref[...], axis=1, keepdims=True)
            dq_ref[...] = jnp.zeros_like(dq_ref)
            dfq_ref[...] = jnp.zeros_like(dfq_ref)

        kj, vj, fkv = k_ref[...], v_ref[...], fk_ref[...]

        def step(i, carry, diagonal):
            dk, dv, dfk = carry
            rows = pl.ds(pl.multiple_of(i * tq, tq), tq)
            qi, doi = q_ref[rows, :], do_ref[rows, :]
            s = lax.dot_general(qi, kj, nt_dims, preferred_element_type=F32) * inv_sqrt - fkv
            if diagonal:
                s = jnp.where(lax.broadcasted_iota(jnp.int32, (tq, tk), 1)
                              <= lax.broadcasted_iota(jnp.int32, (tq, tk), 0), s, NEG_BIG)
            p = jnp.exp(s - lse_ref[rows, :])
            dv = dv + lax.dot_general(p.astype(BF16), doi, tn_dims, preferred_element_type=F32)
            dp = lax.dot_general(doi, vj, nt_dims, preferred_element_type=F32)
            ds = p * (dp - delta_ref[rows, :])
            dsb = ds.astype(BF16)
            dq_ref[rows, :] += jnp.dot(dsb, kj, preferred_element_type=F32) * inv_sqrt
            dk = dk + lax.dot_general(dsb, qi, tn_dims, preferred_element_type=F32)
            dfq_ref[rows, :] += jnp.sum(ds, axis=1, keepdims=True)
            dfk = dfk - jnp.sum(ds, axis=0, keepdims=True)
            return dk, dv, dfk

        init = (jnp.zeros((tk, BLK), F32), jnp.zeros((tk, BLK), F32), jnp.zeros((1, tk), F32))
        carry = step(j, init, True)
        dk, dv, dfk = lax.fori_loop(j + 1, nq, lambda i, c: step(i, c, False), carry)
        dk_ref[...] = dk * inv_sqrt
        dv_ref[...] = dv
        dfk_ref[...] = dfk

    head_col = lambda h, j: (0, h)
    return pl.pallas_call(
        body, name=name, grid=(n_heads, nkb),
        in_specs=[pl.BlockSpec((T, BLK), head_col),
                  pl.BlockSpec((tk, BLK), lambda h, j: (j, h)),
                  pl.BlockSpec((tk, BLK), lambda h, j: (j, h)),
                  pl.BlockSpec((T, BLK), head_col),
                  pl.BlockSpec((T, BLK), head_col),
                  pl.BlockSpec((None, T, 1), lambda h, j: (h, 0, 0)),
                  pl.BlockSpec((None, None, 1, tk), lambda h, j: (h, j, 0, 0))],
        out_specs=[pl.BlockSpec((T, BLK), head_col),
                   pl.BlockSpec((tk, BLK), lambda h, j: (j, h)),
                   pl.BlockSpec((tk, BLK), lambda h, j: (j, h)),
                   pl.BlockSpec((None, T, 1), lambda h, j: (h, 0, 0)),
                   pl.BlockSpec((None, None, 1, tk), lambda h, j: (h, j, 0, 0))],
        out_shape=[_sds((T, n_heads * BLK), F32), _sds((T, n_heads * BLK), F32), _sds((T, n_heads * BLK), F32),
                   _sds((n_heads, T, 1), F32), _sds((n_heads, nkb, 1, tk), F32)],
        scratch_shapes=[pltpu.VMEM((T, 1), F32)],
        compiler_params=_params(("parallel", "arbitrary")),
    )(q, k, v, o, do, lse, fk)


_INV_SQRT2 = 1.0 / math.sqrt(2.0)
_INV_SQRT_2PI = 1.0 / math.sqrt(2.0 * math.pi)


def _gelu_parts(z):
    cdf = 0.5 * (1.0 + lax.erf(z * _INV_SQRT2))
    return cdf, z * cdf


def _mix_mask(transposed):
    row = lax.broadcasted_iota(jnp.int32, (BLK, BLK), 0) // CHUNK
    col = lax.broadcasted_iota(jnp.int32, (BLK, BLK), 1) // CHUNK
    return (row <= col) if transposed else (col <= row)


def _gmlp_mid_fwd(zpre, ln_g, ln_b, ws, bs_t, name):
    T, two_h = zpre.shape
    Hh = two_h // 2
    G = ws.shape[0]
    gd = Hh // G

    def body(z_ref, lg_ref, lb_ref, ws_ref, bs_ref, p_ref):
        _, zg = _gelu_parts(z_ref[...].astype(F32))
        u, v = zg[:, :Hh], zg[:, Hh:]
        mu = jnp.mean(v, axis=-1, keepdims=True)
        vc = v - mu
        rstd = lax.rsqrt(jnp.mean(vc * vc, axis=-1, keepdims=True) + EPS)
        vn = ((vc * rstd) * lg_ref[...] + lb_ref[...]).astype(BF16)
        mask = _mix_mask(False)
        for g in range(G):
            wm = jnp.where(mask, ws_ref[g], 0.0).astype(BF16)
            sv = jnp.dot(wm, vn[:, g * gd:(g + 1) * gd], preferred_element_type=F32) + bs_ref[:, g:g + 1]
            p_ref[:, g * gd:(g + 1) * gd] = (u[:, g * gd:(g + 1) * gd] * sv).astype(BF16)

    return _rowcall(body, name=name, tr=BLK, row_ins=[zpre], full_ins=[ln_g, ln_b, ws, bs_t],
                    row_outs=[_sds((T, Hh), BF16)])[0]


def _gmlp_mid_bwd(zpre, dp, ln_g, ln_b, ws, ws_t, bs_t, name):
    T, two_h = zpre.shape
    Hh = two_h // 2
    G = ws.shape[0]
    gd = Hh // G
    nt_dims = (((1,), (1,)), ((), ()))

    def body(z_ref, dp_ref, lg_ref, lb_ref, ws_ref, wst_ref, bs_ref, dz_ref, dws_ref, dbs_ref, dlg_ref, dlb_ref,
             dvn_ref):
        @pl.when(pl.program_id(0) == 0)
        def _():
            dws_ref[...] = jnp.zeros_like(dws_ref)
            dbs_ref[...] = jnp.zeros_like(dbs_ref)
            dlg_ref[...] = jnp.zeros_like(dlg_ref)
            dlb_ref[...] = jnp.zeros_like(dlb_ref)

        z = z_ref[...].astype(F32)
        cdf, zg = _gelu_parts(z)
        dgelu = cdf + z * (jnp.exp(-0.5 * z * z) * _INV_SQRT_2PI)
        u, v = zg[:, :Hh], zg[:, Hh:]
        mu = jnp.mean(v, axis=-1, keepdims=True)
        vc = v - mu
        rstd = lax.rsqrt(jnp.mean(vc * vc, axis=-1, keepdims=True) + EPS)
        vhat = vc * rstd
        vn = (vhat * lg_ref[...] + lb_ref[...]).astype(BF16)
        mask, mask_t = _mix_mask(False), _mix_mask(True)
        lane = lax.broadcasted_iota(jnp.int32, (BLK, BLK), 1)
        dbs = jnp.zeros((BLK, BLK), F32)
        for g in range(G):
            cols = slice(g * gd, (g + 1) * gd)
            wm = jnp.where(mask, ws_ref[g], 0.0).astype(BF16)
            wm_t = jnp.where(mask_t, wst_ref[g], 0.0).astype(BF16)
            vn_g = vn[:, cols]
            sv = jnp.dot(wm, vn_g, preferred_element_type=F32) + bs_ref[:, g:g + 1]
            dp_g = dp_ref[:, cols].astype(F32)
            dz_ref[:, cols] = ((dp_g * sv) * dgelu[:, cols]).astype(BF16)
            dsv = dp_g * u[:, cols]
            dsv_b = dsv.astype(BF16)
            dbs = dbs + jnp.where(lane == g, jnp.sum(dsv, axis=1, keepdims=True), 0.0)
            dws_ref[g] += jnp.where(mask, lax.dot_general(dsv_b, vn_g, nt_dims, preferred_element_type=F32), 0.0)
            dvn_ref[:, cols] = jnp.dot(wm_t, dsv_b, preferred_element_type=F32)
        dbs_ref[...] += dbs
        dvn = dvn_ref[...]
        dlg_ref[0:1, :] += jnp.sum(dvn * vhat, axis=0, keepdims=True)
        dlb_ref[0:1, :] += jnp.sum(dvn, axis=0, keepdims=True)
        dvh = dvn * lg_ref[...]
        dv = rstd * (dvh - jnp.mean(dvh, axis=-1, keepdims=True) - vhat * jnp.mean(dvh * vhat, axis=-1, keepdims=True))
        dz_ref[:, Hh:] = (dv * dgelu[:, Hh:]).astype(BF16)

    return _rowcall(body, name=name, tr=BLK, row_ins=[zpre, dp], full_ins=[ln_g, ln_b, ws, ws_t, bs_t],
                    row_outs=[_sds((T, two_h), BF16)],
                    acc_outs=[_sds((G, BLK, BLK), F32), _sds((BLK, BLK), F32), _sds((8, Hh), F32), _sds((8, Hh), F32)],
                    scratch=[pltpu.VMEM((BLK, Hh), F32)])


def _mods(c_all, w, layer, bias, name):
    nb, K = c_all.shape
    N = w.shape[-1]
    tn = _tile(N, 512)

    def body(c_ref, w_ref, b_ref, o_ref):
        cv = c_ref[...]
        sc = cv / (1.0 + jnp.exp(-cv))
        o_ref[...] = jnp.dot(sc, w_ref[...], preferred_element_type=F32, precision=lax.Precision.HIGHEST) + b_ref[...]

    return pl.pallas_call(
        body, name=name, grid=(N // tn,),
        in_specs=[pl.BlockSpec((nb, K), lambda j: (0, 0)),
                  pl.BlockSpec((None, K, tn), lambda j: (layer, 0, j)),
                  pl.BlockSpec((1, tn), lambda j: (0, j))],
        out_specs=pl.BlockSpec((nb, tn), lambda j: (0, j)), out_shape=_sds((nb, N), F32),
        compiler_params=_params(("parallel",)),
    )(c_all, w, bias)


def _sum_slabs(x, name):
    _, R, C = x.shape
    tr = _row_tile(R, C * N_DEV)

    def body(x_ref, o_ref):
        acc = x_ref[0]
        for s in range(1, N_DEV):
            acc = acc + x_ref[s]
        o_ref[...] = acc

    return pl.pallas_call(
        body, name=name, grid=(R // tr,),
        in_specs=[pl.BlockSpec((N_DEV, tr, C), lambda i: (0, i, 0))],
        out_specs=pl.BlockSpec((tr, C), lambda i: (i, 0)), out_shape=_sds((R, C), F32),
        compiler_params=_params(("parallel",)),
    )(x)


def _adamw_math(w, g, m, v):
    m = ADAM_B1 * m + (1.0 - ADAM_B1) * g
    v = ADAM_B2 * v + (1.0 - ADAM_B2) * (g * g)
    m_hat = m / (1.0 - ADAM_B1 ** ADAM_STEP)
    v_hat = v / (1.0 - ADAM_B2 ** ADAM_STEP)
    delta = -ADAM_LR * (m_hat / (jnp.sqrt(v_hat) + ADAM_EPS) + ADAM_WD * w)
    return delta, m, v


class _AdamStack:
    def __init__(self, w, m, v, name):
        self.w, self.m, self.v, self.name = w, m, v, name
        self.L, self.R, self.C = w.shape
        self.tr = _row_tile(self.R, self.C, 128 * 1024)
        self.outs = None

    def _layer(self, l, who, srcs, src_specs, make_grad):
        n_src = len(srcs)
        L, R, C, tr = self.L, self.R, self.C, self.tr
        wspec = pl.BlockSpec((None, tr, C), lambda i, who_ref: (l, i, 0))

        def body(who_ref, *refs):
            src_refs = refs[:n_src]
            w_ref, m_ref, v_ref = refs[n_src:n_src + 3]
            g_ref, d_ref, m2_ref, v2_ref = refs[-4:]
            g = make_grad(who_ref[0], *src_refs)
            delta, m2, v2 = _adamw_math(w_ref[...], g, m_ref[...], v_ref[...])
            g_ref[...] = g
            d_ref[...] = delta
            m2_ref[...] = m2
            v2_ref[...] = v2

        prev = [] if self.outs is None else list(self.outs)
        aliases = {} if self.outs is None else {1 + n_src + 3 + t: t for t in range(4)}
        self.outs = pl.pallas_call(
            body, name=f"{self.name}_l{l}",
            grid_spec=pltpu.PrefetchScalarGridSpec(
                num_scalar_prefetch=1, grid=(R // tr,),
                in_specs=list(src_specs) + [wspec] * 3 + [_ANY] * len(prev), out_specs=[wspec] * 4),
            out_shape=[_sds((L, R, C), F32)] * 4,
            input_output_aliases=aliases,
            compiler_params=_params(("parallel",)),
        )(who, *srcs, self.w, self.m, self.v, *prev)

    def from_parts(self, l, who, landed, sent):
        tr, C = self.tr, self.C

        def make_grad(me, p_ref, own_ref):
            mine = own_ref[...].astype(F32)
            g = jnp.where(me == 0, mine, p_ref[0].astype(F32))
            for s in range(1, N_DEV):
                g = g + jnp.where(me == s, mine, p_ref[s].astype(F32))
            return g

        self._layer(l, who, [landed, sent],
                    [pl.BlockSpec((N_DEV, tr, C), lambda i, who_ref: (0, i, 0)),
                     pl.BlockSpec((None, tr, C), lambda i, who_ref: (who_ref[0], i, 0))], make_grad)

    def from_outer(self, l, who, sc_t, dmod):
        tr, C = self.tr, self.C

        def make_grad(me, s_ref, d_ref):
            g = s_ref[:, 0:1] * d_ref[0:1, :]
            for b in range(1, N_DEV):
                g = g + s_ref[:, b:b + 1] * d_ref[b:b + 1, :]
            return g

        self._layer(l, who, [sc_t, dmod], [pl.BlockSpec((tr, N_DEV), lambda i, who_ref: (i, 0)),
                                           pl.BlockSpec((N_DEV, C), lambda i, who_ref: (0, 0))], make_grad)


def _adamw_flat(w, g, m, v, name):
    R, C = w.shape
    tr = _row_tile(R, C, 128 * 1024)

    def body(w_ref, g_ref, m_ref, v_ref, d_ref, m2_ref, v2_ref):
        delta, m2, v2 = _adamw_math(w_ref[...], g_ref[...], m_ref[...], v_ref[...])
        d_ref[...] = delta
        m2_ref[...] = m2
        v2_ref[...] = v2

    spec = pl.BlockSpec((tr, C), lambda i: (i, 0))
    return pl.pallas_call(
        body, name=name, grid=(R // tr,), in_specs=[spec] * 4, out_specs=[spec] * 3,
        out_shape=[_sds((R, C), F32)] * 3, compiler_params=_params(("parallel",)),
    )(w, g, m, v)


def _pack(arrays):
    flat = jnp.concatenate([a.reshape(-1).astype(F32) for a in arrays])
    pad = (-flat.shape[0]) % (8 * BLK)
    if pad:
        flat = jnp.concatenate([flat, jnp.zeros((pad,), F32)])
    return flat.reshape(-1, BLK)


def _unpack(buf, shapes, lead=()):
    sizes = [int(math.prod(s)) for s in shapes]
    out, off = [], 0
    if all(n % BLK == 0 for n in sizes):
        for s, n in zip(shapes, sizes):
            out.append(buf[..., off // BLK:(off + n) // BLK, :].reshape(tuple(lead) + tuple(s)))
            off += n
        return out
    flat = buf.reshape(tuple(lead) + (-1,))
    for s, n in zip(shapes, sizes):
        out.append(flat[..., off:off + n].reshape(tuple(lead) + tuple(s)))
        off += n
    return out


def _row(vec):
    return vec.reshape(1, -1)


def _shard_of(full, axis, me, size):
    return lax.dynamic_slice_in_dim(full, me * size, size, axis=axis)


def kernel(x, c, ada_w, ada_b, norm_g, mlp_w1, mlp_w2, gmlp_w_in, gmlp_ln_g, gmlp_ln_b, gmlp_ws, gmlp_bs, gmlp_w_out, kv_norm_g, kv_ada_w, kv_ada_b, w_kv, k_norm_g, w_f, b_f, attn_wq, q_norm_g, attn_wo, loss_target, m_ada_w, m_ada_b, m_norm_g, m_mlp_w1, m_mlp_w2, m_gmlp_w_in, m_gmlp_ln_g, m_gmlp_ln_b, m_gmlp_ws, m_gmlp_bs, m_gmlp_w_out, m_kv_norm_g, m_kv_ada_w, m_kv_ada_b, m_w_kv, m_k_norm_g, m_w_f, m_b_f, m_attn_wq, m_q_norm_g, m_attn_wo, v_ada_w, v_ada_b, v_norm_g, v_mlp_w1, v_mlp_w2, v_gmlp_w_in, v_gmlp_ln_g, v_gmlp_ln_b, v_gmlp_ws, v_gmlp_bs, v_gmlp_w_out, v_kv_norm_g, v_kv_ada_w, v_kv_ada_b, v_w_kv, v_k_norm_g, v_w_f, v_b_f, v_attn_wq, v_q_norm_g, v_attn_wo):
    given = dict(locals())
    weights = {n: given[n] for n in WEIGHT_NAMES}
    mom_m = {n: given["m_" + n] for n in WEIGHT_NAMES}
    mom_v = {n: given["v_" + n] for n in WEIGHT_NAMES}

    me = _my_index()
    T, D = x.shape[1], x.shape[2]
    depth = ada_w.shape[0]
    n_a = gmlp_w_in.shape[0]
    n_heads = b_f.shape[0]
    G = gmlp_ws.shape[1]
    Hh = gmlp_ln_g.shape[1] * N_DEV
    mod_cols = ada_w.shape[2]
    kv_cols = kv_ada_w.shape[1]
    x0 = x.reshape(T, D)
    target = loss_target.reshape(T, D)

    small_in = [c, norm_g, gmlp_ln_g, gmlp_ln_b, w_f]
    small_shapes = [a.shape for a in small_in]
    got = _all_to_all(_pack(small_in), "gather_small_inputs", bcast=True)
    c_all, norm_g_sh, ln_g_sh, ln_b_sh, w_f_sh = _unpack(got, small_shapes, lead=(N_DEV,))
    c_all = c_all.reshape(N_DEV, D)
    norm_g_full = jnp.moveaxis(norm_g_sh, 0, 2).reshape(depth, 2, D)
    ln_g_full = jnp.moveaxis(ln_g_sh, 0, 1).reshape(n_a, Hh)
    ln_b_full = jnp.moveaxis(ln_b_sh, 0, 1).reshape(n_a, Hh)
    w_f_full = w_f_sh.reshape(D, n_heads)
    w_f_pad = jnp.pad(w_f_full, ((0, 0), (0, BLK - n_heads))).astype(BF16)
    b_f_pad = jnp.pad(b_f, (0, BLK - n_heads)).reshape(1, BLK)

    mod_parts = []
    for l in range(depth):
        bias = _shard_of(ada_b[l], 0, me, mod_cols).reshape(1, mod_cols)
        mod_parts.append(_mods(c_all, ada_w, l, bias, f"mods_l{l}"))
    kv_bias = _shard_of(kv_ada_b, 0, me, kv_cols).reshape(1, kv_cols)
    mod_parts.append(_mods(c_all, kv_ada_w.reshape(1, D, kv_cols), 0, kv_bias, "mods_kv"))
    mods_mine = jnp.concatenate(mod_parts, axis=1)
    mod_width = mods_mine.shape[1]
    mods_pack = jnp.pad(mods_mine, ((0, 0), (0, (-mod_width) % (8 * BLK)))).reshape(N_DEV, -1, BLK)
    mods_got = _all_to_all(mods_pack, "exchange_mods").reshape(N_DEV, -1)[:, :mod_width]
    mods = []
    for l in range(depth):
        mods.append(mods_got[:, l * mod_cols:(l + 1) * mod_cols].reshape(N_MOD, D))
    kv_mod = mods_got[:, depth * mod_cols:].reshape(2, D)
    silu_all = c_all / (1.0 + jnp.exp(-c_all))

    assert 1 <= n_a < depth
    who = me.astype(jnp.int32).reshape(1)
    big = {"mlp_w1": mlp_w1, "mlp_w2": mlp_w2, "gmlp_w_in": gmlp_w_in, "gmlp_w_out": gmlp_w_out,
           "w_kv": w_kv.reshape((1,) + w_kv.shape), "attn_wq": attn_wq, "attn_wo": attn_wo}
    groups = [[("gmlp_w_in", 0)], [("gmlp_w_out", 0), ("mlp_w1", 0), ("mlp_w2", 0)]]
    for l in range(1, depth):
        if l < n_a:
            groups.append([("gmlp_w_in", l), ("gmlp_w_out", l), ("mlp_w1", l), ("mlp_w2", l)])
        else:
            first = [("w_kv", 0)] if l == n_a else []
            groups.append(first + [("attn_wq", l - n_a), ("attn_wo", l - n_a), ("mlp_w1", l), ("mlp_w2", l)])
    tokens = []

    def behind_starts():
        out = tuple(tokens)
        tokens.clear()
        return out

    over_ici = []
    for gi, grp in enumerate(groups):
        placed = [_place_shard(big[n], l, who, f"place_{n}_l{l}") for n, l in grp]
        over_ici.append(_gather_start(placed, 1, f"gather_g{gi}_ici_start"))
        tokens.append(over_ici[gi][3])
    to_sibling = {}
    gw = {}

    def gather_forward(gi, after):
        landed = _gather_wait(over_ici[gi], 1, after, f"gather_g{gi}_ici_wait")
        to_sibling[gi] = _gather_start(landed, 2, f"gather_g{gi}_d2d_start")
        tokens.append(to_sibling[gi][3])

    def gather_finish(gi, after):
        for key, arr in zip(groups[gi], _gather_wait(to_sibling[gi], 2, after, f"gather_g{gi}_d2d_wait")):
            gw[key] = arr

    tkk = _attn_tile(T)

    saved = []
    xs = x0
    pending = None
    kv = None
    for l in range(depth):
        sh1, sc1, g1, sh2, sc2, g2 = [_row(mods[l][t]) for t in range(N_MOD)]
        ng1, ng2 = _row(norm_g_full[l, 0]), _row(norm_g_full[l, 1])
        st = dict(sc1=sc1, g1=g1, sc2=sc2, g2=g2, ng1=ng1, ng2=ng2)
        if pending is None:
            h1 = _norm_mod(xs, ng1, sc1, sh1, f"norm1_l{l}")
            gather_forward(0, h1)
            gather_finish(0, h1)
        else:
            gather_finish(l + 1, pending[0])
            xs, h1 = _res_norm_mod(xs, pending[0], pending[1], ng1, sc1, sh1, f"res_norm1_l{l}")
        st["x_in"], st["h1"] = xs, h1
        if l < n_a:
            a = l
            zpre = _mm(h1, gw["gmlp_w_in", a], bmode="col", out_dtypes=(BF16,), name=f"gmlp_in_l{l}",
                       deps=behind_starts())
            if l == 0:
                gather_forward(1, zpre)
            bs_t = gmlp_bs[a].T
            p = _gmlp_mid_fwd(zpre, _row(ln_g_full[a]), _row(ln_b_full[a]), gmlp_ws[a], bs_t, f"gmlp_mid_l{l}")
            if l == 0:
                gather_finish(1, p)
            y = _mm(p, gw["gmlp_w_out", a], bmode="row", name=f"gmlp_out_l{l}", deps=behind_starts())
            st.update(zpre=zpre, p=p)
        else:
            if kv is None:
                kv_ng, kv_sh, kv_sc = _row(kv_norm_g), _row(kv_mod[0]), _row(kv_mod[1])
                hkv = _norm_mod(xs, kv_ng, kv_sc, kv_sh, "norm_kv")
                kvp = _mm(hkv, gw["w_kv", 0], bmode="col", name="kv_proj", deps=behind_starts())
                kk, vv = _head_norm(kvp, _row(k_norm_g), n_heads, "k_norm", tail=True)
                fl = _mm(hkv, w_f_pad, name="gate_logits")
                fcum = _fcum_fwd(fl, b_f_pad, "fcum")
                fk = fcum[:, :n_heads].T.reshape(n_heads, T // tkk, 1, tkk)
                kv = dict(x=xs, hkv=hkv, kvp=kvp, k=kk, v=vv, fl=fl, fcum=fcum, fk=fk, ng=kv_ng, sc=kv_sc)
            bl = l - n_a
            qp = _mm(h1, gw["attn_wq", bl], bmode="row", name=f"q_proj_l{l}", deps=behind_starts())
            q = _head_norm(qp, _row(q_norm_g[bl]), n_heads, f"q_norm_l{l}")[0]
            o, o32, lse = _attn_fwd(q, kv["k"], kv["v"], kv["fk"], n_heads, f"attn_l{l}")
            y = _mm(o, gw["attn_wo", bl], bmode="row", name=f"attn_out_l{l}")
            st.update(qp=qp, q=q, o=o, o32=o32, lse=lse)
        xs, h2 = _res_norm_mod(xs, y, g1, ng2, sc2, sh2, f"res_norm2_l{l}")
        a_pre, s_act = _mm(h2, gw["mlp_w1", l], bmode="col", out_dtypes=(BF16, BF16), epilogue=_relu2_epilogue,
                           name=f"mlp_up_l{l}")
        if l + 1 < depth:
            gather_forward(l + 2, a_pre)
        mo = _mm(s_act, gw["mlp_w2", l], bmode="row", name=f"mlp_down_l{l}", deps=behind_starts())
        st.update(y=y, x_mid=xs, h2=h2, a_pre=a_pre, s=s_act, m=mo)
        saved.append(st)
        pending = (mo, g2)

    dx, loss_row = _res_loss(xs, pending[0], pending[1], target, "loss")
    loss = lax.psum(loss_row[0, 0], ("x", "y", "c"))

    started = {}

    def scatter(dw_slabs, key, idx):
        started[(key, idx)] = _a2a_start(dw_slabs, f"scatter_{key}_l{idx}_start")
        tokens.append(started[(key, idx)][4])

    d_mod = [None] * depth
    d_norm_g = [None] * depth
    d_ln_g, d_ln_b, d_ws, d_bs = [None] * n_a, [None] * n_a, [None] * n_a, [None] * n_a
    d_qg = [None] * (depth - n_a)
    dk_list, dv_list, dfk_list = [], [], []
    small = {}

    for l in reversed(range(depth)):
        st = saved[l]
        dm, dg2 = _gate_bwd(dx, st["m"], st["g2"], f"gate2_bwd_l{l}")
        da = _mm(dm, gw["mlp_w2", l], tb=True, bmode="row", out_dtypes=(BF16,), epilogue=_relu2_bwd_epilogue,
                 extra=(st["a_pre"],), name=f"mlp_down_dx_l{l}")
        dw2 = _mm(st["s"], dm, ta=True, out_dtypes=(BF16,), name=f"mlp_down_dw_l{l}", deps=behind_starts())
        scatter(dw2.reshape(N_DEV, -1, D), "mlp_w2", l)
        dw1 = _mm(st["h2"], da, ta=True, out_mode="col", out_dtypes=(BF16,), name=f"mlp_up_dw_l{l}",
                  deps=behind_starts())
        scatter(dw1, "mlp_w1", l)
        dh2 = _mm(da, gw["mlp_w1", l], tb=True, bmode="col", out_dtypes=(BF16,), name=f"mlp_up_dx_l{l}",
                  deps=behind_starts())
        dx, sums2 = _norm_mod_bwd(st["x_mid"], [dh2], dx, st["ng2"], st["sc2"], f"norm2_bwd_l{l}")
        dy, dg1 = _gate_bwd(dx, st["y"], st["g1"], f"gate1_bwd_l{l}")
        if l < n_a:
            a = l
            dwo = _mm(st["p"], dy, ta=True, out_dtypes=(BF16,), name=f"gmlp_out_dw_l{l}", deps=behind_starts())
            scatter(dwo.reshape(N_DEV, -1, D), "gmlp_w_out", a)
            dp = _mm(dy, gw["gmlp_w_out", a], tb=True, bmode="row", out_dtypes=(BF16,), name=f"gmlp_out_dx_l{l}",
                     deps=behind_starts())
            dz, d_ws[a], dbs_t, dlg, dlb = _gmlp_mid_bwd(
                st["zpre"], dp, _row(ln_g_full[a]), _row(ln_b_full[a]), gmlp_ws[a],
                jnp.swapaxes(gmlp_ws[a], 1, 2), gmlp_bs[a].T, f"gmlp_mid_bwd_l{l}")
            d_bs[a], d_ln_g[a], d_ln_b[a] = dbs_t[:, :G].T, dlg[0], dlb[0]
            dwi = _mm(st["h1"], dz, ta=True, out_mode="col", out_dtypes=(BF16,), name=f"gmlp_in_dw_l{l}",
                      deps=behind_starts())
            scatter(dwi, "gmlp_w_in", a)
            dh1s = [_mm(dz, gw["gmlp_w_in", a], tb=True, bmode="col", out_dtypes=(BF16,), name=f"gmlp_in_dx_l{l}",
                        deps=behind_starts())]
        else:
            bl = l - n_a
            dwo = _mm(st["o"], dy, ta=True, out_dtypes=(BF16,), name=f"attn_out_dw_l{l}", deps=behind_starts())
            scatter(dwo.reshape(N_DEV, -1, D), "attn_wo", bl)
            do = _mm(dy, gw["attn_wo", bl], tb=True, bmode="row", out_dtypes=(BF16,), name=f"attn_out_dx_l{l}",
                     deps=behind_starts())
            dq, dk, dv, dfq, dfk = _attn_bwd(st["q"], kv["k"], kv["v"], st["o32"], do, st["lse"], kv["fk"],
                                             n_heads, f"attn_bwd_l{l}")
            dfk_list += [dfq, dfk]
            dk_list.append(dk)
            dv_list.append(dv)
            dqp, dqg = _head_norm_bwd(st["qp"], [dq], _row(q_norm_g[bl]), n_heads, f"q_norm_bwd_l{l}")
            d_qg[bl] = dqg[0]
            dwq = _mm(st["h1"], dqp, ta=True, out_dtypes=(BF16,), name=f"q_proj_dw_l{l}", deps=behind_starts())
            scatter(dwq.reshape(N_DEV, -1, D), "attn_wq", bl)
            dh1s = [_mm(dqp, gw["attn_wq", bl], tb=True, bmode="row", out_dtypes=(BF16,), name=f"q_proj_dx_l{l}",
                        deps=behind_starts())]
        dx, sums1 = _norm_mod_bwd(st["x_in"], dh1s, dx, st["ng1"], st["sc1"], f"norm1_bwd_l{l}")
        d_mod[l] = jnp.stack([sums1[0], sums1[1], dg1[0], sums2[0], sums2[1], dg2[0]])
        d_norm_g[l] = jnp.stack([sums1[2], sums2[2]])
        if l == n_a:
            dkvp, dkg = _head_norm_bwd(kv["kvp"], dk_list, _row(k_norm_g), n_heads, "k_norm_bwd", tails=dv_list)
            dfc = [jnp.pad(d.reshape(n_heads, T).T, ((0, 0), (0, BLK - n_heads))) for d in dfk_list]
            dfl, dbf = _fcum_bwd(dfc, kv["fl"], b_f_pad, "fcum_bwd")
            dwkv = _mm(kv["hkv"], dkvp, ta=True, out_mode="col", out_dtypes=(BF16,), name="kv_proj_dw",
                       deps=behind_starts())
            scatter(dwkv, "w_kv", 0)
            dwf = _mm(kv["hkv"], dfl, ta=True, name="gate_logits_dw", deps=behind_starts())
            dh_a = _mm(dkvp, gw["w_kv", 0], tb=True, bmode="col", out_dtypes=(BF16,), name="kv_proj_dx")
            dh_b = _mm(dfl, w_f_pad, tb=True, out_dtypes=(BF16,), name="gate_logits_dx")
            dx, sums_kv = _norm_mod_bwd(kv["x"], [dh_a, dh_b], dx, kv["ng"], kv["sc"], "norm_kv_bwd")
            small.update(d_kv_mod=jnp.stack([sums_kv[0], sums_kv[1]]), d_kv_norm_g=sums_kv[2], d_k_norm_g=dkg[0],
                         d_w_f=dwf[:, :n_heads], d_b_f=dbf[0, :n_heads])

    grad_x = dx.reshape(x.shape)

    contrib = [jnp.stack(d_mod).reshape(depth, N_MOD * D), small["d_kv_mod"].reshape(-1),
               jnp.stack(d_norm_g), jnp.stack(d_ln_g), jnp.stack(d_ln_b), jnp.stack(d_ws), jnp.stack(d_bs),
               small["d_kv_norm_g"], small["d_k_norm_g"], small["d_w_f"], small["d_b_f"], jnp.stack(d_qg)]
    contrib_shapes = [a.shape for a in contrib]
    all_contrib = _all_to_all(_pack(contrib), "gather_small_grads", bcast=True)
    summed = _unpack(_sum_slabs(all_contrib, "sum_small_grads"), contrib_shapes)
    (g_ada_b, g_kv_ada_b, g_norm_g_full, g_ln_g_full, g_ln_b_full, g_ws, g_bs, g_kv_norm_g, g_k_norm_g, g_w_f_full,
     g_b_f, g_q_norm_g) = summed
    dmod_all, dkvmod_all = _unpack(all_contrib, contrib_shapes[:2], lead=(N_DEV,))

    grads = {
        "ada_b": g_ada_b, "kv_ada_b": g_kv_ada_b.reshape(kv_ada_b.shape),
        "norm_g": _shard_of(g_norm_g_full, 2, me, norm_g.shape[2]),
        "gmlp_ln_g": _shard_of(g_ln_g_full, 1, me, gmlp_ln_g.shape[1]),
        "gmlp_ln_b": _shard_of(g_ln_b_full, 1, me, gmlp_ln_b.shape[1]),
        "gmlp_ws": g_ws, "gmlp_bs": g_bs, "kv_norm_g": g_kv_norm_g, "k_norm_g": g_k_norm_g,
        "w_f": _shard_of(g_w_f_full, 0, me, w_f.shape[0]), "b_f": g_b_f, "q_norm_g": g_q_norm_g,
    }
    small_names = list(grads)
    small_w_shapes = [weights[n].shape for n in small_names]
    d_pack, m_pack, v_pack = _adamw_flat(_pack([weights[n] for n in small_names]), _pack([grads[n] for n in small_names]),
                                         _pack([mom_m[n] for n in small_names]), _pack([mom_v[n] for n in small_names]),
                                         "adamw_small")
    deltas = dict(zip(small_names, _unpack(d_pack, small_w_shapes)))
    new_m = dict(zip(small_names, _unpack(m_pack, small_w_shapes)))
    new_v = dict(zip(small_names, _unpack(v_pack, small_w_shapes)))

    def stack_of(n):
        lead = () if weights[n].ndim == 3 else (1,)
        return _AdamStack(*[a.reshape(lead + a.shape) for a in (weights[n], mom_m[n], mom_v[n])], f"adamw_{n}")

    def results_of(n, stack):
        grads[n], deltas[n], new_m[n], new_v[n] = [a.reshape(weights[n].shape) for a in stack.outs]

    sc_t = silu_all.T
    ada_stack, kv_ada_stack = stack_of("ada_w"), stack_of("kv_ada_w")
    for l in range(depth):
        ada_stack.from_outer(l, who, sc_t, _shard_of(dmod_all[:, l], 1, me, mod_cols))
    kv_ada_stack.from_outer(0, who, sc_t, _shard_of(dkvmod_all, 1, me, kv_cols))
    results_of("ada_w", ada_stack)
    results_of("kv_ada_w", kv_ada_stack)

    stacks = {n: stack_of(n) for n in ("mlp_w1", "mlp_w2", "gmlp_w_in", "gmlp_w_out", "attn_wq", "attn_wo", "w_kv")}
    after = ada_stack.outs[0]
    for l in reversed(range(depth)):
        keys = [("mlp_w2", l), ("mlp_w1", l)]
        keys += [("gmlp_w_out", l), ("gmlp_w_in", l)] if l < n_a else [("attn_wo", l - n_a), ("attn_wq", l - n_a)]
        keys += [("w_kv", 0)] if l == n_a else []
        landed = {}
        for key in keys:
            sent, land = _a2a_wait(started[key], after, f"scatter_{key[0]}_l{key[1]}_wait")
            landed[key] = (land, sent)
        for key in keys:
            stacks[key[0]].from_parts(key[1], who, *landed[key])
            after = stacks[key[0]].outs[0]
    for n, stack in stacks.items():
        results_of(n, stack)

    return (loss, grad_x, *[grads[n] for n in WEIGHT_NAMES], *[deltas[n] for n in WEIGHT_NAMES],
            *[new_m[n] for n in WEIGHT_NAMES], *[new_v[n] for n in WEIGHT_NAMES])
```

```python
import functools
import math

import jax
import jax.numpy as jnp
from jax import lax
from jax.experimental import pallas as pl
from jax.experimental.pallas import tpu as pltpu

F32 = jnp.float32
BF16 = jnp.bfloat16
N_DEV = 8
EPS = 1e-6
CHUNK = 64
BLK = 128
N_MOD = 6
ADAM_LR = 0.001
ADAM_B1 = 0.9
ADAM_B2 = 0.999
ADAM_EPS = 1e-08
ADAM_WD = 0.01
ADAM_STEP = 10
VMEM_LIMIT_BYTES = 56 * 2 ** 20
NEG_BIG = -1e30
WEIGHT_NAMES = ['ada_w', 'ada_b', 'norm_g', 'mlp_w1', 'mlp_w2', 'gmlp_w_in', 'gmlp_ln_g', 'gmlp_ln_b', 'gmlp_ws',
                'gmlp_bs', 'gmlp_w_out', 'kv_norm_g', 'kv_ada_w', 'kv_ada_b', 'w_kv', 'k_norm_g', 'w_f', 'b_f',
                'attn_wq', 'q_norm_g', 'attn_wo']
MESH = pl.DeviceIdType.MESH


def _params(sem):
    return pltpu.CompilerParams(dimension_semantics=sem, vmem_limit_bytes=VMEM_LIMIT_BYTES)


def _tile(n, cap, unit=128):
    if n <= cap:
        return n
    t = (cap // unit) * unit
    while t > unit and n % t:
        t -= unit
    assert n % t == 0, (n, cap, unit)
    return t


def _my_index():
    return 4 * lax.axis_index("x") + 2 * lax.axis_index("y") + lax.axis_index("c")


def _all_to_all(x, name, bcast=False):
    slab = x.shape if bcast else x.shape[1:]

    def body(x_ref, o_ref, send_sems, recv_sems, local_sem):
        me = _my_index()

        def src(j):
            return x_ref if bcast else x_ref.at[j]

        mine = pltpu.make_async_copy(src(me), o_ref.at[me], local_sem)
        mine.start()
        sends = []
        for d in range(1, N_DEV):
            peer = (me + d) % N_DEV
            cp = pltpu.make_async_remote_copy(
                src_ref=src(peer), dst_ref=o_ref.at[me],
                send_sem=send_sems.at[d - 1], recv_sem=recv_sems.at[d - 1],
                device_id=(peer // 4, (peer // 2) % 2, peer % 2), device_id_type=MESH)
            cp.start()
            sends.append(cp)
        for d in range(1, N_DEV):
            frm = (me + N_DEV - d) % N_DEV
            pltpu.make_async_remote_copy(
                src_ref=src(frm), dst_ref=o_ref.at[frm],
                send_sem=send_sems.at[d - 1], recv_sem=recv_sems.at[d - 1],
                device_id=(frm // 4, (frm // 2) % 2, frm % 2), device_id_type=MESH).wait_recv()
        for cp in sends:
            cp.wait_send()
        mine.wait()

    return pl.pallas_call(
        body, name=name,
        out_shape=jax.ShapeDtypeStruct((N_DEV,) + tuple(slab), x.dtype),
        in_specs=[pl.BlockSpec(memory_space=pl.ANY)],
        out_specs=pl.BlockSpec(memory_space=pl.ANY),
        scratch_shapes=[pltpu.SemaphoreType.DMA((N_DEV - 1,)), pltpu.SemaphoreType.DMA((N_DEV - 1,)),
                        pltpu.SemaphoreType.DMA],
        compiler_params=pltpu.CompilerParams(has_side_effects=True),
    )(x)


_HBM = pl.BlockSpec(memory_space=pltpu.HBM)
_SEM = pl.BlockSpec(memory_space=pltpu.SEMAPHORE)
_ANY = pl.BlockSpec(memory_space=pl.ANY)
_DATAFLOW = pltpu.SideEffectType.DATAFLOW_SIDE_EFFECTING


def _a2a_peer_copy(x_ref, land_ref, send_sems, recv_sems, d, me, incoming, bcast):
    peer = (me + N_DEV - d) % N_DEV if incoming else (me + d) % N_DEV
    return pltpu.make_async_remote_copy(
        src_ref=x_ref if bcast else x_ref.at[peer], dst_ref=land_ref.at[peer if incoming else me],
        send_sem=send_sems.at[d - 1], recv_sem=recv_sems.at[d - 1],
        device_id=(peer // 4, (peer // 2) % 2, peer % 2), device_id_type=MESH)


def _a2a_start(x, name, bcast=False):
    land_shape = ((N_DEV,) + tuple(x.shape)) if bcast else x.shape

    def body(x_ref, land_ref, send_sems, recv_sems, x_thru, land_thru, token):
        me = _my_index()
        for d in range(1, N_DEV):
            _a2a_peer_copy(x_ref, land_ref, send_sems, recv_sems, d, me, False, bcast).start()
        token[...] = jnp.zeros_like(token)

    return pl.pallas_call(
        body, name=name,
        out_shape=(pltpu.SemaphoreType.DMA((N_DEV - 1,)), pltpu.SemaphoreType.DMA((N_DEV - 1,)),
                   pltpu.HBM(x.shape, x.dtype), pltpu.HBM(land_shape, x.dtype), jax.ShapeDtypeStruct((8, BLK), F32)),
        in_specs=(_HBM, _HBM), out_specs=(_SEM, _SEM, _HBM, _HBM, pl.BlockSpec(memory_space=pltpu.VMEM)),
        input_output_aliases={0: 2, 1: 3},
        compiler_params=pltpu.CompilerParams(has_side_effects=_DATAFLOW),
    )(pltpu.with_memory_space_constraint(x, pltpu.HBM),
      pltpu.with_memory_space_constraint(lax.empty(land_shape, x.dtype), pltpu.HBM))


def _a2a_wait(started, after, name, bcast=False):
    send_sems, recv_sems, x_thru, land_thru, _ = started

    def body(x_ref, land_ref, send_sems, recv_sems, after_ref, x_dead, land_out):
        me = _my_index()
        for d in range(1, N_DEV):
            _a2a_peer_copy(x_ref, land_ref, send_sems, recv_sems, d, me, False, bcast).wait_send()
        for d in range(1, N_DEV):
            _a2a_peer_copy(x_ref, land_ref, send_sems, recv_sems, d, me, True, bcast).wait_recv()

    return pl.pallas_call(
        body, name=name,
        out_shape=(pltpu.HBM(x_thru.shape, x_thru.dtype), pltpu.HBM(land_thru.shape, land_thru.dtype)),
        in_specs=(_HBM, _HBM, _SEM, _SEM, _ANY), out_specs=(_HBM, _HBM),
        input_output_aliases={0: 0, 1: 1},
        compiler_params=pltpu.CompilerParams(has_side_effects=_DATAFLOW),
    )(x_thru, land_thru, send_sems, recv_sems, after)


def _place_shard(w, layer, who, name, deps=()):
    _, R, C = w.shape
    tr = _row_tile(R, C)

    def body(who_ref, w_ref, *rest):
        rest[-1][...] = w_ref[...].astype(BF16)

    return pl.pallas_call(
        body, name=name,
        grid_spec=pltpu.PrefetchScalarGridSpec(
            num_scalar_prefetch=1, grid=(R // tr,),
            in_specs=[pl.BlockSpec((None, tr, C), lambda i, who_ref: (layer, i, 0))] + [_ANY for _ in deps],
            out_specs=pl.BlockSpec((None, None, tr, C), lambda i, who_ref: (who_ref[0], 0, i, 0))),
        out_shape=_sds((N_DEV, 1, R, C), BF16),
        compiler_params=_params(("parallel",)),
    )(who, w, *deps)


def _gather_copies(land_ref, send_sems, recv_sems, base, phase, incoming):
    cx, cy, cc = lax.axis_index("x"), lax.axis_index("y"), lax.axis_index("c")
    sibling = (cx, cy, 1 - cc)
    chips = [(1 - cx, cy), (cx, 1 - cy), (1 - cx, 1 - cy)]
    if phase == 1:
        out = [((cx, cy, cc), sibling)] + [((cx, cy, cc), (*chip, cc)) for chip in chips]
        inc = [(sibling, sibling)] + [((*chip, cc), (*chip, cc)) for chip in chips]
    else:
        out = [((*chip, cc), sibling) for chip in chips]
        inc = [((*chip, 1 - cc), sibling) for chip in chips]
    copies = []
    for k, (block, peer) in enumerate(inc if incoming else out):
        slab = land_ref.at[4 * block[0] + 2 * block[1] + block[2]]
        copies.append(pltpu.make_async_remote_copy(
            src_ref=slab, dst_ref=slab, send_sem=send_sems.at[base + k], recv_sem=recv_sems.at[base + k],
            device_id=peer, device_id_type=MESH))
    return copies


def _gather_start(lands, phase, name):
    n, per = len(lands), (4 if phase == 1 else 3)

    def body(*refs):
        land_refs, send_sems, recv_sems, token = refs[:n], refs[n], refs[n + 1], refs[-1]
        for a, land_ref in enumerate(land_refs):
            for cp in _gather_copies(land_ref, send_sems, recv_sems, a * per, phase, False):
                cp.start()
        token[...] = jnp.zeros_like(token)

    outs = pl.pallas_call(
        body, name=name,
        out_shape=(pltpu.SemaphoreType.DMA((n * per,)), pltpu.SemaphoreType.DMA((n * per,)),
                   *[pltpu.HBM(x.shape, x.dtype) for x in lands], jax.ShapeDtypeStruct((8, BLK), F32)),
        in_specs=tuple(_HBM for _ in lands),
        out_specs=(_SEM, _SEM, *[_HBM for _ in lands], pl.BlockSpec(memory_space=pltpu.VMEM)),
        input_output_aliases={a: 2 + a for a in range(n)},
        compiler_params=pltpu.CompilerParams(has_side_effects=_DATAFLOW),
    )(*[pltpu.with_memory_space_constraint(x, pltpu.HBM) for x in lands])
    return outs[0], outs[1], list(outs[2:2 + n]), outs[-1]


def _gather_wait(started, phase, after, name):
    send_sems, recv_sems, lands, _ = started
    n, per = len(lands), (4 if phase == 1 else 3)

    def body(*refs):
        land_refs, send_sems, recv_sems = refs[:n], refs[n], refs[n + 1]
        for a, land_ref in enumerate(land_refs):
            for cp in _gather_copies(land_ref, send_sems, recv_sems, a * per, phase, False):
                cp.wait_send()
            for cp in _gather_copies(land_ref, send_sems, recv_sems, a * per, phase, True):
                cp.wait_recv()

    outs = pl.pallas_call(
        body, name=name,
        out_shape=tuple(pltpu.HBM(x.shape, x.dtype) for x in lands),
        in_specs=(*[_HBM for _ in lands], _SEM, _SEM, _ANY), out_specs=tuple(_HBM for _ in lands),
        input_output_aliases={a: a for a in range(n)},
        compiler_params=pltpu.CompilerParams(has_side_effects=_DATAFLOW),
    )(*lands, send_sems, recv_sems, after)
    return list(outs)


def _mm(a, b, *, name, ta=False, tb=False, bmode="plain", layer=0, out_mode="plain", out_dtypes=(F32,),
        epilogue=None, extra=(), caps=(1024, 1024, 2048), deps=()):
    if ta:
        K, M = a.shape
    else:
        M, K = a.shape
    n_unit = k_unit = None
    if bmode == "plain":
        N, Kb = (b.shape if tb else b.shape[::-1])
    elif bmode == "col":
        _, _, Kw, Ns = b.shape
        if tb:
            N, Kb, k_unit = Kw, N_DEV * Ns, Ns
        else:
            N, Kb, n_unit = N_DEV * Ns, Kw, Ns
    else:
        _, _, Ks, Nw = b.shape
        if tb:
            N, Kb, n_unit = N_DEV * Ks, Nw, Ks
        else:
            N, Kb, k_unit = Nw, N_DEV * Ks, Ks
    assert K == Kb, (name, a.shape, b.shape)
    if out_mode == "col":
        assert n_unit is None
        n_unit = N // N_DEV
    tm = _tile(M, caps[0])
    tn = _tile(n_unit or N, caps[1])
    span = 1
    if k_unit and 2 * k_unit <= caps[2]:
        while 2 * span * k_unit <= caps[2] and N_DEV % (2 * span) == 0:
            span *= 2
        tk = span * k_unit
    else:
        tk = _tile(k_unit or K, caps[2])
    nk = K // tk
    npb = (n_unit // tn) if n_unit else None
    kpb = (k_unit // tk) if (k_unit and span == 1) else None
    grid = (M // tm, N // tn, nk)

    a_spec = pl.BlockSpec((tk, tm), lambda i, j, k: (k, i)) if ta else pl.BlockSpec((tm, tk), lambda i, j, k: (i, k))
    if bmode == "plain":
        b_spec = (pl.BlockSpec((tn, tk), lambda i, j, k: (j, k)) if tb
                  else pl.BlockSpec((tk, tn), lambda i, j, k: (k, j)))
    elif bmode == "col":
        if tb and span > 1:
            b_spec = pl.BlockSpec((span, None, tn, k_unit), lambda i, j, k: (k, layer, j, 0))
        elif tb:
            b_spec = pl.BlockSpec((None, None, tn, tk), lambda i, j, k: (k // kpb, layer, j, k % kpb))
        else:
            b_spec = pl.BlockSpec((None, None, tk, tn), lambda i, j, k: (j // npb, layer, k, j % npb))
    else:
        if tb:
            b_spec = pl.BlockSpec((None, None, tn, tk), lambda i, j, k: (j // npb, layer, j % npb, k))
        elif span > 1:
            b_spec = pl.BlockSpec((span, None, k_unit, tn), lambda i, j, k: (k, layer, 0, j))
        else:
            b_spec = pl.BlockSpec((None, None, tk, tn), lambda i, j, k: (k // kpb, layer, k % kpb, j))
    mn_spec = pl.BlockSpec((tm, tn), lambda i, j, k: (i, j))
    if out_mode == "col":
        o_specs = [pl.BlockSpec((None, tm, tn), lambda i, j, k: (j // npb, i, j % npb))]
        o_shapes = [jax.ShapeDtypeStruct((N_DEV, M, N // N_DEV), out_dtypes[0])]
    else:
        o_specs = [mn_spec for _ in out_dtypes]
        o_shapes = [jax.ShapeDtypeStruct((M, N), dt) for dt in out_dtypes]
    dims = (((0 if ta else 1,), (1 if tb else 0,)), ((), ()))
    n_extra, n_out, n_dep = len(extra), len(out_dtypes), len(deps)

    def body(a_ref, b_ref, *rest):
        extra_refs, out_refs = rest[:n_extra], rest[n_extra + n_dep:n_extra + n_dep + n_out]
        k = pl.program_id(2)

        def product():
            if span == 1:
                return lax.dot_general(a_ref[...].astype(BF16), b_ref[...].astype(BF16), dims,
                                       preferred_element_type=F32)
            if not tb:
                return lax.dot_general(a_ref[...].astype(BF16), b_ref[...].reshape(tk, tn).astype(BF16), dims,
                                       preferred_element_type=F32)
            out = None
            for s in range(span):
                part = lax.dot_general(a_ref[:, s * k_unit:(s + 1) * k_unit].astype(BF16), b_ref[s].astype(BF16),
                                       dims, preferred_element_type=F32)
                out = part if out is None else out + part
            return out

        def finish(acc):
            outs = (acc,) if epilogue is None else epilogue(acc, *[r[...] for r in extra_refs])
            for o_ref, val in zip(out_refs, outs):
                o_ref[...] = val.astype(o_ref.dtype)

        if nk == 1:
            finish(product())
            return
        acc_ref = rest[-1]

        @pl.when(k == 0)
        def _():
            acc_ref[...] = product()

        if nk > 2:
            @pl.when(jnp.logical_and(k > 0, k < nk - 1))
            def _():
                acc_ref[...] += product()

        @pl.when(k == nk - 1)
        def _():
            finish(acc_ref[...] + product())

    outs = pl.pallas_call(
        body, name=name, grid=grid,
        in_specs=[a_spec, b_spec] + [mn_spec for _ in extra] + [_ANY for _ in deps],
        out_specs=o_specs, out_shape=o_shapes,
        scratch_shapes=[pltpu.VMEM((tm, tn), F32)] if nk > 1 else [],
        compiler_params=_params(("parallel", "parallel", "arbitrary")),
    )(a, b, *extra, *deps)
    return outs[0] if n_out == 1 else outs


def _relu2_epilogue(acc):
    r = jnp.maximum(acc, 0.0)
    return acc, r * r


def _relu2_bwd_epilogue(acc, a_pre):
    return (acc * (2.0 * jnp.maximum(a_pre.astype(F32), 0.0)),)


def _rowcall(body, *, name, tr, row_ins, full_ins=(), row_outs=(), acc_outs=(), scratch=(), reverse=False):
    T = row_ins[0].shape[0]
    nb = T // tr
    rmap = (lambda i: (nb - 1 - i, 0)) if reverse else (lambda i: (i, 0))

    def full_spec(shape):
        nd = len(shape)
        return pl.BlockSpec(tuple(shape), lambda i: (0,) * nd)

    in_specs = [pl.BlockSpec((tr, a.shape[1]), rmap) for a in row_ins] + [full_spec(a.shape) for a in full_ins]
    out_specs = [pl.BlockSpec((tr, s.shape[1]), rmap) for s in row_outs] + [full_spec(s.shape) for s in acc_outs]
    outs = pl.pallas_call(
        body, name=name, grid=(nb,), in_specs=in_specs, out_specs=out_specs,
        out_shape=list(row_outs) + list(acc_outs), scratch_shapes=list(scratch),
        compiler_params=_params(("arbitrary",)),
    )(*row_ins, *full_ins)
    return outs


def _sds(shape, dtype):
    return jax.ShapeDtypeStruct(tuple(shape), dtype)


def _row_tile(T, C, elems=512 * 1024):
    t = max(8, min(T, elems // C))
    p = 8
    while p * 2 <= t and T % (p * 2) == 0:
        p *= 2
    return p


def _norm_mod(x, ng, sc, sh, name):
    T, D = x.shape

    def body(x_ref, ng_ref, sc_ref, sh_ref, h_ref):
        xv = x_ref[...]
        r = lax.rsqrt(jnp.mean(xv * xv, axis=-1, keepdims=True) + EPS)
        h_ref[...] = (((xv * r) * ng_ref[...]) * (1.0 + sc_ref[...]) + sh_ref[...]).astype(BF16)

    return _rowcall(body, name=name, tr=_row_tile(T, D), row_ins=[x], full_ins=[ng, sc, sh],
                    row_outs=[_sds((T, D), BF16)])[0]


def _res_norm_mod(x, y, gate, ng, sc, sh, name):
    T, D = x.shape

    def body(x_ref, y_ref, g_ref, ng_ref, sc_ref, sh_ref, x2_ref, h_ref):
        xv = x_ref[...] + g_ref[...] * y_ref[...]
        x2_ref[...] = xv
        r = lax.rsqrt(jnp.mean(xv * xv, axis=-1, keepdims=True) + EPS)
        h_ref[...] = (((xv * r) * ng_ref[...]) * (1.0 + sc_ref[...]) + sh_ref[...]).astype(BF16)

    return _rowcall(body, name=name, tr=_row_tile(T, D, 256 * 1024), row_ins=[x, y], full_ins=[gate, ng, sc, sh],
                    row_outs=[_sds((T, D), F32), _sds((T, D), BF16)])


def _res_add(x, y, gate, name):
    T, D = x.shape

    def body(x_ref, y_ref, g_ref, x2_ref):
        x2_ref[...] = x_ref[...] + g_ref[...] * y_ref[...]

    return _rowcall(body, name=name, tr=_row_tile(T, D), row_ins=[x, y], full_ins=[gate],
                    row_outs=[_sds((T, D), F32)])[0]


def _res_loss(x, y, gate, target, name):
    T, D = x.shape

    def body(x_ref, y_ref, t_ref, g_ref, dy_ref, loss_ref):
        @pl.when(pl.program_id(0) == 0)
        def _():
            loss_ref[...] = jnp.zeros_like(loss_ref)

        diff = x_ref[...] + g_ref[...] * y_ref[...] - t_ref[...]
        dy_ref[...] = diff * (1.0 / D)
        loss_ref[...] += jnp.sum(diff * diff) * (0.5 / D)

    return _rowcall(body, name=name, tr=_row_tile(T, D, 256 * 1024), row_ins=[x, y, target], full_ins=[gate],
                    row_outs=[_sds((T, D), F32)], acc_outs=[_sds((1, BLK), F32)])


def _gate_bwd(dx, y, gate, name):
    T, D = dx.shape

    def body(dx_ref, y_ref, g_ref, dy_ref, dg_ref):
        @pl.when(pl.program_id(0) == 0)
        def _():
            dg_ref[...] = jnp.zeros_like(dg_ref)

        dxv = dx_ref[...]
        dy_ref[...] = (dxv * g_ref[...]).astype(BF16)
        dg_ref[...] += jnp.sum(dxv * y_ref[...], axis=0, keepdims=True)

    return _rowcall(body, name=name, tr=_row_tile(T, D), row_ins=[dx, y], full_ins=[gate],
                    row_outs=[_sds((T, D), BF16)], acc_outs=[_sds((1, D), F32)])


def _norm_mod_bwd(x, dhs, dres, ng, sc, name):
    T, D = x.shape
    n_dh = len(dhs)

    def body(*refs):
        x_ref, dh_refs, dres_ref = refs[0], refs[1:1 + n_dh], refs[1 + n_dh]
        ng_ref, sc_ref, dx_ref, sums_ref = refs[2 + n_dh:]

        @pl.when(pl.program_id(0) == 0)
        def _():
            sums_ref[...] = jnp.zeros_like(sums_ref)

        xv = x_ref[...]
        dh = dh_refs[0][...].astype(F32)
        for r_ in dh_refs[1:]:
            dh = dh + r_[...].astype(F32)
        r = lax.rsqrt(jnp.mean(xv * xv, axis=-1, keepdims=True) + EPS)
        n = xv * r
        ngv, scale1 = ng_ref[...], 1.0 + sc_ref[...]
        dn = dh * (ngv * scale1)
        dx_ref[...] = dres_ref[...] + r * (dn - n * jnp.mean(dn * n, axis=-1, keepdims=True))
        dhn = dh * n
        sums_ref[0:1, :] += jnp.sum(dh, axis=0, keepdims=True)
        sums_ref[1:2, :] += jnp.sum(dhn * ngv, axis=0, keepdims=True)
        sums_ref[2:3, :] += jnp.sum(dhn * scale1, axis=0, keepdims=True)

    return _rowcall(body, name=name, tr=_row_tile(T, D, 256 * 1024), row_ins=[x, *dhs, dres], full_ins=[ng, sc],
                    row_outs=[_sds((T, D), F32)], acc_outs=[_sds((8, D), F32)])


def _head_norm(x, g, n_heads, name, tail=False):
    T = x.shape[0]
    D = n_heads * BLK
    W = x.shape[1] if tail else D

    def body(x_ref, g_ref, o_ref, *tail_ref):
        for h in range(n_heads):
            xv = x_ref[:, h * BLK:(h + 1) * BLK]
            r = lax.rsqrt(jnp.mean(xv * xv, axis=-1, keepdims=True) + EPS)
            o_ref[:, h * BLK:(h + 1) * BLK] = ((xv * r) * g_ref[...]).astype(BF16)
        if tail:
            tail_ref[0][...] = x_ref[:, D:2 * D].astype(BF16)

    tr = _row_tile(T, x.shape[1])
    o_spec = pl.BlockSpec((tr, D), lambda i: (i, 0))
    return pl.pallas_call(
        body, name=name, grid=(T // tr,),
        in_specs=[pl.BlockSpec((tr, W), lambda i: (i, 0)), pl.BlockSpec((1, BLK), lambda i: (0, 0))],
        out_specs=[o_spec] * (2 if tail else 1), out_shape=[_sds((T, D), BF16)] * (2 if tail else 1),
        compiler_params=_params(("parallel",)),
    )(x, g)


def _head_norm_bwd(x, dys, g, n_heads, name, tails=()):
    T = x.shape[0]
    D = n_heads * BLK
    n_dy, n_tail = len(dys), len(tails)
    W = 2 * D if tails else D

    def body(*refs):
        x_ref, dy_refs, tail_refs = refs[0], refs[1:1 + n_dy], refs[1 + n_dy:1 + n_dy + n_tail]
        g_ref, dx_ref, dg_ref = refs[1 + n_dy + n_tail:]

        @pl.when(pl.program_id(0) == 0)
        def _():
            dg_ref[...] = jnp.zeros_like(dg_ref)

        tot = jnp.zeros((1, BLK), F32)
        for h in range(n_heads):
            cols = slice(h * BLK, (h + 1) * BLK)
            xv = x_ref[:, cols]
            dyv = dy_refs[0][:, cols]
            for r_ in dy_refs[1:]:
                dyv = dyv + r_[:, cols]
            r = lax.rsqrt(jnp.mean(xv * xv, axis=-1, keepdims=True) + EPS)
            n = xv * r
            dn = dyv * g_ref[...]
            dx_ref[:, cols] = (r * (dn - n * jnp.mean(dn * n, axis=-1, keepdims=True))).astype(BF16)
            tot = tot + jnp.sum(dyv * n, axis=0, keepdims=True)
        dg_ref[0:1, :] += tot
        if n_tail:
            tv = tail_refs[0][...]
            for r_ in tail_refs[1:]:
                tv = tv + r_[...]
            dx_ref[:, D:] = tv.astype(BF16)

    tr = _row_tile(T, 2 * D, 256 * 1024)
    d_spec = pl.BlockSpec((tr, D), lambda i: (i, 0))
    return pl.pallas_call(
        body, name=name, grid=(T // tr,),
        in_specs=[d_spec] * (1 + n_dy + n_tail) + [pl.BlockSpec((1, BLK), lambda i: (0, 0))],
        out_specs=[pl.BlockSpec((tr, W), lambda i: (i, 0)), pl.BlockSpec((8, BLK), lambda i: (0, 0))],
        out_shape=[_sds((T, W), BF16), _sds((8, BLK), F32)],
        compiler_params=_params(("arbitrary",)),
    )(x, *dys, *tails, g)


def _fcum_fwd(fl, bf, name):
    T = fl.shape[0]

    def body(fl_ref, b_ref, o_ref, carry_ref):
        @pl.when(pl.program_id(0) == 0)
        def _():
            carry_ref[...] = jnp.zeros_like(carry_ref)

        z = fl_ref[...] + b_ref[...]
        logf = jnp.minimum(z, 0.0) - jnp.log(1.0 + jnp.exp(-jnp.abs(z)))
        row = lax.broadcasted_iota(jnp.int32, (BLK, BLK), 0)
        col = lax.broadcasted_iota(jnp.int32, (BLK, BLK), 1)
        tri = (col <= row).astype(F32)
        run = jnp.dot(tri, logf, preferred_element_type=F32, precision=lax.Precision.HIGHEST) + carry_ref[0:1, :]
        o_ref[...] = run
        carry_ref[0:1, :] = run[BLK - 1:BLK, :]

    return _rowcall(body, name=name, tr=BLK, row_ins=[fl], full_ins=[bf], row_outs=[_sds((T, BLK), F32)],
                    scratch=[pltpu.VMEM((8, BLK), F32)])[0]


def _fcum_bwd(dfs, fl, bf, name):
    T = fl.shape[0]
    n_df = len(dfs)

    def body(*refs):
        df_refs = refs[:n_df]
        fl_ref, b_ref, dfl_ref, dbias_ref, carry_ref = refs[n_df:]

        @pl.when(pl.program_id(0) == 0)
        def _():
            carry_ref[...] = jnp.zeros_like(carry_ref)
            dbias_ref[...] = jnp.zeros_like(dbias_ref)

        dfc = df_refs[0][...]
        for r_ in df_refs[1:]:
            dfc = dfc + r_[...]
        row = lax.broadcasted_iota(jnp.int32, (BLK, BLK), 0)
        col = lax.broadcasted_iota(jnp.int32, (BLK, BLK), 1)
        tri = (col >= row).astype(F32)
        suffix = jnp.dot(tri, dfc, preferred_element_type=F32, precision=lax.Precision.HIGHEST) + carry_ref[0:1, :]
        carry_ref[0:1, :] = suffix[0:1, :]
        z = fl_ref[...] + b_ref[...]
        dfl = suffix / (1.0 + jnp.exp(z))
        dfl_ref[...] = dfl.astype(BF16)
        dbias_ref[0:1, :] += jnp.sum(dfl, axis=0, keepdims=True)

    return _rowcall(body, name=name, tr=BLK, row_ins=[*dfs, fl], full_ins=[bf], reverse=True,
                    row_outs=[_sds((T, BLK), BF16)], acc_outs=[_sds((8, BLK), F32)],
                    scratch=[pltpu.VMEM((8, BLK), F32)])


def _attn_tile(T):
    return min(T, 512)


def _attn_fwd(q, k, v, fk, n_heads, name):
    T = q.shape[0]
    tq = tk = _attn_tile(T)
    nkb = T // tk
    inv_sqrt = 1.0 / float(math.sqrt(BLK))

    def body(q_ref, k_ref, v_ref, fk_ref, o_ref, o32_ref, lse_ref):
        i = pl.program_id(1)
        qv = q_ref[...]

        def block(j, carry, diagonal):
            m, l, acc = carry
            rows = pl.ds(pl.multiple_of(j * tk, tk), tk)
            kj, vj = k_ref[rows, :], v_ref[rows, :]
            s = lax.dot_general(qv, kj, (((1,), (1,)), ((), ())), preferred_element_type=F32) * inv_sqrt
            s = s - fk_ref[j]
            if diagonal:
                s = jnp.where(lax.broadcasted_iota(jnp.int32, (tq, tk), 1)
                              <= lax.broadcasted_iota(jnp.int32, (tq, tk), 0), s, NEG_BIG)
            m_new = jnp.maximum(m, jnp.max(s, axis=-1, keepdims=True))
            alpha = jnp.exp(m - m_new)
            p = jnp.exp(s - m_new)
            l = alpha * l + jnp.sum(p, axis=-1, keepdims=True)
            acc = alpha * acc + jnp.dot(p.astype(BF16), vj, preferred_element_type=F32)
            return m_new, l, acc

        init = (jnp.full((tq, 1), NEG_BIG, F32), jnp.zeros((tq, 1), F32), jnp.zeros((tq, BLK), F32))
        carry = lax.fori_loop(0, i, lambda j, c: block(j, c, False), init)
        m, l, acc = block(i, carry, True)
        out = acc / l
        o_ref[...] = out.astype(BF16)
        o32_ref[...] = out
        lse_ref[...] = m + jnp.log(l)

    return pl.pallas_call(
        body, name=name, grid=(n_heads, T // tq),
        in_specs=[pl.BlockSpec((tq, BLK), lambda h, i: (i, h)),
                  pl.BlockSpec((T, BLK), lambda h, i: (0, h)),
                  pl.BlockSpec((T, BLK), lambda h, i: (0, h)),
                  pl.BlockSpec((None, nkb, 1, tk), lambda h, i: (h, 0, 0, 0))],
        out_specs=[pl.BlockSpec((tq, BLK), lambda h, i: (i, h)),
                   pl.BlockSpec((tq, BLK), lambda h, i: (i, h)),
                   pl.BlockSpec((None, tq, 1), lambda h, i: (h, i, 0))],
        out_shape=[_sds((T, n_heads * BLK), BF16), _sds((T, n_heads * BLK), F32), _sds((n_heads, T, 1), F32)],
        compiler_params=_params(("parallel", "arbitrary")),
    )(q, k, v, fk)


def _attn_bwd(q, k, v, o, do, lse, fk, n_heads, name):
    T = q.shape[0]
    tq = tk = _attn_tile(T)
    nkb = T // tk
    nq = T // tq
    inv_sqrt = 1.0 / float(math.sqrt(BLK))
    tn_dims = (((0,), (0,)), ((), ()))
    nt_dims = (((1,), (1,)), ((), ()))

    def body(q_ref, k_ref, v_ref, o_ref, do_ref, lse_ref, fk_ref, dq_ref, dk_ref, dv_ref, dfq_ref, dfk_ref, delta_ref):
        j = pl.program_id(1)

        @pl.when(j == 0)
        def _():
            delta_ref[...] = jnp.sum(do_ref[...].astype(F32) * o_ref[...], axis=1, keepdims=True)
            dq_ref[...] = jnp.zeros_like(dq_ref)
            dfq_ref[...] = jnp.zeros_like(dfq_ref)

        kj, vj, fkv = k_ref[...], v_ref[...], fk_ref[...]

        def step(i, carry, diagonal):
            dk, dv, dfk = carry
            rows = pl.ds(pl.multiple_of(i * tq, tq), tq)
            qi, doi = q_ref[rows, :], do_ref[rows, :]
            s = lax.dot_general(qi, kj, nt_dims, preferred_element_type=F32) * inv_sqrt - fkv
            if diagonal:
                s = jnp.where(lax.broadcasted_iota(jnp.int32, (tq, tk), 1)
                              <= lax.broadcasted_iota(jnp.int32, (tq, tk), 0), s, NEG_BIG)
            p = jnp.exp(s - lse_ref[rows, :])
            dv = dv + lax.dot_general(p.astype(BF16), doi, tn_dims, preferred_element_type=F32)
            dp = lax.dot_general(doi, vj, nt_dims, preferred_element_type=F32)
            ds = p * (dp - delta_ref[rows, :])
            dsb = ds.astype(BF16)
            dq_ref[rows, :] += jnp.dot(dsb, kj, preferred_element_type=F32) * inv_sqrt
            dk = dk + lax.dot_general(dsb, qi, tn_dims, preferred_element_type=F32)
            dfq_ref[rows, :] += jnp.sum(ds, axis=1, keepdims=True)
            dfk = dfk - jnp.sum(ds, axis=0, keepdims=True)
            return dk, dv, dfk

        init = (jnp.zeros((tk, BLK), F32), jnp.zeros((tk, BLK), F32), jnp.zeros((1, tk), F32))
        carry = step(j, init, True)
        dk, dv, dfk = lax.fori_loop(j + 1, nq, lambda i, c: step(i, c, False), carry)
        dk_ref[...] = dk * inv_sqrt
        dv_ref[...] = dv
        dfk_ref[...] = dfk

    head_col = lambda h, j: (0, h)
    return pl.pallas_call(
        body, name=name, grid=(n_heads, nkb),
        in_specs=[pl.BlockSpec((T, BLK), head_col),
                  pl.BlockSpec((tk, BLK), lambda h, j: (j, h)),
                  pl.BlockSpec((tk, BLK), lambda h, j: (j, h)),
                  pl.BlockSpec((T, BLK), head_col),
                  pl.BlockSpec((T, BLK), head_col),
                  pl.BlockSpec((None, T, 1), lambda h, j: (h, 0, 0)),
                  pl.BlockSpec((None, None, 1, tk), lambda h, j: (h, j, 0, 0))],
        out_specs=[pl.BlockSpec((T, BLK), head_col),
                   pl.BlockSpec((tk, BLK), lambda h, j: (j, h)),
                   pl.BlockSpec((tk, BLK), lambda h, j: (j, h)),
                   pl.BlockSpec((None, T, 1), lambda h, j: (h, 0, 0)),
                   pl.BlockSpec((None, None, 1, tk), lambda h, j: (h, j, 0, 0))],
        out_shape=[_sds((T, n_heads * BLK), F32), _sds((T, n_heads * BLK), F32), _sds((T, n_heads * BLK), F32),
                   _sds((n_heads, T, 1), F32), _sds((n_heads, nkb, 1, tk), F32)],
        scratch_shapes=[pltpu.VMEM((T, 1), F32)],
        compiler_params=_params(("parallel", "arbitrary")),
    )(q, k, v, o, do, lse, fk)


_INV_SQRT2 = 1.0 / math.sqrt(2.0)
_INV_SQRT_2PI = 1.0 / math.sqrt(2.0 * math.pi)


def _gelu_parts(z):
    cdf = 0.5 * (1.0 + lax.erf(z * _INV_SQRT2))
    return cdf, z * cdf


def _mix_mask(transposed):
    row = lax.broadcasted_iota(jnp.int32, (BLK, BLK), 0) // CHUNK
    col = lax.broadcasted_iota(jnp.int32, (BLK, BLK), 1) // CHUNK
    return (row <= col) if transposed else (col <= row)


def _gmlp_mid_fwd(zpre, ln_g, ln_b, ws, bs_t, name):
    T, two_h = zpre.shape
    Hh = two_h // 2
    G = ws.shape[0]
    gd = Hh // G

    def body(z_ref, lg_ref, lb_ref, ws_ref, bs_ref, p_ref):
        _, zg = _gelu_parts(z_ref[...].astype(F32))
        u, v = zg[:, :Hh], zg[:, Hh:]
        mu = jnp.mean(v, axis=-1, keepdims=True)
        vc = v - mu
        rstd = lax.rsqrt(jnp.mean(vc * vc, axis=-1, keepdims=True) + EPS)
        vn = ((vc * rstd) * lg_ref[...] + lb_ref[...]).astype(BF16)
        mask = _mix_mask(False)
        for g in range(G):
            wm = jnp.where(mask, ws_ref[g], 0.0).astype(BF16)
            sv = jnp.dot(wm, vn[:, g * gd:(g + 1) * gd], preferred_element_type=F32) + bs_ref[:, g:g + 1]
            p_ref[:, g * gd:(g + 1) * gd] = (u[:, g * gd:(g + 1) * gd] * sv).astype(BF16)

    return _rowcall(body, name=name, tr=BLK, row_ins=[zpre], full_ins=[ln_g, ln_b, ws, bs_t],
                    row_outs=[_sds((T, Hh), BF16)])[0]


def _gmlp_mid_bwd(zpre, dp, ln_g, ln_b, ws, ws_t, bs_t, name):
    T, two_h = zpre.shape
    Hh = two_h // 2
    G = ws.shape[0]
    gd = Hh // G
    nt_dims = (((1,), (1,)), ((), ()))

    def body(z_ref, dp_ref, lg_ref, lb_ref, ws_ref, wst_ref, bs_ref, dz_ref, dws_ref, dbs_ref, dlg_ref, dlb_ref,
             dvn_ref):
        @pl.when(pl.program_id(0) == 0)
        def _():
            dws_ref[...] = jnp.zeros_like(dws_ref)
            dbs_ref[...] = jnp.zeros_like(dbs_ref)
            dlg_ref[...] = jnp.zeros_like(dlg_ref)
            dlb_ref[...] = jnp.zeros_like(dlb_ref)

        z = z_ref[...].astype(F32)
        cdf, zg = _gelu_parts(z)
        dgelu = cdf + z * (jnp.exp(-0.5 * z * z) * _INV_SQRT_2PI)
        u, v = zg[:, :Hh], zg[:, Hh:]
        mu = jnp.mean(v, axis=-1, keepdims=True)
        vc = v - mu
        rstd = lax.rsqrt(jnp.mean(vc * vc, axis=-1, keepdims=True) + EPS)
        vhat = vc * rstd
        vn = (vhat * lg_ref[...] + lb_ref[...]).astype(BF16)
        mask, mask_t = _mix_mask(False), _mix_mask(True)
        lane = lax.broadcasted_iota(jnp.int32, (BLK, BLK), 1)
        dbs = jnp.zeros((BLK, BLK), F32)
        for g in range(G):
            cols = slice(g * gd, (g + 1) * gd)
            wm = jnp.where(mask, ws_ref[g], 0.0).astype(BF16)
            wm_t = jnp.where(mask_t, wst_ref[g], 0.0).astype(BF16)
            vn_g = vn[:, cols]
            sv = jnp.dot(wm, vn_g, preferred_element_type=F32) + bs_ref[:, g:g + 1]
            dp_g = dp_ref[:, cols].astype(F32)
            dz_ref[:, cols] = ((dp_g * sv) * dgelu[:, cols]).astype(BF16)
            dsv = dp_g * u[:, cols]
            dsv_b = dsv.astype(BF16)
            dbs = dbs + jnp.where(lane == g, jnp.sum(dsv, axis=1, keepdims=True), 0.0)
            dws_ref[g] += jnp.where(mask, lax.dot_general(dsv_b, vn_g, nt_dims, preferred_element_type=F32), 0.0)
            dvn_ref[:, cols] = jnp.dot(wm_t, dsv_b, preferred_element_type=F32)
        dbs_ref[...] += dbs
        dvn = dvn_ref[...]
        dlg_ref[0:1, :] += jnp.sum(dvn * vhat, axis=0, keepdims=True)
        dlb_ref[0:1, :] += jnp.sum(dvn, axis=0, keepdims=True)
        dvh = dvn * lg_ref[...]
        dv = rstd * (dvh - jnp.mean(dvh, axis=-1, keepdims=True) - vhat * jnp.mean(dvh * vhat, axis=-1, keepdims=True))
        dz_ref[:, Hh:] = (dv * dgelu[:, Hh:]).astype(BF16)

    return _rowcall(body, name=name, tr=BLK, row_ins=[zpre, dp], full_ins=[ln_g, ln_b, ws, ws_t, bs_t],
                    row_outs=[_sds((T, two_h), BF16)],
                    acc_outs=[_sds((G, BLK, BLK), F32), _sds((BLK, BLK), F32), _sds((8, Hh), F32), _sds((8, Hh), F32)],
                    scratch=[pltpu.VMEM((BLK, Hh), F32)])


def _mods(c_all, w, layer, bias, name):
    nb, K = c_all.shape
    N = w.shape[-1]
    tn = _tile(N, 512)

    def body(c_ref, w_ref, b_ref, o_ref):
        cv = c_ref[...]
        sc = cv / (1.0 + jnp.exp(-cv))
        o_ref[...] = jnp.dot(sc, w_ref[...], preferred_element_type=F32, precision=lax.Precision.HIGHEST) + b_ref[...]

    return pl.pallas_call(
        body, name=name, grid=(N // tn,),
        in_specs=[pl.BlockSpec((nb, K), lambda j: (0, 0)),
                  pl.BlockSpec((None, K, tn), lambda j: (layer, 0, j)),
                  pl.BlockSpec((1, tn), lambda j: (0, j))],
        out_specs=pl.BlockSpec((nb, tn), lambda j: (0, j)), out_shape=_sds((nb, N), F32),
        compiler_params=_params(("parallel",)),
    )(c_all, w, bias)


def _sum_slabs(landed, own, who, name):
    _, R, C = landed.shape
    tr = _row_tile(R, C * N_DEV)

    def body(who_ref, x_ref, own_ref, o_ref):
        me, mine = who_ref[0], own_ref[...]
        acc = jnp.where(me == 0, mine, x_ref[0])
        for s in range(1, N_DEV):
            acc = acc + jnp.where(me == s, mine, x_ref[s])
        o_ref[...] = acc

    return pl.pallas_call(
        body, name=name,
        grid_spec=pltpu.PrefetchScalarGridSpec(
            num_scalar_prefetch=1, grid=(R // tr,),
            in_specs=[pl.BlockSpec((N_DEV, tr, C), lambda i, who_ref: (0, i, 0)),
                      pl.BlockSpec((tr, C), lambda i, who_ref: (i, 0))],
            out_specs=pl.BlockSpec((tr, C), lambda i, who_ref: (i, 0))),
        out_shape=_sds((R, C), F32),
        compiler_params=_params(("parallel",)),
    )(who, landed, own)


def _adamw_math(w, g, m, v):
    m = ADAM_B1 * m + (1.0 - ADAM_B1) * g
    v = ADAM_B2 * v + (1.0 - ADAM_B2) * (g * g)
    m_hat = m / (1.0 - ADAM_B1 ** ADAM_STEP)
    v_hat = v / (1.0 - ADAM_B2 ** ADAM_STEP)
    delta = -ADAM_LR * (m_hat / (jnp.sqrt(v_hat) + ADAM_EPS) + ADAM_WD * w)
    return delta, m, v


class _AdamStack:
    def __init__(self, w, m, v, name):
        self.w, self.m, self.v, self.name = w, m, v, name
        self.L, self.R, self.C = w.shape
        self.tr = _row_tile(self.R, self.C, 256 * 1024)
        self.outs = None

    def _layer(self, l, who, srcs, src_specs, make_grad):
        n_src = len(srcs)
        L, R, C, tr = self.L, self.R, self.C, self.tr
        wspec = pl.BlockSpec((None, tr, C), lambda i, who_ref: (l, i, 0))

        def body(who_ref, *refs):
            src_refs = refs[:n_src]
            w_ref, m_ref, v_ref = refs[n_src:n_src + 3]
            g_ref, d_ref, m2_ref, v2_ref = refs[-4:]
            g = make_grad(who_ref[0], *src_refs)
            delta, m2, v2 = _adamw_math(w_ref[...], g, m_ref[...], v_ref[...])
            g_ref[...] = g
            d_ref[...] = delta
            m2_ref[...] = m2
            v2_ref[...] = v2

        prev = [] if self.outs is None else list(self.outs)
        aliases = {} if self.outs is None else {1 + n_src + 3 + t: t for t in range(4)}
        self.outs = pl.pallas_call(
            body, name=f"{self.name}_l{l}",
            grid_spec=pltpu.PrefetchScalarGridSpec(
                num_scalar_prefetch=1, grid=(R // tr,),
                in_specs=list(src_specs) + [wspec] * 3 + [_ANY] * len(prev), out_specs=[wspec] * 4),
            out_shape=[_sds((L, R, C), F32)] * 4,
            input_output_aliases=aliases,
            compiler_params=_params(("parallel",)),
        )(who, *srcs, self.w, self.m, self.v, *prev)

    def from_parts(self, l, who, landed, sent):
        tr, C = self.tr, self.C

        def make_grad(me, p_ref, own_ref):
            mine = own_ref[...].astype(F32)
            g = jnp.where(me == 0, mine, p_ref[0].astype(F32))
            for s in range(1, N_DEV):
                g = g + jnp.where(me == s, mine, p_ref[s].astype(F32))
            return g

        self._layer(l, who, [landed, sent],
                    [pl.BlockSpec((N_DEV, tr, C), lambda i, who_ref: (0, i, 0)),
                     pl.BlockSpec((None, tr, C), lambda i, who_ref: (who_ref[0], i, 0))], make_grad)

    def from_outer(self, l, who, sc_t, dmod):
        tr, C = self.tr, self.C

        def make_grad(me, s_ref, d_ref):
            g = s_ref[:, 0:1] * d_ref[0:1, :]
            for b in range(1, N_DEV):
                g = g + s_ref[:, b:b + 1] * d_ref[b:b + 1, :]
            return g

        self._layer(l, who, [sc_t, dmod], [pl.BlockSpec((tr, N_DEV), lambda i, who_ref: (i, 0)),
                                           pl.BlockSpec((N_DEV, C), lambda i, who_ref: (0, 0))], make_grad)


def _adamw_flat(w, g, m, v, name):
    R, C = w.shape
    tr = _row_tile(R, C, 128 * 1024)

    def body(w_ref, g_ref, m_ref, v_ref, d_ref, m2_ref, v2_ref):
        delta, m2, v2 = _adamw_math(w_ref[...], g_ref[...], m_ref[...], v_ref[...])
        d_ref[...] = delta
        m2_ref[...] = m2
        v2_ref[...] = v2

    spec = pl.BlockSpec((tr, C), lambda i: (i, 0))
    return pl.pallas_call(
        body, name=name, grid=(R // tr,), in_specs=[spec] * 4, out_specs=[spec] * 3,
        out_shape=[_sds((R, C), F32)] * 3, compiler_params=_params(("parallel",)),
    )(w, g, m, v)


def _pack(arrays):
    flat = jnp.concatenate([a.reshape(-1).astype(F32) for a in arrays])
    pad = (-flat.shape[0]) % (64 * BLK)
    if pad:
        flat = jnp.concatenate([flat, jnp.zeros((pad,), F32)])
    return flat.reshape(-1, BLK)


def _unpack(buf, shapes, lead=()):
    sizes = [int(math.prod(s)) for s in shapes]
    out, off = [], 0
    if all(n % BLK == 0 for n in sizes):
        for s, n in zip(shapes, sizes):
            out.append(buf[..., off // BLK:(off + n) // BLK, :].reshape(tuple(lead) + tuple(s)))
            off += n
        return out
    flat = buf.reshape(tuple(lead) + (-1,))
    for s, n in zip(shapes, sizes):
        out.append(flat[..., off:off + n].reshape(tuple(lead) + tuple(s)))
        off += n
    return out


def _row(vec):
    return vec.reshape(1, -1)


def _shard_of(full, axis, me, size):
    return lax.dynamic_slice_in_dim(full, me * size, size, axis=axis)


def kernel(x, c, ada_w, ada_b, norm_g, mlp_w1, mlp_w2, gmlp_w_in, gmlp_ln_g, gmlp_ln_b, gmlp_ws, gmlp_bs, gmlp_w_out, kv_norm_g, kv_ada_w, kv_ada_b, w_kv, k_norm_g, w_f, b_f, attn_wq, q_norm_g, attn_wo, loss_target, m_ada_w, m_ada_b, m_norm_g, m_mlp_w1, m_mlp_w2, m_gmlp_w_in, m_gmlp_ln_g, m_gmlp_ln_b, m_gmlp_ws, m_gmlp_bs, m_gmlp_w_out, m_kv_norm_g, m_kv_ada_w, m_kv_ada_b, m_w_kv, m_k_norm_g, m_w_f, m_b_f, m_attn_wq, m_q_norm_g, m_attn_wo, v_ada_w, v_ada_b, v_norm_g, v_mlp_w1, v_mlp_w2, v_gmlp_w_in, v_gmlp_ln_g, v_gmlp_ln_b, v_gmlp_ws, v_gmlp_bs, v_gmlp_w_out, v_kv_norm_g, v_kv_ada_w, v_kv_ada_b, v_w_kv, v_k_norm_g, v_w_f, v_b_f, v_attn_wq, v_q_norm_g, v_attn_wo):
    given = dict(locals())
    weights = {n: given[n] for n in WEIGHT_NAMES}
    mom_m = {n: given["m_" + n] for n in WEIGHT_NAMES}
    mom_v = {n: given["v_" + n] for n in WEIGHT_NAMES}

    me = _my_index()
    T, D = x.shape[1], x.shape[2]
    depth = ada_w.shape[0]
    n_a = gmlp_w_in.shape[0]
    n_heads = b_f.shape[0]
    G = gmlp_ws.shape[1]
    Hh = gmlp_ln_g.shape[1] * N_DEV
    mod_cols = ada_w.shape[2]
    kv_cols = kv_ada_w.shape[1]
    x0 = x.reshape(T, D)
    target = loss_target.reshape(T, D)

    small_in = [c, norm_g, gmlp_ln_g, gmlp_ln_b, w_f]
    small_shapes = [a.shape for a in small_in]
    got = _all_to_all(_pack(small_in), "gather_small_inputs", bcast=True)
    c_all, norm_g_sh, ln_g_sh, ln_b_sh, w_f_sh = _unpack(got, small_shapes, lead=(N_DEV,))
    c_all = c_all.reshape(N_DEV, D)
    norm_g_full = jnp.moveaxis(norm_g_sh, 0, 2).reshape(depth, 2, D)
    ln_g_full = jnp.moveaxis(ln_g_sh, 0, 1).reshape(n_a, Hh)
    ln_b_full = jnp.moveaxis(ln_b_sh, 0, 1).reshape(n_a, Hh)
    w_f_full = w_f_sh.reshape(D, n_heads)
    w_f_pad = jnp.pad(w_f_full, ((0, 0), (0, BLK - n_heads))).astype(BF16)
    b_f_pad = jnp.pad(b_f, (0, BLK - n_heads)).reshape(1, BLK)

    mod_parts = []
    for l in range(depth):
        bias = _shard_of(ada_b[l], 0, me, mod_cols).reshape(1, mod_cols)
        mod_parts.append(_mods(c_all, ada_w, l, bias, f"mods_l{l}"))
    kv_bias = _shard_of(kv_ada_b, 0, me, kv_cols).reshape(1, kv_cols)
    mod_parts.append(_mods(c_all, kv_ada_w.reshape(1, D, kv_cols), 0, kv_bias, "mods_kv"))
    mods_mine = jnp.concatenate(mod_parts, axis=1)
    mod_width = mods_mine.shape[1]
    mods_pack = jnp.pad(mods_mine, ((0, 0), (0, (-mod_width) % (8 * BLK)))).reshape(N_DEV, -1, BLK)
    mods_got = _all_to_all(mods_pack, "exchange_mods").reshape(N_DEV, -1)[:, :mod_width]
    mods = []
    for l in range(depth):
        mods.append(mods_got[:, l * mod_cols:(l + 1) * mod_cols].reshape(N_MOD, D))
    kv_mod = mods_got[:, depth * mod_cols:].reshape(2, D)
    silu_all = c_all / (1.0 + jnp.exp(-c_all))

    assert 1 <= n_a < depth
    who = me.astype(jnp.int32).reshape(1)
    big = {"mlp_w1": mlp_w1, "mlp_w2": mlp_w2, "gmlp_w_in": gmlp_w_in, "gmlp_w_out": gmlp_w_out,
           "w_kv": w_kv.reshape((1,) + w_kv.shape), "attn_wq": attn_wq, "attn_wo": attn_wo}
    groups = [[("gmlp_w_in", 0)], [("gmlp_w_out", 0), ("mlp_w1", 0), ("mlp_w2", 0)]]
    for l in range(1, depth):
        if l < n_a:
            groups.append([("gmlp_w_in", l), ("gmlp_w_out", l), ("mlp_w1", l), ("mlp_w2", l)])
        else:
            first = [("w_kv", 0)] if l == n_a else []
            groups.append(first + [("attn_wq", l - n_a), ("attn_wo", l - n_a), ("mlp_w1", l), ("mlp_w2", l)])
    tokens = []

    def behind_starts():
        out = tuple(tokens)
        tokens.clear()
        return out

    over_ici, placed = {}, {}

    def gather_begin(gi):
        over_ici[gi] = _gather_start(placed[gi], 1, f"gather_g{gi}_ici_start")
        tokens.append(over_ici[gi][3])

    for gi, grp in enumerate(groups):
        placed[gi] = [_place_shard(big[n], l, who, f"place_{n}_l{l}", deps=tuple(tokens)) for n, l in grp]
        if gi < 2:
            gather_begin(gi)
    to_sibling = {}
    gw = {}

    def gather_forward(gi, after):
        landed = _gather_wait(over_ici[gi], 1, after, f"gather_g{gi}_ici_wait")
        to_sibling[gi] = _gather_start(landed, 2, f"gather_g{gi}_d2d_start")
        tokens.append(to_sibling[gi][3])
        if gi + 2 < len(groups):
            gather_begin(gi + 2)

    def gather_finish(gi, after):
        for key, arr in zip(groups[gi], _gather_wait(to_sibling[gi], 2, after, f"gather_g{gi}_d2d_wait")):
            gw[key] = arr

    tkk = _attn_tile(T)

    saved = []
    xs = x0
    pending = None
    kv = None
    for l in range(depth):
        sh1, sc1, g1, sh2, sc2, g2 = [_row(mods[l][t]) for t in range(N_MOD)]
        ng1, ng2 = _row(norm_g_full[l, 0]), _row(norm_g_full[l, 1])
        st = dict(sc1=sc1, g1=g1, sc2=sc2, g2=g2, ng1=ng1, ng2=ng2)
        if pending is None:
            h1 = _norm_mod(xs, ng1, sc1, sh1, f"norm1_l{l}")
            gather_forward(0, h1)
            gather_finish(0, h1)
        else:
            gather_finish(l + 1, pending[0])
            xs, h1 = _res_norm_mod(xs, pending[0], pending[1], ng1, sc1, sh1, f"res_norm1_l{l}")
        st["x_in"], st["h1"] = xs, h1
        if l < n_a:
            a = l
            zpre = _mm(h1, gw["gmlp_w_in", a], bmode="col", out_dtypes=(BF16,), name=f"gmlp_in_l{l}",
                       deps=behind_starts())
            if l == 0:
                gather_forward(1, zpre)
            bs_t = gmlp_bs[a].T
            p = _gmlp_mid_fwd(zpre, _row(ln_g_full[a]), _row(ln_b_full[a]), gmlp_ws[a], bs_t, f"gmlp_mid_l{l}")
            if l == 0:
                gather_finish(1, p)
            y = _mm(p, gw["gmlp_w_out", a], bmode="row", name=f"gmlp_out_l{l}", deps=behind_starts())
            st.update(zpre=zpre, p=p)
        else:
            if kv is None:
                kv_ng, kv_sh, kv_sc = _row(kv_norm_g), _row(kv_mod[0]), _row(kv_mod[1])
                hkv = _norm_mod(xs, kv_ng, kv_sc, kv_sh, "norm_kv")
                kvp = _mm(hkv, gw["w_kv", 0], bmode="col", name="kv_proj", deps=behind_starts())
                kk, vv = _head_norm(kvp, _row(k_norm_g), n_heads, "k_norm", tail=True)
                fl = _mm(hkv, w_f_pad, name="gate_logits")
                fcum = _fcum_fwd(fl, b_f_pad, "fcum")
                fk = fcum[:, :n_heads].T.reshape(n_heads, T // tkk, 1, tkk)
                kv = dict(x=xs, hkv=hkv, kvp=kvp, k=kk, v=vv, fl=fl, fcum=fcum, fk=fk, ng=kv_ng, sc=kv_sc)
            bl = l - n_a
            qp = _mm(h1, gw["attn_wq", bl], bmode="row", name=f"q_proj_l{l}", deps=behind_starts())
            q = _head_norm(qp, _row(q_norm_g[bl]), n_heads, f"q_norm_l{l}")[0]
            o, o32, lse = _attn_fwd(q, kv["k"], kv["v"], kv["fk"], n_heads, f"attn_l{l}")
            y = _mm(o, gw["attn_wo", bl], bmode="row", name=f"attn_out_l{l}")
            st.update(qp=qp, q=q, o=o, o32=o32, lse=lse)
        xs, h2 = _res_norm_mod(xs, y, g1, ng2, sc2, sh2, f"res_norm2_l{l}")
        a_pre, s_act = _mm(h2, gw["mlp_w1", l], bmode="col", out_dtypes=(BF16, BF16), epilogue=_relu2_epilogue,
                           name=f"mlp_up_l{l}")
        if l + 1 < depth:
            gather_forward(l + 2, a_pre)
        mo = _mm(s_act, gw["mlp_w2", l], bmode="row", name=f"mlp_down_l{l}", deps=behind_starts())
        st.update(y=y, x_mid=xs, h2=h2, a_pre=a_pre, s=s_act, m=mo)
        saved.append(st)
        pending = (mo, g2)

    dx, loss_row = _res_loss(xs, pending[0], pending[1], target, "loss")
    loss = lax.psum(loss_row[0, 0], ("x", "y", "c"))

    started = {}

    def scatter(dw_slabs, key, idx):
        started[(key, idx)] = _a2a_start(dw_slabs, f"scatter_{key}_l{idx}_start")
        tokens.append(started[(key, idx)][4])

    d_mod = [None] * depth
    d_norm_g = [None] * depth
    d_ln_g, d_ln_b, d_ws, d_bs = [None] * n_a, [None] * n_a, [None] * n_a, [None] * n_a
    d_qg = [None] * (depth - n_a)
    dk_list, dv_list, dfk_list = [], [], []
    small = {}

    for l in reversed(range(depth)):
        st = saved[l]
        dm, dg2 = _gate_bwd(dx, st["m"], st["g2"], f"gate2_bwd_l{l}")
        da = _mm(dm, gw["mlp_w2", l], tb=True, bmode="row", out_dtypes=(BF16,), epilogue=_relu2_bwd_epilogue,
                 extra=(st["a_pre"],), name=f"mlp_down_dx_l{l}")
        dw2 = _mm(st["s"], dm, ta=True, out_dtypes=(BF16,), name=f"mlp_down_dw_l{l}", deps=behind_starts())
        scatter(dw2.reshape(N_DEV, -1, D), "mlp_w2", l)
        dw1 = _mm(st["h2"], da, ta=True, out_mode="col", out_dtypes=(BF16,), name=f"mlp_up_dw_l{l}",
                  deps=behind_starts())
        scatter(dw1, "mlp_w1", l)
        dh2 = _mm(da, gw["mlp_w1", l], tb=True, bmode="col", out_dtypes=(BF16,), name=f"mlp_up_dx_l{l}",
                  deps=behind_starts())
        dx, sums2 = _norm_mod_bwd(st["x_mid"], [dh2], dx, st["ng2"], st["sc2"], f"norm2_bwd_l{l}")
        dy, dg1 = _gate_bwd(dx, st["y"], st["g1"], f"gate1_bwd_l{l}")
        if l < n_a:
            a = l
            dwo = _mm(st["p"], dy, ta=True, out_dtypes=(BF16,), name=f"gmlp_out_dw_l{l}", deps=behind_starts())
            scatter(dwo.reshape(N_DEV, -1, D), "gmlp_w_out", a)
            dp = _mm(dy, gw["gmlp_w_out", a], tb=True, bmode="row", out_dtypes=(BF16,), name=f"gmlp_out_dx_l{l}",
                     deps=behind_starts())
            dz, d_ws[a], dbs_t, dlg, dlb = _gmlp_mid_bwd(
                st["zpre"], dp, _row(ln_g_full[a]), _row(ln_b_full[a]), gmlp_ws[a],
                jnp.swapaxes(gmlp_ws[a], 1, 2), gmlp_bs[a].T, f"gmlp_mid_bwd_l{l}")
            d_bs[a], d_ln_g[a], d_ln_b[a] = dbs_t[:, :G].T, dlg[0], dlb[0]
            dwi = _mm(st["h1"], dz, ta=True, out_mode="col", out_dtypes=(BF16,), name=f"gmlp_in_dw_l{l}",
                      deps=behind_starts())
            scatter(dwi, "gmlp_w_in", a)
            dh1s = [_mm(dz, gw["gmlp_w_in", a], tb=True, bmode="col", out_dtypes=(BF16,), name=f"gmlp_in_dx_l{l}",
                        deps=behind_starts())]
        else:
            bl = l - n_a
            dwo = _mm(st["o"], dy, ta=True, out_dtypes=(BF16,), name=f"attn_out_dw_l{l}", deps=behind_starts())
            scatter(dwo.reshape(N_DEV, -1, D), "attn_wo", bl)
            do = _mm(dy, gw["attn_wo", bl], tb=True, bmode="row", out_dtypes=(BF16,), name=f"attn_out_dx_l{l}",
                     deps=behind_starts())
            dq, dk, dv, dfq, dfk = _attn_bwd(st["q"], kv["k"], kv["v"], st["o32"], do, st["lse"], kv["fk"],
                                             n_heads, f"attn_bwd_l{l}")
            dfk_list += [dfq, dfk]
            dk_list.append(dk)
            dv_list.append(dv)
            dqp, dqg = _head_norm_bwd(st["qp"], [dq], _row(q_norm_g[bl]), n_heads, f"q_norm_bwd_l{l}")
            d_qg[bl] = dqg[0]
            dwq = _mm(st["h1"], dqp, ta=True, out_dtypes=(BF16,), name=f"q_proj_dw_l{l}", deps=behind_starts())
            scatter(dwq.reshape(N_DEV, -1, D), "attn_wq", bl)
            dh1s = [_mm(dqp, gw["attn_wq", bl], tb=True, bmode="row", out_dtypes=(BF16,), name=f"q_proj_dx_l{l}",
                        deps=behind_starts())]
        dx, sums1 = _norm_mod_bwd(st["x_in"], dh1s, dx, st["ng1"], st["sc1"], f"norm1_bwd_l{l}")
        d_mod[l] = jnp.stack([sums1[0], sums1[1], dg1[0], sums2[0], sums2[1], dg2[0]])
        d_norm_g[l] = jnp.stack([sums1[2], sums2[2]])
        if l == n_a:
            dkvp, dkg = _head_norm_bwd(kv["kvp"], dk_list, _row(k_norm_g), n_heads, "k_norm_bwd", tails=dv_list)
            dfc = [jnp.pad(d.reshape(n_heads, T).T, ((0, 0), (0, BLK - n_heads))) for d in dfk_list]
            dfl, dbf = _fcum_bwd(dfc, kv["fl"], b_f_pad, "fcum_bwd")
            dwkv = _mm(kv["hkv"], dkvp, ta=True, out_mode="col", out_dtypes=(BF16,), name="kv_proj_dw",
                       deps=behind_starts())
            scatter(dwkv, "w_kv", 0)
            dwf = _mm(kv["hkv"], dfl, ta=True, name="gate_logits_dw", deps=behind_starts())
            dh_a = _mm(dkvp, gw["w_kv", 0], tb=True, bmode="col", out_dtypes=(BF16,), name="kv_proj_dx")
            dh_b = _mm(dfl, w_f_pad, tb=True, out_dtypes=(BF16,), name="gate_logits_dx")
            dx, sums_kv = _norm_mod_bwd(kv["x"], [dh_a, dh_b], dx, kv["ng"], kv["sc"], "norm_kv_bwd")
            small.update(d_kv_mod=jnp.stack([sums_kv[0], sums_kv[1]]), d_kv_norm_g=sums_kv[2], d_k_norm_g=dkg[0],
                         d_w_f=dwf[:, :n_heads], d_b_f=dbf[0, :n_heads])

    grad_x = dx.reshape(x.shape)

    mod_contrib = [jnp.stack(d_mod).reshape(depth, N_MOD * D), small["d_kv_mod"].reshape(-1)]
    rest_contrib = [jnp.stack(d_norm_g), jnp.stack(d_ln_g), jnp.stack(d_ln_b), jnp.stack(d_ws), jnp.stack(d_bs),
                    small["d_kv_norm_g"], small["d_k_norm_g"], small["d_w_f"], small["d_b_f"], jnp.stack(d_qg)]
    mod_shapes, rest_shapes = [a.shape for a in mod_contrib], [a.shape for a in rest_contrib]
    mod_started = _a2a_start(_pack(mod_contrib), "gather_mod_grads_start", bcast=True)
    rest_started = _a2a_start(_pack(rest_contrib), "gather_small_grads_start", bcast=True)
    grads, deltas, new_m, new_v = {}, {}, {}, {}

    def stack_of(n):
        lead = () if weights[n].ndim == 3 else (1,)
        return _AdamStack(*[a.reshape(lead + a.shape) for a in (weights[n], mom_m[n], mom_v[n])], f"adamw_{n}")

    def results_of(n, stack):
        grads[n], deltas[n], new_m[n], new_v[n] = [a.reshape(weights[n].shape) for a in stack.outs]

    mod_sent, mod_land = _a2a_wait(mod_started, rest_started[4], "gather_mod_grads_wait", bcast=True)
    is_me = (jnp.arange(N_DEV) == me).reshape(N_DEV, 1, 1)
    dmod_all, dkvmod_all = _unpack(jnp.where(is_me, mod_sent[None], mod_land), mod_shapes, lead=(N_DEV,))
    g_ada_b, g_kv_ada_b = _unpack(_sum_slabs(mod_land, mod_sent, who, "sum_mod_grads"), mod_shapes)
    sc_t = silu_all.T
    ada_stack, kv_ada_stack = stack_of("ada_w"), stack_of("kv_ada_w")
    for l in range(depth):
        ada_stack.from_outer(l, who, sc_t, _shard_of(dmod_all[:, l], 1, me, mod_cols))
    kv_ada_stack.from_outer(0, who, sc_t, _shard_of(dkvmod_all, 1, me, kv_cols))
    results_of("ada_w", ada_stack)
    results_of("kv_ada_w", kv_ada_stack)

    stacks = {n: stack_of(n) for n in ("mlp_w1", "mlp_w2", "gmlp_w_in", "gmlp_w_out", "attn_wq", "attn_wo", "w_kv")}
    after = ada_stack.outs[0]
    for l in reversed(range(depth)):
        keys = [("mlp_w2", l), ("mlp_w1", l)]
        keys += [("gmlp_w_out", l), ("gmlp_w_in", l)] if l < n_a else [("attn_wo", l - n_a), ("attn_wq", l - n_a)]
        keys += [("w_kv", 0)] if l == n_a else []
        landed = {}
        for key in keys:
            sent, land = _a2a_wait(started[key], after, f"scatter_{key[0]}_l{key[1]}_wait")
            landed[key] = (land, sent)
        for key in keys:
            stacks[key[0]].from_parts(key[1], who, *landed[key])
            after = stacks[key[0]].outs[0]
    for n, stack in stacks.items():
        results_of(n, stack)

    rest_sent, rest_land = _a2a_wait(rest_started, after, "gather_small_grads_wait", bcast=True)
    (g_norm_g_full, g_ln_g_full, g_ln_b_full, g_ws, g_bs, g_kv_norm_g, g_k_norm_g, g_w_f_full, g_b_f,
     g_q_norm_g) = _unpack(_sum_slabs(rest_land, rest_sent, who, "sum_small_grads"), rest_shapes)
    small_grads = {
        "ada_b": g_ada_b, "kv_ada_b": g_kv_ada_b.reshape(kv_ada_b.shape),
        "norm_g": _shard_of(g_norm_g_full, 2, me, norm_g.shape[2]),
        "gmlp_ln_g": _shard_of(g_ln_g_full, 1, me, gmlp_ln_g.shape[1]),
        "gmlp_ln_b": _shard_of(g_ln_b_full, 1, me, gmlp_ln_b.shape[1]),
        "gmlp_ws": g_ws, "gmlp_bs": g_bs, "kv_norm_g": g_kv_norm_g, "k_norm_g": g_k_norm_g,
        "w_f": _shard_of(g_w_f_full, 0, me, w_f.shape[0]), "b_f": g_b_f, "q_norm_g": g_q_norm_g,
    }
    small_names = list(small_grads)
    small_w_shapes = [weights[n].shape for n in small_names]
    d_pack, m_pack, v_pack = _adamw_flat(_pack([weights[n] for n in small_names]),
                                         _pack([small_grads[n] for n in small_names]),
                                         _pack([mom_m[n] for n in small_names]), _pack([mom_v[n] for n in small_names]),
                                         "adamw_small")
    grads.update(small_grads)
    deltas.update(zip(small_names, _unpack(d_pack, small_w_shapes)))
    new_m.update(zip(small_names, _unpack(m_pack, small_w_shapes)))
    new_v.update(zip(small_names, _unpack(v_pack, small_w_shapes)))

    return (loss, grad_x, *[grads[n] for n in WEIGHT_NAMES], *[deltas[n] for n in WEIGHT_NAMES],
            *[new_m[n] for n in WEIGHT_NAMES], *[new_v[n] for n in WEIGHT_NAMES])
```

```python
import functools
import math

import jax
import jax.numpy as jnp
from jax import lax
from jax.experimental import pallas as pl
from jax.experimental.pallas import tpu as pltpu

F32 = jnp.float32
BF16 = jnp.bfloat16
N_DEV = 8
EPS = 1e-6
CHUNK = 64
BLK = 128
N_MOD = 6
ADAM_LR = 0.001
ADAM_B1 = 0.9
ADAM_B2 = 0.999
ADAM_EPS = 1e-08
ADAM_WD = 0.01
ADAM_STEP = 10
VMEM_LIMIT_BYTES = 56 * 2 ** 20
NEG_BIG = -1e30
WEIGHT_NAMES = ['ada_w', 'ada_b', 'norm_g', 'mlp_w1', 'mlp_w2', 'gmlp_w_in', 'gmlp_ln_g', 'gmlp_ln_b', 'gmlp_ws',
                'gmlp_bs', 'gmlp_w_out', 'kv_norm_g', 'kv_ada_w', 'kv_ada_b', 'w_kv', 'k_norm_g', 'w_f', 'b_f',
                'attn_wq', 'q_norm_g', 'attn_wo']
MESH = pl.DeviceIdType.MESH


def _params(sem):
    return pltpu.CompilerParams(dimension_semantics=sem, vmem_limit_bytes=VMEM_LIMIT_BYTES)


def _tile(n, cap, unit=128):
    if n <= cap:
        return n
    t = (cap // unit) * unit
    while t > unit and n % t:
        t -= unit
    assert n % t == 0, (n, cap, unit)
    return t


def _my_index():
    return 4 * lax.axis_index("x") + 2 * lax.axis_index("y") + lax.axis_index("c")


def _all_to_all(x, name, bcast=False):
    slab = x.shape if bcast else x.shape[1:]

    def body(x_ref, o_ref, send_sems, recv_sems, local_sem):
        me = _my_index()

        def src(j):
            return x_ref if bcast else x_ref.at[j]

        mine = pltpu.make_async_copy(src(me), o_ref.at[me], local_sem)
        mine.start()
        sends = []
        for d in range(1, N_DEV):
            peer = (me + d) % N_DEV
            cp = pltpu.make_async_remote_copy(
                src_ref=src(peer), dst_ref=o_ref.at[me],
                send_sem=send_sems.at[d - 1], recv_sem=recv_sems.at[d - 1],
                device_id=(peer // 4, (peer // 2) % 2, peer % 2), device_id_type=MESH)
            cp.start()
            sends.append(cp)
        for d in range(1, N_DEV):
            frm = (me + N_DEV - d) % N_DEV
            pltpu.make_async_remote_copy(
                src_ref=src(frm), dst_ref=o_ref.at[frm],
                send_sem=send_sems.at[d - 1], recv_sem=recv_sems.at[d - 1],
                device_id=(frm // 4, (frm // 2) % 2, frm % 2), device_id_type=MESH).wait_recv()
        for cp in sends:
            cp.wait_send()
        mine.wait()

    return pl.pallas_call(
        body, name=name,
        out_shape=jax.ShapeDtypeStruct((N_DEV,) + tuple(slab), x.dtype),
        in_specs=[pl.BlockSpec(memory_space=pl.ANY)],
        out_specs=pl.BlockSpec(memory_space=pl.ANY),
        scratch_shapes=[pltpu.SemaphoreType.DMA((N_DEV - 1,)), pltpu.SemaphoreType.DMA((N_DEV - 1,)),
                        pltpu.SemaphoreType.DMA],
        compiler_params=pltpu.CompilerParams(has_side_effects=True),
    )(x)


_HBM = pl.BlockSpec(memory_space=pltpu.HBM)
_SEM = pl.BlockSpec(memory_space=pltpu.SEMAPHORE)
_ANY = pl.BlockSpec(memory_space=pl.ANY)
_DATAFLOW = pltpu.SideEffectType.DATAFLOW_SIDE_EFFECTING


def _a2a_peer_copy(x_ref, land_ref, send_sems, recv_sems, d, me, incoming, bcast):
    peer = (me + N_DEV - d) % N_DEV if incoming else (me + d) % N_DEV
    return pltpu.make_async_remote_copy(
        src_ref=x_ref if bcast else x_ref.at[peer], dst_ref=land_ref.at[peer if incoming else me],
        send_sem=send_sems.at[d - 1], recv_sem=recv_sems.at[d - 1],
        device_id=(peer // 4, (peer // 2) % 2, peer % 2), device_id_type=MESH)


def _a2a_start(x, name, bcast=False):
    land_shape = ((N_DEV,) + tuple(x.shape)) if bcast else x.shape

    def body(x_ref, land_ref, send_sems, recv_sems, x_thru, land_thru, token):
        me = _my_index()
        for d in range(1, N_DEV):
            _a2a_peer_copy(x_ref, land_ref, send_sems, recv_sems, d, me, False, bcast).start()
        token[...] = jnp.zeros_like(token)

    return pl.pallas_call(
        body, name=name,
        out_shape=(pltpu.SemaphoreType.DMA((N_DEV - 1,)), pltpu.SemaphoreType.DMA((N_DEV - 1,)),
                   pltpu.HBM(x.shape, x.dtype), pltpu.HBM(land_shape, x.dtype), jax.ShapeDtypeStruct((8, BLK), F32)),
        in_specs=(_HBM, _HBM), out_specs=(_SEM, _SEM, _HBM, _HBM, pl.BlockSpec(memory_space=pltpu.VMEM)),
        input_output_aliases={0: 2, 1: 3},
        compiler_params=pltpu.CompilerParams(has_side_effects=_DATAFLOW),
    )(pltpu.with_memory_space_constraint(x, pltpu.HBM),
      pltpu.with_memory_space_constraint(lax.empty(land_shape, x.dtype), pltpu.HBM))


def _a2a_wait(started, after, name, bcast=False):
    send_sems, recv_sems, x_thru, land_thru, _ = started

    def body(x_ref, land_ref, send_sems, recv_sems, after_ref, x_dead, land_out):
        me = _my_index()
        for d in range(1, N_DEV):
            _a2a_peer_copy(x_ref, land_ref, send_sems, recv_sems, d, me, False, bcast).wait_send()
        for d in range(1, N_DEV):
            _a2a_peer_copy(x_ref, land_ref, send_sems, recv_sems, d, me, True, bcast).wait_recv()

    return pl.pallas_call(
        body, name=name,
        out_shape=(pltpu.HBM(x_thru.shape, x_thru.dtype), pltpu.HBM(land_thru.shape, land_thru.dtype)),
        in_specs=(_HBM, _HBM, _SEM, _SEM, _ANY), out_specs=(_HBM, _HBM),
        input_output_aliases={0: 0, 1: 1},
        compiler_params=pltpu.CompilerParams(has_side_effects=_DATAFLOW),
    )(x_thru, land_thru, send_sems, recv_sems, after)


def _place_shard(w, layer, who, name, deps=()):
    _, R, C = w.shape
    tr = _row_tile(R, C)

    def body(who_ref, w_ref, *rest):
        rest[-1][...] = w_ref[...].astype(BF16)

    return pl.pallas_call(
        body, name=name,
        grid_spec=pltpu.PrefetchScalarGridSpec(
            num_scalar_prefetch=1, grid=(R // tr,),
            in_specs=[pl.BlockSpec((None, tr, C), lambda i, who_ref: (layer, i, 0))] + [_ANY for _ in deps],
            out_specs=pl.BlockSpec((None, None, tr, C), lambda i, who_ref: (who_ref[0], 0, i, 0))),
        out_shape=_sds((N_DEV, 1, R, C), BF16),
        compiler_params=_params(("parallel",)),
    )(who, w, *deps)


def _gather_copies(land_ref, send_sems, recv_sems, base, phase, incoming):
    cx, cy, cc = lax.axis_index("x"), lax.axis_index("y"), lax.axis_index("c")
    sibling = (cx, cy, 1 - cc)
    chips = [(1 - cx, cy), (cx, 1 - cy), (1 - cx, 1 - cy)]
    if phase == 1:
        out = [((cx, cy, cc), sibling)] + [((cx, cy, cc), (*chip, cc)) for chip in chips]
        inc = [(sibling, sibling)] + [((*chip, cc), (*chip, cc)) for chip in chips]
    else:
        out = [((*chip, cc), sibling) for chip in chips]
        inc = [((*chip, 1 - cc), sibling) for chip in chips]
    copies = []
    for k, (block, peer) in enumerate(inc if incoming else out):
        slab = land_ref.at[4 * block[0] + 2 * block[1] + block[2]]
        copies.append(pltpu.make_async_remote_copy(
            src_ref=slab, dst_ref=slab, send_sem=send_sems.at[base + k], recv_sem=recv_sems.at[base + k],
            device_id=peer, device_id_type=MESH))
    return copies


def _gather_start(lands, phase, name):
    n, per = len(lands), (4 if phase == 1 else 3)

    def body(*refs):
        land_refs, send_sems, recv_sems, token = refs[:n], refs[n], refs[n + 1], refs[-1]
        for a, land_ref in enumerate(land_refs):
            for cp in _gather_copies(land_ref, send_sems, recv_sems, a * per, phase, False):
                cp.start()
        token[...] = jnp.zeros_like(token)

    outs = pl.pallas_call(
        body, name=name,
        out_shape=(pltpu.SemaphoreType.DMA((n * per,)), pltpu.SemaphoreType.DMA((n * per,)),
                   *[pltpu.HBM(x.shape, x.dtype) for x in lands], jax.ShapeDtypeStruct((8, BLK), F32)),
        in_specs=tuple(_HBM for _ in lands),
        out_specs=(_SEM, _SEM, *[_HBM for _ in lands], pl.BlockSpec(memory_space=pltpu.VMEM)),
        input_output_aliases={a: 2 + a for a in range(n)},
        compiler_params=pltpu.CompilerParams(has_side_effects=_DATAFLOW),
    )(*[pltpu.with_memory_space_constraint(x, pltpu.HBM) for x in lands])
    return outs[0], outs[1], list(outs[2:2 + n]), outs[-1]


def _gather_wait(started, phase, after, name):
    send_sems, recv_sems, lands, _ = started
    n, per = len(lands), (4 if phase == 1 else 3)

    def body(*refs):
        land_refs, send_sems, recv_sems = refs[:n], refs[n], refs[n + 1]
        for a, land_ref in enumerate(land_refs):
            for cp in _gather_copies(land_ref, send_sems, recv_sems, a * per, phase, False):
                cp.wait_send()
            for cp in _gather_copies(land_ref, send_sems, recv_sems, a * per, phase, True):
                cp.wait_recv()

    outs = pl.pallas_call(
        body, name=name,
        out_shape=tuple(pltpu.HBM(x.shape, x.dtype) for x in lands),
        in_specs=(*[_HBM for _ in lands], _SEM, _SEM, _ANY), out_specs=tuple(_HBM for _ in lands),
        input_output_aliases={a: a for a in range(n)},
        compiler_params=pltpu.CompilerParams(has_side_effects=_DATAFLOW),
    )(*lands, send_sems, recv_sems, after)
    return list(outs)


def _mm(a, b, *, name, ta=False, tb=False, bmode="plain", layer=0, out_mode="plain", out_dtypes=(F32,),
        epilogue=None, extra=(), caps=(1024, 1024, 2048), deps=()):
    if ta:
        K, M = a.shape
    else:
        M, K = a.shape
    n_unit = k_unit = None
    if bmode == "plain":
        N, Kb = (b.shape if tb else b.shape[::-1])
    elif bmode == "col":
        _, _, Kw, Ns = b.shape
        if tb:
            N, Kb, k_unit = Kw, N_DEV * Ns, Ns
        else:
            N, Kb, n_unit = N_DEV * Ns, Kw, Ns
    else:
        _, _, Ks, Nw = b.shape
        if tb:
            N, Kb, n_unit = N_DEV * Ks, Nw, Ks
        else:
            N, Kb, k_unit = Nw, N_DEV * Ks, Ks
    assert K == Kb, (name, a.shape, b.shape)
    if out_mode == "col":
        assert n_unit is None
        n_unit = N // N_DEV
    tm = _tile(M, caps[0])
    tn = _tile(n_unit or N, caps[1])
    span = 1
    if k_unit and 2 * k_unit <= caps[2]:
        while 2 * span * k_unit <= caps[2] and N_DEV % (2 * span) == 0:
            span *= 2
        tk = span * k_unit
    else:
        tk = _tile(k_unit or K, caps[2])
    nk = K // tk
    npb = (n_unit // tn) if n_unit else None
    kpb = (k_unit // tk) if (k_unit and span == 1) else None
    grid = (M // tm, N // tn, nk)

    a_spec = pl.BlockSpec((tk, tm), lambda i, j, k: (k, i)) if ta else pl.BlockSpec((tm, tk), lambda i, j, k: (i, k))
    if bmode == "plain":
        b_spec = (pl.BlockSpec((tn, tk), lambda i, j, k: (j, k)) if tb
                  else pl.BlockSpec((tk, tn), lambda i, j, k: (k, j)))
    elif bmode == "col":
        if tb and span > 1:
            b_spec = pl.BlockSpec((span, None, tn, k_unit), lambda i, j, k: (k, layer, j, 0))
        elif tb:
            b_spec = pl.BlockSpec((None, None, tn, tk), lambda i, j, k: (k // kpb, layer, j, k % kpb))
        else:
            b_spec = pl.BlockSpec((None, None, tk, tn), lambda i, j, k: (j // npb, layer, k, j % npb))
    else:
        if tb:
            b_spec = pl.BlockSpec((None, None, tn, tk), lambda i, j, k: (j // npb, layer, j % npb, k))
        elif span > 1:
            b_spec = pl.BlockSpec((span, None, k_unit, tn), lambda i, j, k: (k, layer, 0, j))
        else:
            b_spec = pl.BlockSpec((None, None, tk, tn), lambda i, j, k: (k // kpb, layer, k % kpb, j))
    mn_spec = pl.BlockSpec((tm, tn), lambda i, j, k: (i, j))
    if out_mode == "col":
        o_specs = [pl.BlockSpec((None, tm, tn), lambda i, j, k: (j // npb, i, j % npb))]
        o_shapes = [jax.ShapeDtypeStruct((N_DEV, M, N // N_DEV), out_dtypes[0])]
    else:
        o_specs = [mn_spec for _ in out_dtypes]
        o_shapes = [jax.ShapeDtypeStruct((M, N), dt) for dt in out_dtypes]
    dims = (((0 if ta else 1,), (1 if tb else 0,)), ((), ()))
    n_extra, n_out, n_dep = len(extra), len(out_dtypes), len(deps)

    def body(a_ref, b_ref, *rest):
        extra_refs, out_refs = rest[:n_extra], rest[n_extra + n_dep:n_extra + n_dep + n_out]
        k = pl.program_id(2)

        def product():
            if span == 1:
                return lax.dot_general(a_ref[...].astype(BF16), b_ref[...].astype(BF16), dims,
                                       preferred_element_type=F32)
            if not tb:
                return lax.dot_general(a_ref[...].astype(BF16), b_ref[...].reshape(tk, tn).astype(BF16), dims,
                                       preferred_element_type=F32)
            out = None
            for s in range(span):
                part = lax.dot_general(a_ref[:, s * k_unit:(s + 1) * k_unit].astype(BF16), b_ref[s].astype(BF16),
                                       dims, preferred_element_type=F32)
                out = part if out is None else out + part
            return out

        def finish(acc):
            outs = (acc,) if epilogue is None else epilogue(acc, *[r[...] for r in extra_refs])
            for o_ref, val in zip(out_refs, outs):
                o_ref[...] = val.astype(o_ref.dtype)

        if nk == 1:
            finish(product())
            return
        acc_ref = rest[-1]

        @pl.when(k == 0)
        def _():
            acc_ref[...] = product()

        if nk > 2:
            @pl.when(jnp.logical_and(k > 0, k < nk - 1))
            def _():
                acc_ref[...] += product()

        @pl.when(k == nk - 1)
        def _():
            finish(acc_ref[...] + product())

    outs = pl.pallas_call(
        body, name=name, grid=grid,
        in_specs=[a_spec, b_spec] + [mn_spec for _ in extra] + [_ANY for _ in deps],
        out_specs=o_specs, out_shape=o_shapes,
        scratch_shapes=[pltpu.VMEM((tm, tn), F32)] if nk > 1 else [],
        compiler_params=_params(("parallel", "parallel", "arbitrary")),
    )(a, b, *extra, *deps)
    return outs[0] if n_out == 1 else outs


def _relu2_epilogue(acc):
    r = jnp.maximum(acc, 0.0)
    return acc, r * r


def _relu2_bwd_epilogue(acc, a_pre):
    return (acc * (2.0 * jnp.maximum(a_pre.astype(F32), 0.0)),)


def _rowcall(body, *, name, tr, row_ins, full_ins=(), row_outs=(), acc_outs=(), scratch=(), reverse=False):
    T = row_ins[0].shape[0]
    nb = T // tr
    rmap = (lambda i: (nb - 1 - i, 0)) if reverse else (lambda i: (i, 0))

    def full_spec(shape):
        nd = len(shape)
        return pl.BlockSpec(tuple(shape), lambda i: (0,) * nd)

    in_specs = [pl.BlockSpec((tr, a.shape[1]), rmap) for a in row_ins] + [full_spec(a.shape) for a in full_ins]
    out_specs = [pl.BlockSpec((tr, s.shape[1]), rmap) for s in row_outs] + [full_spec(s.shape) for s in acc_outs]
    outs = pl.pallas_call(
        body, name=name, grid=(nb,), in_specs=in_specs, out_specs=out_specs,
        out_shape=list(row_outs) + list(acc_outs), scratch_shapes=list(scratch),
        compiler_params=_params(("arbitrary",)),
    )(*row_ins, *full_ins)
    return outs


def _sds(shape, dtype):
    return jax.ShapeDtypeStruct(tuple(shape), dtype)


def _row_tile(T, C, elems=512 * 1024):
    t = max(8, min(T, elems // C))
    p = 8
    while p * 2 <= t and T % (p * 2) == 0:
        p *= 2
    return p


def _norm_mod(x, ng, sc, sh, name):
    T, D = x.shape

    def body(x_ref, ng_ref, sc_ref, sh_ref, h_ref):
        xv = x_ref[...]
        r = lax.rsqrt(jnp.mean(xv * xv, axis=-1, keepdims=True) + EPS)
        h_ref[...] = (((xv * r) * ng_ref[...]) * (1.0 + sc_ref[...]) + sh_ref[...]).astype(BF16)

    return _rowcall(body, name=name, tr=_row_tile(T, D), row_ins=[x], full_ins=[ng, sc, sh],
                    row_outs=[_sds((T, D), BF16)])[0]


def _res_norm_mod(x, y, gate, ng, sc, sh, name):
    T, D = x.shape

    def body(x_ref, y_ref, g_ref, ng_ref, sc_ref, sh_ref, x2_ref, h_ref):
        xv = x_ref[...] + g_ref[...] * y_ref[...]
        x2_ref[...] = xv
        r = lax.rsqrt(jnp.mean(xv * xv, axis=-1, keepdims=True) + EPS)
        h_ref[...] = (((xv * r) * ng_ref[...]) * (1.0 + sc_ref[...]) + sh_ref[...]).astype(BF16)

    return _rowcall(body, name=name, tr=_row_tile(T, D, 256 * 1024), row_ins=[x, y], full_ins=[gate, ng, sc, sh],
                    row_outs=[_sds((T, D), F32), _sds((T, D), BF16)])


def _res_loss(x, y, gate, target, name):
    T, D = x.shape

    def body(x_ref, y_ref, t_ref, g_ref, dout_ref, dy_ref, loss_ref, dg_ref):
        @pl.when(pl.program_id(0) == 0)
        def _():
            loss_ref[...] = jnp.zeros_like(loss_ref)
            dg_ref[...] = jnp.zeros_like(dg_ref)

        yv, gv = y_ref[...], g_ref[...]
        diff = x_ref[...] + gv * yv - t_ref[...]
        dout = diff * (1.0 / D)
        dout_ref[...] = dout
        dy_ref[...] = (dout * gv).astype(BF16)
        loss_ref[...] += jnp.sum(diff * diff) * (0.5 / D)
        dg_ref[...] += jnp.sum(dout * yv, axis=0, keepdims=True)

    return _rowcall(body, name=name, tr=_row_tile(T, D, 256 * 1024), row_ins=[x, y, target], full_ins=[gate],
                    row_outs=[_sds((T, D), F32), _sds((T, D), BF16)],
                    acc_outs=[_sds((1, BLK), F32), _sds((1, D), F32)])


def _norm_mod_bwd(x, dhs, dres, ng, sc, name, gated=None):
    T, D = x.shape
    n_dh = len(dhs)
    n_row = 2 + n_dh + (1 if gated else 0)

    def body(*refs):
        x_ref, dh_refs, dres_ref = refs[0], refs[1:1 + n_dh], refs[1 + n_dh]
        ng_ref, sc_ref = refs[n_row:n_row + 2]
        outs = refs[n_row + (3 if gated else 2):]
        dx_ref, sums_ref = (outs[0], outs[2]) if gated else (outs[0], outs[1])

        @pl.when(pl.program_id(0) == 0)
        def _():
            sums_ref[...] = jnp.zeros_like(sums_ref)
            if gated:
                outs[3][...] = jnp.zeros_like(outs[3])

        xv = x_ref[...]
        dh = dh_refs[0][...].astype(F32)
        for r_ in dh_refs[1:]:
            dh = dh + r_[...].astype(F32)
        r = lax.rsqrt(jnp.mean(xv * xv, axis=-1, keepdims=True) + EPS)
        n = xv * r
        ngv, scale1 = ng_ref[...], 1.0 + sc_ref[...]
        dn = dh * (ngv * scale1)
        dx = dres_ref[...] + r * (dn - n * jnp.mean(dn * n, axis=-1, keepdims=True))
        dx_ref[...] = dx
        dhn = dh * n
        sums_ref[0:1, :] += jnp.sum(dh, axis=0, keepdims=True)
        sums_ref[1:2, :] += jnp.sum(dhn * ngv, axis=0, keepdims=True)
        sums_ref[2:3, :] += jnp.sum(dhn * scale1, axis=0, keepdims=True)
        if gated:
            y_ref, g_ref = refs[2 + n_dh], refs[n_row + 2]
            outs[1][...] = (dx * g_ref[...]).astype(BF16)
            outs[3][...] += jnp.sum(dx * y_ref[...], axis=0, keepdims=True)

    return _rowcall(body, name=name, tr=_row_tile(T, D, 256 * 1024),
                    row_ins=[x, *dhs, dres] + ([gated[0]] if gated else []),
                    full_ins=[ng, sc] + ([gated[1]] if gated else []),
                    row_outs=[_sds((T, D), F32)] + ([_sds((T, D), BF16)] if gated else []),
                    acc_outs=[_sds((8, D), F32)] + ([_sds((1, D), F32)] if gated else []))


def _head_norm(x, g, n_heads, name, tail=False):
    T = x.shape[0]
    D = n_heads * BLK
    W = x.shape[1] if tail else D

    def body(x_ref, g_ref, o_ref, *tail_ref):
        for h in range(n_heads):
            xv = x_ref[:, h * BLK:(h + 1) * BLK]
            r = lax.rsqrt(jnp.mean(xv * xv, axis=-1, keepdims=True) + EPS)
            o_ref[:, h * BLK:(h + 1) * BLK] = ((xv * r) * g_ref[...]).astype(BF16)
        if tail:
            tail_ref[0][...] = x_ref[:, D:2 * D].astype(BF16)

    tr = _row_tile(T, x.shape[1])
    o_spec = pl.BlockSpec((tr, D), lambda i: (i, 0))
    return pl.pallas_call(
        body, name=name, grid=(T // tr,),
        in_specs=[pl.BlockSpec((tr, W), lambda i: (i, 0)), pl.BlockSpec((1, BLK), lambda i: (0, 0))],
        out_specs=[o_spec] * (2 if tail else 1), out_shape=[_sds((T, D), BF16)] * (2 if tail else 1),
        compiler_params=_params(("parallel",)),
    )(x, g)


def _head_norm_bwd(x, dys, g, n_heads, name, tails=()):
    T = x.shape[0]
    D = n_heads * BLK
    n_dy, n_tail = len(dys), len(tails)
    W = 2 * D if tails else D

    def body(*refs):
        x_ref, dy_refs, tail_refs = refs[0], refs[1:1 + n_dy], refs[1 + n_dy:1 + n_dy + n_tail]
        g_ref, dx_ref, dg_ref = refs[1 + n_dy + n_tail:]

        @pl.when(pl.program_id(0) == 0)
        def _():
            dg_ref[...] = jnp.zeros_like(dg_ref)

        tot = jnp.zeros((1, BLK), F32)
        for h in range(n_heads):
            cols = slice(h * BLK, (h + 1) * BLK)
            xv = x_ref[:, cols]
            dyv = dy_refs[0][:, cols]
            for r_ in dy_refs[1:]:
                dyv = dyv + r_[:, cols]
            r = lax.rsqrt(jnp.mean(xv * xv, axis=-1, keepdims=True) + EPS)
            n = xv * r
            dn = dyv * g_ref[...]
            dx_ref[:, cols] = (r * (dn - n * jnp.mean(dn * n, axis=-1, keepdims=True))).astype(BF16)
            tot = tot + jnp.sum(dyv * n, axis=0, keepdims=True)
        dg_ref[0:1, :] += tot
        if n_tail:
            tv = tail_refs[0][...]
            for r_ in tail_refs[1:]:
                tv = tv + r_[...]
            dx_ref[:, D:] = tv.astype(BF16)

    tr = _row_tile(T, 2 * D, 256 * 1024)
    d_spec = pl.BlockSpec((tr, D), lambda i: (i, 0))
    return pl.pallas_call(
        body, name=name, grid=(T // tr,),
        in_specs=[d_spec] * (1 + n_dy + n_tail) + [pl.BlockSpec((1, BLK), lambda i: (0, 0))],
        out_specs=[pl.BlockSpec((tr, W), lambda i: (i, 0)), pl.BlockSpec((8, BLK), lambda i: (0, 0))],
        out_shape=[_sds((T, W), BF16), _sds((8, BLK), F32)],
        compiler_params=_params(("arbitrary",)),
    )(x, *dys, *tails, g)


def _fcum_fwd(fl, bf, name):
    T = fl.shape[0]

    def body(fl_ref, b_ref, o_ref, carry_ref):
        @pl.when(pl.program_id(0) == 0)
        def _():
            carry_ref[...] = jnp.zeros_like(carry_ref)

        z = fl_ref[...] + b_ref[...]
        logf = jnp.minimum(z, 0.0) - jnp.log(1.0 + jnp.exp(-jnp.abs(z)))
        row = lax.broadcasted_iota(jnp.int32, (BLK, BLK), 0)
        col = lax.broadcasted_iota(jnp.int32, (BLK, BLK), 1)
        tri = (col <= row).astype(F32)
        run = jnp.dot(tri, logf, preferred_element_type=F32, precision=lax.Precision.HIGHEST) + carry_ref[0:1, :]
        o_ref[...] = run
        carry_ref[0:1, :] = run[BLK - 1:BLK, :]

    return _rowcall(body, name=name, tr=BLK, row_ins=[fl], full_ins=[bf], row_outs=[_sds((T, BLK), F32)],
                    scratch=[pltpu.VMEM((8, BLK), F32)])[0]


def _fcum_bwd(dfs, fl, bf, name):
    T = fl.shape[0]
    n_df = len(dfs)

    def body(*refs):
        df_refs = refs[:n_df]
        fl_ref, b_ref, dfl_ref, dbias_ref, carry_ref = refs[n_df:]

        @pl.when(pl.program_id(0) == 0)
        def _():
            carry_ref[...] = jnp.zeros_like(carry_ref)
            dbias_ref[...] = jnp.zeros_like(dbias_ref)

        dfc = df_refs[0][...]
        for r_ in df_refs[1:]:
            dfc = dfc + r_[...]
        row = lax.broadcasted_iota(jnp.int32, (BLK, BLK), 0)
        col = lax.broadcasted_iota(jnp.int32, (BLK, BLK), 1)
        tri = (col >= row).astype(F32)
        suffix = jnp.dot(tri, dfc, preferred_element_type=F32, precision=lax.Precision.HIGHEST) + carry_ref[0:1, :]
        carry_ref[0:1, :] = suffix[0:1, :]
        z = fl_ref[...] + b_ref[...]
        dfl = suffix / (1.0 + jnp.exp(z))
        dfl_ref[...] = dfl.astype(BF16)
        dbias_ref[0:1, :] += jnp.sum(dfl, axis=0, keepdims=True)

    return _rowcall(body, name=name, tr=BLK, row_ins=[*dfs, fl], full_ins=[bf], reverse=True,
                    row_outs=[_sds((T, BLK), BF16)], acc_outs=[_sds((8, BLK), F32)],
                    scratch=[pltpu.VMEM((8, BLK), F32)])


def _attn_tile(T):
    return min(T, 512)


def _attn_fwd(q, k, v, fk, n_heads, name):
    T = q.shape[0]
    tq = tk = _attn_tile(T)
    nkb = T // tk
    inv_sqrt = 1.0 / float(math.sqrt(BLK))

    def body(q_ref, k_ref, v_ref, fk_ref, o_ref, o32_ref, lse_ref):
        i = pl.program_id(1)
        qv = q_ref[...]

        def block(j, carry, diagonal):
            m, l, acc = carry
            rows = pl.ds(pl.multiple_of(j * tk, tk), tk)
            kj, vj = k_ref[rows, :], v_ref[rows, :]
            s = lax.dot_general(qv, kj, (((1,), (1,)), ((), ())), preferred_element_type=F32) * inv_sqrt
            s = s - fk_ref[j]
            if diagonal:
                s = jnp.where(lax.broadcasted_iota(jnp.int32, (tq, tk), 1)
                              <= lax.broadcasted_iota(jnp.int32, (tq, tk), 0), s, NEG_BIG)
            m_new = jnp.maximum(m, jnp.max(s, axis=-1, keepdims=True))
            alpha = jnp.exp(m - m_new)
            p = jnp.exp(s - m_new)
            l = alpha * l + jnp.sum(p, axis=-1, keepdims=True)
            acc = alpha * acc + jnp.dot(p.astype(BF16), vj, preferred_element_type=F32)
            return m_new, l, acc

        init = (jnp.full((tq, 1), NEG_BIG, F32), jnp.zeros((tq, 1), F32), jnp.zeros((tq, BLK), F32))
        carry = lax.fori_loop(0, i, lambda j, c: block(j, c, False), init)
        m, l, acc = block(i, carry, True)
        out = acc / l
        o_ref[...] = out.astype(BF16)
        o32_ref[...] = out
        lse_ref[...] = m + jnp.log(l)

    return pl.pallas_call(
        body, name=name, grid=(n_heads, T // tq),
        in_specs=[pl.BlockSpec((tq, BLK), lambda h, i: (i, h)),
                  pl.BlockSpec((T, BLK), lambda h, i: (0, h)),
                  pl.BlockSpec((T, BLK), lambda h, i: (0, h)),
                  pl.BlockSpec((None, nkb, 1, tk), lambda h, i: (h, 0, 0, 0))],
        out_specs=[pl.BlockSpec((tq, BLK), lambda h, i: (i, h)),
                   pl.BlockSpec((tq, BLK), lambda h, i: (i, h)),
                   pl.BlockSpec((None, tq, 1), lambda h, i: (h, i, 0))],
        out_shape=[_sds((T, n_heads * BLK), BF16), _sds((T, n_heads * BLK), F32), _sds((n_heads, T, 1), F32)],
        compiler_params=_params(("parallel", "arbitrary")),
    )(q, k, v, fk)


def _attn_bwd(q, k, v, o, do, lse, fk, n_heads, name):
    T = q.shape[0]
    tq = tk = _attn_tile(T)
    nkb = T // tk
    nq = T // tq
    inv_sqrt = 1.0 / float(math.sqrt(BLK))
    tn_dims = (((0,), (0,)), ((), ()))
    nt_dims = (((1,), (1,)), ((), ()))

    def body(q_ref, k_ref, v_ref, o_ref, do_ref, lse_ref, fk_ref, dq_ref, dk_ref, dv_ref, dfq_ref, dfk_ref, delta_ref):
        j = pl.program_id(1)

        @pl.when(j == 0)
        def _():
            delta_ref[...] = jnp.sum(do_ref[...].astype(F32) * o_ref[...], axis=1, keepdims=True)
            dq_ref[...] = jnp.zeros_like(dq_ref)
            dfq_ref[...] = jnp.zeros_like(dfq_ref)

        kj, vj, fkv = k_ref[...], v_ref[...], fk_ref[...]

        def step(i, carry, diagonal):
            dk, dv, dfk = carry
            rows = pl.ds(pl.multiple_of(i * tq, tq), tq)
            qi, doi = q_ref[rows, :], do_ref[rows, :]
            s = lax.dot_general(qi, kj, nt_dims, preferred_element_type=F32) * inv_sqrt - fkv
            if diagonal:
                s = jnp.where(lax.broadcasted_iota(jnp.int32, (tq, tk), 1)
                              <= lax.broadcasted_iota(jnp.int32, (tq, tk), 0), s, NEG_BIG)
            p = jnp.exp(s - lse_ref[rows, :])
            dv = dv + lax.dot_general(p.astype(BF16), doi, tn_dims, preferred_element_type=F32)
            dp = lax.dot_general(doi, vj, nt_dims, preferred_element_type=F32)
            ds = p * (dp - delta_ref[rows, :])
            dsb = ds.astype(BF16)
            dq_ref[rows, :] += jnp.dot(dsb, kj, preferred_element_type=F32) * inv_sqrt
            dk = dk + lax.dot_general(dsb, qi, tn_dims, preferred_element_type=F32)
            dfq_ref[rows, :] += jnp.sum(ds, axis=1, keepdims=True)
            dfk = dfk - jnp.sum(ds, axis=0, keepdims=True)
            return dk, dv, dfk

        init = (jnp.zeros((tk, BLK), F32), jnp.zeros((tk, BLK), F32), jnp.zeros((1, tk), F32))
        carry = step(j, init, True)
        dk, dv, dfk = lax.fori_loop(j + 1, nq, lambda i, c: step(i, c, False), carry)
        dk_ref[...] = dk * inv_sqrt
        dv_ref[...] = dv
        dfk_ref[...] = dfk

    head_col = lambda h, j: (0, h)
    return pl.pallas_call(
        body, name=name, grid=(n_heads, nkb),
        in_specs=[pl.BlockSpec((T, BLK), head_col),
                  pl.BlockSpec((tk, BLK), lambda h, j: (j, h)),
                  pl.BlockSpec((tk, BLK), lambda h, j: (j, h)),
                  pl.BlockSpec((T, BLK), head_col),
                  pl.BlockSpec((T, BLK), head_col),
                  pl.BlockSpec((None, T, 1), lambda h, j: (h, 0, 0)),
                  pl.BlockSpec((None, None, 1, tk), lambda h, j: (h, j, 0, 0))],
        out_specs=[pl.BlockSpec((T, BLK), head_col),
                   pl.BlockSpec((tk, BLK), lambda h, j: (j, h)),
                   pl.BlockSpec((tk, BLK), lambda h, j: (j, h)),
                   pl.BlockSpec((None, T, 1), lambda h, j: (h, 0, 0)),
                   pl.BlockSpec((None, None, 1, tk), lambda h, j: (h, j, 0, 0))],
        out_shape=[_sds((T, n_heads * BLK), F32), _sds((T, n_heads * BLK), F32), _sds((T, n_heads * BLK), F32),
                   _sds((n_heads, T, 1), F32), _sds((n_heads, nkb, 1, tk), F32)],
        scratch_shapes=[pltpu.VMEM((T, 1), F32)],
        compiler_params=_params(("parallel", "arbitrary")),
    )(q, k, v, o, do, lse, fk)


_INV_SQRT2 = 1.0 / math.sqrt(2.0)
_INV_SQRT_2PI = 1.0 / math.sqrt(2.0 * math.pi)


def _gelu_parts(z):
    cdf = 0.5 * (1.0 + lax.erf(z * _INV_SQRT2))
    return cdf, z * cdf


def _mix_mask(transposed):
    row = lax.broadcasted_iota(jnp.int32, (BLK, BLK), 0) // CHUNK
    col = lax.broadcasted_iota(jnp.int32, (BLK, BLK), 1) // CHUNK
    return (row <= col) if transposed else (col <= row)


def _gmlp_mid_fwd(zpre, ln_g, ln_b, ws, bs_t, name):
    T, two_h = zpre.shape
    Hh = two_h // 2
    G = ws.shape[0]
    gd = Hh // G

    def body(z_ref, lg_ref, lb_ref, ws_ref, bs_ref, p_ref):
        _, zg = _gelu_parts(z_ref[...].astype(F32))
        u, v = zg[:, :Hh], zg[:, Hh:]
        mu = jnp.mean(v, axis=-1, keepdims=True)
        vc = v - mu
        rstd = lax.rsqrt(jnp.mean(vc * vc, axis=-1, keepdims=True) + EPS)
        vn = ((vc * rstd) * lg_ref[...] + lb_ref[...]).astype(BF16)
        mask = _mix_mask(False)
        for g in range(G):
            wm = jnp.where(mask, ws_ref[g], 0.0).astype(BF16)
            sv = jnp.dot(wm, vn[:, g * gd:(g + 1) * gd], preferred_element_type=F32) + bs_ref[:, g:g + 1]
            p_ref[:, g * gd:(g + 1) * gd] = (u[:, g * gd:(g + 1) * gd] * sv).astype(BF16)

    return _rowcall(body, name=name, tr=BLK, row_ins=[zpre], full_ins=[ln_g, ln_b, ws, bs_t],
                    row_outs=[_sds((T, Hh), BF16)])[0]


def _gmlp_mid_bwd(zpre, dp, ln_g, ln_b, ws, ws_t, bs_t, name):
    T, two_h = zpre.shape
    Hh = two_h // 2
    G = ws.shape[0]
    gd = Hh // G
    nt_dims = (((1,), (1,)), ((), ()))

    def body(z_ref, dp_ref, lg_ref, lb_ref, ws_ref, wst_ref, bs_ref, dz_ref, dws_ref, dbs_ref, dlg_ref, dlb_ref,
             dvn_ref):
        @pl.when(pl.program_id(0) == 0)
        def _():
            dws_ref[...] = jnp.zeros_like(dws_ref)
            dbs_ref[...] = jnp.zeros_like(dbs_ref)
            dlg_ref[...] = jnp.zeros_like(dlg_ref)
            dlb_ref[...] = jnp.zeros_like(dlb_ref)

        z = z_ref[...].astype(F32)
        cdf, zg = _gelu_parts(z)
        dgelu = cdf + z * (jnp.exp(-0.5 * z * z) * _INV_SQRT_2PI)
        u, v = zg[:, :Hh], zg[:, Hh:]
        mu = jnp.mean(v, axis=-1, keepdims=True)
        vc = v - mu
        rstd = lax.rsqrt(jnp.mean(vc * vc, axis=-1, keepdims=True) + EPS)
        vhat = vc * rstd
        vn = (vhat * lg_ref[...] + lb_ref[...]).astype(BF16)
        mask, mask_t = _mix_mask(False), _mix_mask(True)
        lane = lax.broadcasted_iota(jnp.int32, (BLK, BLK), 1)
        dbs = jnp.zeros((BLK, BLK), F32)
        for g in range(G):
            cols = slice(g * gd, (g + 1) * gd)
            wm = jnp.where(mask, ws_ref[g], 0.0).astype(BF16)
            wm_t = jnp.where(mask_t, wst_ref[g], 0.0).astype(BF16)
            vn_g = vn[:, cols]
            sv = jnp.dot(wm, vn_g, preferred_element_type=F32) + bs_ref[:, g:g + 1]
            dp_g = dp_ref[:, cols].astype(F32)
            dz_ref[:, cols] = ((dp_g * sv) * dgelu[:, cols]).astype(BF16)
            dsv = dp_g * u[:, cols]
            dsv_b = dsv.astype(BF16)
            dbs = dbs + jnp.where(lane == g, jnp.sum(dsv, axis=1, keepdims=True), 0.0)
            dws_ref[g] += jnp.where(mask, lax.dot_general(dsv_b, vn_g, nt_dims, preferred_element_type=F32), 0.0)
            dvn_ref[:, cols] = jnp.dot(wm_t, dsv_b, preferred_element_type=F32)
        dbs_ref[...] += dbs
        dvn = dvn_ref[...]
        dlg_ref[0:1, :] += jnp.sum(dvn * vhat, axis=0, keepdims=True)
        dlb_ref[0:1, :] += jnp.sum(dvn, axis=0, keepdims=True)
        dvh = dvn * lg_ref[...]
        dv = rstd * (dvh - jnp.mean(dvh, axis=-1, keepdims=True) - vhat * jnp.mean(dvh * vhat, axis=-1, keepdims=True))
        dz_ref[:, Hh:] = (dv * dgelu[:, Hh:]).astype(BF16)

    return _rowcall(body, name=name, tr=BLK, row_ins=[zpre, dp], full_ins=[ln_g, ln_b, ws, ws_t, bs_t],
                    row_outs=[_sds((T, two_h), BF16)],
                    acc_outs=[_sds((G, BLK, BLK), F32), _sds((BLK, BLK), F32), _sds((8, Hh), F32), _sds((8, Hh), F32)],
                    scratch=[pltpu.VMEM((BLK, Hh), F32)])


def _mods(c_all, w, layer, bias, name):
    nb, K = c_all.shape
    N = w.shape[-1]
    tn = _tile(N, 512)

    def body(c_ref, w_ref, b_ref, o_ref):
        cv = c_ref[...]
        sc = cv / (1.0 + jnp.exp(-cv))
        o_ref[...] = jnp.dot(sc, w_ref[...], preferred_element_type=F32, precision=lax.Precision.HIGHEST) + b_ref[...]

    return pl.pallas_call(
        body, name=name, grid=(N // tn,),
        in_specs=[pl.BlockSpec((nb, K), lambda j: (0, 0)),
                  pl.BlockSpec((None, K, tn), lambda j: (layer, 0, j)),
                  pl.BlockSpec((1, tn), lambda j: (0, j))],
        out_specs=pl.BlockSpec((nb, tn), lambda j: (0, j)), out_shape=_sds((nb, N), F32),
        compiler_params=_params(("parallel",)),
    )(c_all, w, bias)


def _sum_slabs(landed, own, who, name):
    _, R, C = landed.shape
    tr = _row_tile(R, C * N_DEV)

    def body(who_ref, x_ref, own_ref, o_ref):
        me, mine = who_ref[0], own_ref[...]
        acc = jnp.where(me == 0, mine, x_ref[0])
        for s in range(1, N_DEV):
            acc = acc + jnp.where(me == s, mine, x_ref[s])
        o_ref[...] = acc

    return pl.pallas_call(
        body, name=name,
        grid_spec=pltpu.PrefetchScalarGridSpec(
            num_scalar_prefetch=1, grid=(R // tr,),
            in_specs=[pl.BlockSpec((N_DEV, tr, C), lambda i, who_ref: (0, i, 0)),
                      pl.BlockSpec((tr, C), lambda i, who_ref: (i, 0))],
            out_specs=pl.BlockSpec((tr, C), lambda i, who_ref: (i, 0))),
        out_shape=_sds((R, C), F32),
        compiler_params=_params(("parallel",)),
    )(who, landed, own)


def _adamw_math(w, g, m, v):
    m = ADAM_B1 * m + (1.0 - ADAM_B1) * g
    v = ADAM_B2 * v + (1.0 - ADAM_B2) * (g * g)
    m_hat = m / (1.0 - ADAM_B1 ** ADAM_STEP)
    v_hat = v / (1.0 - ADAM_B2 ** ADAM_STEP)
    delta = -ADAM_LR * (m_hat / (jnp.sqrt(v_hat) + ADAM_EPS) + ADAM_WD * w)
    return delta, m, v


class _AdamStack:
    def __init__(self, w, m, v, name):
        self.w, self.m, self.v, self.name = w, m, v, name
        self.L, self.R, self.C = w.shape
        self.tr = _row_tile(self.R, self.C, 256 * 1024)
        self.outs = None

    def _layer(self, l, who, srcs, src_specs, make_grad):
        n_src = len(srcs)
        L, R, C, tr = self.L, self.R, self.C, self.tr
        wspec = pl.BlockSpec((None, tr, C), lambda i, who_ref: (l, i, 0))

        def body(who_ref, *refs):
            src_refs = refs[:n_src]
            w_ref, m_ref, v_ref = refs[n_src:n_src + 3]
            g_ref, d_ref, m2_ref, v2_ref = refs[-4:]
            g = make_grad(who_ref[0], *src_refs)
            delta, m2, v2 = _adamw_math(w_ref[...], g, m_ref[...], v_ref[...])
            g_ref[...] = g
            d_ref[...] = delta
            m2_ref[...] = m2
            v2_ref[...] = v2

        prev = [] if self.outs is None else list(self.outs)
        aliases = {} if self.outs is None else {1 + n_src + 3 + t: t for t in range(4)}
        self.outs = pl.pallas_call(
            body, name=f"{self.name}_l{l}",
            grid_spec=pltpu.PrefetchScalarGridSpec(
                num_scalar_prefetch=1, grid=(R // tr,),
                in_specs=list(src_specs) + [wspec] * 3 + [_ANY] * len(prev), out_specs=[wspec] * 4),
            out_shape=[_sds((L, R, C), F32)] * 4,
            input_output_aliases=aliases,
            compiler_params=_params(("parallel",)),
        )(who, *srcs, self.w, self.m, self.v, *prev)

    def from_parts(self, l, who, landed, sent):
        tr, C = self.tr, self.C

        def make_grad(me, p_ref, own_ref):
            mine = own_ref[...].astype(F32)
            g = jnp.where(me == 0, mine, p_ref[0].astype(F32))
            for s in range(1, N_DEV):
                g = g + jnp.where(me == s, mine, p_ref[s].astype(F32))
            return g

        self._layer(l, who, [landed, sent],
                    [pl.BlockSpec((N_DEV, tr, C), lambda i, who_ref: (0, i, 0)),
                     pl.BlockSpec((None, tr, C), lambda i, who_ref: (who_ref[0], i, 0))], make_grad)

    def from_outer(self, l, who, sc_t, dmod):
        tr, C = self.tr, self.C

        def make_grad(me, s_ref, d_ref):
            g = s_ref[:, 0:1] * d_ref[0:1, :]
            for b in range(1, N_DEV):
                g = g + s_ref[:, b:b + 1] * d_ref[b:b + 1, :]
            return g

        self._layer(l, who, [sc_t, dmod], [pl.BlockSpec((tr, N_DEV), lambda i, who_ref: (i, 0)),
                                           pl.BlockSpec((N_DEV, C), lambda i, who_ref: (0, 0))], make_grad)


def _adamw_flat(w, g, m, v, name):
    R, C = w.shape
    tr = _row_tile(R, C, 128 * 1024)

    def body(w_ref, g_ref, m_ref, v_ref, d_ref, m2_ref, v2_ref):
        delta, m2, v2 = _adamw_math(w_ref[...], g_ref[...], m_ref[...], v_ref[...])
        d_ref[...] = delta
        m2_ref[...] = m2
        v2_ref[...] = v2

    spec = pl.BlockSpec((tr, C), lambda i: (i, 0))
    return pl.pallas_call(
        body, name=name, grid=(R // tr,), in_specs=[spec] * 4, out_specs=[spec] * 3,
        out_shape=[_sds((R, C), F32)] * 3, compiler_params=_params(("parallel",)),
    )(w, g, m, v)


def _pack(arrays):
    flat = jnp.concatenate([a.reshape(-1).astype(F32) for a in arrays])
    pad = (-flat.shape[0]) % (64 * BLK)
    if pad:
        flat = jnp.concatenate([flat, jnp.zeros((pad,), F32)])
    return flat.reshape(-1, BLK)


def _unpack(buf, shapes, lead=()):
    sizes = [int(math.prod(s)) for s in shapes]
    out, off = [], 0
    if all(n % BLK == 0 for n in sizes):
        for s, n in zip(shapes, sizes):
            out.append(buf[..., off // BLK:(off + n) // BLK, :].reshape(tuple(lead) + tuple(s)))
            off += n
        return out
    flat = buf.reshape(tuple(lead) + (-1,))
    for s, n in zip(shapes, sizes):
        out.append(flat[..., off:off + n].reshape(tuple(lead) + tuple(s)))
        off += n
    return out


def _row(vec):
    return vec.reshape(1, -1)


def _shard_of(full, axis, me, size):
    return lax.dynamic_slice_in_dim(full, me * size, size, axis=axis)


def kernel(x, c, ada_w, ada_b, norm_g, mlp_w1, mlp_w2, gmlp_w_in, gmlp_ln_g, gmlp_ln_b, gmlp_ws, gmlp_bs, gmlp_w_out, kv_norm_g, kv_ada_w, kv_ada_b, w_kv, k_norm_g, w_f, b_f, attn_wq, q_norm_g, attn_wo, loss_target, m_ada_w, m_ada_b, m_norm_g, m_mlp_w1, m_mlp_w2, m_gmlp_w_in, m_gmlp_ln_g, m_gmlp_ln_b, m_gmlp_ws, m_gmlp_bs, m_gmlp_w_out, m_kv_norm_g, m_kv_ada_w, m_kv_ada_b, m_w_kv, m_k_norm_g, m_w_f, m_b_f, m_attn_wq, m_q_norm_g, m_attn_wo, v_ada_w, v_ada_b, v_norm_g, v_mlp_w1, v_mlp_w2, v_gmlp_w_in, v_gmlp_ln_g, v_gmlp_ln_b, v_gmlp_ws, v_gmlp_bs, v_gmlp_w_out, v_kv_norm_g, v_kv_ada_w, v_kv_ada_b, v_w_kv, v_k_norm_g, v_w_f, v_b_f, v_attn_wq, v_q_norm_g, v_attn_wo):
    given = dict(locals())
    weights = {n: given[n] for n in WEIGHT_NAMES}
    mom_m = {n: given["m_" + n] for n in WEIGHT_NAMES}
    mom_v = {n: given["v_" + n] for n in WEIGHT_NAMES}

    me = _my_index()
    T, D = x.shape[1], x.shape[2]
    depth = ada_w.shape[0]
    n_a = gmlp_w_in.shape[0]
    n_heads = b_f.shape[0]
    G = gmlp_ws.shape[1]
    Hh = gmlp_ln_g.shape[1] * N_DEV
    mod_cols = ada_w.shape[2]
    kv_cols = kv_ada_w.shape[1]
    x0 = x.reshape(T, D)
    target = loss_target.reshape(T, D)

    small_in = [c, norm_g, gmlp_ln_g, gmlp_ln_b, w_f]
    small_shapes = [a.shape for a in small_in]
    got = _all_to_all(_pack(small_in), "gather_small_inputs", bcast=True)
    c_all, norm_g_sh, ln_g_sh, ln_b_sh, w_f_sh = _unpack(got, small_shapes, lead=(N_DEV,))
    c_all = c_all.reshape(N_DEV, D)
    norm_g_full = jnp.moveaxis(norm_g_sh, 0, 2).reshape(depth, 2, D)
    ln_g_full = jnp.moveaxis(ln_g_sh, 0, 1).reshape(n_a, Hh)
    ln_b_full = jnp.moveaxis(ln_b_sh, 0, 1).reshape(n_a, Hh)
    w_f_full = w_f_sh.reshape(D, n_heads)
    w_f_pad = jnp.pad(w_f_full, ((0, 0), (0, BLK - n_heads))).astype(BF16)
    b_f_pad = jnp.pad(b_f, (0, BLK - n_heads)).reshape(1, BLK)

    mod_parts = []
    for l in range(depth):
        bias = _shard_of(ada_b[l], 0, me, mod_cols).reshape(1, mod_cols)
        mod_parts.append(_mods(c_all, ada_w, l, bias, f"mods_l{l}"))
    kv_bias = _shard_of(kv_ada_b, 0, me, kv_cols).reshape(1, kv_cols)
    mod_parts.append(_mods(c_all, kv_ada_w.reshape(1, D, kv_cols), 0, kv_bias, "mods_kv"))
    mods_mine = jnp.concatenate(mod_parts, axis=1)
    mod_width = mods_mine.shape[1]
    mods_pack = jnp.pad(mods_mine, ((0, 0), (0, (-mod_width) % (8 * BLK)))).reshape(N_DEV, -1, BLK)
    mods_got = _all_to_all(mods_pack, "exchange_mods").reshape(N_DEV, -1)[:, :mod_width]
    mods = []
    for l in range(depth):
        mods.append(mods_got[:, l * mod_cols:(l + 1) * mod_cols].reshape(N_MOD, D))
    kv_mod = mods_got[:, depth * mod_cols:].reshape(2, D)
    silu_all = c_all / (1.0 + jnp.exp(-c_all))

    assert 1 <= n_a < depth
    who = me.astype(jnp.int32).reshape(1)
    big = {"mlp_w1": mlp_w1, "mlp_w2": mlp_w2, "gmlp_w_in": gmlp_w_in, "gmlp_w_out": gmlp_w_out,
           "w_kv": w_kv.reshape((1,) + w_kv.shape), "attn_wq": attn_wq, "attn_wo": attn_wo}
    groups = [[("gmlp_w_in", 0)], [("gmlp_w_out", 0), ("mlp_w1", 0), ("mlp_w2", 0)]]
    for l in range(1, depth):
        if l < n_a:
            groups.append([("gmlp_w_in", l), ("gmlp_w_out", l), ("mlp_w1", l), ("mlp_w2", l)])
        else:
            first = [("w_kv", 0)] if l == n_a else []
            groups.append(first + [("attn_wq", l - n_a), ("attn_wo", l - n_a), ("mlp_w1", l), ("mlp_w2", l)])
    tokens = []

    def behind_starts():
        out = tuple(tokens)
        tokens.clear()
        return out

    over_ici, placed = {}, {}

    def gather_begin(gi):
        over_ici[gi] = _gather_start(placed[gi], 1, f"gather_g{gi}_ici_start")
        tokens.append(over_ici[gi][3])

    for gi, grp in enumerate(groups):
        placed[gi] = [_place_shard(big[n], l, who, f"place_{n}_l{l}", deps=tuple(tokens)) for n, l in grp]
        if gi < 2:
            gather_begin(gi)
    to_sibling = {}
    gw = {}

    def gather_forward(gi, after):
        landed = _gather_wait(over_ici[gi], 1, after, f"gather_g{gi}_ici_wait")
        to_sibling[gi] = _gather_start(landed, 2, f"gather_g{gi}_d2d_start")
        tokens.append(to_sibling[gi][3])
        if gi + 2 < len(groups):
            gather_begin(gi + 2)

    def gather_finish(gi, after):
        for key, arr in zip(groups[gi], _gather_wait(to_sibling[gi], 2, after, f"gather_g{gi}_d2d_wait")):
            gw[key] = arr

    tkk = _attn_tile(T)

    saved = []
    xs = x0
    pending = None
    kv = None
    for l in range(depth):
        sh1, sc1, g1, sh2, sc2, g2 = [_row(mods[l][t]) for t in range(N_MOD)]
        ng1, ng2 = _row(norm_g_full[l, 0]), _row(norm_g_full[l, 1])
        st = dict(sc1=sc1, g1=g1, sc2=sc2, g2=g2, ng1=ng1, ng2=ng2)
        if pending is None:
            h1 = _norm_mod(xs, ng1, sc1, sh1, f"norm1_l{l}")
            gather_forward(0, h1)
            gather_finish(0, h1)
        else:
            gather_finish(l + 1, pending[0])
            xs, h1 = _res_norm_mod(xs, pending[0], pending[1], ng1, sc1, sh1, f"res_norm1_l{l}")
        st["x_in"], st["h1"] = xs, h1
        if l < n_a:
            a = l
            zpre = _mm(h1, gw["gmlp_w_in", a], bmode="col", out_dtypes=(BF16,), name=f"gmlp_in_l{l}",
                       deps=behind_starts())
            if l == 0:
                gather_forward(1, zpre)
            bs_t = gmlp_bs[a].T
            p = _gmlp_mid_fwd(zpre, _row(ln_g_full[a]), _row(ln_b_full[a]), gmlp_ws[a], bs_t, f"gmlp_mid_l{l}")
            if l == 0:
                gather_finish(1, p)
            y = _mm(p, gw["gmlp_w_out", a], bmode="row", name=f"gmlp_out_l{l}", deps=behind_starts())
            st.update(zpre=zpre, p=p)
        else:
            if kv is None:
                kv_ng, kv_sh, kv_sc = _row(kv_norm_g), _row(kv_mod[0]), _row(kv_mod[1])
                hkv = _norm_mod(xs, kv_ng, kv_sc, kv_sh, "norm_kv")
                kvp = _mm(hkv, gw["w_kv", 0], bmode="col", name="kv_proj", deps=behind_starts())
                kk, vv = _head_norm(kvp, _row(k_norm_g), n_heads, "k_norm", tail=True)
                fl = _mm(hkv, w_f_pad, name="gate_logits")
                fcum = _fcum_fwd(fl, b_f_pad, "fcum")
                fk = fcum[:, :n_heads].T.reshape(n_heads, T // tkk, 1, tkk)
                kv = dict(x=xs, hkv=hkv, kvp=kvp, k=kk, v=vv, fl=fl, fcum=fcum, fk=fk, ng=kv_ng, sc=kv_sc)
            bl = l - n_a
            qp = _mm(h1, gw["attn_wq", bl], bmode="row", name=f"q_proj_l{l}", deps=behind_starts())
            q = _head_norm(qp, _row(q_norm_g[bl]), n_heads, f"q_norm_l{l}")[0]
            o, o32, lse = _attn_fwd(q, kv["k"], kv["v"], kv["fk"], n_heads, f"attn_l{l}")
            y = _mm(o, gw["attn_wo", bl], bmode="row", name=f"attn_out_l{l}")
            st.update(qp=qp, q=q, o=o, o32=o32, lse=lse)
        xs, h2 = _res_norm_mod(xs, y, g1, ng2, sc2, sh2, f"res_norm2_l{l}")
        a_pre, s_act = _mm(h2, gw["mlp_w1", l], bmode="col", out_dtypes=(BF16, BF16), epilogue=_relu2_epilogue,
                           name=f"mlp_up_l{l}")
        if l + 1 < depth:
            gather_forward(l + 2, a_pre)
        mo = _mm(s_act, gw["mlp_w2", l], bmode="row", name=f"mlp_down_l{l}", deps=behind_starts())
        st.update(y=y, x_mid=xs, h2=h2, a_pre=a_pre, s=s_act, m=mo)
        saved.append(st)
        pending = (mo, g2)

    dx, dm, loss_row, dg2 = _res_loss(xs, pending[0], pending[1], target, "loss")
    loss = lax.psum(loss_row[0, 0], ("x", "y", "c"))

    started = {}

    def scatter(dw_slabs, key, idx):
        started[(key, idx)] = _a2a_start(dw_slabs, f"scatter_{key}_l{idx}_start")
        tokens.append(started[(key, idx)][4])

    d_mod = [None] * depth
    d_norm_g = [None] * depth
    d_ln_g, d_ln_b, d_ws, d_bs = [None] * n_a, [None] * n_a, [None] * n_a, [None] * n_a
    d_qg = [None] * (depth - n_a)
    dk_list, dv_list, dfk_list = [], [], []
    small = {}

    for l in reversed(range(depth)):
        st = saved[l]
        da = _mm(dm, gw["mlp_w2", l], tb=True, bmode="row", out_dtypes=(BF16,), epilogue=_relu2_bwd_epilogue,
                 extra=(st["a_pre"],), name=f"mlp_down_dx_l{l}")
        dw2 = _mm(st["s"], dm, ta=True, out_dtypes=(BF16,), name=f"mlp_down_dw_l{l}", deps=behind_starts())
        scatter(dw2.reshape(N_DEV, -1, D), "mlp_w2", l)
        dw1 = _mm(st["h2"], da, ta=True, out_mode="col", out_dtypes=(BF16,), name=f"mlp_up_dw_l{l}",
                  deps=behind_starts())
        scatter(dw1, "mlp_w1", l)
        dh2 = _mm(da, gw["mlp_w1", l], tb=True, bmode="col", out_dtypes=(BF16,), name=f"mlp_up_dx_l{l}",
                  deps=behind_starts())
        dx, dy, sums2, dg1 = _norm_mod_bwd(st["x_mid"], [dh2], dx, st["ng2"], st["sc2"], f"norm2_bwd_l{l}",
                                           gated=(st["y"], st["g1"]))
        if l < n_a:
            a = l
            dwo = _mm(st["p"], dy, ta=True, out_dtypes=(BF16,), name=f"gmlp_out_dw_l{l}", deps=behind_starts())
            scatter(dwo.reshape(N_DEV, -1, D), "gmlp_w_out", a)
            dp = _mm(dy, gw["gmlp_w_out", a], tb=True, bmode="row", out_dtypes=(BF16,), name=f"gmlp_out_dx_l{l}",
                     deps=behind_starts())
            dz, d_ws[a], dbs_t, dlg, dlb = _gmlp_mid_bwd(
                st["zpre"], dp, _row(ln_g_full[a]), _row(ln_b_full[a]), gmlp_ws[a],
                jnp.swapaxes(gmlp_ws[a], 1, 2), gmlp_bs[a].T, f"gmlp_mid_bwd_l{l}")
            d_bs[a], d_ln_g[a], d_ln_b[a] = dbs_t[:, :G].T, dlg[0], dlb[0]
            dwi = _mm(st["h1"], dz, ta=True, out_mode="col", out_dtypes=(BF16,), name=f"gmlp_in_dw_l{l}",
                      deps=behind_starts())
            scatter(dwi, "gmlp_w_in", a)
            dh1s = [_mm(dz, gw["gmlp_w_in", a], tb=True, bmode="col", out_dtypes=(BF16,), name=f"gmlp_in_dx_l{l}",
                        deps=behind_starts())]
        else:
            bl = l - n_a
            dwo = _mm(st["o"], dy, ta=True, out_dtypes=(BF16,), name=f"attn_out_dw_l{l}", deps=behind_starts())
            scatter(dwo.reshape(N_DEV, -1, D), "attn_wo", bl)
            do = _mm(dy, gw["attn_wo", bl], tb=True, bmode="row", out_dtypes=(BF16,), name=f"attn_out_dx_l{l}",
                     deps=behind_starts())
            dq, dk, dv, dfq, dfk = _attn_bwd(st["q"], kv["k"], kv["v"], st["o32"], do, st["lse"], kv["fk"],
                                             n_heads, f"attn_bwd_l{l}")
            dfk_list += [dfq, dfk]
            dk_list.append(dk)
            dv_list.append(dv)
            dqp, dqg = _head_norm_bwd(st["qp"], [dq], _row(q_norm_g[bl]), n_heads, f"q_norm_bwd_l{l}")
            d_qg[bl] = dqg[0]
            dwq = _mm(st["h1"], dqp, ta=True, out_dtypes=(BF16,), name=f"q_proj_dw_l{l}", deps=behind_starts())
            scatter(dwq.reshape(N_DEV, -1, D), "attn_wq", bl)
            dh1s = [_mm(dqp, gw["attn_wq", bl], tb=True, bmode="row", out_dtypes=(BF16,), name=f"q_proj_dx_l{l}",
                        deps=behind_starts())]
        below = (saved[l - 1]["m"], saved[l - 1]["g2"]) if l > 0 else None
        if below is None or l == n_a:
            dx, sums1 = _norm_mod_bwd(st["x_in"], dh1s, dx, st["ng1"], st["sc1"], f"norm1_bwd_l{l}")
        else:
            dx, dm_below, sums1, dg2_below = _norm_mod_bwd(st["x_in"], dh1s, dx, st["ng1"], st["sc1"],
                                                           f"norm1_bwd_l{l}", gated=below)
        d_mod[l] = jnp.stack([sums1[0], sums1[1], dg1[0], sums2[0], sums2[1], dg2[0]])
        d_norm_g[l] = jnp.stack([sums1[2], sums2[2]])
        if l == n_a:
            dkvp, dkg = _head_norm_bwd(kv["kvp"], dk_list, _row(k_norm_g), n_heads, "k_norm_bwd", tails=dv_list)
            dfc = [jnp.pad(d.reshape(n_heads, T).T, ((0, 0), (0, BLK - n_heads))) for d in dfk_list]
            dfl, dbf = _fcum_bwd(dfc, kv["fl"], b_f_pad, "fcum_bwd")
            dwkv = _mm(kv["hkv"], dkvp, ta=True, out_mode="col", out_dtypes=(BF16,), name="kv_proj_dw",
                       deps=behind_starts())
            scatter(dwkv, "w_kv", 0)
            dwf = _mm(kv["hkv"], dfl, ta=True, name="gate_logits_dw", deps=behind_starts())
            dh_a = _mm(dkvp, gw["w_kv", 0], tb=True, bmode="col", out_dtypes=(BF16,), name="kv_proj_dx")
            dh_b = _mm(dfl, w_f_pad, tb=True, out_dtypes=(BF16,), name="gate_logits_dx")
            dx, dm_below, sums_kv, dg2_below = _norm_mod_bwd(kv["x"], [dh_a, dh_b], dx, kv["ng"], kv["sc"],
                                                             "norm_kv_bwd", gated=below)
            small.update(d_kv_mod=jnp.stack([sums_kv[0], sums_kv[1]]), d_kv_norm_g=sums_kv[2], d_k_norm_g=dkg[0],
                         d_w_f=dwf[:, :n_heads], d_b_f=dbf[0, :n_heads])
        if l > 0:
            dm, dg2 = dm_below, dg2_below

    grad_x = dx.reshape(x.shape)

    mod_contrib = [jnp.stack(d_mod).reshape(depth, N_MOD * D), small["d_kv_mod"].reshape(-1)]
    rest_contrib = [jnp.stack(d_norm_g), jnp.stack(d_ln_g), jnp.stack(d_ln_b), jnp.stack(d_ws), jnp.stack(d_bs),
                    small["d_kv_norm_g"], small["d_k_norm_g"], small["d_w_f"], small["d_b_f"], jnp.stack(d_qg)]
    mod_shapes, rest_shapes = [a.shape for a in mod_contrib], [a.shape for a in rest_contrib]
    mod_started = _a2a_start(_pack(mod_contrib), "gather_mod_grads_start", bcast=True)
    rest_started = _a2a_start(_pack(rest_contrib), "gather_small_grads_start", bcast=True)
    grads, deltas, new_m, new_v = {}, {}, {}, {}

    def stack_of(n):
        lead = () if weights[n].ndim == 3 else (1,)
        return _AdamStack(*[a.reshape(lead + a.shape) for a in (weights[n], mom_m[n], mom_v[n])], f"adamw_{n}")

    def results_of(n, stack):
        grads[n], deltas[n], new_m[n], new_v[n] = [a.reshape(weights[n].shape) for a in stack.outs]

    def update_conditioning(after):
        mod_sent, mod_land = _a2a_wait(mod_started, after, "gather_mod_grads_wait", bcast=True)
        is_me = (jnp.arange(N_DEV) == me).reshape(N_DEV, 1, 1)
        dmod_all, dkvmod_all = _unpack(jnp.where(is_me, mod_sent[None], mod_land), mod_shapes, lead=(N_DEV,))
        sums = _unpack(_sum_slabs(mod_land, mod_sent, who, "sum_mod_grads"), mod_shapes)
        sc_t = silu_all.T
        ada_stack, kv_ada_stack = stack_of("ada_w"), stack_of("kv_ada_w")
        for l in range(depth):
            ada_stack.from_outer(l, who, sc_t, _shard_of(dmod_all[:, l], 1, me, mod_cols))
        kv_ada_stack.from_outer(0, who, sc_t, _shard_of(dkvmod_all, 1, me, kv_cols))
        results_of("ada_w", ada_stack)
        results_of("kv_ada_w", kv_ada_stack)
        return sums, ada_stack.outs[0]

    stacks = {n: stack_of(n) for n in ("mlp_w1", "mlp_w2", "gmlp_w_in", "gmlp_w_out", "attn_wq", "attn_wo", "w_kv")}
    after = rest_started[4]
    for l in reversed(range(depth)):
        if l == n_a - 1:
            (g_ada_b, g_kv_ada_b), after = update_conditioning(after)
        keys = [("mlp_w2", l), ("mlp_w1", l)]
        keys += [("gmlp_w_out", l), ("gmlp_w_in", l)] if l < n_a else [("attn_wo", l - n_a), ("attn_wq", l - n_a)]
        keys += [("w_kv", 0)] if l == n_a else []
        landed = {}
        for key in keys:
            sent, land = _a2a_wait(started[key], after, f"scatter_{key[0]}_l{key[1]}_wait")
            landed[key] = (land, sent)
        for key in keys:
            stacks[key[0]].from_parts(key[1], who, *landed[key])
            after = stacks[key[0]].outs[0]
    for n, stack in stacks.items():
        results_of(n, stack)

    rest_sent, rest_land = _a2a_wait(rest_started, after, "gather_small_grads_wait", bcast=True)
    (g_norm_g_full, g_ln_g_full, g_ln_b_full, g_ws, g_bs, g_kv_norm_g, g_k_norm_g, g_w_f_full, g_b_f,
     g_q_norm_g) = _unpack(_sum_slabs(rest_land, rest_sent, who, "sum_small_grads"), rest_shapes)
    small_grads = {
        "ada_b": g_ada_b, "kv_ada_b": g_kv_ada_b.reshape(kv_ada_b.shape),
        "norm_g": _shard_of(g_norm_g_full, 2, me, norm_g.shape[2]),
        "gmlp_ln_g": _shard_of(g_ln_g_full, 1, me, gmlp_ln_g.shape[1]),
        "gmlp_ln_b": _shard_of(g_ln_b_full, 1, me, gmlp_ln_b.shape[1]),
        "gmlp_ws": g_ws, "gmlp_bs": g_bs, "kv_norm_g": g_kv_norm_g, "k_norm_g": g_k_norm_g,
        "w_f": _shard_of(g_w_f_full, 0, me, w_f.shape[0]), "b_f": g_b_f, "q_norm_g": g_q_norm_g,
    }
    small_names = list(small_grads)
    small_w_shapes = [weights[n].shape for n in small_names]
    d_pack, m_pack, v_pack = _adamw_flat(_pack([weights[n] for n in small_names]),
                                         _pack([small_grads[n] for n in small_names]),
                                         _pack([mom_m[n] for n in small_names]), _pack([mom_v[n] for n in small_names]),
                                         "adamw_small")
    grads.update(small_grads)
    deltas.update(zip(small_names, _unpack(d_pack, small_w_shapes)))
    new_m.update(zip(small_names, _unpack(m_pack, small_w_shapes)))
    new_v.update(zip(small_names, _unpack(v_pack, small_w_shapes)))

    return (loss, grad_x, *[grads[n] for n in WEIGHT_NAMES], *[deltas[n] for n in WEIGHT_NAMES],
            *[new_m[n] for n in WEIGHT_NAMES], *[new_v[n] for n in WEIGHT_NAMES])
```

```python
import functools
import math

import jax
import jax.numpy as jnp
from jax import lax
from jax.experimental import pallas as pl
from jax.experimental.pallas import tpu as pltpu

F32 = jnp.float32
BF16 = jnp.bfloat16
N_DEV = 8
EPS = 1e-6
CHUNK = 64
BLK = 128
N_MOD = 6
ADAM_LR = 0.001
ADAM_B1 = 0.9
ADAM_B2 = 0.999
ADAM_EPS = 1e-08
ADAM_WD = 0.01
ADAM_STEP = 10
VMEM_LIMIT_BYTES = 56 * 2 ** 20
NEG_BIG = -1e30
WEIGHT_NAMES = ['ada_w', 'ada_b', 'norm_g', 'mlp_w1', 'mlp_w2', 'gmlp_w_in', 'gmlp_ln_g', 'gmlp_ln_b', 'gmlp_ws',
                'gmlp_bs', 'gmlp_w_out', 'kv_norm_g', 'kv_ada_w', 'kv_ada_b', 'w_kv', 'k_norm_g', 'w_f', 'b_f',
                'attn_wq', 'q_norm_g', 'attn_wo']
MESH = pl.DeviceIdType.MESH


def _params(sem):
    return pltpu.CompilerParams(dimension_semantics=sem, vmem_limit_bytes=VMEM_LIMIT_BYTES)


def _tile(n, cap, unit=128):
    if n <= cap:
        return n
    t = (cap // unit) * unit
    while t > unit and n % t:
        t -= unit
    assert n % t == 0, (n, cap, unit)
    return t


def _my_index():
    return 4 * lax.axis_index("x") + 2 * lax.axis_index("y") + lax.axis_index("c")


def _all_to_all(x, name, bcast=False):
    slab = x.shape if bcast else x.shape[1:]

    def body(x_ref, o_ref, send_sems, recv_sems, local_sem):
        me = _my_index()

        def src(j):
            return x_ref if bcast else x_ref.at[j]

        mine = pltpu.make_async_copy(src(me), o_ref.at[me], local_sem)
        mine.start()
        sends = []
        for d in range(1, N_DEV):
            peer = (me + d) % N_DEV
            cp = pltpu.make_async_remote_copy(
                src_ref=src(peer), dst_ref=o_ref.at[me],
                send_sem=send_sems.at[d - 1], recv_sem=recv_sems.at[d - 1],
                device_id=(peer // 4, (peer // 2) % 2, peer % 2), device_id_type=MESH)
            cp.start()
            sends.append(cp)
        for d in range(1, N_DEV):
            frm = (me + N_DEV - d) % N_DEV
            pltpu.make_async_remote_copy(
                src_ref=src(frm), dst_ref=o_ref.at[frm],
                send_sem=send_sems.at[d - 1], recv_sem=recv_sems.at[d - 1],
                device_id=(frm // 4, (frm // 2) % 2, frm % 2), device_id_type=MESH).wait_recv()
        for cp in sends:
            cp.wait_send()
        mine.wait()

    return pl.pallas_call(
        body, name=name,
        out_shape=jax.ShapeDtypeStruct((N_DEV,) + tuple(slab), x.dtype),
        in_specs=[pl.BlockSpec(memory_space=pl.ANY)],
        out_specs=pl.BlockSpec(memory_space=pl.ANY),
        scratch_shapes=[pltpu.SemaphoreType.DMA((N_DEV - 1,)), pltpu.SemaphoreType.DMA((N_DEV - 1,)),
                        pltpu.SemaphoreType.DMA],
        compiler_params=pltpu.CompilerParams(has_side_effects=True),
    )(x)


_HBM = pl.BlockSpec(memory_space=pltpu.HBM)
_SEM = pl.BlockSpec(memory_space=pltpu.SEMAPHORE)
_ANY = pl.BlockSpec(memory_space=pl.ANY)
_DATAFLOW = pltpu.SideEffectType.DATAFLOW_SIDE_EFFECTING


def _a2a_peer_copy(x_ref, land_ref, send_sems, recv_sems, d, me, incoming, bcast):
    peer = (me + N_DEV - d) % N_DEV if incoming else (me + d) % N_DEV
    return pltpu.make_async_remote_copy(
        src_ref=x_ref if bcast else x_ref.at[peer], dst_ref=land_ref.at[peer if incoming else me],
        send_sem=send_sems.at[d - 1], recv_sem=recv_sems.at[d - 1],
        device_id=(peer // 4, (peer // 2) % 2, peer % 2), device_id_type=MESH)


def _a2a_start(x, name, bcast=False, deps=()):
    land_shape = ((N_DEV,) + tuple(x.shape)) if bcast else x.shape
    n_dep = len(deps)

    def body(x_ref, land_ref, *rest):
        send_sems, recv_sems, token = rest[n_dep], rest[n_dep + 1], rest[-1]
        me = _my_index()
        for d in range(1, N_DEV):
            _a2a_peer_copy(x_ref, land_ref, send_sems, recv_sems, d, me, False, bcast).start()
        token[...] = jnp.zeros_like(token)

    return pl.pallas_call(
        body, name=name,
        out_shape=(pltpu.SemaphoreType.DMA((N_DEV - 1,)), pltpu.SemaphoreType.DMA((N_DEV - 1,)),
                   pltpu.HBM(x.shape, x.dtype), pltpu.HBM(land_shape, x.dtype), jax.ShapeDtypeStruct((8, BLK), F32)),
        in_specs=(_HBM, _HBM, *[_ANY for _ in deps]),
        out_specs=(_SEM, _SEM, _HBM, _HBM, pl.BlockSpec(memory_space=pltpu.VMEM)),
        input_output_aliases={0: 2, 1: 3},
        compiler_params=pltpu.CompilerParams(has_side_effects=_DATAFLOW),
    )(pltpu.with_memory_space_constraint(x, pltpu.HBM),
      pltpu.with_memory_space_constraint(lax.empty(land_shape, x.dtype), pltpu.HBM), *deps)


def _a2a_wait(started, after, name, bcast=False):
    send_sems, recv_sems, x_thru, land_thru, _ = started

    def body(x_ref, land_ref, send_sems, recv_sems, after_ref, x_dead, land_out):
        me = _my_index()
        for d in range(1, N_DEV):
            _a2a_peer_copy(x_ref, land_ref, send_sems, recv_sems, d, me, False, bcast).wait_send()
        for d in range(1, N_DEV):
            _a2a_peer_copy(x_ref, land_ref, send_sems, recv_sems, d, me, True, bcast).wait_recv()

    return pl.pallas_call(
        body, name=name,
        out_shape=(pltpu.HBM(x_thru.shape, x_thru.dtype), pltpu.HBM(land_thru.shape, land_thru.dtype)),
        in_specs=(_HBM, _HBM, _SEM, _SEM, _ANY), out_specs=(_HBM, _HBM),
        input_output_aliases={0: 0, 1: 1},
        compiler_params=pltpu.CompilerParams(has_side_effects=_DATAFLOW),
    )(x_thru, land_thru, send_sems, recv_sems, after)


def _place_shard(w, layer, who, name, deps=()):
    _, R, C = w.shape
    tr = _row_tile(R, C)

    def body(who_ref, w_ref, *rest):
        rest[-1][...] = w_ref[...].astype(BF16)

    return pl.pallas_call(
        body, name=name,
        grid_spec=pltpu.PrefetchScalarGridSpec(
            num_scalar_prefetch=1, grid=(R // tr,),
            in_specs=[pl.BlockSpec((None, tr, C), lambda i, who_ref: (layer, i, 0))] + [_ANY for _ in deps],
            out_specs=pl.BlockSpec((None, None, tr, C), lambda i, who_ref: (who_ref[0], 0, i, 0))),
        out_shape=_sds((N_DEV, 1, R, C), BF16),
        compiler_params=_params(("parallel",)),
    )(who, w, *deps)


def _gather_copies(land_ref, send_sems, recv_sems, base, phase, incoming):
    cx, cy, cc = lax.axis_index("x"), lax.axis_index("y"), lax.axis_index("c")
    sibling = (cx, cy, 1 - cc)
    chips = [(1 - cx, cy), (cx, 1 - cy), (1 - cx, 1 - cy)]
    if phase == 1:
        out = [((cx, cy, cc), sibling)] + [((cx, cy, cc), (*chip, cc)) for chip in chips]
        inc = [(sibling, sibling)] + [((*chip, cc), (*chip, cc)) for chip in chips]
    else:
        out = [((*chip, cc), sibling) for chip in chips]
        inc = [((*chip, 1 - cc), sibling) for chip in chips]
    copies = []
    for k, (block, peer) in enumerate(inc if incoming else out):
        slab = land_ref.at[4 * block[0] + 2 * block[1] + block[2]]
        copies.append(pltpu.make_async_remote_copy(
            src_ref=slab, dst_ref=slab, send_sem=send_sems.at[base + k], recv_sem=recv_sems.at[base + k],
            device_id=peer, device_id_type=MESH))
    return copies


def _gather_start(lands, phase, name, deps=()):
    n, per, n_dep = len(lands), (4 if phase == 1 else 3), len(deps)

    def body(*refs):
        land_refs, send_sems, recv_sems, token = refs[:n], refs[n + n_dep], refs[n + n_dep + 1], refs[-1]
        for a, land_ref in enumerate(land_refs):
            for cp in _gather_copies(land_ref, send_sems, recv_sems, a * per, phase, False):
                cp.start()
        token[...] = jnp.zeros_like(token)

    outs = pl.pallas_call(
        body, name=name,
        out_shape=(pltpu.SemaphoreType.DMA((n * per,)), pltpu.SemaphoreType.DMA((n * per,)),
                   *[pltpu.HBM(x.shape, x.dtype) for x in lands], jax.ShapeDtypeStruct((8, BLK), F32)),
        in_specs=(*[_HBM for _ in lands], *[_ANY for _ in deps]),
        out_specs=(_SEM, _SEM, *[_HBM for _ in lands], pl.BlockSpec(memory_space=pltpu.VMEM)),
        input_output_aliases={a: 2 + a for a in range(n)},
        compiler_params=pltpu.CompilerParams(has_side_effects=_DATAFLOW),
    )(*[pltpu.with_memory_space_constraint(x, pltpu.HBM) for x in lands], *deps)
    return outs[0], outs[1], list(outs[2:2 + n]), outs[-1]


def _gather_wait(started, phase, after, name):
    send_sems, recv_sems, lands, _ = started
    n, per = len(lands), (4 if phase == 1 else 3)

    def body(*refs):
        land_refs, send_sems, recv_sems = refs[:n], refs[n], refs[n + 1]
        for a, land_ref in enumerate(land_refs):
            for cp in _gather_copies(land_ref, send_sems, recv_sems, a * per, phase, False):
                cp.wait_send()
            for cp in _gather_copies(land_ref, send_sems, recv_sems, a * per, phase, True):
                cp.wait_recv()

    outs = pl.pallas_call(
        body, name=name,
        out_shape=tuple(pltpu.HBM(x.shape, x.dtype) for x in lands),
        in_specs=(*[_HBM for _ in lands], _SEM, _SEM, _ANY), out_specs=tuple(_HBM for _ in lands),
        input_output_aliases={a: a for a in range(n)},
        compiler_params=pltpu.CompilerParams(has_side_effects=_DATAFLOW),
    )(*lands, send_sems, recv_sems, after)
    return list(outs)


def _mm(a, b, *, name, ta=False, tb=False, bmode="plain", layer=0, out_mode="plain", out_dtypes=(F32,),
        epilogue=None, extra=(), caps=(1024, 1024, 2048), deps=()):
    if ta:
        K, M = a.shape
    else:
        M, K = a.shape
    n_unit = k_unit = None
    if bmode == "plain":
        N, Kb = (b.shape if tb else b.shape[::-1])
    elif bmode == "col":
        _, _, Kw, Ns = b.shape
        if tb:
            N, Kb, k_unit = Kw, N_DEV * Ns, Ns
        else:
            N, Kb, n_unit = N_DEV * Ns, Kw, Ns
    else:
        _, _, Ks, Nw = b.shape
        if tb:
            N, Kb, n_unit = N_DEV * Ks, Nw, Ks
        else:
            N, Kb, k_unit = Nw, N_DEV * Ks, Ks
    assert K == Kb, (name, a.shape, b.shape)
    if out_mode == "col":
        assert n_unit is None
        n_unit = N // N_DEV
    tm = _tile(M, caps[0])
    tn = _tile(n_unit or N, caps[1])
    span = 1
    if k_unit and 2 * k_unit <= caps[2]:
        while 2 * span * k_unit <= caps[2] and N_DEV % (2 * span) == 0:
            span *= 2
        tk = span * k_unit
    else:
        tk = _tile(k_unit or K, caps[2])
    nk = K // tk
    npb = (n_unit // tn) if n_unit else None
    kpb = (k_unit // tk) if (k_unit and span == 1) else None
    grid = (M // tm, N // tn, nk)

    a_spec = pl.BlockSpec((tk, tm), lambda i, j, k: (k, i)) if ta else pl.BlockSpec((tm, tk), lambda i, j, k: (i, k))
    if bmode == "plain":
        b_spec = (pl.BlockSpec((tn, tk), lambda i, j, k: (j, k)) if tb
                  else pl.BlockSpec((tk, tn), lambda i, j, k: (k, j)))
    elif bmode == "col":
        if tb and span > 1:
            b_spec = pl.BlockSpec((span, None, tn, k_unit), lambda i, j, k: (k, layer, j, 0))
        elif tb:
            b_spec = pl.BlockSpec((None, None, tn, tk), lambda i, j, k: (k // kpb, layer, j, k % kpb))
        else:
            b_spec = pl.BlockSpec((None, None, tk, tn), lambda i, j, k: (j // npb, layer, k, j % npb))
    else:
        if tb:
            b_spec = pl.BlockSpec((None, None, tn, tk), lambda i, j, k: (j // npb, layer, j % npb, k))
        elif span > 1:
            b_spec = pl.BlockSpec((span, None, k_unit, tn), lambda i, j, k: (k, layer, 0, j))
        else:
            b_spec = pl.BlockSpec((None, None, tk, tn), lambda i, j, k: (k // kpb, layer, k % kpb, j))
    mn_spec = pl.BlockSpec((tm, tn), lambda i, j, k: (i, j))
    if out_mode == "col":
        o_specs = [pl.BlockSpec((None, tm, tn), lambda i, j, k: (j // npb, i, j % npb))]
        o_shapes = [jax.ShapeDtypeStruct((N_DEV, M, N // N_DEV), out_dtypes[0])]
    else:
        o_specs = [mn_spec for _ in out_dtypes]
        o_shapes = [jax.ShapeDtypeStruct((M, N), dt) for dt in out_dtypes]
    dims = (((0 if ta else 1,), (1 if tb else 0,)), ((), ()))
    n_extra, n_out, n_dep = len(extra), len(out_dtypes), len(deps)

    def body(a_ref, b_ref, *rest):
        extra_refs, out_refs = rest[:n_extra], rest[n_extra + n_dep:n_extra + n_dep + n_out]
        k = pl.program_id(2)

        def product():
            if span == 1:
                return lax.dot_general(a_ref[...].astype(BF16), b_ref[...].astype(BF16), dims,
                                       preferred_element_type=F32)
            if not tb:
                return lax.dot_general(a_ref[...].astype(BF16), b_ref[...].reshape(tk, tn).astype(BF16), dims,
                                       preferred_element_type=F32)
            out = None
            for s in range(span):
                part = lax.dot_general(a_ref[:, s * k_unit:(s + 1) * k_unit].astype(BF16), b_ref[s].astype(BF16),
                                       dims, preferred_element_type=F32)
                out = part if out is None else out + part
            return out

        def finish(acc):
            outs = (acc,) if epilogue is None else epilogue(acc, *[r[...] for r in extra_refs])
            for o_ref, val in zip(out_refs, outs):
                o_ref[...] = val.astype(o_ref.dtype)

        if nk == 1:
            finish(product())
            return
        acc_ref = rest[-1]

        @pl.when(k == 0)
        def _():
            acc_ref[...] = product()

        if nk > 2:
            @pl.when(jnp.logical_and(k > 0, k < nk - 1))
            def _():
                acc_ref[...] += product()

        @pl.when(k == nk - 1)
        def _():
            finish(acc_ref[...] + product())

    outs = pl.pallas_call(
        body, name=name, grid=grid,
        in_specs=[a_spec, b_spec] + [mn_spec for _ in extra] + [_ANY for _ in deps],
        out_specs=o_specs, out_shape=o_shapes,
        scratch_shapes=[pltpu.VMEM((tm, tn), F32)] if nk > 1 else [],
        compiler_params=_params(("parallel", "parallel", "arbitrary")),
    )(a, b, *extra, *deps)
    return outs[0] if n_out == 1 else outs


def _relu2_epilogue(acc):
    r = jnp.maximum(acc, 0.0)
    return acc, r * r


def _relu2_bwd_epilogue(acc, a_pre):
    return (acc * (2.0 * jnp.maximum(a_pre.astype(F32), 0.0)),)


def _rowcall(body, *, name, tr, row_ins, full_ins=(), row_outs=(), acc_outs=(), scratch=(), reverse=False):
    T = row_ins[0].shape[0]
    nb = T // tr
    rmap = (lambda i: (nb - 1 - i, 0)) if reverse else (lambda i: (i, 0))

    def full_spec(shape):
        nd = len(shape)
        return pl.BlockSpec(tuple(shape), lambda i: (0,) * nd)

    in_specs = [pl.BlockSpec((tr, a.shape[1]), rmap) for a in row_ins] + [full_spec(a.shape) for a in full_ins]
    out_specs = [pl.BlockSpec((tr, s.shape[1]), rmap) for s in row_outs] + [full_spec(s.shape) for s in acc_outs]
    outs = pl.pallas_call(
        body, name=name, grid=(nb,), in_specs=in_specs, out_specs=out_specs,
        out_shape=list(row_outs) + list(acc_outs), scratch_shapes=list(scratch),
        compiler_params=_params(("arbitrary",)),
    )(*row_ins, *full_ins)
    return outs


def _sds(shape, dtype):
    return jax.ShapeDtypeStruct(tuple(shape), dtype)


def _row_tile(T, C, elems=512 * 1024):
    t = max(8, min(T, elems // C))
    p = 8
    while p * 2 <= t and T % (p * 2) == 0:
        p *= 2
    return p


def _norm_mod(x, ng, sc, sh, name):
    T, D = x.shape

    def body(x_ref, ng_ref, sc_ref, sh_ref, h_ref):
        xv = x_ref[...]
        r = lax.rsqrt(jnp.mean(xv * xv, axis=-1, keepdims=True) + EPS)
        h_ref[...] = (((xv * r) * ng_ref[...]) * (1.0 + sc_ref[...]) + sh_ref[...]).astype(BF16)

    return _rowcall(body, name=name, tr=_row_tile(T, D), row_ins=[x], full_ins=[ng, sc, sh],
                    row_outs=[_sds((T, D), BF16)])[0]


def _res_norm_mod(x, y, gate, ng, sc, sh, name):
    T, D = x.shape

    def body(x_ref, y_ref, g_ref, ng_ref, sc_ref, sh_ref, x2_ref, h_ref):
        xv = x_ref[...] + g_ref[...] * y_ref[...]
        x2_ref[...] = xv
        r = lax.rsqrt(jnp.mean(xv * xv, axis=-1, keepdims=True) + EPS)
        h_ref[...] = (((xv * r) * ng_ref[...]) * (1.0 + sc_ref[...]) + sh_ref[...]).astype(BF16)

    return _rowcall(body, name=name, tr=_row_tile(T, D, 256 * 1024), row_ins=[x, y], full_ins=[gate, ng, sc, sh],
                    row_outs=[_sds((T, D), F32), _sds((T, D), BF16)])


def _res_loss(x, y, gate, target, name):
    T, D = x.shape

    def body(x_ref, y_ref, t_ref, g_ref, dout_ref, dy_ref, loss_ref, dg_ref):
        @pl.when(pl.program_id(0) == 0)
        def _():
            loss_ref[...] = jnp.zeros_like(loss_ref)
            dg_ref[...] = jnp.zeros_like(dg_ref)

        yv, gv = y_ref[...], g_ref[...]
        diff = x_ref[...] + gv * yv - t_ref[...]
        dout = diff * (1.0 / D)
        dout_ref[...] = dout
        dy_ref[...] = (dout * gv).astype(BF16)
        loss_ref[...] += jnp.sum(diff * diff) * (0.5 / D)
        dg_ref[...] += jnp.sum(dout * yv, axis=0, keepdims=True)

    return _rowcall(body, name=name, tr=_row_tile(T, D, 256 * 1024), row_ins=[x, y, target], full_ins=[gate],
                    row_outs=[_sds((T, D), F32), _sds((T, D), BF16)],
                    acc_outs=[_sds((1, BLK), F32), _sds((1, D), F32)])


def _norm_mod_bwd(x, dhs, dres, ng, sc, name, gated=None):
    T, D = x.shape
    n_dh = len(dhs)
    n_row = 2 + n_dh + (1 if gated else 0)

    def body(*refs):
        x_ref, dh_refs, dres_ref = refs[0], refs[1:1 + n_dh], refs[1 + n_dh]
        ng_ref, sc_ref = refs[n_row:n_row + 2]
        outs = refs[n_row + (3 if gated else 2):]
        dx_ref, sums_ref = (outs[0], outs[2]) if gated else (outs[0], outs[1])

        @pl.when(pl.program_id(0) == 0)
        def _():
            sums_ref[...] = jnp.zeros_like(sums_ref)
            if gated:
                outs[3][...] = jnp.zeros_like(outs[3])

        xv = x_ref[...]
        dh = dh_refs[0][...].astype(F32)
        for r_ in dh_refs[1:]:
            dh = dh + r_[...].astype(F32)
        r = lax.rsqrt(jnp.mean(xv * xv, axis=-1, keepdims=True) + EPS)
        n = xv * r
        ngv, scale1 = ng_ref[...], 1.0 + sc_ref[...]
        dn = dh * (ngv * scale1)
        dx = dres_ref[...] + r * (dn - n * jnp.mean(dn * n, axis=-1, keepdims=True))
        dx_ref[...] = dx
        dhn = dh * n
        sums_ref[0:1, :] += jnp.sum(dh, axis=0, keepdims=True)
        sums_ref[1:2, :] += jnp.sum(dhn * ngv, axis=0, keepdims=True)
        sums_ref[2:3, :] += jnp.sum(dhn * scale1, axis=0, keepdims=True)
        if gated:
            y_ref, g_ref = refs[2 + n_dh], refs[n_row + 2]
            outs[1][...] = (dx * g_ref[...]).astype(BF16)
            outs[3][...] += jnp.sum(dx * y_ref[...], axis=0, keepdims=True)

    return _rowcall(body, name=name, tr=_row_tile(T, D, 256 * 1024),
                    row_ins=[x, *dhs, dres] + ([gated[0]] if gated else []),
                    full_ins=[ng, sc] + ([gated[1]] if gated else []),
                    row_outs=[_sds((T, D), F32)] + ([_sds((T, D), BF16)] if gated else []),
                    acc_outs=[_sds((8, D), F32)] + ([_sds((1, D), F32)] if gated else []))


def _head_norm(x, g, n_heads, name, tail=False):
    T = x.shape[0]
    D = n_heads * BLK
    W = x.shape[1] if tail else D

    def body(x_ref, g_ref, o_ref, *tail_ref):
        for h in range(n_heads):
            xv = x_ref[:, h * BLK:(h + 1) * BLK]
            r = lax.rsqrt(jnp.mean(xv * xv, axis=-1, keepdims=True) + EPS)
            o_ref[:, h * BLK:(h + 1) * BLK] = ((xv * r) * g_ref[...]).astype(BF16)
        if tail:
            tail_ref[0][...] = x_ref[:, D:2 * D].astype(BF16)

    tr = _row_tile(T, x.shape[1])
    o_spec = pl.BlockSpec((tr, D), lambda i: (i, 0))
    return pl.pallas_call(
        body, name=name, grid=(T // tr,),
        in_specs=[pl.BlockSpec((tr, W), lambda i: (i, 0)), pl.BlockSpec((1, BLK), lambda i: (0, 0))],
        out_specs=[o_spec] * (2 if tail else 1), out_shape=[_sds((T, D), BF16)] * (2 if tail else 1),
        compiler_params=_params(("parallel",)),
    )(x, g)


def _head_norm_bwd(x, dys, g, n_heads, name, tails=()):
    T = x.shape[0]
    D = n_heads * BLK
    n_dy, n_tail = len(dys), len(tails)
    W = 2 * D if tails else D

    def body(*refs):
        x_ref, dy_refs, tail_refs = refs[0], refs[1:1 + n_dy], refs[1 + n_dy:1 + n_dy + n_tail]
        g_ref, dx_ref, dg_ref = refs[1 + n_dy + n_tail:]

        @pl.when(pl.program_id(0) == 0)
        def _():
            dg_ref[...] = jnp.zeros_like(dg_ref)

        tot = jnp.zeros((1, BLK), F32)
        for h in range(n_heads):
            cols = slice(h * BLK, (h + 1) * BLK)
            xv = x_ref[:, cols]
            dyv = dy_refs[0][:, cols]
            for r_ in dy_refs[1:]:
                dyv = dyv + r_[:, cols]
            r = lax.rsqrt(jnp.mean(xv * xv, axis=-1, keepdims=True) + EPS)
            n = xv * r
            dn = dyv * g_ref[...]
            dx_ref[:, cols] = (r * (dn - n * jnp.mean(dn * n, axis=-1, keepdims=True))).astype(BF16)
            tot = tot + jnp.sum(dyv * n, axis=0, keepdims=True)
        dg_ref[0:1, :] += tot
        if n_tail:
            tv = tail_refs[0][...]
            for r_ in tail_refs[1:]:
                tv = tv + r_[...]
            dx_ref[:, D:] = tv.astype(BF16)

    tr = _row_tile(T, 2 * D, 256 * 1024)
    d_spec = pl.BlockSpec((tr, D), lambda i: (i, 0))
    return pl.pallas_call(
        body, name=name, grid=(T // tr,),
        in_specs=[d_spec] * (1 + n_dy + n_tail) + [pl.BlockSpec((1, BLK), lambda i: (0, 0))],
        out_specs=[pl.BlockSpec((tr, W), lambda i: (i, 0)), pl.BlockSpec((8, BLK), lambda i: (0, 0))],
        out_shape=[_sds((T, W), BF16), _sds((8, BLK), F32)],
        compiler_params=_params(("arbitrary",)),
    )(x, *dys, *tails, g)


def _fcum_fwd(fl, bf, name):
    T = fl.shape[0]

    def body(fl_ref, b_ref, o_ref, carry_ref):
        @pl.when(pl.program_id(0) == 0)
        def _():
            carry_ref[...] = jnp.zeros_like(carry_ref)

        z = fl_ref[...] + b_ref[...]
        logf = jnp.minimum(z, 0.0) - jnp.log(1.0 + jnp.exp(-jnp.abs(z)))
        row = lax.broadcasted_iota(jnp.int32, (BLK, BLK), 0)
        col = lax.broadcasted_iota(jnp.int32, (BLK, BLK), 1)
        tri = (col <= row).astype(F32)
        run = jnp.dot(tri, logf, preferred_element_type=F32, precision=lax.Precision.HIGHEST) + carry_ref[0:1, :]
        o_ref[...] = run
        carry_ref[0:1, :] = run[BLK - 1:BLK, :]

    return _rowcall(body, name=name, tr=BLK, row_ins=[fl], full_ins=[bf], row_outs=[_sds((T, BLK), F32)],
                    scratch=[pltpu.VMEM((8, BLK), F32)])[0]


def _fcum_bwd(dfs, fl, bf, name):
    T = fl.shape[0]
    n_df = len(dfs)

    def body(*refs):
        df_refs = refs[:n_df]
        fl_ref, b_ref, dfl_ref, dbias_ref, carry_ref = refs[n_df:]

        @pl.when(pl.program_id(0) == 0)
        def _():
            carry_ref[...] = jnp.zeros_like(carry_ref)
            dbias_ref[...] = jnp.zeros_like(dbias_ref)

        dfc = df_refs[0][...]
        for r_ in df_refs[1:]:
            dfc = dfc + r_[...]
        row = lax.broadcasted_iota(jnp.int32, (BLK, BLK), 0)
        col = lax.broadcasted_iota(jnp.int32, (BLK, BLK), 1)
        tri = (col >= row).astype(F32)
        suffix = jnp.dot(tri, dfc, preferred_element_type=F32, precision=lax.Precision.HIGHEST) + carry_ref[0:1, :]
        carry_ref[0:1, :] = suffix[0:1, :]
        z = fl_ref[...] + b_ref[...]
        dfl = suffix / (1.0 + jnp.exp(z))
        dfl_ref[...] = dfl.astype(BF16)
        dbias_ref[0:1, :] += jnp.sum(dfl, axis=0, keepdims=True)

    return _rowcall(body, name=name, tr=BLK, row_ins=[*dfs, fl], full_ins=[bf], reverse=True,
                    row_outs=[_sds((T, BLK), BF16)], acc_outs=[_sds((8, BLK), F32)],
                    scratch=[pltpu.VMEM((8, BLK), F32)])


def _attn_tile(T):
    return min(T, 512)


def _attn_fwd(q, k, v, fk, n_heads, name):
    T = q.shape[0]
    tq = tk = _attn_tile(T)
    nkb = T // tk
    inv_sqrt = 1.0 / float(math.sqrt(BLK))

    def body(q_ref, k_ref, v_ref, fk_ref, o_ref, o32_ref, lse_ref):
        i = pl.program_id(1)
        qv = q_ref[...]

        def block(j, carry, diagonal):
            m, l, acc = carry
            rows = pl.ds(pl.multiple_of(j * tk, tk), tk)
            kj, vj = k_ref[rows, :], v_ref[rows, :]
            s = lax.dot_general(qv, kj, (((1,), (1,)), ((), ())), preferred_element_type=F32) * inv_sqrt
            s = s - fk_ref[j]
            if diagonal:
                s = jnp.where(lax.broadcasted_iota(jnp.int32, (tq, tk), 1)
                              <= lax.broadcasted_iota(jnp.int32, (tq, tk), 0), s, NEG_BIG)
            m_new = jnp.maximum(m, jnp.max(s, axis=-1, keepdims=True))
            alpha = jnp.exp(m - m_new)
            p = jnp.exp(s - m_new)
            l = alpha * l + jnp.sum(p, axis=-1, keepdims=True)
            acc = alpha * acc + jnp.dot(p.astype(BF16), vj, preferred_element_type=F32)
            return m_new, l, acc

        init = (jnp.full((tq, 1), NEG_BIG, F32), jnp.zeros((tq, 1), F32), jnp.zeros((tq, BLK), F32))
        carry = lax.fori_loop(0, i, lambda j, c: block(j, c, False), init)
        m, l, acc = block(i, carry, True)
        out = acc / l
        o_ref[...] = out.astype(BF16)
        o32_ref[...] = out
        lse_ref[...] = m + jnp.log(l)

    return pl.pallas_call(
        body, name=name, grid=(n_heads, T // tq),
        in_specs=[pl.BlockSpec((tq, BLK), lambda h, i: (i, h)),
                  pl.BlockSpec((T, BLK), lambda h, i: (0, h)),
                  pl.BlockSpec((T, BLK), lambda h, i: (0, h)),
                  pl.BlockSpec((None, nkb, 1, tk), lambda h, i: (h, 0, 0, 0))],
        out_specs=[pl.BlockSpec((tq, BLK), lambda h, i: (i, h)),
                   pl.BlockSpec((tq, BLK), lambda h, i: (i, h)),
                   pl.BlockSpec((None, tq, 1), lambda h, i: (h, i, 0))],
        out_shape=[_sds((T, n_heads * BLK), BF16), _sds((T, n_heads * BLK), F32), _sds((n_heads, T, 1), F32)],
        compiler_params=_params(("parallel", "arbitrary")),
    )(q, k, v, fk)


def _attn_bwd(q, k, v, o, do, lse, fk, n_heads, name):
    T = q.shape[0]
    tq = tk = _attn_tile(T)
    nkb = T // tk
    nq = T // tq
    inv_sqrt = 1.0 / float(math.sqrt(BLK))
    tn_dims = (((0,), (0,)), ((), ()))
    nt_dims = (((1,), (1,)), ((), ()))

    def body(q_ref, k_ref, v_ref, o_ref, do_ref, lse_ref, fk_ref, dq_ref, dk_ref, dv_ref, dfq_ref, dfk_ref, delta_ref):
        j = pl.program_id(1)

        @pl.when(j == 0)
        def _():
            delta_ref[...] = jnp.sum(do_ref[...].astype(F32) * o_ref[...], axis=1, keepdims=True)
            dq_ref[...] = jnp.zeros_like(dq_ref)
            dfq_ref[...] = jnp.zeros_like(dfq_ref)

        kj, vj, fkv = k_ref[...], v_ref[...], fk_ref[...]

        def step(i, carry, diagonal):
            dk, dv, dfk = carry
            rows = pl.ds(pl.multiple_of(i * tq, tq), tq)
            qi, doi = q_ref[rows, :], do_ref[rows, :]
            s = lax.dot_general(qi, kj, nt_dims, preferred_element_type=F32) * inv_sqrt - fkv
            if diagonal:
                s = jnp.where(lax.broadcasted_iota(jnp.int32, (tq, tk), 1)
                              <= lax.broadcasted_iota(jnp.int32, (tq, tk), 0), s, NEG_BIG)
            p = jnp.exp(s - lse_ref[rows, :])
            dv = dv + lax.dot_general(p.astype(BF16), doi, tn_dims, preferred_element_type=F32)
            dp = lax.dot_general(doi, vj, nt_dims, preferred_element_type=F32)
            ds = p * (dp - delta_ref[rows, :])
            dsb = ds.astype(BF16)
            dq_ref[rows, :] += jnp.dot(dsb, kj, preferred_element_type=F32) * inv_sqrt
            dk = dk + lax.dot_general(dsb, qi, tn_dims, preferred_element_type=F32)
            dfq_ref[rows, :] += jnp.sum(ds, axis=1, keepdims=True)
            dfk = dfk - jnp.sum(ds, axis=0, keepdims=True)
            return dk, dv, dfk

        init = (jnp.zeros((tk, BLK), F32), jnp.zeros((tk, BLK), F32), jnp.zeros((1, tk), F32))
        carry = step(j, init, True)
        dk, dv, dfk = lax.fori_loop(j + 1, nq, lambda i, c: step(i, c, False), carry)
        dk_ref[...] = dk * inv_sqrt
        dv_ref[...] = dv
        dfk_ref[...] = dfk

    head_col = lambda h, j: (0, h)
    return pl.pallas_call(
        body, name=name, grid=(n_heads, nkb),
        in_specs=[pl.BlockSpec((T, BLK), head_col),
                  pl.BlockSpec((tk, BLK), lambda h, j: (j, h)),
                  pl.BlockSpec((tk, BLK), lambda h, j: (j, h)),
                  pl.BlockSpec((T, BLK), head_col),
                  pl.BlockSpec((T, BLK), head_col),
                  pl.BlockSpec((None, T, 1), lambda h, j: (h, 0, 0)),
                  pl.BlockSpec((None, None, 1, tk), lambda h, j: (h, j, 0, 0))],
        out_specs=[pl.BlockSpec((T, BLK), head_col),
                   pl.BlockSpec((tk, BLK), lambda h, j: (j, h)),
                   pl.BlockSpec((tk, BLK), lambda h, j: (j, h)),
                   pl.BlockSpec((None, T, 1), lambda h, j: (h, 0, 0)),
                   pl.BlockSpec((None, None, 1, tk), lambda h, j: (h, j, 0, 0))],
        out_shape=[_sds((T, n_heads * BLK), F32), _sds((T, n_heads * BLK), F32), _sds((T, n_heads * BLK), F32),
                   _sds((n_heads, T, 1), F32), _sds((n_heads, nkb, 1, tk), F32)],
        scratch_shapes=[pltpu.VMEM((T, 1), F32)],
        compiler_params=_params(("parallel", "arbitrary")),
    )(q, k, v, o, do, lse, fk)


_INV_SQRT2 = 1.0 / math.sqrt(2.0)
_INV_SQRT_2PI = 1.0 / math.sqrt(2.0 * math.pi)


def _gelu_parts(z):
    cdf = 0.5 * (1.0 + lax.erf(z * _INV_SQRT2))
    return cdf, z * cdf


def _mix_mask(transposed):
    row = lax.broadcasted_iota(jnp.int32, (BLK, BLK), 0) // CHUNK
    col = lax.broadcasted_iota(jnp.int32, (BLK, BLK), 1) // CHUNK
    return (row <= col) if transposed else (col <= row)


def _gmlp_mid_fwd(zpre, ln_g, ln_b, ws, bs_t, name):
    T, two_h = zpre.shape
    Hh = two_h // 2
    G = ws.shape[0]
    gd = Hh // G

    def body(z_ref, lg_ref, lb_ref, ws_ref, bs_ref, p_ref):
        _, zg = _gelu_parts(z_ref[...].astype(F32))
        u, v = zg[:, :Hh], zg[:, Hh:]
        mu = jnp.mean(v, axis=-1, keepdims=True)
        vc = v - mu
        rstd = lax.rsqrt(jnp.mean(vc * vc, axis=-1, keepdims=True) + EPS)
        vn = ((vc * rstd) * lg_ref[...] + lb_ref[...]).astype(BF16)
        mask = _mix_mask(False)
        for g in range(G):
            wm = jnp.where(mask, ws_ref[g], 0.0).astype(BF16)
            sv = jnp.dot(wm, vn[:, g * gd:(g + 1) * gd], preferred_element_type=F32) + bs_ref[:, g:g + 1]
            p_ref[:, g * gd:(g + 1) * gd] = (u[:, g * gd:(g + 1) * gd] * sv).astype(BF16)

    return _rowcall(body, name=name, tr=BLK, row_ins=[zpre], full_ins=[ln_g, ln_b, ws, bs_t],
                    row_outs=[_sds((T, Hh), BF16)])[0]


def _gmlp_mid_bwd(zpre, dp, ln_g, ln_b, ws, ws_t, bs_t, name):
    T, two_h = zpre.shape
    Hh = two_h // 2
    G = ws.shape[0]
    gd = Hh // G
    nt_dims = (((1,), (1,)), ((), ()))

    def body(z_ref, dp_ref, lg_ref, lb_ref, ws_ref, wst_ref, bs_ref, dz_ref, dws_ref, dbs_ref, dlg_ref, dlb_ref,
             dvn_ref):
        @pl.when(pl.program_id(0) == 0)
        def _():
            dws_ref[...] = jnp.zeros_like(dws_ref)
            dbs_ref[...] = jnp.zeros_like(dbs_ref)
            dlg_ref[...] = jnp.zeros_like(dlg_ref)
            dlb_ref[...] = jnp.zeros_like(dlb_ref)

        z = z_ref[...].astype(F32)
        cdf, zg = _gelu_parts(z)
        dgelu = cdf + z * (jnp.exp(-0.5 * z * z) * _INV_SQRT_2PI)
        u, v = zg[:, :Hh], zg[:, Hh:]
        mu = jnp.mean(v, axis=-1, keepdims=True)
        vc = v - mu
        rstd = lax.rsqrt(jnp.mean(vc * vc, axis=-1, keepdims=True) + EPS)
        vhat = vc * rstd
        vn = (vhat * lg_ref[...] + lb_ref[...]).astype(BF16)
        mask, mask_t = _mix_mask(False), _mix_mask(True)
        lane = lax.broadcasted_iota(jnp.int32, (BLK, BLK), 1)
        dbs = jnp.zeros((BLK, BLK), F32)
        for g in range(G):
            cols = slice(g * gd, (g + 1) * gd)
            wm = jnp.where(mask, ws_ref[g], 0.0).astype(BF16)
            wm_t = jnp.where(mask_t, wst_ref[g], 0.0).astype(BF16)
            vn_g = vn[:, cols]
            sv = jnp.dot(wm, vn_g, preferred_element_type=F32) + bs_ref[:, g:g + 1]
            dp_g = dp_ref[:, cols].astype(F32)
            dz_ref[:, cols] = ((dp_g * sv) * dgelu[:, cols]).astype(BF16)
            dsv = dp_g * u[:, cols]
            dsv_b = dsv.astype(BF16)
            dbs = dbs + jnp.where(lane == g, jnp.sum(dsv, axis=1, keepdims=True), 0.0)
            dws_ref[g] += jnp.where(mask, lax.dot_general(dsv_b, vn_g, nt_dims, preferred_element_type=F32), 0.0)
            dvn_ref[:, cols] = jnp.dot(wm_t, dsv_b, preferred_element_type=F32)
        dbs_ref[...] += dbs
        dvn = dvn_ref[...]
        dlg_ref[0:1, :] += jnp.sum(dvn * vhat, axis=0, keepdims=True)
        dlb_ref[0:1, :] += jnp.sum(dvn, axis=0, keepdims=True)
        dvh = dvn * lg_ref[...]
        dv = rstd * (dvh - jnp.mean(dvh, axis=-1, keepdims=True) - vhat * jnp.mean(dvh * vhat, axis=-1, keepdims=True))
        dz_ref[:, Hh:] = (dv * dgelu[:, Hh:]).astype(BF16)

    return _rowcall(body, name=name, tr=BLK, row_ins=[zpre, dp], full_ins=[ln_g, ln_b, ws, ws_t, bs_t],
                    row_outs=[_sds((T, two_h), BF16)],
                    acc_outs=[_sds((G, BLK, BLK), F32), _sds((BLK, BLK), F32), _sds((8, Hh), F32), _sds((8, Hh), F32)],
                    scratch=[pltpu.VMEM((BLK, Hh), F32)])


def _mods(c_all, w, layer, bias, name):
    nb, K = c_all.shape
    N = w.shape[-1]
    tn = _tile(N, 512)

    def body(c_ref, w_ref, b_ref, o_ref):
        cv = c_ref[...]
        sc = cv / (1.0 + jnp.exp(-cv))
        o_ref[...] = jnp.dot(sc, w_ref[...], preferred_element_type=F32, precision=lax.Precision.HIGHEST) + b_ref[...]

    return pl.pallas_call(
        body, name=name, grid=(N // tn,),
        in_specs=[pl.BlockSpec((nb, K), lambda j: (0, 0)),
                  pl.BlockSpec((None, K, tn), lambda j: (layer, 0, j)),
                  pl.BlockSpec((1, tn), lambda j: (0, j))],
        out_specs=pl.BlockSpec((nb, tn), lambda j: (0, j)), out_shape=_sds((nb, N), F32),
        compiler_params=_params(("parallel",)),
    )(c_all, w, bias)


def _sum_slabs(landed, own, who, name):
    _, R, C = landed.shape
    tr = _row_tile(R, C * N_DEV)

    def body(who_ref, x_ref, own_ref, o_ref):
        me, mine = who_ref[0], own_ref[...]
        acc = jnp.where(me == 0, mine, x_ref[0])
        for s in range(1, N_DEV):
            acc = acc + jnp.where(me == s, mine, x_ref[s])
        o_ref[...] = acc

    return pl.pallas_call(
        body, name=name,
        grid_spec=pltpu.PrefetchScalarGridSpec(
            num_scalar_prefetch=1, grid=(R // tr,),
            in_specs=[pl.BlockSpec((N_DEV, tr, C), lambda i, who_ref: (0, i, 0)),
                      pl.BlockSpec((tr, C), lambda i, who_ref: (i, 0))],
            out_specs=pl.BlockSpec((tr, C), lambda i, who_ref: (i, 0))),
        out_shape=_sds((R, C), F32),
        compiler_params=_params(("parallel",)),
    )(who, landed, own)


def _adamw_math(w, g, m, v):
    m = ADAM_B1 * m + (1.0 - ADAM_B1) * g
    v = ADAM_B2 * v + (1.0 - ADAM_B2) * (g * g)
    m_hat = m / (1.0 - ADAM_B1 ** ADAM_STEP)
    v_hat = v / (1.0 - ADAM_B2 ** ADAM_STEP)
    delta = -ADAM_LR * (m_hat / (jnp.sqrt(v_hat) + ADAM_EPS) + ADAM_WD * w)
    return delta, m, v


class _AdamStack:
    def __init__(self, w, m, v, name):
        self.w, self.m, self.v, self.name = w, m, v, name
        self.L, self.R, self.C = w.shape
        self.tr = _row_tile(self.R, self.C, 256 * 1024)
        self.outs = None

    def _layer(self, l, who, srcs, src_specs, make_grad):
        n_src = len(srcs)
        L, R, C, tr = self.L, self.R, self.C, self.tr
        wspec = pl.BlockSpec((None, tr, C), lambda i, who_ref: (l, i, 0))

        def body(who_ref, *refs):
            src_refs = refs[:n_src]
            w_ref, m_ref, v_ref = refs[n_src:n_src + 3]
            g_ref, d_ref, m2_ref, v2_ref = refs[-4:]
            g = make_grad(who_ref[0], *src_refs)
            delta, m2, v2 = _adamw_math(w_ref[...], g, m_ref[...], v_ref[...])
            g_ref[...] = g
            d_ref[...] = delta
            m2_ref[...] = m2
            v2_ref[...] = v2

        prev = [] if self.outs is None else list(self.outs)
        aliases = {} if self.outs is None else {1 + n_src + 3 + t: t for t in range(4)}
        self.outs = pl.pallas_call(
            body, name=f"{self.name}_l{l}",
            grid_spec=pltpu.PrefetchScalarGridSpec(
                num_scalar_prefetch=1, grid=(R // tr,),
                in_specs=list(src_specs) + [wspec] * 3 + [_ANY] * len(prev), out_specs=[wspec] * 4),
            out_shape=[_sds((L, R, C), F32)] * 4,
            input_output_aliases=aliases,
            compiler_params=_params(("parallel",)),
        )(who, *srcs, self.w, self.m, self.v, *prev)

    def from_parts(self, l, who, landed, sent):
        tr, C = self.tr, self.C

        def make_grad(me, p_ref, own_ref):
            mine = own_ref[...].astype(F32)
            g = jnp.where(me == 0, mine, p_ref[0].astype(F32))
            for s in range(1, N_DEV):
                g = g + jnp.where(me == s, mine, p_ref[s].astype(F32))
            return g

        self._layer(l, who, [landed, sent],
                    [pl.BlockSpec((N_DEV, tr, C), lambda i, who_ref: (0, i, 0)),
                     pl.BlockSpec((None, tr, C), lambda i, who_ref: (who_ref[0], i, 0))], make_grad)

    def from_outer(self, l, who, sc_t, dmod):
        tr, C = self.tr, self.C

        def make_grad(me, s_ref, d_ref):
            g = s_ref[:, 0:1] * d_ref[0:1, :]
            for b in range(1, N_DEV):
                g = g + s_ref[:, b:b + 1] * d_ref[b:b + 1, :]
            return g

        self._layer(l, who, [sc_t, dmod], [pl.BlockSpec((tr, N_DEV), lambda i, who_ref: (i, 0)),
                                           pl.BlockSpec((N_DEV, C), lambda i, who_ref: (0, 0))], make_grad)


def _adamw_flat(w, g, m, v, name):
    R, C = w.shape
    tr = _row_tile(R, C, 128 * 1024)

    def body(w_ref, g_ref, m_ref, v_ref, d_ref, m2_ref, v2_ref):
        delta, m2, v2 = _adamw_math(w_ref[...], g_ref[...], m_ref[...], v_ref[...])
        d_ref[...] = delta
        m2_ref[...] = m2
        v2_ref[...] = v2

    spec = pl.BlockSpec((tr, C), lambda i: (i, 0))
    return pl.pallas_call(
        body, name=name, grid=(R // tr,), in_specs=[spec] * 4, out_specs=[spec] * 3,
        out_shape=[_sds((R, C), F32)] * 3, compiler_params=_params(("parallel",)),
    )(w, g, m, v)


def _pack(arrays):
    flat = jnp.concatenate([a.reshape(-1).astype(F32) for a in arrays])
    pad = (-flat.shape[0]) % (64 * BLK)
    if pad:
        flat = jnp.concatenate([flat, jnp.zeros((pad,), F32)])
    return flat.reshape(-1, BLK)


def _unpack(buf, shapes, lead=()):
    sizes = [int(math.prod(s)) for s in shapes]
    out, off = [], 0
    if all(n % BLK == 0 for n in sizes):
        for s, n in zip(shapes, sizes):
            out.append(buf[..., off // BLK:(off + n) // BLK, :].reshape(tuple(lead) + tuple(s)))
            off += n
        return out
    flat = buf.reshape(tuple(lead) + (-1,))
    for s, n in zip(shapes, sizes):
        out.append(flat[..., off:off + n].reshape(tuple(lead) + tuple(s)))
        off += n
    return out


def _row(vec):
    return vec.reshape(1, -1)


def _shard_of(full, axis, me, size):
    return lax.dynamic_slice_in_dim(full, me * size, size, axis=axis)


def kernel(x, c, ada_w, ada_b, norm_g, mlp_w1, mlp_w2, gmlp_w_in, gmlp_ln_g, gmlp_ln_b, gmlp_ws, gmlp_bs, gmlp_w_out, kv_norm_g, kv_ada_w, kv_ada_b, w_kv, k_norm_g, w_f, b_f, attn_wq, q_norm_g, attn_wo, loss_target, m_ada_w, m_ada_b, m_norm_g, m_mlp_w1, m_mlp_w2, m_gmlp_w_in, m_gmlp_ln_g, m_gmlp_ln_b, m_gmlp_ws, m_gmlp_bs, m_gmlp_w_out, m_kv_norm_g, m_kv_ada_w, m_kv_ada_b, m_w_kv, m_k_norm_g, m_w_f, m_b_f, m_attn_wq, m_q_norm_g, m_attn_wo, v_ada_w, v_ada_b, v_norm_g, v_mlp_w1, v_mlp_w2, v_gmlp_w_in, v_gmlp_ln_g, v_gmlp_ln_b, v_gmlp_ws, v_gmlp_bs, v_gmlp_w_out, v_kv_norm_g, v_kv_ada_w, v_kv_ada_b, v_w_kv, v_k_norm_g, v_w_f, v_b_f, v_attn_wq, v_q_norm_g, v_attn_wo):
    given = dict(locals())
    weights = {n: given[n] for n in WEIGHT_NAMES}
    mom_m = {n: given["m_" + n] for n in WEIGHT_NAMES}
    mom_v = {n: given["v_" + n] for n in WEIGHT_NAMES}

    me = _my_index()
    T, D = x.shape[1], x.shape[2]
    depth = ada_w.shape[0]
    n_a = gmlp_w_in.shape[0]
    n_heads = b_f.shape[0]
    G = gmlp_ws.shape[1]
    Hh = gmlp_ln_g.shape[1] * N_DEV
    mod_cols = ada_w.shape[2]
    kv_cols = kv_ada_w.shape[1]
    x0 = x.reshape(T, D)
    target = loss_target.reshape(T, D)

    small_in = [c, norm_g, gmlp_ln_g, gmlp_ln_b, w_f]
    small_shapes = [a.shape for a in small_in]
    got = _all_to_all(_pack(small_in), "gather_small_inputs", bcast=True)
    c_all, norm_g_sh, ln_g_sh, ln_b_sh, w_f_sh = _unpack(got, small_shapes, lead=(N_DEV,))
    c_all = c_all.reshape(N_DEV, D)
    norm_g_full = jnp.moveaxis(norm_g_sh, 0, 2).reshape(depth, 2, D)
    ln_g_full = jnp.moveaxis(ln_g_sh, 0, 1).reshape(n_a, Hh)
    ln_b_full = jnp.moveaxis(ln_b_sh, 0, 1).reshape(n_a, Hh)
    w_f_full = w_f_sh.reshape(D, n_heads)
    w_f_pad = jnp.pad(w_f_full, ((0, 0), (0, BLK - n_heads))).astype(BF16)
    b_f_pad = jnp.pad(b_f, (0, BLK - n_heads)).reshape(1, BLK)

    mod_parts = []
    for l in range(depth):
        bias = _shard_of(ada_b[l], 0, me, mod_cols).reshape(1, mod_cols)
        mod_parts.append(_mods(c_all, ada_w, l, bias, f"mods_l{l}"))
    kv_bias = _shard_of(kv_ada_b, 0, me, kv_cols).reshape(1, kv_cols)
    mod_parts.append(_mods(c_all, kv_ada_w.reshape(1, D, kv_cols), 0, kv_bias, "mods_kv"))
    mods_mine = jnp.concatenate(mod_parts, axis=1)
    mod_width = mods_mine.shape[1]
    mods_pack = jnp.pad(mods_mine, ((0, 0), (0, (-mod_width) % (8 * BLK)))).reshape(N_DEV, -1, BLK)
    mods_landed = _all_to_all(mods_pack, "exchange_mods")
    mods_got = mods_landed.reshape(N_DEV, -1)[:, :mod_width]
    mods = []
    for l in range(depth):
        mods.append(mods_got[:, l * mod_cols:(l + 1) * mod_cols].reshape(N_MOD, D))
    kv_mod = mods_got[:, depth * mod_cols:].reshape(2, D)
    silu_all = c_all / (1.0 + jnp.exp(-c_all))

    assert 1 <= n_a < depth
    who = me.astype(jnp.int32).reshape(1)
    big = {"mlp_w1": mlp_w1, "mlp_w2": mlp_w2, "gmlp_w_in": gmlp_w_in, "gmlp_w_out": gmlp_w_out,
           "w_kv": w_kv.reshape((1,) + w_kv.shape), "attn_wq": attn_wq, "attn_wo": attn_wo}
    groups = [[("gmlp_w_in", 0)], [("gmlp_w_out", 0), ("mlp_w1", 0), ("mlp_w2", 0)]]
    for l in range(1, depth):
        if l < n_a:
            groups.append([("gmlp_w_in", l), ("gmlp_w_out", l), ("mlp_w1", l), ("mlp_w2", l)])
        else:
            first = [("w_kv", 0)] if l == n_a else []
            groups.append(first + [("attn_wq", l - n_a), ("attn_wo", l - n_a), ("mlp_w1", l), ("mlp_w2", l)])
    tokens = []

    def behind_starts():
        out = tuple(tokens)
        tokens.clear()
        return out

    over_ici, placed = {}, {}

    def gather_begin(gi, deps=()):
        over_ici[gi] = _gather_start(placed[gi], 1, f"gather_g{gi}_ici_start", deps=deps)
        tokens.append(over_ici[gi][3])

    for gi, grp in enumerate(groups):
        behind = (got,) if gi == 0 else (mods_landed, over_ici[0][3])
        placed[gi] = [_place_shard(big[n], l, who, f"place_{n}_l{l}", deps=behind) for n, l in grp]
        if gi < 2:
            gather_begin(gi)
    to_sibling = {}
    gw = {}

    def gather_forward(gi, after):
        landed = _gather_wait(over_ici[gi], 1, after, f"gather_g{gi}_ici_wait")
        to_sibling[gi] = _gather_start(landed, 2, f"gather_g{gi}_d2d_start")
        tokens.append(to_sibling[gi][3])
        if gi + 2 < len(groups):
            gather_begin(gi + 2, deps=(to_sibling[gi][3],))

    def gather_finish(gi, after):
        for key, arr in zip(groups[gi], _gather_wait(to_sibling[gi], 2, after, f"gather_g{gi}_d2d_wait")):
            gw[key] = arr

    tkk = _attn_tile(T)

    saved = []
    xs = x0
    pending = None
    kv = None
    for l in range(depth):
        sh1, sc1, g1, sh2, sc2, g2 = [_row(mods[l][t]) for t in range(N_MOD)]
        ng1, ng2 = _row(norm_g_full[l, 0]), _row(norm_g_full[l, 1])
        st = dict(sc1=sc1, g1=g1, sc2=sc2, g2=g2, ng1=ng1, ng2=ng2)
        if pending is None:
            h1 = _norm_mod(xs, ng1, sc1, sh1, f"norm1_l{l}")
            gather_forward(0, h1)
            gather_finish(0, h1)
        else:
            gather_finish(l + 1, pending[0])
            xs, h1 = _res_norm_mod(xs, pending[0], pending[1], ng1, sc1, sh1, f"res_norm1_l{l}")
        st["x_in"], st["h1"] = xs, h1
        if l < n_a:
            a = l
            zpre = _mm(h1, gw["gmlp_w_in", a], bmode="col", out_dtypes=(BF16,), name=f"gmlp_in_l{l}",
                       deps=behind_starts())
            if l == 0:
                gather_forward(1, zpre)
            bs_t = gmlp_bs[a].T
            p = _gmlp_mid_fwd(zpre, _row(ln_g_full[a]), _row(ln_b_full[a]), gmlp_ws[a], bs_t, f"gmlp_mid_l{l}")
            if l == 0:
                gather_finish(1, p)
            y = _mm(p, gw["gmlp_w_out", a], bmode="row", name=f"gmlp_out_l{l}", deps=behind_starts())
            st.update(zpre=zpre, p=p)
        else:
            if kv is None:
                kv_ng, kv_sh, kv_sc = _row(kv_norm_g), _row(kv_mod[0]), _row(kv_mod[1])
                hkv = _norm_mod(xs, kv_ng, kv_sc, kv_sh, "norm_kv")
                kvp = _mm(hkv, gw["w_kv", 0], bmode="col", name="kv_proj", deps=behind_starts())
                kk, vv = _head_norm(kvp, _row(k_norm_g), n_heads, "k_norm", tail=True)
                fl = _mm(hkv, w_f_pad, name="gate_logits")
                fcum = _fcum_fwd(fl, b_f_pad, "fcum")
                fk = fcum[:, :n_heads].T.reshape(n_heads, T // tkk, 1, tkk)
                kv = dict(x=xs, hkv=hkv, kvp=kvp, k=kk, v=vv, fl=fl, fcum=fcum, fk=fk, ng=kv_ng, sc=kv_sc)
            bl = l - n_a
            qp = _mm(h1, gw["attn_wq", bl], bmode="row", name=f"q_proj_l{l}", deps=behind_starts())
            q = _head_norm(qp, _row(q_norm_g[bl]), n_heads, f"q_norm_l{l}")[0]
            o, o32, lse = _attn_fwd(q, kv["k"], kv["v"], kv["fk"], n_heads, f"attn_l{l}")
            y = _mm(o, gw["attn_wo", bl], bmode="row", name=f"attn_out_l{l}")
            st.update(qp=qp, q=q, o=o, o32=o32, lse=lse)
        xs, h2 = _res_norm_mod(xs, y, g1, ng2, sc2, sh2, f"res_norm2_l{l}")
        a_pre, s_act = _mm(h2, gw["mlp_w1", l], bmode="col", out_dtypes=(BF16, BF16), epilogue=_relu2_epilogue,
                           name=f"mlp_up_l{l}")
        if l + 1 < depth:
            gather_forward(l + 2, a_pre)
        mo = _mm(s_act, gw["mlp_w2", l], bmode="row", name=f"mlp_down_l{l}", deps=behind_starts())
        st.update(y=y, x_mid=xs, h2=h2, a_pre=a_pre, s=s_act, m=mo)
        saved.append(st)
        pending = (mo, g2)

    dx, dm, loss_row, dg2 = _res_loss(xs, pending[0], pending[1], target, "loss")
    loss = lax.psum(loss_row[0, 0], ("x", "y", "c"))

    started = {}

    def scatter(dw_slabs, key, idx):
        started[(key, idx)] = _a2a_start(dw_slabs, f"scatter_{key}_l{idx}_start")
        tokens.append(started[(key, idx)][4])

    d_mod = [None] * depth
    d_norm_g = [None] * depth
    d_ln_g, d_ln_b, d_ws, d_bs = [None] * n_a, [None] * n_a, [None] * n_a, [None] * n_a
    d_qg = [None] * (depth - n_a)
    dk_list, dv_list, dfk_list = [], [], []
    small = {}

    for l in reversed(range(depth)):
        st = saved[l]
        da = _mm(dm, gw["mlp_w2", l], tb=True, bmode="row", out_dtypes=(BF16,), epilogue=_relu2_bwd_epilogue,
                 extra=(st["a_pre"],), name=f"mlp_down_dx_l{l}")
        dw2 = _mm(st["s"], dm, ta=True, out_dtypes=(BF16,), name=f"mlp_down_dw_l{l}", deps=behind_starts())
        scatter(dw2.reshape(N_DEV, -1, D), "mlp_w2", l)
        dw1 = _mm(st["h2"], da, ta=True, out_mode="col", out_dtypes=(BF16,), name=f"mlp_up_dw_l{l}",
                  deps=behind_starts())
        scatter(dw1, "mlp_w1", l)
        dh2 = _mm(da, gw["mlp_w1", l], tb=True, bmode="col", out_dtypes=(BF16,), name=f"mlp_up_dx_l{l}",
                  deps=behind_starts())
        dx, dy, sums2, dg1 = _norm_mod_bwd(st["x_mid"], [dh2], dx, st["ng2"], st["sc2"], f"norm2_bwd_l{l}",
                                           gated=(st["y"], st["g1"]))
        if l < n_a:
            a = l
            dwo = _mm(st["p"], dy, ta=True, out_dtypes=(BF16,), name=f"gmlp_out_dw_l{l}", deps=behind_starts())
            scatter(dwo.reshape(N_DEV, -1, D), "gmlp_w_out", a)
            dp = _mm(dy, gw["gmlp_w_out", a], tb=True, bmode="row", out_dtypes=(BF16,), name=f"gmlp_out_dx_l{l}",
                     deps=behind_starts())
            dz, d_ws[a], dbs_t, dlg, dlb = _gmlp_mid_bwd(
                st["zpre"], dp, _row(ln_g_full[a]), _row(ln_b_full[a]), gmlp_ws[a],
                jnp.swapaxes(gmlp_ws[a], 1, 2), gmlp_bs[a].T, f"gmlp_mid_bwd_l{l}")
            d_bs[a], d_ln_g[a], d_ln_b[a] = dbs_t[:, :G].T, dlg[0], dlb[0]
            dwi = _mm(st["h1"], dz, ta=True, out_mode="col", out_dtypes=(BF16,), name=f"gmlp_in_dw_l{l}",
                      deps=behind_starts())
            scatter(dwi, "gmlp_w_in", a)
            dh1s = [_mm(dz, gw["gmlp_w_in", a], tb=True, bmode="col", out_dtypes=(BF16,), name=f"gmlp_in_dx_l{l}",
                        deps=behind_starts())]
        else:
            bl = l - n_a
            dwo = _mm(st["o"], dy, ta=True, out_dtypes=(BF16,), name=f"attn_out_dw_l{l}", deps=behind_starts())
            scatter(dwo.reshape(N_DEV, -1, D), "attn_wo", bl)
            do = _mm(dy, gw["attn_wo", bl], tb=True, bmode="row", out_dtypes=(BF16,), name=f"attn_out_dx_l{l}",
                     deps=behind_starts())
            dq, dk, dv, dfq, dfk = _attn_bwd(st["q"], kv["k"], kv["v"], st["o32"], do, st["lse"], kv["fk"],
                                             n_heads, f"attn_bwd_l{l}")
            dfk_list += [dfq, dfk]
            dk_list.append(dk)
            dv_list.append(dv)
            dqp, dqg = _head_norm_bwd(st["qp"], [dq], _row(q_norm_g[bl]), n_heads, f"q_norm_bwd_l{l}")
            d_qg[bl] = dqg[0]
            dwq = _mm(st["h1"], dqp, ta=True, out_dtypes=(BF16,), name=f"q_proj_dw_l{l}", deps=behind_starts())
            scatter(dwq.reshape(N_DEV, -1, D), "attn_wq", bl)
            dh1s = [_mm(dqp, gw["attn_wq", bl], tb=True, bmode="row", out_dtypes=(BF16,), name=f"q_proj_dx_l{l}",
                        deps=behind_starts())]
        below = (saved[l - 1]["m"], saved[l - 1]["g2"]) if l > 0 else None
        if below is None or l == n_a:
            dx, sums1 = _norm_mod_bwd(st["x_in"], dh1s, dx, st["ng1"], st["sc1"], f"norm1_bwd_l{l}")
        else:
            dx, dm_below, sums1, dg2_below = _norm_mod_bwd(st["x_in"], dh1s, dx, st["ng1"], st["sc1"],
                                                           f"norm1_bwd_l{l}", gated=below)
        d_mod[l] = jnp.stack([sums1[0], sums1[1], dg1[0], sums2[0], sums2[1], dg2[0]])
        d_norm_g[l] = jnp.stack([sums1[2], sums2[2]])
        if l == n_a:
            dkvp, dkg = _head_norm_bwd(kv["kvp"], dk_list, _row(k_norm_g), n_heads, "k_norm_bwd", tails=dv_list)
            dfc = [jnp.pad(d.reshape(n_heads, T).T, ((0, 0), (0, BLK - n_heads))) for d in dfk_list]
            dfl, dbf = _fcum_bwd(dfc, kv["fl"], b_f_pad, "fcum_bwd")
            dwkv = _mm(kv["hkv"], dkvp, ta=True, out_mode="col", out_dtypes=(BF16,), name="kv_proj_dw",
                       deps=behind_starts())
            scatter(dwkv, "w_kv", 0)
            dwf = _mm(kv["hkv"], dfl, ta=True, name="gate_logits_dw", deps=behind_starts())
            dh_a = _mm(dkvp, gw["w_kv", 0], tb=True, bmode="col", out_dtypes=(BF16,), name="kv_proj_dx")
            dh_b = _mm(dfl, w_f_pad, tb=True, out_dtypes=(BF16,), name="gate_logits_dx")
            dx, dm_below, sums_kv, dg2_below = _norm_mod_bwd(kv["x"], [dh_a, dh_b], dx, kv["ng"], kv["sc"],
                                                             "norm_kv_bwd", gated=below)
            small.update(d_kv_mod=jnp.stack([sums_kv[0], sums_kv[1]]), d_kv_norm_g=sums_kv[2], d_k_norm_g=dkg[0],
                         d_w_f=dwf[:, :n_heads], d_b_f=dbf[0, :n_heads])
        if l > 0:
            dm, dg2 = dm_below, dg2_below
        if l == 1:
            hi_contrib = [jnp.stack(d_mod[1:]).reshape(depth - 1, N_MOD * D), small["d_kv_mod"].reshape(-1)]
            hi_started = _a2a_start(_pack(hi_contrib), "gather_mod_grads_hi_start", bcast=True)
            tokens.append(hi_started[4])

    grad_x = dx.reshape(x.shape)

    lo_contrib = [d_mod[0].reshape(1, N_MOD * D)]
    rest_contrib = [jnp.stack(d_norm_g), jnp.stack(d_ln_g), jnp.stack(d_ln_b), jnp.stack(d_ws), jnp.stack(d_bs),
                    small["d_kv_norm_g"], small["d_k_norm_g"], small["d_w_f"], small["d_b_f"], jnp.stack(d_qg)]
    hi_shapes, lo_shapes = [a.shape for a in hi_contrib], [a.shape for a in lo_contrib]
    rest_shapes = [a.shape for a in rest_contrib]
    lo_started = _a2a_start(_pack(lo_contrib), "gather_mod_grads_lo_start", bcast=True, deps=behind_starts())
    rest_started = _a2a_start(_pack(rest_contrib), "gather_small_grads_start", bcast=True, deps=(lo_started[4],))
    grads, deltas, new_m, new_v = {}, {}, {}, {}

    def stack_of(n):
        lead = () if weights[n].ndim == 3 else (1,)
        return _AdamStack(*[a.reshape(lead + a.shape) for a in (weights[n], mom_m[n], mom_v[n])], f"adamw_{n}")

    def results_of(n, stack):
        grads[n], deltas[n], new_m[n], new_v[n] = [a.reshape(weights[n].shape) for a in stack.outs]

    sc_t = silu_all.T
    ada_stack, kv_ada_stack = stack_of("ada_w"), stack_of("kv_ada_w")
    is_me = (jnp.arange(N_DEV) == me).reshape(N_DEV, 1, 1)

    def landed_mods(started_pack, shapes, after, name):
        sent, land = _a2a_wait(started_pack, after, f"gather_{name}_wait", bcast=True)
        everyone = _unpack(jnp.where(is_me, sent[None], land), shapes, lead=(N_DEV,))
        return everyone, _unpack(_sum_slabs(land, sent, who, f"sum_{name}"), shapes)

    stacks = {n: stack_of(n) for n in ("mlp_w1", "mlp_w2", "gmlp_w_in", "gmlp_w_out", "attn_wq", "attn_wo", "w_kv")}
    after = rest_started[4]
    for l in reversed(range(depth)):
        if l == n_a - 1:
            (dmod_hi, dkvmod_all), (g_ada_b_hi, g_kv_ada_b) = landed_mods(hi_started, hi_shapes, after, "mod_grads_hi")
            for up in range(1, depth):
                ada_stack.from_outer(up, who, sc_t, _shard_of(dmod_hi[:, up - 1], 1, me, mod_cols))
            kv_ada_stack.from_outer(0, who, sc_t, _shard_of(dkvmod_all, 1, me, kv_cols))
            after = ada_stack.outs[0]
        if l == 0:
            (dmod_lo,), (g_ada_b_lo,) = landed_mods(lo_started, lo_shapes, after, "mod_grads_lo")
            ada_stack.from_outer(0, who, sc_t, _shard_of(dmod_lo[:, 0], 1, me, mod_cols))
            after = ada_stack.outs[0]
        keys = [("mlp_w2", l), ("mlp_w1", l)]
        keys += [("gmlp_w_out", l), ("gmlp_w_in", l)] if l < n_a else [("attn_wo", l - n_a), ("attn_wq", l - n_a)]
        keys += [("w_kv", 0)] if l == n_a else []
        landed = {}
        for key in keys:
            sent, land = _a2a_wait(started[key], after, f"scatter_{key[0]}_l{key[1]}_wait")
            landed[key] = (land, sent)
        for key in keys:
            stacks[key[0]].from_parts(key[1], who, *landed[key])
            after = stacks[key[0]].outs[0]
    for n, stack in stacks.items():
        results_of(n, stack)
    results_of("ada_w", ada_stack)
    results_of("kv_ada_w", kv_ada_stack)
    g_ada_b = jnp.concatenate([g_ada_b_lo, g_ada_b_hi], axis=0)

    rest_sent, rest_land = _a2a_wait(rest_started, after, "gather_small_grads_wait", bcast=True)
    (g_norm_g_full, g_ln_g_full, g_ln_b_full, g_ws, g_bs, g_kv_norm_g, g_k_norm_g, g_w_f_full, g_b_f,
     g_q_norm_g) = _unpack(_sum_slabs(rest_land, rest_sent, who, "sum_small_grads"), rest_shapes)
    small_grads = {
        "ada_b": g_ada_b, "kv_ada_b": g_kv_ada_b.reshape(kv_ada_b.shape),
        "norm_g": _shard_of(g_norm_g_full, 2, me, norm_g.shape[2]),
        "gmlp_ln_g": _shard_of(g_ln_g_full, 1, me, gmlp_ln_g.shape[1]),
        "gmlp_ln_b": _shard_of(g_ln_b_full, 1, me, gmlp_ln_b.shape[1]),
        "gmlp_ws": g_ws, "gmlp_bs": g_bs, "kv_norm_g": g_kv_norm_g, "k_norm_g": g_k_norm_g,
        "w_f": _shard_of(g_w_f_full, 0, me, w_f.shape[0]), "b_f": g_b_f, "q_norm_g": g_q_norm_g,
    }
    small_names = list(small_grads)
    small_w_shapes = [weights[n].shape for n in small_names]
    d_pack, m_pack, v_pack = _adamw_flat(_pack([weights[n] for n in small_names]),
                                         _pack([small_grads[n] for n in small_names]),
                                         _pack([mom_m[n] for n in small_names]), _pack([mom_v[n] for n in small_names]),
                                         "adamw_small")
    grads.update(small_grads)
    deltas.update(zip(small_names, _unpack(d_pack, small_w_shapes)))
    new_m.update(zip(small_names, _unpack(m_pack, small_w_shapes)))
    new_v.update(zip(small_names, _unpack(v_pack, small_w_shapes)))

    return (loss, grad_x, *[grads[n] for n in WEIGHT_NAMES], *[deltas[n] for n in WEIGHT_NAMES],
            *[new_m[n] for n in WEIGHT_NAMES], *[new_v[n] for n in WEIGHT_NAMES])
```

```python
import functools
import math

import jax
import jax.numpy as jnp
from jax import lax
from jax.experimental import pallas as pl
from jax.experimental.pallas import tpu as pltpu

F32 = jnp.float32
BF16 = jnp.bfloat16
N_DEV = 8
EPS = 1e-6
CHUNK = 64
BLK = 128
N_MOD = 6
ADAM_LR = 0.001
ADAM_B1 = 0.9
ADAM_B2 = 0.999
ADAM_EPS = 1e-08
ADAM_WD = 0.01
ADAM_STEP = 10
VMEM_LIMIT_BYTES = 56 * 2 ** 20
NEG_BIG = -1e30
WEIGHT_NAMES = ['ada_w', 'ada_b', 'norm_g', 'mlp_w1', 'mlp_w2', 'gmlp_w_in', 'gmlp_ln_g', 'gmlp_ln_b', 'gmlp_ws',
                'gmlp_bs', 'gmlp_w_out', 'kv_norm_g', 'kv_ada_w', 'kv_ada_b', 'w_kv', 'k_norm_g', 'w_f', 'b_f',
                'attn_wq', 'q_norm_g', 'attn_wo']
MESH = pl.DeviceIdType.MESH


def _params(sem):
    return pltpu.CompilerParams(dimension_semantics=sem, vmem_limit_bytes=VMEM_LIMIT_BYTES)


def _tile(n, cap, unit=128):
    if n <= cap:
        return n
    t = (cap // unit) * unit
    while t > unit and n % t:
        t -= unit
    assert n % t == 0, (n, cap, unit)
    return t


def _my_index():
    return 4 * lax.axis_index("x") + 2 * lax.axis_index("y") + lax.axis_index("c")


def _all_to_all(x, name, bcast=False):
    slab = x.shape if bcast else x.shape[1:]

    def body(x_ref, o_ref, send_sems, recv_sems, local_sem):
        me = _my_index()

        def src(j):
            return x_ref if bcast else x_ref.at[j]

        mine = pltpu.make_async_copy(src(me), o_ref.at[me], local_sem)
        mine.start()
        sends = []
        for d in range(1, N_DEV):
            peer = (me + d) % N_DEV
            cp = pltpu.make_async_remote_copy(
                src_ref=src(peer), dst_ref=o_ref.at[me],
                send_sem=send_sems.at[d - 1], recv_sem=recv_sems.at[d - 1],
                device_id=(peer // 4, (peer // 2) % 2, peer % 2), device_id_type=MESH)
            cp.start()
            sends.append(cp)
        for d in range(1, N_DEV):
            frm = (me + N_DEV - d) % N_DEV
            pltpu.make_async_remote_copy(
                src_ref=src(frm), dst_ref=o_ref.at[frm],
                send_sem=send_sems.at[d - 1], recv_sem=recv_sems.at[d - 1],
                device_id=(frm // 4, (frm // 2) % 2, frm % 2), device_id_type=MESH).wait_recv()
        for cp in sends:
            cp.wait_send()
        mine.wait()

    return pl.pallas_call(
        body, name=name,
        out_shape=jax.ShapeDtypeStruct((N_DEV,) + tuple(slab), x.dtype),
        in_specs=[pl.BlockSpec(memory_space=pl.ANY)],
        out_specs=pl.BlockSpec(memory_space=pl.ANY),
        scratch_shapes=[pltpu.SemaphoreType.DMA((N_DEV - 1,)), pltpu.SemaphoreType.DMA((N_DEV - 1,)),
                        pltpu.SemaphoreType.DMA],
        compiler_params=pltpu.CompilerParams(has_side_effects=True),
    )(x)


_HBM = pl.BlockSpec(memory_space=pltpu.HBM)
_SEM = pl.BlockSpec(memory_space=pltpu.SEMAPHORE)
_ANY = pl.BlockSpec(memory_space=pl.ANY)
_DATAFLOW = pltpu.SideEffectType.DATAFLOW_SIDE_EFFECTING


def _a2a_peer_copy(x_ref, land_ref, send_sems, recv_sems, d, me, incoming, bcast):
    peer = (me + N_DEV - d) % N_DEV if incoming else (me + d) % N_DEV
    return pltpu.make_async_remote_copy(
        src_ref=x_ref if bcast else x_ref.at[peer], dst_ref=land_ref.at[peer if incoming else me],
        send_sem=send_sems.at[d - 1], recv_sem=recv_sems.at[d - 1],
        device_id=(peer // 4, (peer // 2) % 2, peer % 2), device_id_type=MESH)


def _a2a_start(x, name, bcast=False, deps=()):
    land_shape = ((N_DEV,) + tuple(x.shape)) if bcast else x.shape
    n_dep = len(deps)

    def body(x_ref, land_ref, *rest):
        send_sems, recv_sems, token = rest[n_dep], rest[n_dep + 1], rest[-1]
        me = _my_index()
        for d in range(1, N_DEV):
            _a2a_peer_copy(x_ref, land_ref, send_sems, recv_sems, d, me, False, bcast).start()
        token[...] = jnp.zeros_like(token)

    return pl.pallas_call(
        body, name=name,
        out_shape=(pltpu.SemaphoreType.DMA((N_DEV - 1,)), pltpu.SemaphoreType.DMA((N_DEV - 1,)),
                   pltpu.HBM(x.shape, x.dtype), pltpu.HBM(land_shape, x.dtype), jax.ShapeDtypeStruct((8, BLK), F32)),
        in_specs=(_HBM, _HBM, *[_ANY for _ in deps]),
        out_specs=(_SEM, _SEM, _HBM, _HBM, pl.BlockSpec(memory_space=pltpu.VMEM)),
        input_output_aliases={0: 2, 1: 3},
        compiler_params=pltpu.CompilerParams(has_side_effects=_DATAFLOW),
    )(pltpu.with_memory_space_constraint(x, pltpu.HBM),
      pltpu.with_memory_space_constraint(lax.empty(land_shape, x.dtype), pltpu.HBM), *deps)


def _a2a_wait(started, after, name, bcast=False):
    send_sems, recv_sems, x_thru, land_thru, _ = started

    def body(x_ref, land_ref, send_sems, recv_sems, after_ref, x_dead, land_out):
        me = _my_index()
        for d in range(1, N_DEV):
            _a2a_peer_copy(x_ref, land_ref, send_sems, recv_sems, d, me, False, bcast).wait_send()
        for d in range(1, N_DEV):
            _a2a_peer_copy(x_ref, land_ref, send_sems, recv_sems, d, me, True, bcast).wait_recv()

    return pl.pallas_call(
        body, name=name,
        out_shape=(pltpu.HBM(x_thru.shape, x_thru.dtype), pltpu.HBM(land_thru.shape, land_thru.dtype)),
        in_specs=(_HBM, _HBM, _SEM, _SEM, _ANY), out_specs=(_HBM, _HBM),
        input_output_aliases={0: 0, 1: 1},
        compiler_params=pltpu.CompilerParams(has_side_effects=_DATAFLOW),
    )(x_thru, land_thru, send_sems, recv_sems, after)


def _place_shard(w, layer, who, name, deps=()):
    _, R, C = w.shape
    tr = _row_tile(R, C)

    def body(who_ref, w_ref, *rest):
        rest[-1][...] = w_ref[...].astype(BF16)

    return pl.pallas_call(
        body, name=name,
        grid_spec=pltpu.PrefetchScalarGridSpec(
            num_scalar_prefetch=1, grid=(R // tr,),
            in_specs=[pl.BlockSpec((None, tr, C), lambda i, who_ref: (layer, i, 0))] + [_ANY for _ in deps],
            out_specs=pl.BlockSpec((None, None, tr, C), lambda i, who_ref: (who_ref[0], 0, i, 0))),
        out_shape=_sds((N_DEV, 1, R, C), BF16),
        compiler_params=_params(("parallel",)),
    )(who, w, *deps)


def _gather_copies(land_ref, send_sems, recv_sems, base, phase, incoming):
    cx, cy, cc = lax.axis_index("x"), lax.axis_index("y"), lax.axis_index("c")
    sibling = (cx, cy, 1 - cc)
    chips = [(1 - cx, cy), (cx, 1 - cy), (1 - cx, 1 - cy)]
    if phase == 1:
        out = [((cx, cy, cc), sibling)] + [((cx, cy, cc), (*chip, cc)) for chip in chips]
        inc = [(sibling, sibling)] + [((*chip, cc), (*chip, cc)) for chip in chips]
    else:
        out = [((*chip, cc), sibling) for chip in chips]
        inc = [((*chip, 1 - cc), sibling) for chip in chips]
    copies = []
    for k, (block, peer) in enumerate(inc if incoming else out):
        slab = land_ref.at[4 * block[0] + 2 * block[1] + block[2]]
        copies.append(pltpu.make_async_remote_copy(
            src_ref=slab, dst_ref=slab, send_sem=send_sems.at[base + k], recv_sem=recv_sems.at[base + k],
            device_id=peer, device_id_type=MESH))
    return copies


def _gather_start(lands, phase, name, deps=()):
    n, per, n_dep = len(lands), (4 if phase == 1 else 3), len(deps)

    def body(*refs):
        land_refs, send_sems, recv_sems, token = refs[:n], refs[n + n_dep], refs[n + n_dep + 1], refs[-1]
        for a, land_ref in enumerate(land_refs):
            for cp in _gather_copies(land_ref, send_sems, recv_sems, a * per, phase, False):
                cp.start()
        token[...] = jnp.zeros_like(token)

    outs = pl.pallas_call(
        body, name=name,
        out_shape=(pltpu.SemaphoreType.DMA((n * per,)), pltpu.SemaphoreType.DMA((n * per,)),
                   *[pltpu.HBM(x.shape, x.dtype) for x in lands], jax.ShapeDtypeStruct((8, BLK), F32)),
        in_specs=(*[_HBM for _ in lands], *[_ANY for _ in deps]),
        out_specs=(_SEM, _SEM, *[_HBM for _ in lands], pl.BlockSpec(memory_space=pltpu.VMEM)),
        input_output_aliases={a: 2 + a for a in range(n)},
        compiler_params=pltpu.CompilerParams(has_side_effects=_DATAFLOW),
    )(*[pltpu.with_memory_space_constraint(x, pltpu.HBM) for x in lands], *deps)
    return outs[0], outs[1], list(outs[2:2 + n]), outs[-1]


def _gather_wait(started, phase, after, name):
    send_sems, recv_sems, lands, _ = started
    n, per = len(lands), (4 if phase == 1 else 3)

    def body(*refs):
        land_refs, send_sems, recv_sems = refs[:n], refs[n], refs[n + 1]
        for a, land_ref in enumerate(land_refs):
            for cp in _gather_copies(land_ref, send_sems, recv_sems, a * per, phase, False):
                cp.wait_send()
            for cp in _gather_copies(land_ref, send_sems, recv_sems, a * per, phase, True):
                cp.wait_recv()

    outs = pl.pallas_call(
        body, name=name,
        out_shape=tuple(pltpu.HBM(x.shape, x.dtype) for x in lands),
        in_specs=(*[_HBM for _ in lands], _SEM, _SEM, _ANY), out_specs=tuple(_HBM for _ in lands),
        input_output_aliases={a: a for a in range(n)},
        compiler_params=pltpu.CompilerParams(has_side_effects=_DATAFLOW),
    )(*lands, send_sems, recv_sems, after)
    return list(outs)


def _mm(a, b, *, name, ta=False, tb=False, bmode="plain", layer=0, out_mode="plain", out_dtypes=(F32,),
        epilogue=None, extra=(), caps=(1024, 1024, 2048), deps=()):
    if ta:
        K, M = a.shape
    else:
        M, K = a.shape
    n_unit = k_unit = None
    if bmode == "plain":
        N, Kb = (b.shape if tb else b.shape[::-1])
    elif bmode == "col":
        _, _, Kw, Ns = b.shape
        if tb:
            N, Kb, k_unit = Kw, N_DEV * Ns, Ns
        else:
            N, Kb, n_unit = N_DEV * Ns, Kw, Ns
    else:
        _, _, Ks, Nw = b.shape
        if tb:
            N, Kb, n_unit = N_DEV * Ks, Nw, Ks
        else:
            N, Kb, k_unit = Nw, N_DEV * Ks, Ks
    assert K == Kb, (name, a.shape, b.shape)
    if out_mode == "col":
        assert n_unit is None
        n_unit = N // N_DEV
    tm = _tile(M, caps[0])
    tn = _tile(n_unit or N, caps[1])
    span = 1
    if k_unit and 2 * k_unit <= caps[2]:
        while 2 * span * k_unit <= caps[2] and N_DEV % (2 * span) == 0:
            span *= 2
        tk = span * k_unit
    else:
        tk = _tile(k_unit or K, caps[2])
    nk = K // tk
    npb = (n_unit // tn) if n_unit else None
    kpb = (k_unit // tk) if (k_unit and span == 1) else None
    grid = (M // tm, N // tn, nk)

    a_spec = pl.BlockSpec((tk, tm), lambda i, j, k: (k, i)) if ta else pl.BlockSpec((tm, tk), lambda i, j, k: (i, k))
    if bmode == "plain":
        b_spec = (pl.BlockSpec((tn, tk), lambda i, j, k: (j, k)) if tb
                  else pl.BlockSpec((tk, tn), lambda i, j, k: (k, j)))
    elif bmode == "col":
        if tb and span > 1:
            b_spec = pl.BlockSpec((span, None, tn, k_unit), lambda i, j, k: (k, layer, j, 0))
        elif tb:
            b_spec = pl.BlockSpec((None, None, tn, tk), lambda i, j, k: (k // kpb, layer, j, k % kpb))
        else:
            b_spec = pl.BlockSpec((None, None, tk, tn), lambda i, j, k: (j // npb, layer, k, j % npb))
    else:
        if tb:
            b_spec = pl.BlockSpec((None, None, tn, tk), lambda i, j, k: (j // npb, layer, j % npb, k))
        elif span > 1:
            b_spec = pl.BlockSpec((span, None, k_unit, tn), lambda i, j, k: (k, layer, 0, j))
        else:
            b_spec = pl.BlockSpec((None, None, tk, tn), lambda i, j, k: (k // kpb, layer, k % kpb, j))
    mn_spec = pl.BlockSpec((tm, tn), lambda i, j, k: (i, j))
    if out_mode == "col":
        o_specs = [pl.BlockSpec((None, tm, tn), lambda i, j, k: (j // npb, i, j % npb))]
        o_shapes = [jax.ShapeDtypeStruct((N_DEV, M, N // N_DEV), out_dtypes[0])]
    else:
        o_specs = [mn_spec for _ in out_dtypes]
        o_shapes = [jax.ShapeDtypeStruct((M, N), dt) for dt in out_dtypes]
    dims = (((0 if ta else 1,), (1 if tb else 0,)), ((), ()))
    n_extra, n_out, n_dep = len(extra), len(out_dtypes), len(deps)

    def body(a_ref, b_ref, *rest):
        extra_refs, out_refs = rest[:n_extra], rest[n_extra + n_dep:n_extra + n_dep + n_out]
        k = pl.program_id(2)

        def product():
            if span == 1:
                return lax.dot_general(a_ref[...].astype(BF16), b_ref[...].astype(BF16), dims,
                                       preferred_element_type=F32)
            if not tb:
                return lax.dot_general(a_ref[...].astype(BF16), b_ref[...].reshape(tk, tn).astype(BF16), dims,
                                       preferred_element_type=F32)
            out = None
            for s in range(span):
                part = lax.dot_general(a_ref[:, s * k_unit:(s + 1) * k_unit].astype(BF16), b_ref[s].astype(BF16),
                                       dims, preferred_element_type=F32)
                out = part if out is None else out + part
            return out

        def finish(acc):
            outs = (acc,) if epilogue is None else epilogue(acc, *[r[...] for r in extra_refs])
            for o_ref, val in zip(out_refs, outs):
                o_ref[...] = val.astype(o_ref.dtype)

        if nk == 1:
            finish(product())
            return
        acc_ref = rest[-1]

        @pl.when(k == 0)
        def _():
            acc_ref[...] = product()

        if nk > 2:
            @pl.when(jnp.logical_and(k > 0, k < nk - 1))
            def _():
                acc_ref[...] += product()

        @pl.when(k == nk - 1)
        def _():
            finish(acc_ref[...] + product())

    outs = pl.pallas_call(
        body, name=name, grid=grid,
        in_specs=[a_spec, b_spec] + [mn_spec for _ in extra] + [_ANY for _ in deps],
        out_specs=o_specs, out_shape=o_shapes,
        scratch_shapes=[pltpu.VMEM((tm, tn), F32)] if nk > 1 else [],
        compiler_params=_params(("parallel", "parallel", "arbitrary")),
    )(a, b, *extra, *deps)
    return outs[0] if n_out == 1 else outs


def _relu2_epilogue(acc):
    r = jnp.maximum(acc, 0.0)
    return acc, r * r


def _relu2_bwd_epilogue(acc, a_pre):
    return (acc * (2.0 * jnp.maximum(a_pre.astype(F32), 0.0)),)


def _rowcall(body, *, name, tr, row_ins, full_ins=(), row_outs=(), acc_outs=(), scratch=(), reverse=False):
    T = row_ins[0].shape[0]
    nb = T // tr
    rmap = (lambda i: (nb - 1 - i, 0)) if reverse else (lambda i: (i, 0))

    def full_spec(shape):
        nd = len(shape)
        return pl.BlockSpec(tuple(shape), lambda i: (0,) * nd)

    in_specs = [pl.BlockSpec((tr, a.shape[1]), rmap) for a in row_ins] + [full_spec(a.shape) for a in full_ins]
    out_specs = [pl.BlockSpec((tr, s.shape[1]), rmap) for s in row_outs] + [full_spec(s.shape) for s in acc_outs]
    outs = pl.pallas_call(
        body, name=name, grid=(nb,), in_specs=in_specs, out_specs=out_specs,
        out_shape=list(row_outs) + list(acc_outs), scratch_shapes=list(scratch),
        compiler_params=_params(("arbitrary",)),
    )(*row_ins, *full_ins)
    return outs


def _sds(shape, dtype):
    return jax.ShapeDtypeStruct(tuple(shape), dtype)


def _row_tile(T, C, elems=512 * 1024):
    t = max(8, min(T, elems // C))
    p = 8
    while p * 2 <= t and T % (p * 2) == 0:
        p *= 2
    return p


def _norm_mod(x, ng, sc, sh, name):
    T, D = x.shape

    def body(x_ref, ng_ref, sc_ref, sh_ref, h_ref):
        xv = x_ref[...]
        r = lax.rsqrt(jnp.mean(xv * xv, axis=-1, keepdims=True) + EPS)
        h_ref[...] = (((xv * r) * ng_ref[...]) * (1.0 + sc_ref[...]) + sh_ref[...]).astype(BF16)

    return _rowcall(body, name=name, tr=_row_tile(T, D), row_ins=[x], full_ins=[ng, sc, sh],
                    row_outs=[_sds((T, D), BF16)])[0]


def _res_norm_mod(x, y, gate, ng, sc, sh, name):
    T, D = x.shape

    def body(x_ref, y_ref, g_ref, ng_ref, sc_ref, sh_ref, x2_ref, h_ref):
        xv = x_ref[...] + g_ref[...] * y_ref[...]
        x2_ref[...] = xv
        r = lax.rsqrt(jnp.mean(xv * xv, axis=-1, keepdims=True) + EPS)
        h_ref[...] = (((xv * r) * ng_ref[...]) * (1.0 + sc_ref[...]) + sh_ref[...]).astype(BF16)

    return _rowcall(body, name=name, tr=_row_tile(T, D, 256 * 1024), row_ins=[x, y], full_ins=[gate, ng, sc, sh],
                    row_outs=[_sds((T, D), F32), _sds((T, D), BF16)])


def _res_loss(x, y, gate, target, name):
    T, D = x.shape

    def body(x_ref, y_ref, t_ref, g_ref, dout_ref, dy_ref, loss_ref, dg_ref):
        @pl.when(pl.program_id(0) == 0)
        def _():
            loss_ref[...] = jnp.zeros_like(loss_ref)
            dg_ref[...] = jnp.zeros_like(dg_ref)

        yv, gv = y_ref[...], g_ref[...]
        diff = x_ref[...] + gv * yv - t_ref[...]
        dout = diff * (1.0 / D)
        dout_ref[...] = dout
        dy_ref[...] = (dout * gv).astype(BF16)
        loss_ref[...] += jnp.sum(diff * diff) * (0.5 / D)
        dg_ref[...] += jnp.sum(dout * yv, axis=0, keepdims=True)

    return _rowcall(body, name=name, tr=_row_tile(T, D, 256 * 1024), row_ins=[x, y, target], full_ins=[gate],
                    row_outs=[_sds((T, D), F32), _sds((T, D), BF16)],
                    acc_outs=[_sds((1, BLK), F32), _sds((1, D), F32)])


def _norm_mod_bwd(x, dhs, dres, ng, sc, name, gated=None):
    T, D = x.shape
    n_dh = len(dhs)
    n_row = 2 + n_dh + (1 if gated else 0)

    def body(*refs):
        x_ref, dh_refs, dres_ref = refs[0], refs[1:1 + n_dh], refs[1 + n_dh]
        ng_ref, sc_ref = refs[n_row:n_row + 2]
        outs = refs[n_row + (3 if gated else 2):]
        dx_ref, sums_ref = (outs[0], outs[2]) if gated else (outs[0], outs[1])

        @pl.when(pl.program_id(0) == 0)
        def _():
            sums_ref[...] = jnp.zeros_like(sums_ref)
            if gated:
                outs[3][...] = jnp.zeros_like(outs[3])

        xv = x_ref[...]
        dh = dh_refs[0][...].astype(F32)
        for r_ in dh_refs[1:]:
            dh = dh + r_[...].astype(F32)
        r = lax.rsqrt(jnp.mean(xv * xv, axis=-1, keepdims=True) + EPS)
        n = xv * r
        ngv, scale1 = ng_ref[...], 1.0 + sc_ref[...]
        dn = dh * (ngv * scale1)
        dx = dres_ref[...] + r * (dn - n * jnp.mean(dn * n, axis=-1, keepdims=True))
        dx_ref[...] = dx
        dhn = dh * n
        sums_ref[0:1, :] += jnp.sum(dh, axis=0, keepdims=True)
        sums_ref[1:2, :] += jnp.sum(dhn * ngv, axis=0, keepdims=True)
        sums_ref[2:3, :] += jnp.sum(dhn * scale1, axis=0, keepdims=True)
        if gated:
            y_ref, g_ref = refs[2 + n_dh], refs[n_row + 2]
            outs[1][...] = (dx * g_ref[...]).astype(BF16)
            outs[3][...] += jnp.sum(dx * y_ref[...], axis=0, keepdims=True)

    return _rowcall(body, name=name, tr=_row_tile(T, D, 256 * 1024),
                    row_ins=[x, *dhs, dres] + ([gated[0]] if gated else []),
                    full_ins=[ng, sc] + ([gated[1]] if gated else []),
                    row_outs=[_sds((T, D), F32)] + ([_sds((T, D), BF16)] if gated else []),
                    acc_outs=[_sds((8, D), F32)] + ([_sds((1, D), F32)] if gated else []))


def _head_norm(x, g, n_heads, name, tail=False):
    T = x.shape[0]
    D = n_heads * BLK
    W = x.shape[1] if tail else D

    def body(x_ref, g_ref, o_ref, *tail_ref):
        for h in range(n_heads):
            xv = x_ref[:, h * BLK:(h + 1) * BLK]
            r = lax.rsqrt(jnp.mean(xv * xv, axis=-1, keepdims=True) + EPS)
            o_ref[:, h * BLK:(h + 1) * BLK] = ((xv * r) * g_ref[...]).astype(BF16)
        if tail:
            tail_ref[0][...] = x_ref[:, D:2 * D].astype(BF16)

    tr = _row_tile(T, x.shape[1])
    o_spec = pl.BlockSpec((tr, D), lambda i: (i, 0))
    return pl.pallas_call(
        body, name=name, grid=(T // tr,),
        in_specs=[pl.BlockSpec((tr, W), lambda i: (i, 0)), pl.BlockSpec((1, BLK), lambda i: (0, 0))],
        out_specs=[o_spec] * (2 if tail else 1), out_shape=[_sds((T, D), BF16)] * (2 if tail else 1),
        compiler_params=_params(("parallel",)),
    )(x, g)


def _head_norm_bwd(x, dys, g, n_heads, name, tails=()):
    T = x.shape[0]
    D = n_heads * BLK
    n_dy, n_tail = len(dys), len(tails)
    W = 2 * D if tails else D

    def body(*refs):
        x_ref, dy_refs, tail_refs = refs[0], refs[1:1 + n_dy], refs[1 + n_dy:1 + n_dy + n_tail]
        g_ref, dx_ref, dg_ref = refs[1 + n_dy + n_tail:]

        @pl.when(pl.program_id(0) == 0)
        def _():
            dg_ref[...] = jnp.zeros_like(dg_ref)

        tot = jnp.zeros((1, BLK), F32)
        for h in range(n_heads):
            cols = slice(h * BLK, (h + 1) * BLK)
            xv = x_ref[:, cols]
            dyv = dy_refs[0][:, cols]
            for r_ in dy_refs[1:]:
                dyv = dyv + r_[:, cols]
            r = lax.rsqrt(jnp.mean(xv * xv, axis=-1, keepdims=True) + EPS)
            n = xv * r
            dn = dyv * g_ref[...]
            dx_ref[:, cols] = (r * (dn - n * jnp.mean(dn * n, axis=-1, keepdims=True))).astype(BF16)
            tot = tot + jnp.sum(dyv * n, axis=0, keepdims=True)
        dg_ref[0:1, :] += tot
        if n_tail:
            tv = tail_refs[0][...]
            for r_ in tail_refs[1:]:
                tv = tv + r_[...]
            dx_ref[:, D:] = tv.astype(BF16)

    tr = _row_tile(T, 2 * D, 256 * 1024)
    d_spec = pl.BlockSpec((tr, D), lambda i: (i, 0))
    return pl.pallas_call(
        body, name=name, grid=(T // tr,),
        in_specs=[d_spec] * (1 + n_dy + n_tail) + [pl.BlockSpec((1, BLK), lambda i: (0, 0))],
        out_specs=[pl.BlockSpec((tr, W), lambda i: (i, 0)), pl.BlockSpec((8, BLK), lambda i: (0, 0))],
        out_shape=[_sds((T, W), BF16), _sds((8, BLK), F32)],
        compiler_params=_params(("arbitrary",)),
    )(x, *dys, *tails, g)


def _fcum_fwd(fl, bf, name):
    T = fl.shape[0]

    def body(fl_ref, b_ref, o_ref, carry_ref):
        @pl.when(pl.program_id(0) == 0)
        def _():
            carry_ref[...] = jnp.zeros_like(carry_ref)

        z = fl_ref[...] + b_ref[...]
        logf = jnp.minimum(z, 0.0) - jnp.log(1.0 + jnp.exp(-jnp.abs(z)))
        row = lax.broadcasted_iota(jnp.int32, (BLK, BLK), 0)
        col = lax.broadcasted_iota(jnp.int32, (BLK, BLK), 1)
        tri = (col <= row).astype(F32)
        run = jnp.dot(tri, logf, preferred_element_type=F32, precision=lax.Precision.HIGHEST) + carry_ref[0:1, :]
        o_ref[...] = run
        carry_ref[0:1, :] = run[BLK - 1:BLK, :]

    return _rowcall(body, name=name, tr=BLK, row_ins=[fl], full_ins=[bf], row_outs=[_sds((T, BLK), F32)],
                    scratch=[pltpu.VMEM((8, BLK), F32)])[0]


def _fcum_bwd(dfs, fl, bf, name):
    T = fl.shape[0]
    n_df = len(dfs)

    def body(*refs):
        df_refs = refs[:n_df]
        fl_ref, b_ref, dfl_ref, dbias_ref, carry_ref = refs[n_df:]

        @pl.when(pl.program_id(0) == 0)
        def _():
            carry_ref[...] = jnp.zeros_like(carry_ref)
            dbias_ref[...] = jnp.zeros_like(dbias_ref)

        dfc = df_refs[0][...]
        for r_ in df_refs[1:]:
            dfc = dfc + r_[...]
        row = lax.broadcasted_iota(jnp.int32, (BLK, BLK), 0)
        col = lax.broadcasted_iota(jnp.int32, (BLK, BLK), 1)
        tri = (col >= row).astype(F32)
        suffix = jnp.dot(tri, dfc, preferred_element_type=F32, precision=lax.Precision.HIGHEST) + carry_ref[0:1, :]
        carry_ref[0:1, :] = suffix[0:1, :]
        z = fl_ref[...] + b_ref[...]
        dfl = suffix / (1.0 + jnp.exp(z))
        dfl_ref[...] = dfl.astype(BF16)
        dbias_ref[0:1, :] += jnp.sum(dfl, axis=0, keepdims=True)

    return _rowcall(body, name=name, tr=BLK, row_ins=[*dfs, fl], full_ins=[bf], reverse=True,
                    row_outs=[_sds((T, BLK), BF16)], acc_outs=[_sds((8, BLK), F32)],
                    scratch=[pltpu.VMEM((8, BLK), F32)])


def _attn_tile(T):
    return min(T, 512)


def _attn_fwd(q, k, v, fk, n_heads, name):
    T = q.shape[0]
    tq = tk = _attn_tile(T)
    nkb = T // tk
    inv_sqrt = 1.0 / float(math.sqrt(BLK))

    def body(q_ref, k_ref, v_ref, fk_ref, o_ref, o32_ref, lse_ref):
        i = pl.program_id(1)
        qv = q_ref[...]

        def block(j, carry, diagonal):
            m, l, acc = carry
            rows = pl.ds(pl.multiple_of(j * tk, tk), tk)
            kj, vj = k_ref[rows, :], v_ref[rows, :]
            s = lax.dot_general(qv, kj, (((1,), (1,)), ((), ())), preferred_element_type=F32) * inv_sqrt
            s = s - fk_ref[j]
            if diagonal:
                s = jnp.where(lax.broadcasted_iota(jnp.int32, (tq, tk), 1)
                              <= lax.broadcasted_iota(jnp.int32, (tq, tk), 0), s, NEG_BIG)
            m_new = jnp.maximum(m, jnp.max(s, axis=-1, keepdims=True))
            alpha = jnp.exp(m - m_new)
            p = jnp.exp(s - m_new)
            l = alpha * l + jnp.sum(p, axis=-1, keepdims=True)
            acc = alpha * acc + jnp.dot(p.astype(BF16), vj, preferred_element_type=F32)
            return m_new, l, acc

        init = (jnp.full((tq, 1), NEG_BIG, F32), jnp.zeros((tq, 1), F32), jnp.zeros((tq, BLK), F32))
        carry = lax.fori_loop(0, i, lambda j, c: block(j, c, False), init)
        m, l, acc = block(i, carry, True)
        out = acc / l
        o_ref[...] = out.astype(BF16)
        o32_ref[...] = out
        lse_ref[...] = m + jnp.log(l)

    return pl.pallas_call(
        body, name=name, grid=(n_heads, T // tq),
        in_specs=[pl.BlockSpec((tq, BLK), lambda h, i: (i, h)),
                  pl.BlockSpec((T, BLK), lambda h, i: (0, h)),
                  pl.BlockSpec((T, BLK), lambda h, i: (0, h)),
                  pl.BlockSpec((None, nkb, 1, tk), lambda h, i: (h, 0, 0, 0))],
        out_specs=[pl.BlockSpec((tq, BLK), lambda h, i: (i, h)),
                   pl.BlockSpec((tq, BLK), lambda h, i: (i, h)),
                   pl.BlockSpec((None, tq, 1), lambda h, i: (h, i, 0))],
        out_shape=[_sds((T, n_heads * BLK), BF16), _sds((T, n_heads * BLK), F32), _sds((n_heads, T, 1), F32)],
        compiler_params=_params(("parallel", "arbitrary")),
    )(q, k, v, fk)


def _attn_bwd(q, k, v, o, do, lse, fk, n_heads, name):
    T = q.shape[0]
    tq = tk = _attn_tile(T)
    nkb = T // tk
    nq = T // tq
    inv_sqrt = 1.0 / float(math.sqrt(BLK))
    tn_dims = (((0,), (0,)), ((), ()))
    nt_dims = (((1,), (1,)), ((), ()))

    def body(q_ref, k_ref, v_ref, o_ref, do_ref, lse_ref, fk_ref, dq_ref, dk_ref, dv_ref, dfq_ref, dfk_ref, delta_ref):
        j = pl.program_id(1)

        @pl.when(j == 0)
        def _():
            delta_ref[...] = jnp.sum(do_ref[...].astype(F32) * o_ref[...], axis=1, keepdims=True)
            dq_ref[...] = jnp.zeros_like(dq_ref)
            dfq_ref[...] = jnp.zeros_like(dfq_ref)

        kj, vj, fkv = k_ref[...], v_ref[...], fk_ref[...]

        def step(i, carry, diagonal):
            dk, dv, dfk = carry
            rows = pl.ds(pl.multiple_of(i * tq, tq), tq)
            qi, doi = q_ref[rows, :], do_ref[rows, :]
            s = lax.dot_general(qi, kj, nt_dims, preferred_element_type=F32) * inv_sqrt - fkv
            if diagonal:
                s = jnp.where(lax.broadcasted_iota(jnp.int32, (tq, tk), 1)
                              <= lax.broadcasted_iota(jnp.int32, (tq, tk), 0), s, NEG_BIG)
            p = jnp.exp(s - lse_ref[rows, :])
            dv = dv + lax.dot_general(p.astype(BF16), doi, tn_dims, preferred_element_type=F32)
            dp = lax.dot_general(doi, vj, nt_dims, preferred_element_type=F32)
            ds = p * (dp - delta_ref[rows, :])
            dsb = ds.astype(BF16)
            dq_ref[rows, :] += jnp.dot(dsb, kj, preferred_element_type=F32) * inv_sqrt
            dk = dk + lax.dot_general(dsb, qi, tn_dims, preferred_element_type=F32)
            dfq_ref[rows, :] += jnp.sum(ds, axis=1, keepdims=True)
            dfk = dfk - jnp.sum(ds, axis=0, keepdims=True)
            return dk, dv, dfk

        init = (jnp.zeros((tk, BLK), F32), jnp.zeros((tk, BLK), F32), jnp.zeros((1, tk), F32))
        carry = step(j, init, True)
        dk, dv, dfk = lax.fori_loop(j + 1, nq, lambda i, c: step(i, c, False), carry)
        dk_ref[...] = dk * inv_sqrt
        dv_ref[...] = dv
        dfk_ref[...] = dfk

    head_col = lambda h, j: (0, h)
    return pl.pallas_call(
        body, name=name, grid=(n_heads, nkb),
        in_specs=[pl.BlockSpec((T, BLK), head_col),
                  pl.BlockSpec((tk, BLK), lambda h, j: (j, h)),
                  pl.BlockSpec((tk, BLK), lambda h, j: (j, h)),
                  pl.BlockSpec((T, BLK), head_col),
                  pl.BlockSpec((T, BLK), head_col),
                  pl.BlockSpec((None, T, 1), lambda h, j: (h, 0, 0)),
                  pl.BlockSpec((None, None, 1, tk), lambda h, j: (h, j, 0, 0))],
        out_specs=[pl.BlockSpec((T, BLK), head_col),
                   pl.BlockSpec((tk, BLK), lambda h, j: (j, h)),
                   pl.BlockSpec((tk, BLK), lambda h, j: (j, h)),
                   pl.BlockSpec((None, T, 1), lambda h, j: (h, 0, 0)),
                   pl.BlockSpec((None, None, 1, tk), lambda h, j: (h, j, 0, 0))],
        out_shape=[_sds((T, n_heads * BLK), F32), _sds((T, n_heads * BLK), F32), _sds((T, n_heads * BLK), F32),
                   _sds((n_heads, T, 1), F32), _sds((n_heads, nkb, 1, tk), F32)],
        scratch_shapes=[pltpu.VMEM((T, 1), F32)],
        compiler_params=_params(("parallel", "arbitrary")),
    )(q, k, v, o, do, lse, fk)


_INV_SQRT2 = 1.0 / math.sqrt(2.0)
_INV_SQRT_2PI = 1.0 / math.sqrt(2.0 * math.pi)


def _gelu_parts(z):
    cdf = 0.5 * (1.0 + lax.erf(z * _INV_SQRT2))
    return cdf, z * cdf


def _mix_mask(transposed):
    row = lax.broadcasted_iota(jnp.int32, (BLK, BLK), 0) // CHUNK
    col = lax.broadcasted_iota(jnp.int32, (BLK, BLK), 1) // CHUNK
    return (row <= col) if transposed else (col <= row)


def _gmlp_mid_fwd(zpre, ln_g, ln_b, ws, bs_t, name):
    T, two_h = zpre.shape
    Hh = two_h // 2
    G = ws.shape[0]
    gd = Hh // G

    def body(z_ref, lg_ref, lb_ref, ws_ref, bs_ref, p_ref):
        _, zg = _gelu_parts(z_ref[...].astype(F32))
        u, v = zg[:, :Hh], zg[:, Hh:]
        mu = jnp.mean(v, axis=-1, keepdims=True)
        vc = v - mu
        rstd = lax.rsqrt(jnp.mean(vc * vc, axis=-1, keepdims=True) + EPS)
        vn = ((vc * rstd) * lg_ref[...] + lb_ref[...]).astype(BF16)
        mask = _mix_mask(False)
        for g in range(G):
            wm = jnp.where(mask, ws_ref[g], 0.0).astype(BF16)
            sv = jnp.dot(wm, vn[:, g * gd:(g + 1) * gd], preferred_element_type=F32) + bs_ref[:, g:g + 1]
            p_ref[:, g * gd:(g + 1) * gd] = (u[:, g * gd:(g + 1) * gd] * sv).astype(BF16)

    return _rowcall(body, name=name, tr=BLK, row_ins=[zpre], full_ins=[ln_g, ln_b, ws, bs_t],
                    row_outs=[_sds((T, Hh), BF16)])[0]


def _gmlp_mid_bwd(zpre, dp, ln_g, ln_b, ws, ws_t, bs_t, name):
    T, two_h = zpre.shape
    Hh = two_h // 2
    G = ws.shape[0]
    gd = Hh // G
    nt_dims = (((1,), (1,)), ((), ()))

    def body(z_ref, dp_ref, lg_ref, lb_ref, ws_ref, wst_ref, bs_ref, dz_ref, dws_ref, dbs_ref, dlg_ref, dlb_ref,
             dvn_ref):
        @pl.when(pl.program_id(0) == 0)
        def _():
            dws_ref[...] = jnp.zeros_like(dws_ref)
            dbs_ref[...] = jnp.zeros_like(dbs_ref)
            dlg_ref[...] = jnp.zeros_like(dlg_ref)
            dlb_ref[...] = jnp.zeros_like(dlb_ref)

        z = z_ref[...].astype(F32)
        cdf, zg = _gelu_parts(z)
        dgelu = cdf + z * (jnp.exp(-0.5 * z * z) * _INV_SQRT_2PI)
        u, v = zg[:, :Hh], zg[:, Hh:]
        mu = jnp.mean(v, axis=-1, keepdims=True)
        vc = v - mu
        rstd = lax.rsqrt(jnp.mean(vc * vc, axis=-1, keepdims=True) + EPS)
        vhat = vc * rstd
        vn = (vhat * lg_ref[...] + lb_ref[...]).astype(BF16)
        mask, mask_t = _mix_mask(False), _mix_mask(True)
        lane = lax.broadcasted_iota(jnp.int32, (BLK, BLK), 1)
        dbs = jnp.zeros((BLK, BLK), F32)
        for g in range(G):
            cols = slice(g * gd, (g + 1) * gd)
            wm = jnp.where(mask, ws_ref[g], 0.0).astype(BF16)
            wm_t = jnp.where(mask_t, wst_ref[g], 0.0).astype(BF16)
            vn_g = vn[:, cols]
            sv = jnp.dot(wm, vn_g, preferred_element_type=F32) + bs_ref[:, g:g + 1]
            dp_g = dp_ref[:, cols].astype(F32)
            dz_ref[:, cols] = ((dp_g * sv) * dgelu[:, cols]).astype(BF16)
            dsv = dp_g * u[:, cols]
            dsv_b = dsv.astype(BF16)
            dbs = dbs + jnp.where(lane == g, jnp.sum(dsv, axis=1, keepdims=True), 0.0)
            dws_ref[g] += jnp.where(mask, lax.dot_general(dsv_b, vn_g, nt_dims, preferred_element_type=F32), 0.0)
            dvn_ref[:, cols] = jnp.dot(wm_t, dsv_b, preferred_element_type=F32)
        dbs_ref[...] += dbs
        dvn = dvn_ref[...]
        dlg_ref[0:1, :] += jnp.sum(dvn * vhat, axis=0, keepdims=True)
        dlb_ref[0:1, :] += jnp.sum(dvn, axis=0, keepdims=True)
        dvh = dvn * lg_ref[...]
        dv = rstd * (dvh - jnp.mean(dvh, axis=-1, keepdims=True) - vhat * jnp.mean(dvh * vhat, axis=-1, keepdims=True))
        dz_ref[:, Hh:] = (dv * dgelu[:, Hh:]).astype(BF16)

    return _rowcall(body, name=name, tr=BLK, row_ins=[zpre, dp], full_ins=[ln_g, ln_b, ws, ws_t, bs_t],
                    row_outs=[_sds((T, two_h), BF16)],
                    acc_outs=[_sds((G, BLK, BLK), F32), _sds((BLK, BLK), F32), _sds((8, Hh), F32), _sds((8, Hh), F32)],
                    scratch=[pltpu.VMEM((BLK, Hh), F32)])


def _mods(c_all, w, layer, bias, name):
    nb, K = c_all.shape
    N = w.shape[-1]
    tn = _tile(N, 512)

    def body(c_ref, w_ref, b_ref, o_ref):
        cv = c_ref[...]
        sc = cv / (1.0 + jnp.exp(-cv))
        o_ref[...] = jnp.dot(sc, w_ref[...], preferred_element_type=F32, precision=lax.Precision.HIGHEST) + b_ref[...]

    return pl.pallas_call(
        body, name=name, grid=(N // tn,),
        in_specs=[pl.BlockSpec((nb, K), lambda j: (0, 0)),
                  pl.BlockSpec((None, K, tn), lambda j: (layer, 0, j)),
                  pl.BlockSpec((1, tn), lambda j: (0, j))],
        out_specs=pl.BlockSpec((nb, tn), lambda j: (0, j)), out_shape=_sds((nb, N), F32),
        compiler_params=_params(("parallel",)),
    )(c_all, w, bias)


def _sum_slabs(landed, own, who, name):
    _, R, C = landed.shape
    tr = _row_tile(R, C * N_DEV)

    def body(who_ref, x_ref, own_ref, o_ref):
        me, mine = who_ref[0], own_ref[...]
        acc = jnp.where(me == 0, mine, x_ref[0])
        for s in range(1, N_DEV):
            acc = acc + jnp.where(me == s, mine, x_ref[s])
        o_ref[...] = acc

    return pl.pallas_call(
        body, name=name,
        grid_spec=pltpu.PrefetchScalarGridSpec(
            num_scalar_prefetch=1, grid=(R // tr,),
            in_specs=[pl.BlockSpec((N_DEV, tr, C), lambda i, who_ref: (0, i, 0)),
                      pl.BlockSpec((tr, C), lambda i, who_ref: (i, 0))],
            out_specs=pl.BlockSpec((tr, C), lambda i, who_ref: (i, 0))),
        out_shape=_sds((R, C), F32),
        compiler_params=_params(("parallel",)),
    )(who, landed, own)


def _adamw_math(w, g, m, v):
    m = ADAM_B1 * m + (1.0 - ADAM_B1) * g
    v = ADAM_B2 * v + (1.0 - ADAM_B2) * (g * g)
    m_hat = m / (1.0 - ADAM_B1 ** ADAM_STEP)
    v_hat = v / (1.0 - ADAM_B2 ** ADAM_STEP)
    delta = -ADAM_LR * (m_hat / (jnp.sqrt(v_hat) + ADAM_EPS) + ADAM_WD * w)
    return delta, m, v


class _AdamStack:
    def __init__(self, w, m, v, name):
        self.w, self.m, self.v, self.name = w, m, v, name
        self.L, self.R, self.C = w.shape
        self.tr = _row_tile(self.R, self.C, 256 * 1024)
        self.outs = None

    def _layer(self, l, who, srcs, src_specs, make_grad):
        n_src = len(srcs)
        L, R, C, tr = self.L, self.R, self.C, self.tr
        wspec = pl.BlockSpec((None, tr, C), lambda i, who_ref: (l, i, 0))

        def body(who_ref, *refs):
            src_refs = refs[:n_src]
            w_ref, m_ref, v_ref = refs[n_src:n_src + 3]
            g_ref, d_ref, m2_ref, v2_ref = refs[-4:]
            g = make_grad(who_ref[0], *src_refs)
            delta, m2, v2 = _adamw_math(w_ref[...], g, m_ref[...], v_ref[...])
            g_ref[...] = g
            d_ref[...] = delta
            m2_ref[...] = m2
            v2_ref[...] = v2

        prev = [] if self.outs is None else list(self.outs)
        aliases = {} if self.outs is None else {1 + n_src + 3 + t: t for t in range(4)}
        self.outs = pl.pallas_call(
            body, name=f"{self.name}_l{l}",
            grid_spec=pltpu.PrefetchScalarGridSpec(
                num_scalar_prefetch=1, grid=(R // tr,),
                in_specs=list(src_specs) + [wspec] * 3 + [_ANY] * len(prev), out_specs=[wspec] * 4),
            out_shape=[_sds((L, R, C), F32)] * 4,
            input_output_aliases=aliases,
            compiler_params=_params(("parallel",)),
        )(who, *srcs, self.w, self.m, self.v, *prev)

    def from_parts(self, l, who, landed, sent):
        tr, C = self.tr, self.C

        def make_grad(me, p_ref, own_ref):
            mine = own_ref[...].astype(F32)
            g = jnp.where(me == 0, mine, p_ref[0].astype(F32))
            for s in range(1, N_DEV):
                g = g + jnp.where(me == s, mine, p_ref[s].astype(F32))
            return g

        self._layer(l, who, [landed, sent],
                    [pl.BlockSpec((N_DEV, tr, C), lambda i, who_ref: (0, i, 0)),
                     pl.BlockSpec((None, tr, C), lambda i, who_ref: (who_ref[0], i, 0))], make_grad)

    def from_outer(self, l, who, sc_t, dmod):
        tr, C = self.tr, self.C

        def make_grad(me, s_ref, d_ref):
            g = s_ref[:, 0:1] * d_ref[0:1, :]
            for b in range(1, N_DEV):
                g = g + s_ref[:, b:b + 1] * d_ref[b:b + 1, :]
            return g

        self._layer(l, who, [sc_t, dmod], [pl.BlockSpec((tr, N_DEV), lambda i, who_ref: (i, 0)),
                                           pl.BlockSpec((N_DEV, C), lambda i, who_ref: (0, 0))], make_grad)


def _adamw_flat(w, g, m, v, name):
    R, C = w.shape
    tr = _row_tile(R, C, 128 * 1024)

    def body(w_ref, g_ref, m_ref, v_ref, d_ref, m2_ref, v2_ref):
        delta, m2, v2 = _adamw_math(w_ref[...], g_ref[...], m_ref[...], v_ref[...])
        d_ref[...] = delta
        m2_ref[...] = m2
        v2_ref[...] = v2

    spec = pl.BlockSpec((tr, C), lambda i: (i, 0))
    return pl.pallas_call(
        body, name=name, grid=(R // tr,), in_specs=[spec] * 4, out_specs=[spec] * 3,
        out_shape=[_sds((R, C), F32)] * 3, compiler_params=_params(("parallel",)),
    )(w, g, m, v)


def _pack(arrays):
    flat = jnp.concatenate([a.reshape(-1).astype(F32) for a in arrays])
    pad = (-flat.shape[0]) % (64 * BLK)
    if pad:
        flat = jnp.concatenate([flat, jnp.zeros((pad,), F32)])
    return flat.reshape(-1, BLK)


def _unpack(buf, shapes, lead=()):
    sizes = [int(math.prod(s)) for s in shapes]
    out, off = [], 0
    if all(n % BLK == 0 for n in sizes):
        for s, n in zip(shapes, sizes):
            out.append(buf[..., off // BLK:(off + n) // BLK, :].reshape(tuple(lead) + tuple(s)))
            off += n
        return out
    flat = buf.reshape(tuple(lead) + (-1,))
    for s, n in zip(shapes, sizes):
        out.append(flat[..., off:off + n].reshape(tuple(lead) + tuple(s)))
        off += n
    return out


def _row(vec):
    return vec.reshape(1, -1)


def _shard_of(full, axis, me, size):
    return lax.dynamic_slice_in_dim(full, me * size, size, axis=axis)


def kernel(x, c, ada_w, ada_b, norm_g, mlp_w1, mlp_w2, gmlp_w_in, gmlp_ln_g, gmlp_ln_b, gmlp_ws, gmlp_bs, gmlp_w_out, kv_norm_g, kv_ada_w, kv_ada_b, w_kv, k_norm_g, w_f, b_f, attn_wq, q_norm_g, attn_wo, loss_target, m_ada_w, m_ada_b, m_norm_g, m_mlp_w1, m_mlp_w2, m_gmlp_w_in, m_gmlp_ln_g, m_gmlp_ln_b, m_gmlp_ws, m_gmlp_bs, m_gmlp_w_out, m_kv_norm_g, m_kv_ada_w, m_kv_ada_b, m_w_kv, m_k_norm_g, m_w_f, m_b_f, m_attn_wq, m_q_norm_g, m_attn_wo, v_ada_w, v_ada_b, v_norm_g, v_mlp_w1, v_mlp_w2, v_gmlp_w_in, v_gmlp_ln_g, v_gmlp_ln_b, v_gmlp_ws, v_gmlp_bs, v_gmlp_w_out, v_kv_norm_g, v_kv_ada_w, v_kv_ada_b, v_w_kv, v_k_norm_g, v_w_f, v_b_f, v_attn_wq, v_q_norm_g, v_attn_wo):
    given = dict(locals())
    weights = {n: given[n] for n in WEIGHT_NAMES}
    mom_m = {n: given["m_" + n] for n in WEIGHT_NAMES}
    mom_v = {n: given["v_" + n] for n in WEIGHT_NAMES}

    me = _my_index()
    T, D = x.shape[1], x.shape[2]
    depth = ada_w.shape[0]
    n_a = gmlp_w_in.shape[0]
    n_heads = b_f.shape[0]
    G = gmlp_ws.shape[1]
    Hh = gmlp_ln_g.shape[1] * N_DEV
    mod_cols = ada_w.shape[2]
    kv_cols = kv_ada_w.shape[1]
    x0 = x.reshape(T, D)
    target = loss_target.reshape(T, D)

    small_in = [c, norm_g, gmlp_ln_g, gmlp_ln_b, w_f]
    small_shapes = [a.shape for a in small_in]
    got = _all_to_all(_pack(small_in), "gather_small_inputs", bcast=True)
    c_all, norm_g_sh, ln_g_sh, ln_b_sh, w_f_sh = _unpack(got, small_shapes, lead=(N_DEV,))
    c_all = c_all.reshape(N_DEV, D)
    norm_g_full = jnp.moveaxis(norm_g_sh, 0, 2).reshape(depth, 2, D)
    ln_g_full = jnp.moveaxis(ln_g_sh, 0, 1).reshape(n_a, Hh)
    ln_b_full = jnp.moveaxis(ln_b_sh, 0, 1).reshape(n_a, Hh)
    w_f_full = w_f_sh.reshape(D, n_heads)
    w_f_pad = jnp.pad(w_f_full, ((0, 0), (0, BLK - n_heads))).astype(BF16)
    b_f_pad = jnp.pad(b_f, (0, BLK - n_heads)).reshape(1, BLK)

    mod_parts = []
    for l in range(depth):
        bias = _shard_of(ada_b[l], 0, me, mod_cols).reshape(1, mod_cols)
        mod_parts.append(_mods(c_all, ada_w, l, bias, f"mods_l{l}"))
    kv_bias = _shard_of(kv_ada_b, 0, me, kv_cols).reshape(1, kv_cols)
    mod_parts.append(_mods(c_all, kv_ada_w.reshape(1, D, kv_cols), 0, kv_bias, "mods_kv"))
    mods_mine = jnp.concatenate(mod_parts, axis=1)
    mod_width = mods_mine.shape[1]
    mods_pack = jnp.pad(mods_mine, ((0, 0), (0, (-mod_width) % (8 * BLK)))).reshape(N_DEV, -1, BLK)
    mods_landed = _all_to_all(mods_pack, "exchange_mods")
    mods_got = mods_landed.reshape(N_DEV, -1)[:, :mod_width]
    mods = []
    for l in range(depth):
        mods.append(mods_got[:, l * mod_cols:(l + 1) * mod_cols].reshape(N_MOD, D))
    kv_mod = mods_got[:, depth * mod_cols:].reshape(2, D)
    silu_all = c_all / (1.0 + jnp.exp(-c_all))

    assert 1 <= n_a < depth
    who = me.astype(jnp.int32).reshape(1)
    big = {"mlp_w1": mlp_w1, "mlp_w2": mlp_w2, "gmlp_w_in": gmlp_w_in, "gmlp_w_out": gmlp_w_out,
           "w_kv": w_kv.reshape((1,) + w_kv.shape), "attn_wq": attn_wq, "attn_wo": attn_wo}
    groups = [[("gmlp_w_in", 0)], [("gmlp_w_out", 0), ("mlp_w1", 0), ("mlp_w2", 0)]]
    for l in range(1, depth):
        if l < n_a:
            groups.append([("gmlp_w_in", l), ("gmlp_w_out", l), ("mlp_w1", l), ("mlp_w2", l)])
        else:
            first = [("w_kv", 0)] if l == n_a else []
            groups.append(first + [("attn_wq", l - n_a), ("attn_wo", l - n_a), ("mlp_w1", l), ("mlp_w2", l)])
    tokens = []

    def behind_starts():
        out = tuple(tokens)
        tokens.clear()
        return out

    over_ici, placed = {}, {}

    def gather_begin(gi, deps=()):
        over_ici[gi] = _gather_start(placed[gi], 1, f"gather_g{gi}_ici_start", deps=deps)
        tokens.append(over_ici[gi][3])

    for gi, grp in enumerate(groups):
        behind = (mods_landed,) if gi == 0 else (over_ici[0][3],)
        placed[gi] = [_place_shard(big[n], l, who, f"place_{n}_l{l}", deps=behind) for n, l in grp]
        if gi < 2:
            gather_begin(gi)
    to_sibling = {}
    gw = {}

    def gather_forward(gi, after):
        landed = _gather_wait(over_ici[gi], 1, after, f"gather_g{gi}_ici_wait")
        to_sibling[gi] = _gather_start(landed, 2, f"gather_g{gi}_d2d_start")
        tokens.append(to_sibling[gi][3])
        if gi + 2 < len(groups):
            gather_begin(gi + 2, deps=(to_sibling[gi][3],))

    def gather_finish(gi, after):
        for key, arr in zip(groups[gi], _gather_wait(to_sibling[gi], 2, after, f"gather_g{gi}_d2d_wait")):
            gw[key] = arr

    tkk = _attn_tile(T)

    saved = []
    xs = x0
    pending = None
    kv = None
    for l in range(depth):
        sh1, sc1, g1, sh2, sc2, g2 = [_row(mods[l][t]) for t in range(N_MOD)]
        ng1, ng2 = _row(norm_g_full[l, 0]), _row(norm_g_full[l, 1])
        st = dict(sc1=sc1, g1=g1, sc2=sc2, g2=g2, ng1=ng1, ng2=ng2)
        if pending is None:
            h1 = _norm_mod(xs, ng1, sc1, sh1, f"norm1_l{l}")
            gather_forward(0, h1)
            gather_finish(0, h1)
        else:
            gather_finish(l + 1, pending[0])
            xs, h1 = _res_norm_mod(xs, pending[0], pending[1], ng1, sc1, sh1, f"res_norm1_l{l}")
        st["x_in"], st["h1"] = xs, h1
        if l < n_a:
            a = l
            zpre = _mm(h1, gw["gmlp_w_in", a], bmode="col", out_dtypes=(BF16,), name=f"gmlp_in_l{l}",
                       deps=behind_starts())
            if l == 0:
                gather_forward(1, zpre)
            bs_t = gmlp_bs[a].T
            p = _gmlp_mid_fwd(zpre, _row(ln_g_full[a]), _row(ln_b_full[a]), gmlp_ws[a], bs_t, f"gmlp_mid_l{l}")
            if l == 0:
                gather_finish(1, p)
            y = _mm(p, gw["gmlp_w_out", a], bmode="row", name=f"gmlp_out_l{l}", deps=behind_starts())
            st.update(zpre=zpre, p=p)
        else:
            if kv is None:
                kv_ng, kv_sh, kv_sc = _row(kv_norm_g), _row(kv_mod[0]), _row(kv_mod[1])
                hkv = _norm_mod(xs, kv_ng, kv_sc, kv_sh, "norm_kv")
                kvp = _mm(hkv, gw["w_kv", 0], bmode="col", name="kv_proj", deps=behind_starts())
                kk, vv = _head_norm(kvp, _row(k_norm_g), n_heads, "k_norm", tail=True)
                fl = _mm(hkv, w_f_pad, name="gate_logits")
                fcum = _fcum_fwd(fl, b_f_pad, "fcum")
                fk = fcum[:, :n_heads].T.reshape(n_heads, T // tkk, 1, tkk)
                kv = dict(x=xs, hkv=hkv, kvp=kvp, k=kk, v=vv, fl=fl, fcum=fcum, fk=fk, ng=kv_ng, sc=kv_sc)
            bl = l - n_a
            qp = _mm(h1, gw["attn_wq", bl], bmode="row", name=f"q_proj_l{l}", deps=behind_starts())
            q = _head_norm(qp, _row(q_norm_g[bl]), n_heads, f"q_norm_l{l}")[0]
            o, o32, lse = _attn_fwd(q, kv["k"], kv["v"], kv["fk"], n_heads, f"attn_l{l}")
            y = _mm(o, gw["attn_wo", bl], bmode="row", name=f"attn_out_l{l}")
            st.update(qp=qp, q=q, o=o, o32=o32, lse=lse)
        xs, h2 = _res_norm_mod(xs, y, g1, ng2, sc2, sh2, f"res_norm2_l{l}")
        a_pre, s_act = _mm(h2, gw["mlp_w1", l], bmode="col", out_dtypes=(BF16, BF16), epilogue=_relu2_epilogue,
                           name=f"mlp_up_l{l}")
        if l + 1 < depth:
            gather_forward(l + 2, a_pre)
        mo = _mm(s_act, gw["mlp_w2", l], bmode="row", name=f"mlp_down_l{l}", deps=behind_starts())
        st.update(y=y, x_mid=xs, h2=h2, a_pre=a_pre, s=s_act, m=mo)
        saved.append(st)
        pending = (mo, g2)

    dx, dm, loss_row, dg2 = _res_loss(xs, pending[0], pending[1], target, "loss")
    loss = lax.psum(loss_row[0, 0], ("x", "y", "c"))

    started = {}

    def scatter(dw_slabs, key, idx):
        started[(key, idx)] = _a2a_start(dw_slabs, f"scatter_{key}_l{idx}_start")
        tokens.append(started[(key, idx)][4])

    d_mod = [None] * depth
    d_norm_g = [None] * depth
    d_ln_g, d_ln_b, d_ws, d_bs = [None] * n_a, [None] * n_a, [None] * n_a, [None] * n_a
    d_qg = [None] * (depth - n_a)
    dk_list, dv_list, dfk_list = [], [], []
    small = {}

    for l in reversed(range(depth)):
        st = saved[l]
        da = _mm(dm, gw["mlp_w2", l], tb=True, bmode="row", out_dtypes=(BF16,), epilogue=_relu2_bwd_epilogue,
                 extra=(st["a_pre"],), name=f"mlp_down_dx_l{l}")
        dw2 = _mm(st["s"], dm, ta=True, out_dtypes=(BF16,), name=f"mlp_down_dw_l{l}", deps=behind_starts())
        scatter(dw2.reshape(N_DEV, -1, D), "mlp_w2", l)
        dw1 = _mm(st["h2"], da, ta=True, out_mode="col", out_dtypes=(BF16,), name=f"mlp_up_dw_l{l}",
                  deps=behind_starts())
        scatter(dw1, "mlp_w1", l)
        dh2 = _mm(da, gw["mlp_w1", l], tb=True, bmode="col", out_dtypes=(BF16,), name=f"mlp_up_dx_l{l}",
                  deps=behind_starts())
        dx, dy, sums2, dg1 = _norm_mod_bwd(st["x_mid"], [dh2], dx, st["ng2"], st["sc2"], f"norm2_bwd_l{l}",
                                           gated=(st["y"], st["g1"]))
        if l < n_a:
            a = l
            dwo = _mm(st["p"], dy, ta=True, out_dtypes=(BF16,), name=f"gmlp_out_dw_l{l}", deps=behind_starts())
            scatter(dwo.reshape(N_DEV, -1, D), "gmlp_w_out", a)
            dp = _mm(dy, gw["gmlp_w_out", a], tb=True, bmode="row", out_dtypes=(BF16,), name=f"gmlp_out_dx_l{l}",
                     deps=behind_starts())
            dz, d_ws[a], dbs_t, dlg, dlb = _gmlp_mid_bwd(
                st["zpre"], dp, _row(ln_g_full[a]), _row(ln_b_full[a]), gmlp_ws[a],
                jnp.swapaxes(gmlp_ws[a], 1, 2), gmlp_bs[a].T, f"gmlp_mid_bwd_l{l}")
            d_bs[a], d_ln_g[a], d_ln_b[a] = dbs_t[:, :G].T, dlg[0], dlb[0]
            dwi = _mm(st["h1"], dz, ta=True, out_mode="col", out_dtypes=(BF16,), name=f"gmlp_in_dw_l{l}",
                      deps=behind_starts())
            scatter(dwi, "gmlp_w_in", a)
            dh1s = [_mm(dz, gw["gmlp_w_in", a], tb=True, bmode="col", out_dtypes=(BF16,), name=f"gmlp_in_dx_l{l}",
                        deps=behind_starts())]
        else:
            bl = l - n_a
            dwo = _mm(st["o"], dy, ta=True, out_dtypes=(BF16,), name=f"attn_out_dw_l{l}", deps=behind_starts())
            scatter(dwo.reshape(N_DEV, -1, D), "attn_wo", bl)
            do = _mm(dy, gw["attn_wo", bl], tb=True, bmode="row", out_dtypes=(BF16,), name=f"attn_out_dx_l{l}",
                     deps=behind_starts())
            dq, dk, dv, dfq, dfk = _attn_bwd(st["q"], kv["k"], kv["v"], st["o32"], do, st["lse"], kv["fk"],
                                             n_heads, f"attn_bwd_l{l}")
            dfk_list += [dfq, dfk]
            dk_list.append(dk)
            dv_list.append(dv)
            dqp, dqg = _head_norm_bwd(st["qp"], [dq], _row(q_norm_g[bl]), n_heads, f"q_norm_bwd_l{l}")
            d_qg[bl] = dqg[0]
            dwq = _mm(st["h1"], dqp, ta=True, out_dtypes=(BF16,), name=f"q_proj_dw_l{l}", deps=behind_starts())
            scatter(dwq.reshape(N_DEV, -1, D), "attn_wq", bl)
            dh1s = [_mm(dqp, gw["attn_wq", bl], tb=True, bmode="row", out_dtypes=(BF16,), name=f"q_proj_dx_l{l}",
                        deps=behind_starts())]
        below = (saved[l - 1]["m"], saved[l - 1]["g2"]) if l > 0 else None
        if below is None or l == n_a:
            dx, sums1 = _norm_mod_bwd(st["x_in"], dh1s, dx, st["ng1"], st["sc1"], f"norm1_bwd_l{l}")
        else:
            dx, dm_below, sums1, dg2_below = _norm_mod_bwd(st["x_in"], dh1s, dx, st["ng1"], st["sc1"],
                                                           f"norm1_bwd_l{l}", gated=below)
        d_mod[l] = jnp.stack([sums1[0], sums1[1], dg1[0], sums2[0], sums2[1], dg2[0]])
        d_norm_g[l] = jnp.stack([sums1[2], sums2[2]])
        if l == n_a:
            dkvp, dkg = _head_norm_bwd(kv["kvp"], dk_list, _row(k_norm_g), n_heads, "k_norm_bwd", tails=dv_list)
            dfc = [jnp.pad(d.reshape(n_heads, T).T, ((0, 0), (0, BLK - n_heads))) for d in dfk_list]
            dfl, dbf = _fcum_bwd(dfc, kv["fl"], b_f_pad, "fcum_bwd")
            dwkv = _mm(kv["hkv"], dkvp, ta=True, out_mode="col", out_dtypes=(BF16,), name="kv_proj_dw",
                       deps=behind_starts())
            scatter(dwkv, "w_kv", 0)
            dwf = _mm(kv["hkv"], dfl, ta=True, name="gate_logits_dw", deps=behind_starts())
            dh_a = _mm(dkvp, gw["w_kv", 0], tb=True, bmode="col", out_dtypes=(BF16,), name="kv_proj_dx")
            dh_b = _mm(dfl, w_f_pad, tb=True, out_dtypes=(BF16,), name="gate_logits_dx")
            dx, dm_below, sums_kv, dg2_below = _norm_mod_bwd(kv["x"], [dh_a, dh_b], dx, kv["ng"], kv["sc"],
                                                             "norm_kv_bwd", gated=below)
            small.update(d_kv_mod=jnp.stack([sums_kv[0], sums_kv[1]]), d_kv_norm_g=sums_kv[2], d_k_norm_g=dkg[0],
                         d_w_f=dwf[:, :n_heads], d_b_f=dbf[0, :n_heads])
        if l > 0:
            dm, dg2 = dm_below, dg2_below
        if l == 1:
            hi_contrib = [jnp.stack(d_mod[1:]).reshape(depth - 1, N_MOD * D), small["d_kv_mod"].reshape(-1)]
            hi_started = _a2a_start(_pack(hi_contrib), "gather_mod_grads_hi_start", bcast=True)
            tokens.append(hi_started[4])

    grad_x = dx.reshape(x.shape)

    lo_contrib = [d_mod[0].reshape(1, N_MOD * D)]
    rest_contrib = [jnp.stack(d_norm_g), jnp.stack(d_ln_g), jnp.stack(d_ln_b), jnp.stack(d_ws), jnp.stack(d_bs),
                    small["d_kv_norm_g"], small["d_k_norm_g"], small["d_w_f"], small["d_b_f"], jnp.stack(d_qg)]
    hi_shapes, lo_shapes = [a.shape for a in hi_contrib], [a.shape for a in lo_contrib]
    rest_shapes = [a.shape for a in rest_contrib]
    lo_started = _a2a_start(_pack(lo_contrib), "gather_mod_grads_lo_start", bcast=True, deps=behind_starts())
    rest_started = _a2a_start(_pack(rest_contrib), "gather_small_grads_start", bcast=True, deps=(lo_started[4],))
    grads, deltas, new_m, new_v = {}, {}, {}, {}

    def stack_of(n):
        lead = () if weights[n].ndim == 3 else (1,)
        return _AdamStack(*[a.reshape(lead + a.shape) for a in (weights[n], mom_m[n], mom_v[n])], f"adamw_{n}")

    def results_of(n, stack):
        grads[n], deltas[n], new_m[n], new_v[n] = [a.reshape(weights[n].shape) for a in stack.outs]

    sc_t = silu_all.T
    ada_stack, kv_ada_stack = stack_of("ada_w"), stack_of("kv_ada_w")
    is_me = (jnp.arange(N_DEV) == me).reshape(N_DEV, 1, 1)

    def landed_mods(started_pack, shapes, after, name):
        sent, land = _a2a_wait(started_pack, after, f"gather_{name}_wait", bcast=True)
        everyone = _unpack(jnp.where(is_me, sent[None], land), shapes, lead=(N_DEV,))
        return everyone, _unpack(_sum_slabs(land, sent, who, f"sum_{name}"), shapes)

    stacks = {n: stack_of(n) for n in ("mlp_w1", "mlp_w2", "gmlp_w_in", "gmlp_w_out", "attn_wq", "attn_wo", "w_kv")}
    after = rest_started[4]
    for l in reversed(range(depth)):
        if l == n_a - 1:
            (dmod_hi, dkvmod_all), (g_ada_b_hi, g_kv_ada_b) = landed_mods(hi_started, hi_shapes, after, "mod_grads_hi")
            for up in range(1, depth):
                ada_stack.from_outer(up, who, sc_t, _shard_of(dmod_hi[:, up - 1], 1, me, mod_cols))
            kv_ada_stack.from_outer(0, who, sc_t, _shard_of(dkvmod_all, 1, me, kv_cols))
            after = ada_stack.outs[0]
        if l == 0:
            (dmod_lo,), (g_ada_b_lo,) = landed_mods(lo_started, lo_shapes, after, "mod_grads_lo")
            ada_stack.from_outer(0, who, sc_t, _shard_of(dmod_lo[:, 0], 1, me, mod_cols))
            after = ada_stack.outs[0]
        keys = [("mlp_w2", l), ("mlp_w1", l)]
        keys += [("gmlp_w_out", l), ("gmlp_w_in", l)] if l < n_a else [("attn_wo", l - n_a), ("attn_wq", l - n_a)]
        keys += [("w_kv", 0)] if l == n_a else []
        landed = {}
        for key in keys:
            sent, land = _a2a_wait(started[key], after, f"scatter_{key[0]}_l{key[1]}_wait")
            landed[key] = (land, sent)
        for key in keys:
            stacks[key[0]].from_parts(key[1], who, *landed[key])
            after = stacks[key[0]].outs[0]
    for n, stack in stacks.items():
        results_of(n, stack)
    results_of("ada_w", ada_stack)
    results_of("kv_ada_w", kv_ada_stack)
    g_ada_b = jnp.concatenate([g_ada_b_lo, g_ada_b_hi], axis=0)

    rest_sent, rest_land = _a2a_wait(rest_started, after, "gather_small_grads_wait", bcast=True)
    (g_norm_g_full, g_ln_g_full, g_ln_b_full, g_ws, g_bs, g_kv_norm_g, g_k_norm_g, g_w_f_full, g_b_f,
     g_q_norm_g) = _unpack(_sum_slabs(rest_land, rest_sent, who, "sum_small_grads"), rest_shapes)
    small_grads = {
        "ada_b": g_ada_b, "kv_ada_b": g_kv_ada_b.reshape(kv_ada_b.shape),
        "norm_g": _shard_of(g_norm_g_full, 2, me, norm_g.shape[2]),
        "gmlp_ln_g": _shard_of(g_ln_g_full, 1, me, gmlp_ln_g.shape[1]),
        "gmlp_ln_b": _shard_of(g_ln_b_full, 1, me, gmlp_ln_b.shape[1]),
        "gmlp_ws": g_ws, "gmlp_bs": g_bs, "kv_norm_g": g_kv_norm_g, "k_norm_g": g_k_norm_g,
        "w_f": _shard_of(g_w_f_full, 0, me, w_f.shape[0]), "b_f": g_b_f, "q_norm_g": g_q_norm_g,
    }
    small_names = list(small_grads)
    small_w_shapes = [weights[n].shape for n in small_names]
    d_pack, m_pack, v_pack = _adamw_flat(_pack([weights[n] for n in small_names]),
                                         _pack([small_grads[n] for n in small_names]),
                                         _pack([mom_m[n] for n in small_names]), _pack([mom_v[n] for n in small_names]),
                                         "adamw_small")
    grads.update(small_grads)
    deltas.update(zip(small_names, _unpack(d_pack, small_w_shapes)))
    new_m.update(zip(small_names, _unpack(m_pack, small_w_shapes)))
    new_v.update(zip(small_names, _unpack(v_pack, small_w_shapes)))

    return (loss, grad_x, *[grads[n] for n in WEIGHT_NAMES], *[deltas[n] for n in WEIGHT_NAMES],
            *[new_m[n] for n in WEIGHT_NAMES], *[new_v[n] for n in WEIGHT_NAMES])
```

```python
import functools
import math

import jax
import jax.numpy as jnp
from jax import lax
from jax.experimental import pallas as pl
from jax.experimental.pallas import tpu as pltpu

F32 = jnp.float32
BF16 = jnp.bfloat16
N_DEV = 8
EPS = 1e-6
CHUNK = 64
BLK = 128
N_MOD = 6
ADAM_LR = 0.001
ADAM_B1 = 0.9
ADAM_B2 = 0.999
ADAM_EPS = 1e-08
ADAM_WD = 0.01
ADAM_STEP = 10
VMEM_LIMIT_BYTES = 56 * 2 ** 20
NEG_BIG = -1e30
WEIGHT_NAMES = ['ada_w', 'ada_b', 'norm_g', 'mlp_w1', 'mlp_w2', 'gmlp_w_in', 'gmlp_ln_g', 'gmlp_ln_b', 'gmlp_ws',
                'gmlp_bs', 'gmlp_w_out', 'kv_norm_g', 'kv_ada_w', 'kv_ada_b', 'w_kv', 'k_norm_g', 'w_f', 'b_f',
                'attn_wq', 'q_norm_g', 'attn_wo']
MESH = pl.DeviceIdType.MESH


def _params(sem):
    return pltpu.CompilerParams(dimension_semantics=sem, vmem_limit_bytes=VMEM_LIMIT_BYTES)


def _tile(n, cap, unit=128):
    if n <= cap:
        return n
    t = (cap // unit) * unit
    while t > unit and n % t:
        t -= unit
    assert n % t == 0, (n, cap, unit)
    return t


def _my_index():
    return 4 * lax.axis_index("x") + 2 * lax.axis_index("y") + lax.axis_index("c")


def _all_to_all(x, name, bcast=False):
    slab = x.shape if bcast else x.shape[1:]

    def body(x_ref, o_ref, send_sems, recv_sems, local_sem):
        me = _my_index()

        def src(j):
            return x_ref if bcast else x_ref.at[j]

        mine = pltpu.make_async_copy(src(me), o_ref.at[me], local_sem)
        mine.start()
        sends = []
        for d in range(1, N_DEV):
            peer = (me + d) % N_DEV
            cp = pltpu.make_async_remote_copy(
                src_ref=src(peer), dst_ref=o_ref.at[me],
                send_sem=send_sems.at[d - 1], recv_sem=recv_sems.at[d - 1],
                device_id=(peer // 4, (peer // 2) % 2, peer % 2), device_id_type=MESH)
            cp.start()
            sends.append(cp)
        for d in range(1, N_DEV):
            frm = (me + N_DEV - d) % N_DEV
            pltpu.make_async_remote_copy(
                src_ref=src(frm), dst_ref=o_ref.at[frm],
                send_sem=send_sems.at[d - 1], recv_sem=recv_sems.at[d - 1],
                device_id=(frm // 4, (frm // 2) % 2, frm % 2), device_id_type=MESH).wait_recv()
        for cp in sends:
            cp.wait_send()
        mine.wait()

    return pl.pallas_call(
        body, name=name,
        out_shape=jax.ShapeDtypeStruct((N_DEV,) + tuple(slab), x.dtype),
        in_specs=[pl.BlockSpec(memory_space=pl.ANY)],
        out_specs=pl.BlockSpec(memory_space=pl.ANY),
        scratch_shapes=[pltpu.SemaphoreType.DMA((N_DEV - 1,)), pltpu.SemaphoreType.DMA((N_DEV - 1,)),
                        pltpu.SemaphoreType.DMA],
        compiler_params=pltpu.CompilerParams(has_side_effects=True),
    )(x)


_HBM = pl.BlockSpec(memory_space=pltpu.HBM)
_SEM = pl.BlockSpec(memory_space=pltpu.SEMAPHORE)
_ANY = pl.BlockSpec(memory_space=pl.ANY)
_DATAFLOW = pltpu.SideEffectType.DATAFLOW_SIDE_EFFECTING


def _a2a_peer_copy(x_ref, land_ref, send_sems, recv_sems, d, me, incoming, bcast):
    peer = (me + N_DEV - d) % N_DEV if incoming else (me + d) % N_DEV
    return pltpu.make_async_remote_copy(
        src_ref=x_ref if bcast else x_ref.at[peer], dst_ref=land_ref.at[peer if incoming else me],
        send_sem=send_sems.at[d - 1], recv_sem=recv_sems.at[d - 1],
        device_id=(peer // 4, (peer // 2) % 2, peer % 2), device_id_type=MESH)


def _a2a_start(x, name, bcast=False, deps=()):
    land_shape = ((N_DEV,) + tuple(x.shape)) if bcast else x.shape
    n_dep = len(deps)

    def body(x_ref, land_ref, *rest):
        send_sems, recv_sems, token = rest[n_dep], rest[n_dep + 1], rest[-1]
        me = _my_index()
        for d in range(1, N_DEV):
            _a2a_peer_copy(x_ref, land_ref, send_sems, recv_sems, d, me, False, bcast).start()
        token[...] = jnp.zeros_like(token)

    return pl.pallas_call(
        body, name=name,
        out_shape=(pltpu.SemaphoreType.DMA((N_DEV - 1,)), pltpu.SemaphoreType.DMA((N_DEV - 1,)),
                   pltpu.HBM(x.shape, x.dtype), pltpu.HBM(land_shape, x.dtype), jax.ShapeDtypeStruct((8, BLK), F32)),
        in_specs=(_HBM, _HBM, *[_ANY for _ in deps]),
        out_specs=(_SEM, _SEM, _HBM, _HBM, pl.BlockSpec(memory_space=pltpu.VMEM)),
        input_output_aliases={0: 2, 1: 3},
        compiler_params=pltpu.CompilerParams(has_side_effects=_DATAFLOW),
    )(pltpu.with_memory_space_constraint(x, pltpu.HBM),
      pltpu.with_memory_space_constraint(lax.empty(land_shape, x.dtype), pltpu.HBM), *deps)


def _a2a_wait(started, after, name, bcast=False):
    send_sems, recv_sems, x_thru, land_thru, _ = started

    def body(x_ref, land_ref, send_sems, recv_sems, after_ref, x_dead, land_out):
        me = _my_index()
        for d in range(1, N_DEV):
            _a2a_peer_copy(x_ref, land_ref, send_sems, recv_sems, d, me, False, bcast).wait_send()
        for d in range(1, N_DEV):
            _a2a_peer_copy(x_ref, land_ref, send_sems, recv_sems, d, me, True, bcast).wait_recv()

    return pl.pallas_call(
        body, name=name,
        out_shape=(pltpu.HBM(x_thru.shape, x_thru.dtype), pltpu.HBM(land_thru.shape, land_thru.dtype)),
        in_specs=(_HBM, _HBM, _SEM, _SEM, _ANY), out_specs=(_HBM, _HBM),
        input_output_aliases={0: 0, 1: 1},
        compiler_params=pltpu.CompilerParams(has_side_effects=_DATAFLOW),
    )(x_thru, land_thru, send_sems, recv_sems, after)


def _place_shard(w, layer, who, name, deps=()):
    _, R, C = w.shape
    tr = _row_tile(R, C)

    def body(who_ref, w_ref, *rest):
        rest[-1][...] = w_ref[...].astype(BF16)

    return pl.pallas_call(
        body, name=name,
        grid_spec=pltpu.PrefetchScalarGridSpec(
            num_scalar_prefetch=1, grid=(R // tr,),
            in_specs=[pl.BlockSpec((None, tr, C), lambda i, who_ref: (layer, i, 0))] + [_ANY for _ in deps],
            out_specs=pl.BlockSpec((None, None, tr, C), lambda i, who_ref: (who_ref[0], 0, i, 0))),
        out_shape=_sds((N_DEV, 1, R, C), BF16),
        compiler_params=_params(("parallel",)),
    )(who, w, *deps)


def _gather_copies(land_ref, send_sems, recv_sems, base, phase, incoming):
    cx, cy, cc = lax.axis_index("x"), lax.axis_index("y"), lax.axis_index("c")
    sibling = (cx, cy, 1 - cc)
    chips = [(1 - cx, cy), (cx, 1 - cy), (1 - cx, 1 - cy)]
    if phase == 1:
        out = [((cx, cy, cc), sibling)] + [((cx, cy, cc), (*chip, cc)) for chip in chips]
        inc = [(sibling, sibling)] + [((*chip, cc), (*chip, cc)) for chip in chips]
    else:
        out = [((*chip, cc), sibling) for chip in chips]
        inc = [((*chip, 1 - cc), sibling) for chip in chips]
    copies = []
    for k, (block, peer) in enumerate(inc if incoming else out):
        slab = land_ref.at[4 * block[0] + 2 * block[1] + block[2]]
        copies.append(pltpu.make_async_remote_copy(
            src_ref=slab, dst_ref=slab, send_sem=send_sems.at[base + k], recv_sem=recv_sems.at[base + k],
            device_id=peer, device_id_type=MESH))
    return copies


def _gather_start(lands, phase, name, deps=()):
    n, per, n_dep = len(lands), (4 if phase == 1 else 3), len(deps)

    def body(*refs):
        land_refs, send_sems, recv_sems, token = refs[:n], refs[n + n_dep], refs[n + n_dep + 1], refs[-1]
        for a, land_ref in enumerate(land_refs):
            for cp in _gather_copies(land_ref, send_sems, recv_sems, a * per, phase, False):
                cp.start()
        token[...] = jnp.zeros_like(token)

    outs = pl.pallas_call(
        body, name=name,
        out_shape=(pltpu.SemaphoreType.DMA((n * per,)), pltpu.SemaphoreType.DMA((n * per,)),
                   *[pltpu.HBM(x.shape, x.dtype) for x in lands], jax.ShapeDtypeStruct((8, BLK), F32)),
        in_specs=(*[_HBM for _ in lands], *[_ANY for _ in deps]),
        out_specs=(_SEM, _SEM, *[_HBM for _ in lands], pl.BlockSpec(memory_space=pltpu.VMEM)),
        input_output_aliases={a: 2 + a for a in range(n)},
        compiler_params=pltpu.CompilerParams(has_side_effects=_DATAFLOW),
    )(*[pltpu.with_memory_space_constraint(x, pltpu.HBM) for x in lands], *deps)
    return outs[0], outs[1], list(outs[2:2 + n]), outs[-1]


def _gather_wait(started, phase, after, name):
    send_sems, recv_sems, lands, _ = started
    n, per = len(lands), (4 if phase == 1 else 3)

    def body(*refs):
        land_refs, send_sems, recv_sems = refs[:n], refs[n], refs[n + 1]
        for a, land_ref in enumerate(land_refs):
            for cp in _gather_copies(land_ref, send_sems, recv_sems, a * per, phase, False):
                cp.wait_send()
            for cp in _gather_copies(land_ref, send_sems, recv_sems, a * per, phase, True):
                cp.wait_recv()

    outs = pl.pallas_call(
        body, name=name,
        out_shape=tuple(pltpu.HBM(x.shape, x.dtype) for x in lands),
        in_specs=(*[_HBM for _ in lands], _SEM, _SEM, _ANY), out_specs=tuple(_HBM for _ in lands),
        input_output_aliases={a: a for a in range(n)},
        compiler_params=pltpu.CompilerParams(has_side_effects=_DATAFLOW),
    )(*lands, send_sems, recv_sems, after)
    return list(outs)


def _mm(a, b, *, name, ta=False, tb=False, bmode="plain", layer=0, out_mode="plain", out_dtypes=(F32,),
        epilogue=None, extra=(), caps=(1024, 1024, 2048), deps=()):
    if ta:
        K, M = a.shape
    else:
        M, K = a.shape
    n_unit = k_unit = None
    if bmode == "plain":
        N, Kb = (b.shape if tb else b.shape[::-1])
    elif bmode == "col":
        _, _, Kw, Ns = b.shape
        if tb:
            N, Kb, k_unit = Kw, N_DEV * Ns, Ns
        else:
            N, Kb, n_unit = N_DEV * Ns, Kw, Ns
    else:
        _, _, Ks, Nw = b.shape
        if tb:
            N, Kb, n_unit = N_DEV * Ks, Nw, Ks
        else:
            N, Kb, k_unit = Nw, N_DEV * Ks, Ks
    assert K == Kb, (name, a.shape, b.shape)
    if out_mode == "col":
        assert n_unit is None
        n_unit = N // N_DEV
    tm = _tile(M, caps[0])
    tn = _tile(n_unit or N, caps[1])
    span = 1
    if k_unit and 2 * k_unit <= caps[2]:
        while 2 * span * k_unit <= caps[2] and N_DEV % (2 * span) == 0:
            span *= 2
        tk = span * k_unit
    else:
        tk = _tile(k_unit or K, caps[2])
    nk = K // tk
    npb = (n_unit // tn) if n_unit else None
    kpb = (k_unit // tk) if (k_unit and span == 1) else None
    grid = (M // tm, N // tn, nk)

    a_spec = pl.BlockSpec((tk, tm), lambda i, j, k: (k, i)) if ta else pl.BlockSpec((tm, tk), lambda i, j, k: (i, k))
    if bmode == "plain":
        b_spec = (pl.BlockSpec((tn, tk), lambda i, j, k: (j, k)) if tb
                  else pl.BlockSpec((tk, tn), lambda i, j, k: (k, j)))
    elif bmode == "col":
        if tb and span > 1:
            b_spec = pl.BlockSpec((span, None, tn, k_unit), lambda i, j, k: (k, layer, j, 0))
        elif tb:
            b_spec = pl.BlockSpec((None, None, tn, tk), lambda i, j, k: (k // kpb, layer, j, k % kpb))
        else:
            b_spec = pl.BlockSpec((None, None, tk, tn), lambda i, j, k: (j // npb, layer, k, j % npb))
    else:
        if tb:
            b_spec = pl.BlockSpec((None, None, tn, tk), lambda i, j, k: (j // npb, layer, j % npb, k))
        elif span > 1:
            b_spec = pl.BlockSpec((span, None, k_unit, tn), lambda i, j, k: (k, layer, 0, j))
        else:
            b_spec = pl.BlockSpec((None, None, tk, tn), lambda i, j, k: (k // kpb, layer, k % kpb, j))
    mn_spec = pl.BlockSpec((tm, tn), lambda i, j, k: (i, j))
    if out_mode == "col":
        o_specs = [pl.BlockSpec((None, tm, tn), lambda i, j, k: (j // npb, i, j % npb))]
        o_shapes = [jax.ShapeDtypeStruct((N_DEV, M, N // N_DEV), out_dtypes[0])]
    else:
        o_specs = [mn_spec for _ in out_dtypes]
        o_shapes = [jax.ShapeDtypeStruct((M, N), dt) for dt in out_dtypes]
    dims = (((0 if ta else 1,), (1 if tb else 0,)), ((), ()))
    n_extra, n_out, n_dep = len(extra), len(out_dtypes), len(deps)

    def body(a_ref, b_ref, *rest):
        extra_refs, out_refs = rest[:n_extra], rest[n_extra + n_dep:n_extra + n_dep + n_out]
        k = pl.program_id(2)

        def product():
            if span == 1:
                return lax.dot_general(a_ref[...].astype(BF16), b_ref[...].astype(BF16), dims,
                                       preferred_element_type=F32)
            if not tb:
                return lax.dot_general(a_ref[...].astype(BF16), b_ref[...].reshape(tk, tn).astype(BF16), dims,
                                       preferred_element_type=F32)
            out = None
            for s in range(span):
                part = lax.dot_general(a_ref[:, s * k_unit:(s + 1) * k_unit].astype(BF16), b_ref[s].astype(BF16),
                                       dims, preferred_element_type=F32)
                out = part if out is None else out + part
            return out

        def finish(acc):
            outs = (acc,) if epilogue is None else epilogue(acc, *[r[...] for r in extra_refs])
            for o_ref, val in zip(out_refs, outs):
                o_ref[...] = val.astype(o_ref.dtype)

        if nk == 1:
            finish(product())
            return
        acc_ref = rest[-1]

        @pl.when(k == 0)
        def _():
            acc_ref[...] = product()

        if nk > 2:
            @pl.when(jnp.logical_and(k > 0, k < nk - 1))
            def _():
                acc_ref[...] += product()

        @pl.when(k == nk - 1)
        def _():
            finish(acc_ref[...] + product())

    outs = pl.pallas_call(
        body, name=name, grid=grid,
        in_specs=[a_spec, b_spec] + [mn_spec for _ in extra] + [_ANY for _ in deps],
        out_specs=o_specs, out_shape=o_shapes,
        scratch_shapes=[pltpu.VMEM((tm, tn), F32)] if nk > 1 else [],
        compiler_params=_params(("parallel", "parallel", "arbitrary")),
    )(a, b, *extra, *deps)
    return outs[0] if n_out == 1 else outs


def _relu2_epilogue(acc):
    r = jnp.maximum(acc, 0.0)
    return acc, r * r


def _relu2_bwd_epilogue(acc, a_pre):
    return (acc * (2.0 * jnp.maximum(a_pre.astype(F32), 0.0)),)


def _rowcall(body, *, name, tr, row_ins, full_ins=(), row_outs=(), acc_outs=(), scratch=(), reverse=False):
    T = row_ins[0].shape[0]
    nb = T // tr
    rmap = (lambda i: (nb - 1 - i, 0)) if reverse else (lambda i: (i, 0))

    def full_spec(shape):
        nd = len(shape)
        return pl.BlockSpec(tuple(shape), lambda i: (0,) * nd)

    in_specs = [pl.BlockSpec((tr, a.shape[1]), rmap) for a in row_ins] + [full_spec(a.shape) for a in full_ins]
    out_specs = [pl.BlockSpec((tr, s.shape[1]), rmap) for s in row_outs] + [full_spec(s.shape) for s in acc_outs]
    outs = pl.pallas_call(
        body, name=name, grid=(nb,), in_specs=in_specs, out_specs=out_specs,
        out_shape=list(row_outs) + list(acc_outs), scratch_shapes=list(scratch),
        compiler_params=_params(("arbitrary",)),
    )(*row_ins, *full_ins)
    return outs


def _sds(shape, dtype):
    return jax.ShapeDtypeStruct(tuple(shape), dtype)


def _row_tile(T, C, elems=512 * 1024):
    t = max(8, min(T, elems // C))
    p = 8
    while p * 2 <= t and T % (p * 2) == 0:
        p *= 2
    return p


def _norm_mod(x, ng, sc, sh, name):
    T, D = x.shape

    def body(x_ref, ng_ref, sc_ref, sh_ref, h_ref):
        xv = x_ref[...]
        r = lax.rsqrt(jnp.mean(xv * xv, axis=-1, keepdims=True) + EPS)
        h_ref[...] = (((xv * r) * ng_ref[...]) * (1.0 + sc_ref[...]) + sh_ref[...]).astype(BF16)

    return _rowcall(body, name=name, tr=_row_tile(T, D), row_ins=[x], full_ins=[ng, sc, sh],
                    row_outs=[_sds((T, D), BF16)])[0]


def _res_norm_mod(x, y, gate, ng, sc, sh, name):
    T, D = x.shape

    def body(x_ref, y_ref, g_ref, ng_ref, sc_ref, sh_ref, x2_ref, h_ref):
        xv = x_ref[...] + g_ref[...] * y_ref[...]
        x2_ref[...] = xv
        r = lax.rsqrt(jnp.mean(xv * xv, axis=-1, keepdims=True) + EPS)
        h_ref[...] = (((xv * r) * ng_ref[...]) * (1.0 + sc_ref[...]) + sh_ref[...]).astype(BF16)

    return _rowcall(body, name=name, tr=_row_tile(T, D, 256 * 1024), row_ins=[x, y], full_ins=[gate, ng, sc, sh],
                    row_outs=[_sds((T, D), F32), _sds((T, D), BF16)])


def _res_loss(x, y, gate, target, name):
    T, D = x.shape

    def body(x_ref, y_ref, t_ref, g_ref, dout_ref, dy_ref, loss_ref, dg_ref):
        @pl.when(pl.program_id(0) == 0)
        def _():
            loss_ref[...] = jnp.zeros_like(loss_ref)
            dg_ref[...] = jnp.zeros_like(dg_ref)

        yv, gv = y_ref[...], g_ref[...]
        diff = x_ref[...] + gv * yv - t_ref[...]
        dout = diff * (1.0 / D)
        dout_ref[...] = dout
        dy_ref[...] = (dout * gv).astype(BF16)
        loss_ref[...] += jnp.sum(diff * diff) * (0.5 / D)
        dg_ref[...] += jnp.sum(dout * yv, axis=0, keepdims=True)

    return _rowcall(body, name=name, tr=_row_tile(T, D, 256 * 1024), row_ins=[x, y, target], full_ins=[gate],
                    row_outs=[_sds((T, D), F32), _sds((T, D), BF16)],
                    acc_outs=[_sds((1, BLK), F32), _sds((1, D), F32)])


def _norm_mod_bwd(x, dhs, dres, ng, sc, name, gated=None):
    T, D = x.shape
    n_dh = len(dhs)
    n_row = 2 + n_dh + (1 if gated else 0)

    def body(*refs):
        x_ref, dh_refs, dres_ref = refs[0], refs[1:1 + n_dh], refs[1 + n_dh]
        ng_ref, sc_ref = refs[n_row:n_row + 2]
        outs = refs[n_row + (3 if gated else 2):]
        dx_ref, sums_ref = (outs[0], outs[2]) if gated else (outs[0], outs[1])

        @pl.when(pl.program_id(0) == 0)
        def _():
            sums_ref[...] = jnp.zeros_like(sums_ref)
            if gated:
                outs[3][...] = jnp.zeros_like(outs[3])

        xv = x_ref[...]
        dh = dh_refs[0][...].astype(F32)
        for r_ in dh_refs[1:]:
            dh = dh + r_[...].astype(F32)
        r = lax.rsqrt(jnp.mean(xv * xv, axis=-1, keepdims=True) + EPS)
        n = xv * r
        ngv, scale1 = ng_ref[...], 1.0 + sc_ref[...]
        dn = dh * (ngv * scale1)
        dx = dres_ref[...] + r * (dn - n * jnp.mean(dn * n, axis=-1, keepdims=True))
        dx_ref[...] = dx
        dhn = dh * n
        sums_ref[0:1, :] += jnp.sum(dh, axis=0, keepdims=True)
        sums_ref[1:2, :] += jnp.sum(dhn * ngv, axis=0, keepdims=True)
        sums_ref[2:3, :] += jnp.sum(dhn * scale1, axis=0, keepdims=True)
        if gated:
            y_ref, g_ref = refs[2 + n_dh], refs[n_row + 2]
            outs[1][...] = (dx * g_ref[...]).astype(BF16)
            outs[3][...] += jnp.sum(dx * y_ref[...], axis=0, keepdims=True)

    return _rowcall(body, name=name, tr=_row_tile(T, D, 256 * 1024),
                    row_ins=[x, *dhs, dres] + ([gated[0]] if gated else []),
                    full_ins=[ng, sc] + ([gated[1]] if gated else []),
                    row_outs=[_sds((T, D), F32)] + ([_sds((T, D), BF16)] if gated else []),
                    acc_outs=[_sds((8, D), F32)] + ([_sds((1, D), F32)] if gated else []))


def _head_norm(x, g, n_heads, name, tail=False):
    T = x.shape[0]
    D = n_heads * BLK
    W = x.shape[1] if tail else D

    def body(x_ref, g_ref, o_ref, *tail_ref):
        for h in range(n_heads):
            xv = x_ref[:, h * BLK:(h + 1) * BLK]
            r = lax.rsqrt(jnp.mean(xv * xv, axis=-1, keepdims=True) + EPS)
            o_ref[:, h * BLK:(h + 1) * BLK] = ((xv * r) * g_ref[...]).astype(BF16)
        if tail:
            tail_ref[0][...] = x_ref[:, D:2 * D].astype(BF16)

    tr = _row_tile(T, x.shape[1])
    o_spec = pl.BlockSpec((tr, D), lambda i: (i, 0))
    return pl.pallas_call(
        body, name=name, grid=(T // tr,),
        in_specs=[pl.BlockSpec((tr, W), lambda i: (i, 0)), pl.BlockSpec((1, BLK), lambda i: (0, 0))],
        out_specs=[o_spec] * (2 if tail else 1), out_shape=[_sds((T, D), BF16)] * (2 if tail else 1),
        compiler_params=_params(("parallel",)),
    )(x, g)


def _head_norm_bwd(x, dys, g, n_heads, name, tails=()):
    T = x.shape[0]
    D = n_heads * BLK
    n_dy, n_tail = len(dys), len(tails)
    W = 2 * D if tails else D

    def body(*refs):
        x_ref, dy_refs, tail_refs = refs[0], refs[1:1 + n_dy], refs[1 + n_dy:1 + n_dy + n_tail]
        g_ref, dx_ref, dg_ref = refs[1 + n_dy + n_tail:]

        @pl.when(pl.program_id(0) == 0)
        def _():
            dg_ref[...] = jnp.zeros_like(dg_ref)

        tot = jnp.zeros((1, BLK), F32)
        for h in range(n_heads):
            cols = slice(h * BLK, (h + 1) * BLK)
            xv = x_ref[:, cols]
            dyv = dy_refs[0][:, cols]
            for r_ in dy_refs[1:]:
                dyv = dyv + r_[:, cols]
            r = lax.rsqrt(jnp.mean(xv * xv, axis=-1, keepdims=True) + EPS)
            n = xv * r
            dn = dyv * g_ref[...]
            dx_ref[:, cols] = (r * (dn - n * jnp.mean(dn * n, axis=-1, keepdims=True))).astype(BF16)
            tot = tot + jnp.sum(dyv * n, axis=0, keepdims=True)
        dg_ref[0:1, :] += tot
        if n_tail:
            tv = tail_refs[0][...]
            for r_ in tail_refs[1:]:
                tv = tv + r_[...]
            dx_ref[:, D:] = tv.astype(BF16)

    tr = _row_tile(T, 2 * D, 256 * 1024)
    d_spec = pl.BlockSpec((tr, D), lambda i: (i, 0))
    return pl.pallas_call(
        body, name=name, grid=(T // tr,),
        in_specs=[d_spec] * (1 + n_dy + n_tail) + [pl.BlockSpec((1, BLK), lambda i: (0, 0))],
        out_specs=[pl.BlockSpec((tr, W), lambda i: (i, 0)), pl.BlockSpec((8, BLK), lambda i: (0, 0))],
        out_shape=[_sds((T, W), BF16), _sds((8, BLK), F32)],
        compiler_params=_params(("arbitrary",)),
    )(x, *dys, *tails, g)


def _fcum_fwd(fl, bf, name):
    T = fl.shape[0]

    def body(fl_ref, b_ref, o_ref, carry_ref):
        @pl.when(pl.program_id(0) == 0)
        def _():
            carry_ref[...] = jnp.zeros_like(carry_ref)

        z = fl_ref[...] + b_ref[...]
        logf = jnp.minimum(z, 0.0) - jnp.log(1.0 + jnp.exp(-jnp.abs(z)))
        row = lax.broadcasted_iota(jnp.int32, (BLK, BLK), 0)
        col = lax.broadcasted_iota(jnp.int32, (BLK, BLK), 1)
        tri = (col <= row).astype(F32)
        run = jnp.dot(tri, logf, preferred_element_type=F32, precision=lax.Precision.HIGHEST) + carry_ref[0:1, :]
        o_ref[...] = run
        carry_ref[0:1, :] = run[BLK - 1:BLK, :]

    return _rowcall(body, name=name, tr=BLK, row_ins=[fl], full_ins=[bf], row_outs=[_sds((T, BLK), F32)],
                    scratch=[pltpu.VMEM((8, BLK), F32)])[0]


def _fcum_bwd(dfs, fl, bf, name):
    T = fl.shape[0]
    n_df = len(dfs)

    def body(*refs):
        df_refs = refs[:n_df]
        fl_ref, b_ref, dfl_ref, dbias_ref, carry_ref = refs[n_df:]

        @pl.when(pl.program_id(0) == 0)
        def _():
            carry_ref[...] = jnp.zeros_like(carry_ref)
            dbias_ref[...] = jnp.zeros_like(dbias_ref)

        dfc = df_refs[0][...]
        for r_ in df_refs[1:]:
            dfc = dfc + r_[...]
        row = lax.broadcasted_iota(jnp.int32, (BLK, BLK), 0)
        col = lax.broadcasted_iota(jnp.int32, (BLK, BLK), 1)
        tri = (col >= row).astype(F32)
        suffix = jnp.dot(tri, dfc, preferred_element_type=F32, precision=lax.Precision.HIGHEST) + carry_ref[0:1, :]
        carry_ref[0:1, :] = suffix[0:1, :]
        z = fl_ref[...] + b_ref[...]
        dfl = suffix / (1.0 + jnp.exp(z))
        dfl_ref[...] = dfl.astype(BF16)
        dbias_ref[0:1, :] += jnp.sum(dfl, axis=0, keepdims=True)

    return _rowcall(body, name=name, tr=BLK, row_ins=[*dfs, fl], full_ins=[bf], reverse=True,
                    row_outs=[_sds((T, BLK), BF16)], acc_outs=[_sds((8, BLK), F32)],
                    scratch=[pltpu.VMEM((8, BLK), F32)])


def _attn_tile(T):
    return min(T, 512)


def _attn_fwd(q, k, v, fk, n_heads, name):
    T = q.shape[0]
    tq = tk = _attn_tile(T)
    nkb = T // tk
    inv_sqrt = 1.0 / float(math.sqrt(BLK))

    def body(q_ref, k_ref, v_ref, fk_ref, o_ref, o32_ref, lse_ref):
        i = pl.program_id(1)
        qv = q_ref[...]

        def block(j, carry, diagonal):
            m, l, acc = carry
            rows = pl.ds(pl.multiple_of(j * tk, tk), tk)
            kj, vj = k_ref[rows, :], v_ref[rows, :]
            s = lax.dot_general(qv, kj, (((1,), (1,)), ((), ())), preferred_element_type=F32) * inv_sqrt
            s = s - fk_ref[j]
            if diagonal:
                s = jnp.where(lax.broadcasted_iota(jnp.int32, (tq, tk), 1)
                              <= lax.broadcasted_iota(jnp.int32, (tq, tk), 0), s, NEG_BIG)
            m_new = jnp.maximum(m, jnp.max(s, axis=-1, keepdims=True))
            alpha = jnp.exp(m - m_new)
            p = jnp.exp(s - m_new)
            l = alpha * l + jnp.sum(p, axis=-1, keepdims=True)
            acc = alpha * acc + jnp.dot(p.astype(BF16), vj, preferred_element_type=F32)
            return m_new, l, acc

        init = (jnp.full((tq, 1), NEG_BIG, F32), jnp.zeros((tq, 1), F32), jnp.zeros((tq, BLK), F32))
        carry = lax.fori_loop(0, i, lambda j, c: block(j, c, False), init)
        m, l, acc = block(i, carry, True)
        out = acc / l
        o_ref[...] = out.astype(BF16)
        o32_ref[...] = out
        lse_ref[...] = m + jnp.log(l)

    return pl.pallas_call(
        body, name=name, grid=(n_heads, T // tq),
        in_specs=[pl.BlockSpec((tq, BLK), lambda h, i: (i, h)),
                  pl.BlockSpec((T, BLK), lambda h, i: (0, h)),
                  pl.BlockSpec((T, BLK), lambda h, i: (0, h)),
                  pl.BlockSpec((None, nkb, 1, tk), lambda h, i: (h, 0, 0, 0))],
        out_specs=[pl.BlockSpec((tq, BLK), lambda h, i: (i, h)),
                   pl.BlockSpec((tq, BLK), lambda h, i: (i, h)),
                   pl.BlockSpec((None, tq, 1), lambda h, i: (h, i, 0))],
        out_shape=[_sds((T, n_heads * BLK), BF16), _sds((T, n_heads * BLK), F32), _sds((n_heads, T, 1), F32)],
        compiler_params=_params(("parallel", "arbitrary")),
    )(q, k, v, fk)


def _attn_bwd(q, k, v, o, do, lse, fk, n_heads, name):
    T = q.shape[0]
    tq = tk = _attn_tile(T)
    nkb = T // tk
    nq = T // tq
    inv_sqrt = 1.0 / float(math.sqrt(BLK))
    tn_dims = (((0,), (0,)), ((), ()))
    nt_dims = (((1,), (1,)), ((), ()))

    def body(q_ref, k_ref, v_ref, o_ref, do_ref, lse_ref, fk_ref, dq_ref, dk_ref, dv_ref, dfq_ref, dfk_ref, delta_ref):
        j = pl.program_id(1)

        @pl.when(j == 0)
        def _():
            delta_ref[...] = jnp.sum(do_ref[...].astype(F32) * o_ref[...], axis=1, keepdims=True)
            dq_ref[...] = jnp.zeros_like(dq_ref)
            dfq_ref[...] = jnp.zeros_like(dfq_ref)

        kj, vj, fkv = k_ref[...], v_ref[...], fk_ref[...]

        def step(i, carry, diagonal):
            dk, dv, dfk = carry
            rows = pl.ds(pl.multiple_of(i * tq, tq), tq)
            qi, doi = q_ref[rows, :], do_ref[rows, :]
            s = lax.dot_general(qi, kj, nt_dims, preferred_element_type=F32) * inv_sqrt - fkv
            if diagonal:
                s = jnp.where(lax.broadcasted_iota(jnp.int32, (tq, tk), 1)
                              <= lax.broadcasted_iota(jnp.int32, (tq, tk), 0), s, NEG_BIG)
            p = jnp.exp(s - lse_ref[rows, :])
            dv = dv + lax.dot_general(p.astype(BF16), doi, tn_dims, preferred_element_type=F32)
            dp = lax.dot_general(doi, vj, nt_dims, preferred_element_type=F32)
            ds = p * (dp - delta_ref[rows, :])
            dsb = ds.astype(BF16)
            dq_ref[rows, :] += jnp.dot(dsb, kj, preferred_element_type=F32) * inv_sqrt
            dk = dk + lax.dot_general(dsb, qi, tn_dims, preferred_element_type=F32)
            dfq_ref[rows, :] += jnp.sum(ds, axis=1, keepdims=True)
            dfk = dfk - jnp.sum(ds, axis=0, keepdims=True)
            return dk, dv, dfk

        init = (jnp.zeros((tk, BLK), F32), jnp.zeros((tk, BLK), F32), jnp.zeros((1, tk), F32))
        carry = step(j, init, True)
        dk, dv, dfk = lax.fori_loop(j + 1, nq, lambda i, c: step(i, c, False), carry)
        dk_ref[...] = dk * inv_sqrt
        dv_ref[...] = dv
        dfk_ref[...] = dfk

    head_col = lambda h, j: (0, h)
    return pl.pallas_call(
        body, name=name, grid=(n_heads, nkb),
        in_specs=[pl.BlockSpec((T, BLK), head_col),
                  pl.BlockSpec((tk, BLK), lambda h, j: (j, h)),
                  pl.BlockSpec((tk, BLK), lambda h, j: (j, h)),
                  pl.BlockSpec((T, BLK), head_col),
                  pl.BlockSpec((T, BLK), head_col),
                  pl.BlockSpec((None, T, 1), lambda h, j: (h, 0, 0)),
                  pl.BlockSpec((None, None, 1, tk), lambda h, j: (h, j, 0, 0))],
        out_specs=[pl.BlockSpec((T, BLK), head_col),
                   pl.BlockSpec((tk, BLK), lambda h, j: (j, h)),
                   pl.BlockSpec((tk, BLK), lambda h, j: (j, h)),
                   pl.BlockSpec((None, T, 1), lambda h, j: (h, 0, 0)),
                   pl.BlockSpec((None, None, 1, tk), lambda h, j: (h, j, 0, 0))],
        out_shape=[_sds((T, n_heads * BLK), F32), _sds((T, n_heads * BLK), F32), _sds((T, n_heads * BLK), F32),
                   _sds((n_heads, T, 1), F32), _sds((n_heads, nkb, 1, tk), F32)],
        scratch_shapes=[pltpu.VMEM((T, 1), F32)],
        compiler_params=_params(("parallel", "arbitrary")),
    )(q, k, v, o, do, lse, fk)


_INV_SQRT2 = 1.0 / math.sqrt(2.0)
_INV_SQRT_2PI = 1.0 / math.sqrt(2.0 * math.pi)


def _gelu_parts(z):
    cdf = 0.5 * (1.0 + lax.erf(z * _INV_SQRT2))
    return cdf, z * cdf


def _mix_mask(transposed):
    row = lax.broadcasted_iota(jnp.int32, (BLK, BLK), 0) // CHUNK
    col = lax.broadcasted_iota(jnp.int32, (BLK, BLK), 1) // CHUNK
    return (row <= col) if transposed else (col <= row)


def _gmlp_mid_fwd(zpre, ln_g, ln_b, ws, bs_t, name):
    T, two_h = zpre.shape
    Hh = two_h // 2
    G = ws.shape[0]
    gd = Hh // G

    def body(z_ref, lg_ref, lb_ref, ws_ref, bs_ref, p_ref):
        _, zg = _gelu_parts(z_ref[...].astype(F32))
        u, v = zg[:, :Hh], zg[:, Hh:]
        mu = jnp.mean(v, axis=-1, keepdims=True)
        vc = v - mu
        rstd = lax.rsqrt(jnp.mean(vc * vc, axis=-1, keepdims=True) + EPS)
        vn = ((vc * rstd) * lg_ref[...] + lb_ref[...]).astype(BF16)
        mask = _mix_mask(False)
        for g in range(G):
            wm = jnp.where(mask, ws_ref[g], 0.0).astype(BF16)
            sv = jnp.dot(wm, vn[:, g * gd:(g + 1) * gd], preferred_element_type=F32) + bs_ref[:, g:g + 1]
            p_ref[:, g * gd:(g + 1) * gd] = (u[:, g * gd:(g + 1) * gd] * sv).astype(BF16)

    return _rowcall(body, name=name, tr=BLK, row_ins=[zpre], full_ins=[ln_g, ln_b, ws, bs_t],
                    row_outs=[_sds((T, Hh), BF16)])[0]


def _gmlp_mid_bwd(zpre, dp, ln_g, ln_b, ws, ws_t, bs_t, name):
    T, two_h = zpre.shape
    Hh = two_h // 2
    G = ws.shape[0]
    gd = Hh // G
    nt_dims = (((1,), (1,)), ((), ()))

    def body(z_ref, dp_ref, lg_ref, lb_ref, ws_ref, wst_ref, bs_ref, dz_ref, dws_ref, dbs_ref, dlg_ref, dlb_ref,
             dvn_ref):
        @pl.when(pl.program_id(0) == 0)
        def _():
            dws_ref[...] = jnp.zeros_like(dws_ref)
            dbs_ref[...] = jnp.zeros_like(dbs_ref)
            dlg_ref[...] = jnp.zeros_like(dlg_ref)
            dlb_ref[...] = jnp.zeros_like(dlb_ref)

        z = z_ref[...].astype(F32)
        cdf, zg = _gelu_parts(z)
        dgelu = cdf + z * (jnp.exp(-0.5 * z * z) * _INV_SQRT_2PI)
        u, v = zg[:, :Hh], zg[:, Hh:]
        mu = jnp.mean(v, axis=-1, keepdims=True)
        vc = v - mu
        rstd = lax.rsqrt(jnp.mean(vc * vc, axis=-1, keepdims=True) + EPS)
        vhat = vc * rstd
        vn = (vhat * lg_ref[...] + lb_ref[...]).astype(BF16)
        mask, mask_t = _mix_mask(False), _mix_mask(True)
        lane = lax.broadcasted_iota(jnp.int32, (BLK, BLK), 1)
        dbs = jnp.zeros((BLK, BLK), F32)
        for g in range(G):
            cols = slice(g * gd, (g + 1) * gd)
            wm = jnp.where(mask, ws_ref[g], 0.0).astype(BF16)
            wm_t = jnp.where(mask_t, wst_ref[g], 0.0).astype(BF16)
            vn_g = vn[:, cols]
            sv = jnp.dot(wm, vn_g, preferred_element_type=F32) + bs_ref[:, g:g + 1]
            dp_g = dp_ref[:, cols].astype(F32)
            dz_ref[:, cols] = ((dp_g * sv) * dgelu[:, cols]).astype(BF16)
            dsv = dp_g * u[:, cols]
            dsv_b = dsv.astype(BF16)
            dbs = dbs + jnp.where(lane == g, jnp.sum(dsv, axis=1, keepdims=True), 0.0)
            dws_ref[g] += jnp.where(mask, lax.dot_general(dsv_b, vn_g, nt_dims, preferred_element_type=F32), 0.0)
            dvn_ref[:, cols] = jnp.dot(wm_t, dsv_b, preferred_element_type=F32)
        dbs_ref[...] += dbs
        dvn = dvn_ref[...]
        dlg_ref[0:1, :] += jnp.sum(dvn * vhat, axis=0, keepdims=True)
        dlb_ref[0:1, :] += jnp.sum(dvn, axis=0, keepdims=True)
        dvh = dvn * lg_ref[...]
        dv = rstd * (dvh - jnp.mean(dvh, axis=-1, keepdims=True) - vhat * jnp.mean(dvh * vhat, axis=-1, keepdims=True))
        dz_ref[:, Hh:] = (dv * dgelu[:, Hh:]).astype(BF16)

    return _rowcall(body, name=name, tr=BLK, row_ins=[zpre, dp], full_ins=[ln_g, ln_b, ws, ws_t, bs_t],
                    row_outs=[_sds((T, two_h), BF16)],
                    acc_outs=[_sds((G, BLK, BLK), F32), _sds((BLK, BLK), F32), _sds((8, Hh), F32), _sds((8, Hh), F32)],
                    scratch=[pltpu.VMEM((BLK, Hh), F32)])


def _mods(c_all, w, layer, bias, name):
    nb, K = c_all.shape
    N = w.shape[-1]
    tn = _tile(N, 512)

    def body(c_ref, w_ref, b_ref, o_ref):
        cv = c_ref[...]
        sc = cv / (1.0 + jnp.exp(-cv))
        o_ref[...] = jnp.dot(sc, w_ref[...], preferred_element_type=F32, precision=lax.Precision.HIGHEST) + b_ref[...]

    return pl.pallas_call(
        body, name=name, grid=(N // tn,),
        in_specs=[pl.BlockSpec((nb, K), lambda j: (0, 0)),
                  pl.BlockSpec((None, K, tn), lambda j: (layer, 0, j)),
                  pl.BlockSpec((1, tn), lambda j: (0, j))],
        out_specs=pl.BlockSpec((nb, tn), lambda j: (0, j)), out_shape=_sds((nb, N), F32),
        compiler_params=_params(("parallel",)),
    )(c_all, w, bias)


def _sum_slabs(landed, own, who, name):
    _, R, C = landed.shape
    tr = _row_tile(R, C * N_DEV)

    def body(who_ref, x_ref, own_ref, o_ref):
        me, mine = who_ref[0], own_ref[...]
        acc = jnp.where(me == 0, mine, x_ref[0])
        for s in range(1, N_DEV):
            acc = acc + jnp.where(me == s, mine, x_ref[s])
        o_ref[...] = acc

    return pl.pallas_call(
        body, name=name,
        grid_spec=pltpu.PrefetchScalarGridSpec(
            num_scalar_prefetch=1, grid=(R // tr,),
            in_specs=[pl.BlockSpec((N_DEV, tr, C), lambda i, who_ref: (0, i, 0)),
                      pl.BlockSpec((tr, C), lambda i, who_ref: (i, 0))],
            out_specs=pl.BlockSpec((tr, C), lambda i, who_ref: (i, 0))),
        out_shape=_sds((R, C), F32),
        compiler_params=_params(("parallel",)),
    )(who, landed, own)


def _adamw_math(w, g, m, v):
    m = ADAM_B1 * m + (1.0 - ADAM_B1) * g
    v = ADAM_B2 * v + (1.0 - ADAM_B2) * (g * g)
    m_hat = m / (1.0 - ADAM_B1 ** ADAM_STEP)
    v_hat = v / (1.0 - ADAM_B2 ** ADAM_STEP)
    delta = -ADAM_LR * (m_hat / (jnp.sqrt(v_hat) + ADAM_EPS) + ADAM_WD * w)
    return delta, m, v


class _AdamStack:
    def __init__(self, w, m, v, name):
        self.w, self.m, self.v, self.name = w, m, v, name
        self.L, self.R, self.C = w.shape
        self.tr = _row_tile(self.R, self.C, 256 * 1024)
        self.outs = None

    def _layer(self, l, who, srcs, src_specs, make_grad):
        n_src = len(srcs)
        L, R, C, tr = self.L, self.R, self.C, self.tr
        wspec = pl.BlockSpec((None, tr, C), lambda i, who_ref: (l, i, 0))

        def body(who_ref, *refs):
            src_refs = refs[:n_src]
            w_ref, m_ref, v_ref = refs[n_src:n_src + 3]
            g_ref, d_ref, m2_ref, v2_ref = refs[-4:]
            g = make_grad(who_ref[0], *src_refs)
            delta, m2, v2 = _adamw_math(w_ref[...], g, m_ref[...], v_ref[...])
            g_ref[...] = g
            d_ref[...] = delta
            m2_ref[...] = m2
            v2_ref[...] = v2

        prev = [] if self.outs is None else list(self.outs)
        aliases = {} if self.outs is None else {1 + n_src + 3 + t: t for t in range(4)}
        self.outs = pl.pallas_call(
            body, name=f"{self.name}_l{l}",
            grid_spec=pltpu.PrefetchScalarGridSpec(
                num_scalar_prefetch=1, grid=(R // tr,),
                in_specs=list(src_specs) + [wspec] * 3 + [_ANY] * len(prev), out_specs=[wspec] * 4),
            out_shape=[_sds((L, R, C), F32)] * 4,
            input_output_aliases=aliases,
            compiler_params=_params(("parallel",)),
        )(who, *srcs, self.w, self.m, self.v, *prev)

    def from_parts(self, l, who, landed, sent):
        tr, C = self.tr, self.C

        def make_grad(me, p_ref, own_ref):
            mine = own_ref[...].astype(F32)
            g = jnp.where(me == 0, mine, p_ref[0].astype(F32))
            for s in range(1, N_DEV):
                g = g + jnp.where(me == s, mine, p_ref[s].astype(F32))
            return g

        self._layer(l, who, [landed, sent],
                    [pl.BlockSpec((N_DEV, tr, C), lambda i, who_ref: (0, i, 0)),
                     pl.BlockSpec((None, tr, C), lambda i, who_ref: (who_ref[0], i, 0))], make_grad)

    def from_outer(self, l, who, sc_t, dmod):
        tr, C = self.tr, self.C

        def make_grad(me, s_ref, d_ref):
            g = s_ref[:, 0:1] * d_ref[0:1, :]
            for b in range(1, N_DEV):
                g = g + s_ref[:, b:b + 1] * d_ref[b:b + 1, :]
            return g

        self._layer(l, who, [sc_t, dmod], [pl.BlockSpec((tr, N_DEV), lambda i, who_ref: (i, 0)),
                                           pl.BlockSpec((N_DEV, C), lambda i, who_ref: (0, 0))], make_grad)


def _adamw_flat(w, g, m, v, name):
    R, C = w.shape
    tr = _row_tile(R, C, 128 * 1024)

    def body(w_ref, g_ref, m_ref, v_ref, d_ref, m2_ref, v2_ref):
        delta, m2, v2 = _adamw_math(w_ref[...], g_ref[...], m_ref[...], v_ref[...])
        d_ref[...] = delta
        m2_ref[...] = m2
        v2_ref[...] = v2

    spec = pl.BlockSpec((tr, C), lambda i: (i, 0))
    return pl.pallas_call(
        body, name=name, grid=(R // tr,), in_specs=[spec] * 4, out_specs=[spec] * 3,
        out_shape=[_sds((R, C), F32)] * 3, compiler_params=_params(("parallel",)),
    )(w, g, m, v)


def _pack(arrays):
    flat = jnp.concatenate([a.reshape(-1).astype(F32) for a in arrays])
    pad = (-flat.shape[0]) % (64 * BLK)
    if pad:
        flat = jnp.concatenate([flat, jnp.zeros((pad,), F32)])
    return flat.reshape(-1, BLK)


def _unpack(buf, shapes, lead=()):
    sizes = [int(math.prod(s)) for s in shapes]
    out, off = [], 0
    if all(n % BLK == 0 for n in sizes):
        for s, n in zip(shapes, sizes):
            out.append(buf[..., off // BLK:(off + n) // BLK, :].reshape(tuple(lead) + tuple(s)))
            off += n
        return out
    flat = buf.reshape(tuple(lead) + (-1,))
    for s, n in zip(shapes, sizes):
        out.append(flat[..., off:off + n].reshape(tuple(lead) + tuple(s)))
        off += n
    return out


def _row(vec):
    return vec.reshape(1, -1)


def _shard_of(full, axis, me, size):
    return lax.dynamic_slice_in_dim(full, me * size, size, axis=axis)


def kernel(x, c, ada_w, ada_b, norm_g, mlp_w1, mlp_w2, gmlp_w_in, gmlp_ln_g, gmlp_ln_b, gmlp_ws, gmlp_bs, gmlp_w_out, kv_norm_g, kv_ada_w, kv_ada_b, w_kv, k_norm_g, w_f, b_f, attn_wq, q_norm_g, attn_wo, loss_target, m_ada_w, m_ada_b, m_norm_g, m_mlp_w1, m_mlp_w2, m_gmlp_w_in, m_gmlp_ln_g, m_gmlp_ln_b, m_gmlp_ws, m_gmlp_bs, m_gmlp_w_out, m_kv_norm_g, m_kv_ada_w, m_kv_ada_b, m_w_kv, m_k_norm_g, m_w_f, m_b_f, m_attn_wq, m_q_norm_g, m_attn_wo, v_ada_w, v_ada_b, v_norm_g, v_mlp_w1, v_mlp_w2, v_gmlp_w_in, v_gmlp_ln_g, v_gmlp_ln_b, v_gmlp_ws, v_gmlp_bs, v_gmlp_w_out, v_kv_norm_g, v_kv_ada_w, v_kv_ada_b, v_w_kv, v_k_norm_g, v_w_f, v_b_f, v_attn_wq, v_q_norm_g, v_attn_wo):
    given = dict(locals())
    weights = {n: given[n] for n in WEIGHT_NAMES}
    mom_m = {n: given["m_" + n] for n in WEIGHT_NAMES}
    mom_v = {n: given["v_" + n] for n in WEIGHT_NAMES}

    me = _my_index()
    T, D = x.shape[1], x.shape[2]
    depth = ada_w.shape[0]
    n_a = gmlp_w_in.shape[0]
    n_heads = b_f.shape[0]
    G = gmlp_ws.shape[1]
    Hh = gmlp_ln_g.shape[1] * N_DEV
    mod_cols = ada_w.shape[2]
    kv_cols = kv_ada_w.shape[1]
    x0 = x.reshape(T, D)
    target = loss_target.reshape(T, D)

    small_in = [c, norm_g, gmlp_ln_g, gmlp_ln_b, w_f]
    small_shapes = [a.shape for a in small_in]
    got = _all_to_all(_pack(small_in), "gather_small_inputs", bcast=True)
    c_all, norm_g_sh, ln_g_sh, ln_b_sh, w_f_sh = _unpack(got, small_shapes, lead=(N_DEV,))
    c_all = c_all.reshape(N_DEV, D)
    norm_g_full = jnp.moveaxis(norm_g_sh, 0, 2).reshape(depth, 2, D)
    ln_g_full = jnp.moveaxis(ln_g_sh, 0, 1).reshape(n_a, Hh)
    ln_b_full = jnp.moveaxis(ln_b_sh, 0, 1).reshape(n_a, Hh)
    w_f_full = w_f_sh.reshape(D, n_heads)
    w_f_pad = jnp.pad(w_f_full, ((0, 0), (0, BLK - n_heads))).astype(BF16)
    b_f_pad = jnp.pad(b_f, (0, BLK - n_heads)).reshape(1, BLK)

    mod_parts = []
    for l in range(depth):
        bias = _shard_of(ada_b[l], 0, me, mod_cols).reshape(1, mod_cols)
        mod_parts.append(_mods(c_all, ada_w, l, bias, f"mods_l{l}"))
    kv_bias = _shard_of(kv_ada_b, 0, me, kv_cols).reshape(1, kv_cols)
    mod_parts.append(_mods(c_all, kv_ada_w.reshape(1, D, kv_cols), 0, kv_bias, "mods_kv"))
    mods_mine = jnp.concatenate(mod_parts, axis=1)
    mod_width = mods_mine.shape[1]
    mods_pack = jnp.pad(mods_mine, ((0, 0), (0, (-mod_width) % (8 * BLK)))).reshape(N_DEV, -1, BLK)
    mods_landed = _all_to_all(mods_pack, "exchange_mods")
    mods_got = mods_landed.reshape(N_DEV, -1)[:, :mod_width]
    mods = []
    for l in range(depth):
        mods.append(mods_got[:, l * mod_cols:(l + 1) * mod_cols].reshape(N_MOD, D))
    kv_mod = mods_got[:, depth * mod_cols:].reshape(2, D)
    silu_all = c_all / (1.0 + jnp.exp(-c_all))

    assert 1 <= n_a < depth
    who = me.astype(jnp.int32).reshape(1)
    big = {"mlp_w1": mlp_w1, "mlp_w2": mlp_w2, "gmlp_w_in": gmlp_w_in, "gmlp_w_out": gmlp_w_out,
           "w_kv": w_kv.reshape((1,) + w_kv.shape), "attn_wq": attn_wq, "attn_wo": attn_wo}
    groups = [[("gmlp_w_in", 0)], [("gmlp_w_out", 0), ("mlp_w1", 0), ("mlp_w2", 0)]]
    for l in range(1, depth):
        if l < n_a:
            groups.append([("gmlp_w_in", l), ("gmlp_w_out", l), ("mlp_w1", l), ("mlp_w2", l)])
        else:
            first = [("w_kv", 0)] if l == n_a else []
            groups.append(first + [("attn_wq", l - n_a), ("attn_wo", l - n_a), ("mlp_w1", l), ("mlp_w2", l)])
    tokens = []

    def behind_starts():
        out = tuple(tokens)
        tokens.clear()
        return out

    over_ici, placed = {}, {}

    def gather_begin(gi, deps=()):
        over_ici[gi] = _gather_start(placed[gi], 1, f"gather_g{gi}_ici_start", deps=deps)
        tokens.append(over_ici[gi][3])

    for gi, grp in enumerate(groups):
        behind = (mods_landed,) if gi == 0 else (over_ici[0][3],)
        placed[gi] = [_place_shard(big[n], l, who, f"place_{n}_l{l}", deps=behind) for n, l in grp]
        gather_begin(gi, deps=tuple(tokens[-1:]))
    to_sibling = {}
    gw = {}

    def gather_forward(gi, after):
        landed = _gather_wait(over_ici[gi], 1, after, f"gather_g{gi}_ici_wait")
        to_sibling[gi] = _gather_start(landed, 2, f"gather_g{gi}_d2d_start")
        tokens.append(to_sibling[gi][3])

    def gather_finish(gi, after):
        for key, arr in zip(groups[gi], _gather_wait(to_sibling[gi], 2, after, f"gather_g{gi}_d2d_wait")):
            gw[key] = arr

    tkk = _attn_tile(T)

    saved = []
    xs = x0
    pending = None
    kv = None
    for l in range(depth):
        sh1, sc1, g1, sh2, sc2, g2 = [_row(mods[l][t]) for t in range(N_MOD)]
        ng1, ng2 = _row(norm_g_full[l, 0]), _row(norm_g_full[l, 1])
        st = dict(sc1=sc1, g1=g1, sc2=sc2, g2=g2, ng1=ng1, ng2=ng2)
        if pending is None:
            h1 = _norm_mod(xs, ng1, sc1, sh1, f"norm1_l{l}")
            gather_forward(0, h1)
            gather_finish(0, h1)
        else:
            gather_finish(l + 1, pending[0])
            xs, h1 = _res_norm_mod(xs, pending[0], pending[1], ng1, sc1, sh1, f"res_norm1_l{l}")
        st["x_in"], st["h1"] = xs, h1
        if l < n_a:
            a = l
            zpre = _mm(h1, gw["gmlp_w_in", a], bmode="col", out_dtypes=(BF16,), name=f"gmlp_in_l{l}",
                       deps=behind_starts())
            if l == 0:
                gather_forward(1, zpre)
            bs_t = gmlp_bs[a].T
            p = _gmlp_mid_fwd(zpre, _row(ln_g_full[a]), _row(ln_b_full[a]), gmlp_ws[a], bs_t, f"gmlp_mid_l{l}")
            if l == 0:
                gather_finish(1, p)
            y = _mm(p, gw["gmlp_w_out", a], bmode="row", name=f"gmlp_out_l{l}", deps=behind_starts())
            st.update(zpre=zpre, p=p)
        else:
            if kv is None:
                kv_ng, kv_sh, kv_sc = _row(kv_norm_g), _row(kv_mod[0]), _row(kv_mod[1])
                hkv = _norm_mod(xs, kv_ng, kv_sc, kv_sh, "norm_kv")
                kvp = _mm(hkv, gw["w_kv", 0], bmode="col", name="kv_proj", deps=behind_starts())
                kk, vv = _head_norm(kvp, _row(k_norm_g), n_heads, "k_norm", tail=True)
                fl = _mm(hkv, w_f_pad, name="gate_logits")
                fcum = _fcum_fwd(fl, b_f_pad, "fcum")
                fk = fcum[:, :n_heads].T.reshape(n_heads, T // tkk, 1, tkk)
                kv = dict(x=xs, hkv=hkv, kvp=kvp, k=kk, v=vv, fl=fl, fcum=fcum, fk=fk, ng=kv_ng, sc=kv_sc)
            bl = l - n_a
            qp = _mm(h1, gw["attn_wq", bl], bmode="row", name=f"q_proj_l{l}", deps=behind_starts())
            q = _head_norm(qp, _row(q_norm_g[bl]), n_heads, f"q_norm_l{l}")[0]
            o, o32, lse = _attn_fwd(q, kv["k"], kv["v"], kv["fk"], n_heads, f"attn_l{l}")
            y = _mm(o, gw["attn_wo", bl], bmode="row", name=f"attn_out_l{l}")
            st.update(qp=qp, q=q, o=o, o32=o32, lse=lse)
        xs, h2 = _res_norm_mod(xs, y, g1, ng2, sc2, sh2, f"res_norm2_l{l}")
        a_pre, s_act = _mm(h2, gw["mlp_w1", l], bmode="col", out_dtypes=(BF16, BF16), epilogue=_relu2_epilogue,
                           name=f"mlp_up_l{l}")
        if l + 1 < depth:
            gather_forward(l + 2, a_pre)
        mo = _mm(s_act, gw["mlp_w2", l], bmode="row", name=f"mlp_down_l{l}", deps=behind_starts())
        st.update(y=y, x_mid=xs, h2=h2, a_pre=a_pre, s=s_act, m=mo)
        saved.append(st)
        pending = (mo, g2)

    dx, dm, loss_row, dg2 = _res_loss(xs, pending[0], pending[1], target, "loss")
    loss = lax.psum(loss_row[0, 0], ("x", "y", "c"))

    started = {}

    def scatter(dw_slabs, key, idx):
        started[(key, idx)] = _a2a_start(dw_slabs, f"scatter_{key}_l{idx}_start")
        tokens.append(started[(key, idx)][4])

    d_mod = [None] * depth
    d_norm_g = [None] * depth
    d_ln_g, d_ln_b, d_ws, d_bs = [None] * n_a, [None] * n_a, [None] * n_a, [None] * n_a
    d_qg = [None] * (depth - n_a)
    dk_list, dv_list, dfk_list = [], [], []
    small = {}

    for l in reversed(range(depth)):
        st = saved[l]
        da = _mm(dm, gw["mlp_w2", l], tb=True, bmode="row", out_dtypes=(BF16,), epilogue=_relu2_bwd_epilogue,
                 extra=(st["a_pre"],), name=f"mlp_down_dx_l{l}")
        dw2 = _mm(st["s"], dm, ta=True, out_dtypes=(BF16,), name=f"mlp_down_dw_l{l}", deps=behind_starts())
        scatter(dw2.reshape(N_DEV, -1, D), "mlp_w2", l)
        dw1 = _mm(st["h2"], da, ta=True, out_mode="col", out_dtypes=(BF16,), name=f"mlp_up_dw_l{l}",
                  deps=behind_starts())
        scatter(dw1, "mlp_w1", l)
        dh2 = _mm(da, gw["mlp_w1", l], tb=True, bmode="col", out_dtypes=(BF16,), name=f"mlp_up_dx_l{l}",
                  deps=behind_starts())
        dx, dy, sums2, dg1 = _norm_mod_bwd(st["x_mid"], [dh2], dx, st["ng2"], st["sc2"], f"norm2_bwd_l{l}",
                                           gated=(st["y"], st["g1"]))
        if l < n_a:
            a = l
            dwo = _mm(st["p"], dy, ta=True, out_dtypes=(BF16,), name=f"gmlp_out_dw_l{l}", deps=behind_starts())
            scatter(dwo.reshape(N_DEV, -1, D), "gmlp_w_out", a)
            dp = _mm(dy, gw["gmlp_w_out", a], tb=True, bmode="row", out_dtypes=(BF16,), name=f"gmlp_out_dx_l{l}",
                     deps=behind_starts())
            dz, d_ws[a], dbs_t, dlg, dlb = _gmlp_mid_bwd(
                st["zpre"], dp, _row(ln_g_full[a]), _row(ln_b_full[a]), gmlp_ws[a],
                jnp.swapaxes(gmlp_ws[a], 1, 2), gmlp_bs[a].T, f"gmlp_mid_bwd_l{l}")
            d_bs[a], d_ln_g[a], d_ln_b[a] = dbs_t[:, :G].T, dlg[0], dlb[0]
            dwi = _mm(st["h1"], dz, ta=True, out_mode="col", out_dtypes=(BF16,), name=f"gmlp_in_dw_l{l}",
                      deps=behind_starts())
            scatter(dwi, "gmlp_w_in", a)
            dh1s = [_mm(dz, gw["gmlp_w_in", a], tb=True, bmode="col", out_dtypes=(BF16,), name=f"gmlp_in_dx_l{l}",
                        deps=behind_starts())]
        else:
            bl = l - n_a
            dwo = _mm(st["o"], dy, ta=True, out_dtypes=(BF16,), name=f"attn_out_dw_l{l}", deps=behind_starts())
            scatter(dwo.reshape(N_DEV, -1, D), "attn_wo", bl)
            do = _mm(dy, gw["attn_wo", bl], tb=True, bmode="row", out_dtypes=(BF16,), name=f"attn_out_dx_l{l}",
                     deps=behind_starts())
            dq, dk, dv, dfq, dfk = _attn_bwd(st["q"], kv["k"], kv["v"], st["o32"], do, st["lse"], kv["fk"],
                                             n_heads, f"attn_bwd_l{l}")
            dfk_list += [dfq, dfk]
            dk_list.append(dk)
            dv_list.append(dv)
            dqp, dqg = _head_norm_bwd(st["qp"], [dq], _row(q_norm_g[bl]), n_heads, f"q_norm_bwd_l{l}")
            d_qg[bl] = dqg[0]
            dwq = _mm(st["h1"], dqp, ta=True, out_dtypes=(BF16,), name=f"q_proj_dw_l{l}", deps=behind_starts())
            scatter(dwq.reshape(N_DEV, -1, D), "attn_wq", bl)
            dh1s = [_mm(dqp, gw["attn_wq", bl], tb=True, bmode="row", out_dtypes=(BF16,), name=f"q_proj_dx_l{l}",
                        deps=behind_starts())]
        below = (saved[l - 1]["m"], saved[l - 1]["g2"]) if l > 0 else None
        if below is None or l == n_a:
            dx, sums1 = _norm_mod_bwd(st["x_in"], dh1s, dx, st["ng1"], st["sc1"], f"norm1_bwd_l{l}")
        else:
            dx, dm_below, sums1, dg2_below = _norm_mod_bwd(st["x_in"], dh1s, dx, st["ng1"], st["sc1"],
                                                           f"norm1_bwd_l{l}", gated=below)
        d_mod[l] = jnp.stack([sums1[0], sums1[1], dg1[0], sums2[0], sums2[1], dg2[0]])
        d_norm_g[l] = jnp.stack([sums1[2], sums2[2]])
        if l == n_a:
            dkvp, dkg = _head_norm_bwd(kv["kvp"], dk_list, _row(k_norm_g), n_heads, "k_norm_bwd", tails=dv_list)
            dfc = [jnp.pad(d.reshape(n_heads, T).T, ((0, 0), (0, BLK - n_heads))) for d in dfk_list]
            dfl, dbf = _fcum_bwd(dfc, kv["fl"], b_f_pad, "fcum_bwd")
            dwkv = _mm(kv["hkv"], dkvp, ta=True, out_mode="col", out_dtypes=(BF16,), name="kv_proj_dw",
                       deps=behind_starts())
            scatter(dwkv, "w_kv", 0)
            dwf = _mm(kv["hkv"], dfl, ta=True, name="gate_logits_dw", deps=behind_starts())
            dh_a = _mm(dkvp, gw["w_kv", 0], tb=True, bmode="col", out_dtypes=(BF16,), name="kv_proj_dx")
            dh_b = _mm(dfl, w_f_pad, tb=True, out_dtypes=(BF16,), name="gate_logits_dx")
            dx, dm_below, sums_kv, dg2_below = _norm_mod_bwd(kv["x"], [dh_a, dh_b], dx, kv["ng"], kv["sc"],
                                                             "norm_kv_bwd", gated=below)
            small.update(d_kv_mod=jnp.stack([sums_kv[0], sums_kv[1]]), d_kv_norm_g=sums_kv[2], d_k_norm_g=dkg[0],
                         d_w_f=dwf[:, :n_heads], d_b_f=dbf[0, :n_heads])
        if l > 0:
            dm, dg2 = dm_below, dg2_below
        if l == 1:
            hi_contrib = [jnp.stack(d_mod[1:]).reshape(depth - 1, N_MOD * D), small["d_kv_mod"].reshape(-1)]
            hi_started = _a2a_start(_pack(hi_contrib), "gather_mod_grads_hi_start", bcast=True)
            tokens.append(hi_started[4])

    grad_x = dx.reshape(x.shape)

    lo_contrib = [d_mod[0].reshape(1, N_MOD * D)]
    rest_contrib = [jnp.stack(d_norm_g), jnp.stack(d_ln_g), jnp.stack(d_ln_b), jnp.stack(d_ws), jnp.stack(d_bs),
                    small["d_kv_norm_g"], small["d_k_norm_g"], small["d_w_f"], small["d_b_f"], jnp.stack(d_qg)]
    hi_shapes, lo_shapes = [a.shape for a in hi_contrib], [a.shape for a in lo_contrib]
    rest_shapes = [a.shape for a in rest_contrib]
    lo_started = _a2a_start(_pack(lo_contrib), "gather_mod_grads_lo_start", bcast=True, deps=behind_starts())
    rest_started = _a2a_start(_pack(rest_contrib), "gather_small_grads_start", bcast=True, deps=(lo_started[4],))
    grads, deltas, new_m, new_v = {}, {}, {}, {}

    def stack_of(n):
        lead = () if weights[n].ndim == 3 else (1,)
        return _AdamStack(*[a.reshape(lead + a.shape) for a in (weights[n], mom_m[n], mom_v[n])], f"adamw_{n}")

    def results_of(n, stack):
        grads[n], deltas[n], new_m[n], new_v[n] = [a.reshape(weights[n].shape) for a in stack.outs]

    sc_t = silu_all.T
    ada_stack, kv_ada_stack = stack_of("ada_w"), stack_of("kv_ada_w")
    is_me = (jnp.arange(N_DEV) == me).reshape(N_DEV, 1, 1)

    def landed_mods(started_pack, shapes, after, name):
        sent, land = _a2a_wait(started_pack, after, f"gather_{name}_wait", bcast=True)
        everyone = _unpack(jnp.where(is_me, sent[None], land), shapes, lead=(N_DEV,))
        return everyone, _unpack(_sum_slabs(land, sent, who, f"sum_{name}"), shapes)

    stacks = {n: stack_of(n) for n in ("mlp_w1", "mlp_w2", "gmlp_w_in", "gmlp_w_out", "attn_wq", "attn_wo", "w_kv")}
    after = rest_started[4]
    for l in reversed(range(depth)):
        if l == n_a - 1:
            (dmod_hi, dkvmod_all), (g_ada_b_hi, g_kv_ada_b) = landed_mods(hi_started, hi_shapes, after, "mod_grads_hi")
            for up in range(1, depth):
                ada_stack.from_outer(up, who, sc_t, _shard_of(dmod_hi[:, up - 1], 1, me, mod_cols))
            kv_ada_stack.from_outer(0, who, sc_t, _shard_of(dkvmod_all, 1, me, kv_cols))
            after = ada_stack.outs[0]
        if l == 0:
            (dmod_lo,), (g_ada_b_lo,) = landed_mods(lo_started, lo_shapes, after, "mod_grads_lo")
            ada_stack.from_outer(0, who, sc_t, _shard_of(dmod_lo[:, 0], 1, me, mod_cols))
            after = ada_stack.outs[0]
        keys = [("mlp_w2", l), ("mlp_w1", l)]
        keys += [("gmlp_w_out", l), ("gmlp_w_in", l)] if l < n_a else [("attn_wo", l - n_a), ("attn_wq", l - n_a)]
        keys += [("w_kv", 0)] if l == n_a else []
        landed = {}
        for key in keys:
            sent, land = _a2a_wait(started[key], after, f"scatter_{key[0]}_l{key[1]}_wait")
            landed[key] = (land, sent)
        for key in keys:
            stacks[key[0]].from_parts(key[1], who, *landed[key])
            after = stacks[key[0]].outs[0]
    for n, stack in stacks.items():
        results_of(n, stack)
    results_of("ada_w", ada_stack)
    results_of("kv_ada_w", kv_ada_stack)
    g_ada_b = jnp.concatenate([g_ada_b_lo, g_ada_b_hi], axis=0)

    rest_sent, rest_land = _a2a_wait(rest_started, after, "gather_small_grads_wait", bcast=True)
    (g_norm_g_full, g_ln_g_full, g_ln_b_full, g_ws, g_bs, g_kv_norm_g, g_k_norm_g, g_w_f_full, g_b_f,
     g_q_norm_g) = _unpack(_sum_slabs(rest_land, rest_sent, who, "sum_small_grads"), rest_shapes)
    small_grads = {
        "ada_b": g_ada_b, "kv_ada_b": g_kv_ada_b.reshape(kv_ada_b.shape),
        "norm_g": _shard_of(g_norm_g_full, 2, me, norm_g.shape[2]),
        "gmlp_ln_g": _shard_of(g_ln_g_full, 1, me, gmlp_ln_g.shape[1]),
        "gmlp_ln_b": _shard_of(g_ln_b_full, 1, me, gmlp_ln_b.shape[1]),
        "gmlp_ws": g_ws, "gmlp_bs": g_bs, "kv_norm_g": g_kv_norm_g, "k_norm_g": g_k_norm_g,
        "w_f": _shard_of(g_w_f_full, 0, me, w_f.shape[0]), "b_f": g_b_f, "q_norm_g": g_q_norm_g,
    }
    small_names = list(small_grads)
    small_w_shapes = [weights[n].shape for n in small_names]
    d_pack, m_pack, v_pack = _adamw_flat(_pack([weights[n] for n in small_names]),
                                         _pack([small_grads[n] for n in small_names]),
                                         _pack([mom_m[n] for n in small_names]), _pack([mom_v[n] for n in small_names]),
                                         "adamw_small")
    grads.update(small_grads)
    deltas.update(zip(small_names, _unpack(d_pack, small_w_shapes)))
    new_m.update(zip(small_names, _unpack(m_pack, small_w_shapes)))
    new_v.update(zip(small_names, _unpack(v_pack, small_w_shapes)))

    return (loss, grad_x, *[grads[n] for n in WEIGHT_NAMES], *[deltas[n] for n in WEIGHT_NAMES],
            *[new_m[n] for n in WEIGHT_NAMES], *[new_v[n] for n in WEIGHT_NAMES])
```

```python
import functools
import math

import jax
import jax.numpy as jnp
from jax import lax
from jax.experimental import pallas as pl
from jax.experimental.pallas import tpu as pltpu

F32 = jnp.float32
BF16 = jnp.bfloat16
N_DEV = 8
EPS = 1e-6
CHUNK = 64
BLK = 128
N_MOD = 6
ADAM_LR = 0.001
ADAM_B1 = 0.9
ADAM_B2 = 0.999
ADAM_EPS = 1e-08
ADAM_WD = 0.01
ADAM_STEP = 10
VMEM_LIMIT_BYTES = 56 * 2 ** 20
NEG_BIG = -1e30
WEIGHT_NAMES = ['ada_w', 'ada_b', 'norm_g', 'mlp_w1', 'mlp_w2', 'gmlp_w_in', 'gmlp_ln_g', 'gmlp_ln_b', 'gmlp_ws',
                'gmlp_bs', 'gmlp_w_out', 'kv_norm_g', 'kv_ada_w', 'kv_ada_b', 'w_kv', 'k_norm_g', 'w_f', 'b_f',
                'attn_wq', 'q_norm_g', 'attn_wo']
MESH = pl.DeviceIdType.MESH


def _params(sem):
    return pltpu.CompilerParams(dimension_semantics=sem, vmem_limit_bytes=VMEM_LIMIT_BYTES)


def _tile(n, cap, unit=128):
    if n <= cap:
        return n
    t = (cap // unit) * unit
    while t > unit and n % t:
        t -= unit
    assert n % t == 0, (n, cap, unit)
    return t


def _my_index():
    return 4 * lax.axis_index("x") + 2 * lax.axis_index("y") + lax.axis_index("c")


def _all_to_all(x, name, bcast=False):
    slab = x.shape if bcast else x.shape[1:]

    def body(x_ref, o_ref, send_sems, recv_sems, local_sem):
        me = _my_index()

        def src(j):
            return x_ref if bcast else x_ref.at[j]

        mine = pltpu.make_async_copy(src(me), o_ref.at[me], local_sem)
        mine.start()
        sends = []
        for d in range(1, N_DEV):
            peer = (me + d) % N_DEV
            cp = pltpu.make_async_remote_copy(
                src_ref=src(peer), dst_ref=o_ref.at[me],
                send_sem=send_sems.at[d - 1], recv_sem=recv_sems.at[d - 1],
                device_id=(peer // 4, (peer // 2) % 2, peer % 2), device_id_type=MESH)
            cp.start()
            sends.append(cp)
        for d in range(1, N_DEV):
            frm = (me + N_DEV - d) % N_DEV
            pltpu.make_async_remote_copy(
                src_ref=src(frm), dst_ref=o_ref.at[frm],
                send_sem=send_sems.at[d - 1], recv_sem=recv_sems.at[d - 1],
                device_id=(frm // 4, (frm // 2) % 2, frm % 2), device_id_type=MESH).wait_recv()
        for cp in sends:
            cp.wait_send()
        mine.wait()

    return pl.pallas_call(
        body, name=name,
        out_shape=jax.ShapeDtypeStruct((N_DEV,) + tuple(slab), x.dtype),
        in_specs=[pl.BlockSpec(memory_space=pl.ANY)],
        out_specs=pl.BlockSpec(memory_space=pl.ANY),
        scratch_shapes=[pltpu.SemaphoreType.DMA((N_DEV - 1,)), pltpu.SemaphoreType.DMA((N_DEV - 1,)),
                        pltpu.SemaphoreType.DMA],
        compiler_params=pltpu.CompilerParams(has_side_effects=True),
    )(x)


_HBM = pl.BlockSpec(memory_space=pltpu.HBM)
_SEM = pl.BlockSpec(memory_space=pltpu.SEMAPHORE)
_ANY = pl.BlockSpec(memory_space=pl.ANY)
_DATAFLOW = pltpu.SideEffectType.DATAFLOW_SIDE_EFFECTING


def _a2a_peer_copy(x_ref, land_ref, send_sems, recv_sems, d, me, incoming, bcast):
    peer = (me + N_DEV - d) % N_DEV if incoming else (me + d) % N_DEV
    return pltpu.make_async_remote_copy(
        src_ref=x_ref if bcast else x_ref.at[peer], dst_ref=land_ref.at[peer if incoming else me],
        send_sem=send_sems.at[d - 1], recv_sem=recv_sems.at[d - 1],
        device_id=(peer // 4, (peer // 2) % 2, peer % 2), device_id_type=MESH)


def _a2a_start(x, name, bcast=False, deps=()):
    land_shape = ((N_DEV,) + tuple(x.shape)) if bcast else x.shape
    n_dep = len(deps)

    def body(x_ref, land_ref, *rest):
        send_sems, recv_sems, token = rest[n_dep], rest[n_dep + 1], rest[-1]
        me = _my_index()
        for d in range(1, N_DEV):
            _a2a_peer_copy(x_ref, land_ref, send_sems, recv_sems, d, me, False, bcast).start()
        token[...] = jnp.zeros_like(token)

    return pl.pallas_call(
        body, name=name,
        out_shape=(pltpu.SemaphoreType.DMA((N_DEV - 1,)), pltpu.SemaphoreType.DMA((N_DEV - 1,)),
                   pltpu.HBM(x.shape, x.dtype), pltpu.HBM(land_shape, x.dtype), jax.ShapeDtypeStruct((8, BLK), F32)),
        in_specs=(_HBM, _HBM, *[_ANY for _ in deps]),
        out_specs=(_SEM, _SEM, _HBM, _HBM, pl.BlockSpec(memory_space=pltpu.VMEM)),
        input_output_aliases={0: 2, 1: 3},
        compiler_params=pltpu.CompilerParams(has_side_effects=_DATAFLOW),
    )(pltpu.with_memory_space_constraint(x, pltpu.HBM),
      pltpu.with_memory_space_constraint(lax.empty(land_shape, x.dtype), pltpu.HBM), *deps)


def _a2a_wait(started, after, name, bcast=False):
    send_sems, recv_sems, x_thru, land_thru, _ = started

    def body(x_ref, land_ref, send_sems, recv_sems, after_ref, x_dead, land_out):
        me = _my_index()
        for d in range(1, N_DEV):
            _a2a_peer_copy(x_ref, land_ref, send_sems, recv_sems, d, me, False, bcast).wait_send()
        for d in range(1, N_DEV):
            _a2a_peer_copy(x_ref, land_ref, send_sems, recv_sems, d, me, True, bcast).wait_recv()

    return pl.pallas_call(
        body, name=name,
        out_shape=(pltpu.HBM(x_thru.shape, x_thru.dtype), pltpu.HBM(land_thru.shape, land_thru.dtype)),
        in_specs=(_HBM, _HBM, _SEM, _SEM, _ANY), out_specs=(_HBM, _HBM),
        input_output_aliases={0: 0, 1: 1},
        compiler_params=pltpu.CompilerParams(has_side_effects=_DATAFLOW),
    )(x_thru, land_thru, send_sems, recv_sems, after)


def _place_shard(w, layer, who, name, deps=()):
    _, R, C = w.shape
    tr = _row_tile(R, C)

    def body(who_ref, w_ref, *rest):
        rest[-1][...] = w_ref[...].astype(BF16)

    return pl.pallas_call(
        body, name=name,
        grid_spec=pltpu.PrefetchScalarGridSpec(
            num_scalar_prefetch=1, grid=(R // tr,),
            in_specs=[pl.BlockSpec((None, tr, C), lambda i, who_ref: (layer, i, 0))] + [_ANY for _ in deps],
            out_specs=pl.BlockSpec((None, None, tr, C), lambda i, who_ref: (who_ref[0], 0, i, 0))),
        out_shape=_sds((N_DEV, 1, R, C), BF16),
        compiler_params=_params(("parallel",)),
    )(who, w, *deps)


def _gather_copies(land_ref, send_sems, recv_sems, base, phase, incoming):
    cx, cy, cc = lax.axis_index("x"), lax.axis_index("y"), lax.axis_index("c")
    sibling = (cx, cy, 1 - cc)
    chips = [(1 - cx, cy), (cx, 1 - cy), (1 - cx, 1 - cy)]
    if phase == 1:
        out = [((cx, cy, cc), sibling)] + [((cx, cy, cc), (*chip, cc)) for chip in chips]
        inc = [(sibling, sibling)] + [((*chip, cc), (*chip, cc)) for chip in chips]
    else:
        out = [((*chip, cc), sibling) for chip in chips]
        inc = [((*chip, 1 - cc), sibling) for chip in chips]
    copies = []
    for k, (block, peer) in enumerate(inc if incoming else out):
        slab = land_ref.at[4 * block[0] + 2 * block[1] + block[2]]
        copies.append(pltpu.make_async_remote_copy(
            src_ref=slab, dst_ref=slab, send_sem=send_sems.at[base + k], recv_sem=recv_sems.at[base + k],
            device_id=peer, device_id_type=MESH))
    return copies


def _gather_start(lands, phase, name, deps=()):
    n, per, n_dep = len(lands), (4 if phase == 1 else 3), len(deps)

    def body(*refs):
        land_refs, send_sems, recv_sems, token = refs[:n], refs[n + n_dep], refs[n + n_dep + 1], refs[-1]
        for a, land_ref in enumerate(land_refs):
            for cp in _gather_copies(land_ref, send_sems, recv_sems, a * per, phase, False):
                cp.start()
        token[...] = jnp.zeros_like(token)

    outs = pl.pallas_call(
        body, name=name,
        out_shape=(pltpu.SemaphoreType.DMA((n * per,)), pltpu.SemaphoreType.DMA((n * per,)),
                   *[pltpu.HBM(x.shape, x.dtype) for x in lands], jax.ShapeDtypeStruct((8, BLK), F32)),
        in_specs=(*[_HBM for _ in lands], *[_ANY for _ in deps]),
        out_specs=(_SEM, _SEM, *[_HBM for _ in lands], pl.BlockSpec(memory_space=pltpu.VMEM)),
        input_output_aliases={a: 2 + a for a in range(n)},
        compiler_params=pltpu.CompilerParams(has_side_effects=_DATAFLOW),
    )(*[pltpu.with_memory_space_constraint(x, pltpu.HBM) for x in lands], *deps)
    return outs[0], outs[1], list(outs[2:2 + n]), outs[-1]


def _gather_wait(started, phase, after, name):
    send_sems, recv_sems, lands, _ = started
    n, per = len(lands), (4 if phase == 1 else 3)

    def body(*refs):
        land_refs, send_sems, recv_sems = refs[:n], refs[n], refs[n + 1]
        for a, land_ref in enumerate(land_refs):
            for cp in _gather_copies(land_ref, send_sems, recv_sems, a * per, phase, False):
                cp.wait_send()
            for cp in _gather_copies(land_ref, send_sems, recv_sems, a * per, phase, True):
                cp.wait_recv()

    outs = pl.pallas_call(
        body, name=name,
        out_shape=tuple(pltpu.HBM(x.shape, x.dtype) for x in lands),
        in_specs=(*[_HBM for _ in lands], _SEM, _SEM, _ANY), out_specs=tuple(_HBM for _ in lands),
        input_output_aliases={a: a for a in range(n)},
        compiler_params=pltpu.CompilerParams(has_side_effects=_DATAFLOW),
    )(*lands, send_sems, recv_sems, after)
    return list(outs)


def _mm(a, b, *, name, ta=False, tb=False, bmode="plain", layer=0, out_mode="plain", out_dtypes=(F32,),
        epilogue=None, extra=(), caps=(1024, 1024, 2048), deps=()):
    if ta:
        K, M = a.shape
    else:
        M, K = a.shape
    n_unit = k_unit = None
    if bmode == "plain":
        N, Kb = (b.shape if tb else b.shape[::-1])
    elif bmode == "col":
        _, _, Kw, Ns = b.shape
        if tb:
            N, Kb, k_unit = Kw, N_DEV * Ns, Ns
        else:
            N, Kb, n_unit = N_DEV * Ns, Kw, Ns
    else:
        _, _, Ks, Nw = b.shape
        if tb:
            N, Kb, n_unit = N_DEV * Ks, Nw, Ks
        else:
            N, Kb, k_unit = Nw, N_DEV * Ks, Ks
    assert K == Kb, (name, a.shape, b.shape)
    if out_mode == "col":
        assert n_unit is None
        n_unit = N // N_DEV
    tm = _tile(M, caps[0])
    tn = _tile(n_unit or N, caps[1])
    span = 1
    if k_unit and 2 * k_unit <= caps[2]:
        while 2 * span * k_unit <= caps[2] and N_DEV % (2 * span) == 0:
            span *= 2
        tk = span * k_unit
    else:
        tk = _tile(k_unit or K, caps[2])
    nk = K // tk
    npb = (n_unit // tn) if n_unit else None
    kpb = (k_unit // tk) if (k_unit and span == 1) else None
    grid = (M // tm, N // tn, nk)

    a_spec = pl.BlockSpec((tk, tm), lambda i, j, k: (k, i)) if ta else pl.BlockSpec((tm, tk), lambda i, j, k: (i, k))
    if bmode == "plain":
        b_spec = (pl.BlockSpec((tn, tk), lambda i, j, k: (j, k)) if tb
                  else pl.BlockSpec((tk, tn), lambda i, j, k: (k, j)))
    elif bmode == "col":
        if tb and span > 1:
            b_spec = pl.BlockSpec((span, None, tn, k_unit), lambda i, j, k: (k, layer, j, 0))
        elif tb:
            b_spec = pl.BlockSpec((None, None, tn, tk), lambda i, j, k: (k // kpb, layer, j, k % kpb))
        else:
            b_spec = pl.BlockSpec((None, None, tk, tn), lambda i, j, k: (j // npb, layer, k, j % npb))
    else:
        if tb:
            b_spec = pl.BlockSpec((None, None, tn, tk), lambda i, j, k: (j // npb, layer, j % npb, k))
        elif span > 1:
            b_spec = pl.BlockSpec((span, None, k_unit, tn), lambda i, j, k: (k, layer, 0, j))
        else:
            b_spec = pl.BlockSpec((None, None, tk, tn), lambda i, j, k: (k // kpb, layer, k % kpb, j))
    mn_spec = pl.BlockSpec((tm, tn), lambda i, j, k: (i, j))
    if out_mode == "col":
        o_specs = [pl.BlockSpec((None, tm, tn), lambda i, j, k: (j // npb, i, j % npb))]
        o_shapes = [jax.ShapeDtypeStruct((N_DEV, M, N // N_DEV), out_dtypes[0])]
    else:
        o_specs = [mn_spec for _ in out_dtypes]
        o_shapes = [jax.ShapeDtypeStruct((M, N), dt) for dt in out_dtypes]
    dims = (((0 if ta else 1,), (1 if tb else 0,)), ((), ()))
    n_extra, n_out, n_dep = len(extra), len(out_dtypes), len(deps)

    def body(a_ref, b_ref, *rest):
        extra_refs, out_refs = rest[:n_extra], rest[n_extra + n_dep:n_extra + n_dep + n_out]
        k = pl.program_id(2)

        def product():
            if span == 1:
                return lax.dot_general(a_ref[...].astype(BF16), b_ref[...].astype(BF16), dims,
                                       preferred_element_type=F32)
            if not tb:
                return lax.dot_general(a_ref[...].astype(BF16), b_ref[...].reshape(tk, tn).astype(BF16), dims,
                                       preferred_element_type=F32)
            out = None
            for s in range(span):
                part = lax.dot_general(a_ref[:, s * k_unit:(s + 1) * k_unit].astype(BF16), b_ref[s].astype(BF16),
                                       dims, preferred_element_type=F32)
                out = part if out is None else out + part
            return out

        def finish(acc):
            outs = (acc,) if epilogue is None else epilogue(acc, *[r[...] for r in extra_refs])
            for o_ref, val in zip(out_refs, outs):
                o_ref[...] = val.astype(o_ref.dtype)

        if nk == 1:
            finish(product())
            return
        acc_ref = rest[-1]

        @pl.when(k == 0)
        def _():
            acc_ref[...] = product()

        if nk > 2:
            @pl.when(jnp.logical_and(k > 0, k < nk - 1))
            def _():
                acc_ref[...] += product()

        @pl.when(k == nk - 1)
        def _():
            finish(acc_ref[...] + product())

    outs = pl.pallas_call(
        body, name=name, grid=grid,
        in_specs=[a_spec, b_spec] + [mn_spec for _ in extra] + [_ANY for _ in deps],
        out_specs=o_specs, out_shape=o_shapes,
        scratch_shapes=[pltpu.VMEM((tm, tn), F32)] if nk > 1 else [],
        compiler_params=_params(("parallel", "parallel", "arbitrary")),
    )(a, b, *extra, *deps)
    return outs[0] if n_out == 1 else outs


def _relu2_epilogue(acc):
    r = jnp.maximum(acc, 0.0)
    return acc, r * r


def _relu2_bwd_epilogue(acc, a_pre):
    return (acc * (2.0 * jnp.maximum(a_pre.astype(F32), 0.0)),)


def _rowcall(body, *, name, tr, row_ins, full_ins=(), row_outs=(), acc_outs=(), scratch=(), reverse=False):
    T = row_ins[0].shape[0]
    nb = T // tr
    rmap = (lambda i: (nb - 1 - i, 0)) if reverse else (lambda i: (i, 0))

    def full_spec(shape):
        nd = len(shape)
        return pl.BlockSpec(tuple(shape), lambda i: (0,) * nd)

    in_specs = [pl.BlockSpec((tr, a.shape[1]), rmap) for a in row_ins] + [full_spec(a.shape) for a in full_ins]
    out_specs = [pl.BlockSpec((tr, s.shape[1]), rmap) for s in row_outs] + [full_spec(s.shape) for s in acc_outs]
    outs = pl.pallas_call(
        body, name=name, grid=(nb,), in_specs=in_specs, out_specs=out_specs,
        out_shape=list(row_outs) + list(acc_outs), scratch_shapes=list(scratch),
        compiler_params=_params(("arbitrary",)),
    )(*row_ins, *full_ins)
    return outs


def _sds(shape, dtype):
    return jax.ShapeDtypeStruct(tuple(shape), dtype)


def _row_tile(T, C, elems=512 * 1024):
    t = max(8, min(T, elems // C))
    p = 8
    while p * 2 <= t and T % (p * 2) == 0:
        p *= 2
    return p


def _norm_mod(x, ng, sc, sh, name):
    T, D = x.shape

    def body(x_ref, ng_ref, sc_ref, sh_ref, h_ref):
        xv = x_ref[...]
        r = lax.rsqrt(jnp.mean(xv * xv, axis=-1, keepdims=True) + EPS)
        h_ref[...] = (((xv * r) * ng_ref[...]) * (1.0 + sc_ref[...]) + sh_ref[...]).astype(BF16)

    return _rowcall(body, name=name, tr=_row_tile(T, D), row_ins=[x], full_ins=[ng, sc, sh],
                    row_outs=[_sds((T, D), BF16)])[0]


def _res_norm_mod(x, y, gate, ng, sc, sh, name):
    T, D = x.shape

    def body(x_ref, y_ref, g_ref, ng_ref, sc_ref, sh_ref, x2_ref, h_ref):
        xv = x_ref[...] + g_ref[...] * y_ref[...]
        x2_ref[...] = xv
        r = lax.rsqrt(jnp.mean(xv * xv, axis=-1, keepdims=True) + EPS)
        h_ref[...] = (((xv * r) * ng_ref[...]) * (1.0 + sc_ref[...]) + sh_ref[...]).astype(BF16)

    return _rowcall(body, name=name, tr=_row_tile(T, D, 256 * 1024), row_ins=[x, y], full_ins=[gate, ng, sc, sh],
                    row_outs=[_sds((T, D), F32), _sds((T, D), BF16)])


def _res_loss(x, y, gate, target, name):
    T, D = x.shape

    def body(x_ref, y_ref, t_ref, g_ref, dout_ref, dy_ref, loss_ref, dg_ref):
        @pl.when(pl.program_id(0) == 0)
        def _():
            loss_ref[...] = jnp.zeros_like(loss_ref)
            dg_ref[...] = jnp.zeros_like(dg_ref)

        yv, gv = y_ref[...], g_ref[...]
        diff = x_ref[...] + gv * yv - t_ref[...]
        dout = diff * (1.0 / D)
        dout_ref[...] = dout
        dy_ref[...] = (dout * gv).astype(BF16)
        loss_ref[...] += jnp.sum(diff * diff) * (0.5 / D)
        dg_ref[...] += jnp.sum(dout * yv, axis=0, keepdims=True)

    return _rowcall(body, name=name, tr=_row_tile(T, D, 256 * 1024), row_ins=[x, y, target], full_ins=[gate],
                    row_outs=[_sds((T, D), F32), _sds((T, D), BF16)],
                    acc_outs=[_sds((1, BLK), F32), _sds((1, D), F32)])


def _norm_mod_bwd(x, dhs, dres, ng, sc, name, gated=None):
    T, D = x.shape
    n_dh = len(dhs)
    n_row = 2 + n_dh + (1 if gated else 0)

    def body(*refs):
        x_ref, dh_refs, dres_ref = refs[0], refs[1:1 + n_dh], refs[1 + n_dh]
        ng_ref, sc_ref = refs[n_row:n_row + 2]
        outs = refs[n_row + (3 if gated else 2):]
        dx_ref, sums_ref = (outs[0], outs[2]) if gated else (outs[0], outs[1])

        @pl.when(pl.program_id(0) == 0)
        def _():
            sums_ref[...] = jnp.zeros_like(sums_ref)
            if gated:
                outs[3][...] = jnp.zeros_like(outs[3])

        xv = x_ref[...]
        dh = dh_refs[0][...].astype(F32)
        for r_ in dh_refs[1:]:
            dh = dh + r_[...].astype(F32)
        r = lax.rsqrt(jnp.mean(xv * xv, axis=-1, keepdims=True) + EPS)
        n = xv * r
        ngv, scale1 = ng_ref[...], 1.0 + sc_ref[...]
        dn = dh * (ngv * scale1)
        dx = dres_ref[...] + r * (dn - n * jnp.mean(dn * n, axis=-1, keepdims=True))
        dx_ref[...] = dx
        dhn = dh * n
        sums_ref[0:1, :] += jnp.sum(dh, axis=0, keepdims=True)
        sums_ref[1:2, :] += jnp.sum(dhn * ngv, axis=0, keepdims=True)
        sums_ref[2:3, :] += jnp.sum(dhn * scale1, axis=0, keepdims=True)
        if gated:
            y_ref, g_ref = refs[2 + n_dh], refs[n_row + 2]
            outs[1][...] = (dx * g_ref[...]).astype(BF16)
            outs[3][...] += jnp.sum(dx * y_ref[...], axis=0, keepdims=True)

    return _rowcall(body, name=name, tr=_row_tile(T, D, 256 * 1024),
                    row_ins=[x, *dhs, dres] + ([gated[0]] if gated else []),
                    full_ins=[ng, sc] + ([gated[1]] if gated else []),
                    row_outs=[_sds((T, D), F32)] + ([_sds((T, D), BF16)] if gated else []),
                    acc_outs=[_sds((8, D), F32)] + ([_sds((1, D), F32)] if gated else []))


def _head_norm(x, g, n_heads, name, tail=False):
    T = x.shape[0]
    D = n_heads * BLK
    W = x.shape[1] if tail else D

    def body(x_ref, g_ref, o_ref, *tail_ref):
        for h in range(n_heads):
            xv = x_ref[:, h * BLK:(h + 1) * BLK]
            r = lax.rsqrt(jnp.mean(xv * xv, axis=-1, keepdims=True) + EPS)
            o_ref[:, h * BLK:(h + 1) * BLK] = ((xv * r) * g_ref[...]).astype(BF16)
        if tail:
            tail_ref[0][...] = x_ref[:, D:2 * D].astype(BF16)

    tr = _row_tile(T, x.shape[1])
    o_spec = pl.BlockSpec((tr, D), lambda i: (i, 0))
    return pl.pallas_call(
        body, name=name, grid=(T // tr,),
        in_specs=[pl.BlockSpec((tr, W), lambda i: (i, 0)), pl.BlockSpec((1, BLK), lambda i: (0, 0))],
        out_specs=[o_spec] * (2 if tail else 1), out_shape=[_sds((T, D), BF16)] * (2 if tail else 1),
        compiler_params=_params(("parallel",)),
    )(x, g)


def _head_norm_bwd(x, dys, g, n_heads, name, tails=()):
    T = x.shape[0]
    D = n_heads * BLK
    n_dy, n_tail = len(dys), len(tails)
    W = 2 * D if tails else D

    def body(*refs):
        x_ref, dy_refs, tail_refs = refs[0], refs[1:1 + n_dy], refs[1 + n_dy:1 + n_dy + n_tail]
        g_ref, dx_ref, dg_ref = refs[1 + n_dy + n_tail:]

        @pl.when(pl.program_id(0) == 0)
        def _():
            dg_ref[...] = jnp.zeros_like(dg_ref)

        tot = jnp.zeros((1, BLK), F32)
        for h in range(n_heads):
            cols = slice(h * BLK, (h + 1) * BLK)
            xv = x_ref[:, cols]
            dyv = dy_refs[0][:, cols]
            for r_ in dy_refs[1:]:
                dyv = dyv + r_[:, cols]
            r = lax.rsqrt(jnp.mean(xv * xv, axis=-1, keepdims=True) + EPS)
            n = xv * r
            dn = dyv * g_ref[...]
            dx_ref[:, cols] = (r * (dn - n * jnp.mean(dn * n, axis=-1, keepdims=True))).astype(BF16)
            tot = tot + jnp.sum(dyv * n, axis=0, keepdims=True)
        dg_ref[0:1, :] += tot
        if n_tail:
            tv = tail_refs[0][...]
            for r_ in tail_refs[1:]:
                tv = tv + r_[...]
            dx_ref[:, D:] = tv.astype(BF16)

    tr = _row_tile(T, 2 * D, 256 * 1024)
    d_spec = pl.BlockSpec((tr, D), lambda i: (i, 0))
    return pl.pallas_call(
        body, name=name, grid=(T // tr,),
        in_specs=[d_spec] * (1 + n_dy + n_tail) + [pl.BlockSpec((1, BLK), lambda i: (0, 0))],
        out_specs=[pl.BlockSpec((tr, W), lambda i: (i, 0)), pl.BlockSpec((8, BLK), lambda i: (0, 0))],
        out_shape=[_sds((T, W), BF16), _sds((8, BLK), F32)],
        compiler_params=_params(("arbitrary",)),
    )(x, *dys, *tails, g)


def _fcum_fwd(fl, bf, name):
    T = fl.shape[0]

    def body(fl_ref, b_ref, o_ref, carry_ref):
        @pl.when(pl.program_id(0) == 0)
        def _():
            carry_ref[...] = jnp.zeros_like(carry_ref)

        z = fl_ref[...] + b_ref[...]
        logf = jnp.minimum(z, 0.0) - jnp.log(1.0 + jnp.exp(-jnp.abs(z)))
        row = lax.broadcasted_iota(jnp.int32, (BLK, BLK), 0)
        col = lax.broadcasted_iota(jnp.int32, (BLK, BLK), 1)
        tri = (col <= row).astype(F32)
        run = jnp.dot(tri, logf, preferred_element_type=F32, precision=lax.Precision.HIGHEST) + carry_ref[0:1, :]
        o_ref[...] = run
        carry_ref[0:1, :] = run[BLK - 1:BLK, :]

    return _rowcall(body, name=name, tr=BLK, row_ins=[fl], full_ins=[bf], row_outs=[_sds((T, BLK), F32)],
                    scratch=[pltpu.VMEM((8, BLK), F32)])[0]


def _fcum_bwd(dfs, fl, bf, name):
    T = fl.shape[0]
    n_df = len(dfs)

    def body(*refs):
        df_refs = refs[:n_df]
        fl_ref, b_ref, dfl_ref, dbias_ref, carry_ref = refs[n_df:]

        @pl.when(pl.program_id(0) == 0)
        def _():
            carry_ref[...] = jnp.zeros_like(carry_ref)
            dbias_ref[...] = jnp.zeros_like(dbias_ref)

        dfc = df_refs[0][...]
        for r_ in df_refs[1:]:
            dfc = dfc + r_[...]
        row = lax.broadcasted_iota(jnp.int32, (BLK, BLK), 0)
        col = lax.broadcasted_iota(jnp.int32, (BLK, BLK), 1)
        tri = (col >= row).astype(F32)
        suffix = jnp.dot(tri, dfc, preferred_element_type=F32, precision=lax.Precision.HIGHEST) + carry_ref[0:1, :]
        carry_ref[0:1, :] = suffix[0:1, :]
        z = fl_ref[...] + b_ref[...]
        dfl = suffix / (1.0 + jnp.exp(z))
        dfl_ref[...] = dfl.astype(BF16)
        dbias_ref[0:1, :] += jnp.sum(dfl, axis=0, keepdims=True)

    return _rowcall(body, name=name, tr=BLK, row_ins=[*dfs, fl], full_ins=[bf], reverse=True,
                    row_outs=[_sds((T, BLK), BF16)], acc_outs=[_sds((8, BLK), F32)],
                    scratch=[pltpu.VMEM((8, BLK), F32)])


def _attn_tile(T):
    return min(T, 512)


def _attn_fwd(q, k, v, fk, n_heads, name):
    T = q.shape[0]
    tq = tk = _attn_tile(T)
    nkb = T // tk
    inv_sqrt = 1.0 / float(math.sqrt(BLK))

    def body(q_ref, k_ref, v_ref, fk_ref, o_ref, o32_ref, lse_ref):
        i = pl.program_id(1)
        qv = q_ref[...]

        def block(j, carry, diagonal):
            m, l, acc = carry
            rows = pl.ds(pl.multiple_of(j * tk, tk), tk)
            kj, vj = k_ref[rows, :], v_ref[rows, :]
            s = lax.dot_general(qv, kj, (((1,), (1,)), ((), ())), preferred_element_type=F32) * inv_sqrt
            s = s - fk_ref[j]
            if diagonal:
                s = jnp.where(lax.broadcasted_iota(jnp.int32, (tq, tk), 1)
                              <= lax.broadcasted_iota(jnp.int32, (tq, tk), 0), s, NEG_BIG)
            m_new = jnp.maximum(m, jnp.max(s, axis=-1, keepdims=True))
            alpha = jnp.exp(m - m_new)
            p = jnp.exp(s - m_new)
            l = alpha * l + jnp.sum(p, axis=-1, keepdims=True)
            acc = alpha * acc + jnp.dot(p.astype(BF16), vj, preferred_element_type=F32)
            return m_new, l, acc

        init = (jnp.full((tq, 1), NEG_BIG, F32), jnp.zeros((tq, 1), F32), jnp.zeros((tq, BLK), F32))
        carry = lax.fori_loop(0, i, lambda j, c: block(j, c, False), init)
        m, l, acc = block(i, carry, True)
        out = acc / l
        o_ref[...] = out.astype(BF16)
        o32_ref[...] = out
        lse_ref[...] = m + jnp.log(l)

    return pl.pallas_call(
        body, name=name, grid=(n_heads, T // tq),
        in_specs=[pl.BlockSpec((tq, BLK), lambda h, i: (i, h)),
                  pl.BlockSpec((T, BLK), lambda h, i: (0, h)),
                  pl.BlockSpec((T, BLK), lambda h, i: (0, h)),
                  pl.BlockSpec((None, nkb, 1, tk), lambda h, i: (h, 0, 0, 0))],
        out_specs=[pl.BlockSpec((tq, BLK), lambda h, i: (i, h)),
                   pl.BlockSpec((tq, BLK), lambda h, i: (i, h)),
                   pl.BlockSpec((None, tq, 1), lambda h, i: (h, i, 0))],
        out_shape=[_sds((T, n_heads * BLK), BF16), _sds((T, n_heads * BLK), F32), _sds((n_heads, T, 1), F32)],
        compiler_params=_params(("parallel", "arbitrary")),
    )(q, k, v, fk)


def _attn_bwd(q, k, v, o, do, lse, fk, n_heads, name):
    T = q.shape[0]
    tq = tk = _attn_tile(T)
    nkb = T // tk
    nq = T // tq
    inv_sqrt = 1.0 / float(math.sqrt(BLK))
    tn_dims = (((0,), (0,)), ((), ()))
    nt_dims = (((1,), (1,)), ((), ()))

    def body(q_ref, k_ref, v_ref, o_ref, do_ref, lse_ref, fk_ref, dq_ref, dk_ref, dv_ref, dfq_ref, dfk_ref, delta_ref):
        j = pl.program_id(1)

        @pl.when(j == 0)
        def _():
            delta_ref[...] = jnp.sum(do_ref[...].astype(F32) * o_ref[...], axis=1, keepdims=True)
            dq_ref[...] = jnp.zeros_like(dq_ref)
            dfq_ref[...] = jnp.zeros_like(dfq_ref)

        kj, vj, fkv = k_ref[...], v_ref[...], fk_ref[...]

        def step(i, carry, diagonal):
            dk, dv, dfk = carry
            rows = pl.ds(pl.multiple_of(i * tq, tq), tq)
            qi, doi = q_ref[rows, :], do_ref[rows, :]
            s = lax.dot_general(qi, kj, nt_dims, preferred_element_type=F32) * inv_sqrt - fkv
            if diagonal:
                s = jnp.where(lax.broadcasted_iota(jnp.int32, (tq, tk), 1)
                              <= lax.broadcasted_iota(jnp.int32, (tq, tk), 0), s, NEG_BIG)
            p = jnp.exp(s - lse_ref[rows, :])
            dv = dv + lax.dot_general(p.astype(BF16), doi, tn_dims, preferred_element_type=F32)
            dp = lax.dot_general(doi, vj, nt_dims, preferred_element_type=F32)
            ds = p * (dp - delta_ref[rows, :])
            dsb = ds.astype(BF16)
            dq_ref[rows, :] += jnp.dot(dsb, kj, preferred_element_type=F32) * inv_sqrt
            dk = dk + lax.dot_general(dsb, qi, tn_dims, preferred_element_type=F32)
            dfq_ref[rows, :] += jnp.sum(ds, axis=1, keepdims=True)
            dfk = dfk - jnp.sum(ds, axis=0, keepdims=True)
            return dk, dv, dfk

        init = (jnp.zeros((tk, BLK), F32), jnp.zeros((tk, BLK), F32), jnp.zeros((1, tk), F32))
        carry = step(j, init, True)
        dk, dv, dfk = lax.fori_loop(j + 1, nq, lambda i, c: step(i, c, False), carry)
        dk_ref[...] = dk * inv_sqrt
        dv_ref[...] = dv
        dfk_ref[...] = dfk

    head_col = lambda h, j: (0, h)
    return pl.pallas_call(
        body, name=name, grid=(n_heads, nkb),
        in_specs=[pl.BlockSpec((T, BLK), head_col),
                  pl.BlockSpec((tk, BLK), lambda h, j: (j, h)),
                  pl.BlockSpec((tk, BLK), lambda h, j: (j, h)),
                  pl.BlockSpec((T, BLK), head_col),
                  pl.BlockSpec((T, BLK), head_col),
                  pl.BlockSpec((None, T, 1), lambda h, j: (h, 0, 0)),
                  pl.BlockSpec((None, None, 1, tk), lambda h, j: (h, j, 0, 0))],
        out_specs=[pl.BlockSpec((T, BLK), head_col),
                   pl.BlockSpec((tk, BLK), lambda h, j: (j, h)),
                   pl.BlockSpec((tk, BLK), lambda h, j: (j, h)),
                   pl.BlockSpec((None, T, 1), lambda h, j: (h, 0, 0)),
                   pl.BlockSpec((None, None, 1, tk), lambda h, j: (h, j, 0, 0))],
        out_shape=[_sds((T, n_heads * BLK), F32), _sds((T, n_heads * BLK), F32), _sds((T, n_heads * BLK), F32),
                   _sds((n_heads, T, 1), F32), _sds((n_heads, nkb, 1, tk), F32)],
        scratch_shapes=[pltpu.VMEM((T, 1), F32)],
        compiler_params=_params(("parallel", "arbitrary")),
    )(q, k, v, o, do, lse, fk)


_INV_SQRT2 = 1.0 / math.sqrt(2.0)
_INV_SQRT_2PI = 1.0 / math.sqrt(2.0 * math.pi)


def _gelu_parts(z):
    cdf = 0.5 * (1.0 + lax.erf(z * _INV_SQRT2))
    return cdf, z * cdf


def _mix_mask(transposed):
    row = lax.broadcasted_iota(jnp.int32, (BLK, BLK), 0) // CHUNK
    col = lax.broadcasted_iota(jnp.int32, (BLK, BLK), 1) // CHUNK
    return (row <= col) if transposed else (col <= row)


def _gmlp_mid_fwd(zpre, ln_g, ln_b, ws, bs_t, name):
    T, two_h = zpre.shape
    Hh = two_h // 2
    G = ws.shape[0]
    gd = Hh // G

    def body(z_ref, lg_ref, lb_ref, ws_ref, bs_ref, p_ref):
        _, zg = _gelu_parts(z_ref[...].astype(F32))
        u, v = zg[:, :Hh], zg[:, Hh:]
        mu = jnp.mean(v, axis=-1, keepdims=True)
        vc = v - mu
        rstd = lax.rsqrt(jnp.mean(vc * vc, axis=-1, keepdims=True) + EPS)
        vn = ((vc * rstd) * lg_ref[...] + lb_ref[...]).astype(BF16)
        mask = _mix_mask(False)
        for g in range(G):
            wm = jnp.where(mask, ws_ref[g], 0.0).astype(BF16)
            sv = jnp.dot(wm, vn[:, g * gd:(g + 1) * gd], preferred_element_type=F32) + bs_ref[:, g:g + 1]
            p_ref[:, g * gd:(g + 1) * gd] = (u[:, g * gd:(g + 1) * gd] * sv).astype(BF16)

    return _rowcall(body, name=name, tr=BLK, row_ins=[zpre], full_ins=[ln_g, ln_b, ws, bs_t],
                    row_outs=[_sds((T, Hh), BF16)])[0]


def _gmlp_mid_bwd(zpre, dp, ln_g, ln_b, ws, ws_t, bs_t, name):
    T, two_h = zpre.shape
    Hh = two_h // 2
    G = ws.shape[0]
    gd = Hh // G
    nt_dims = (((1,), (1,)), ((), ()))

    def body(z_ref, dp_ref, lg_ref, lb_ref, ws_ref, wst_ref, bs_ref, dz_ref, dws_ref, dbs_ref, dlg_ref, dlb_ref,
             dvn_ref):
        @pl.when(pl.program_id(0) == 0)
        def _():
            dws_ref[...] = jnp.zeros_like(dws_ref)
            dbs_ref[...] = jnp.zeros_like(dbs_ref)
            dlg_ref[...] = jnp.zeros_like(dlg_ref)
            dlb_ref[...] = jnp.zeros_like(dlb_ref)

        z = z_ref[...].astype(F32)
        cdf, zg = _gelu_parts(z)
        dgelu = cdf + z * (jnp.exp(-0.5 * z * z) * _INV_SQRT_2PI)
        u, v = zg[:, :Hh], zg[:, Hh:]
        mu = jnp.mean(v, axis=-1, keepdims=True)
        vc = v - mu
        rstd = lax.rsqrt(jnp.mean(vc * vc, axis=-1, keepdims=True) + EPS)
        vhat = vc * rstd
        vn = (vhat * lg_ref[...] + lb_ref[...]).astype(BF16)
        mask, mask_t = _mix_mask(False), _mix_mask(True)
        lane = lax.broadcasted_iota(jnp.int32, (BLK, BLK), 1)
        dbs = jnp.zeros((BLK, BLK), F32)
        for g in range(G):
            cols = slice(g * gd, (g + 1) * gd)
            wm = jnp.where(mask, ws_ref[g], 0.0).astype(BF16)
            wm_t = jnp.where(mask_t, wst_ref[g], 0.0).astype(BF16)
            vn_g = vn[:, cols]
            sv = jnp.dot(wm, vn_g, preferred_element_type=F32) + bs_ref[:, g:g + 1]
            dp_g = dp_ref[:, cols].astype(F32)
            dz_ref[:, cols] = ((dp_g * sv) * dgelu[:, cols]).astype(BF16)
            dsv = dp_g * u[:, cols]
            dsv_b = dsv.astype(BF16)
            dbs = dbs + jnp.where(lane == g, jnp.sum(dsv, axis=1, keepdims=True), 0.0)
            dws_ref[g] += jnp.where(mask, lax.dot_general(dsv_b, vn_g, nt_dims, preferred_element_type=F32), 0.0)
            dvn_ref[:, cols] = jnp.dot(wm_t, dsv_b, preferred_element_type=F32)
        dbs_ref[...] += dbs
        dvn = dvn_ref[...]
        dlg_ref[0:1, :] += jnp.sum(dvn * vhat, axis=0, keepdims=True)
        dlb_ref[0:1, :] += jnp.sum(dvn, axis=0, keepdims=True)
        dvh = dvn * lg_ref[...]
        dv = rstd * (dvh - jnp.mean(dvh, axis=-1, keepdims=True) - vhat * jnp.mean(dvh * vhat, axis=-1, keepdims=True))
        dz_ref[:, Hh:] = (dv * dgelu[:, Hh:]).astype(BF16)

    return _rowcall(body, name=name, tr=BLK, row_ins=[zpre, dp], full_ins=[ln_g, ln_b, ws, ws_t, bs_t],
                    row_outs=[_sds((T, two_h), BF16)],
                    acc_outs=[_sds((G, BLK, BLK), F32), _sds((BLK, BLK), F32), _sds((8, Hh), F32), _sds((8, Hh), F32)],
                    scratch=[pltpu.VMEM((BLK, Hh), F32)])


def _mods(c_all, w, layer, bias, name):
    nb, K = c_all.shape
    N = w.shape[-1]
    tn = _tile(N, 512)

    def body(c_ref, w_ref, b_ref, o_ref):
        cv = c_ref[...]
        sc = cv / (1.0 + jnp.exp(-cv))
        o_ref[...] = jnp.dot(sc, w_ref[...], preferred_element_type=F32, precision=lax.Precision.HIGHEST) + b_ref[...]

    return pl.pallas_call(
        body, name=name, grid=(N // tn,),
        in_specs=[pl.BlockSpec((nb, K), lambda j: (0, 0)),
                  pl.BlockSpec((None, K, tn), lambda j: (layer, 0, j)),
                  pl.BlockSpec((1, tn), lambda j: (0, j))],
        out_specs=pl.BlockSpec((nb, tn), lambda j: (0, j)), out_shape=_sds((nb, N), F32),
        compiler_params=_params(("parallel",)),
    )(c_all, w, bias)


def _sum_slabs(landed, own, who, name):
    _, R, C = landed.shape
    tr = _row_tile(R, C * N_DEV)

    def body(who_ref, x_ref, own_ref, o_ref):
        me, mine = who_ref[0], own_ref[...]
        acc = jnp.where(me == 0, mine, x_ref[0])
        for s in range(1, N_DEV):
            acc = acc + jnp.where(me == s, mine, x_ref[s])
        o_ref[...] = acc

    return pl.pallas_call(
        body, name=name,
        grid_spec=pltpu.PrefetchScalarGridSpec(
            num_scalar_prefetch=1, grid=(R // tr,),
            in_specs=[pl.BlockSpec((N_DEV, tr, C), lambda i, who_ref: (0, i, 0)),
                      pl.BlockSpec((tr, C), lambda i, who_ref: (i, 0))],
            out_specs=pl.BlockSpec((tr, C), lambda i, who_ref: (i, 0))),
        out_shape=_sds((R, C), F32),
        compiler_params=_params(("parallel",)),
    )(who, landed, own)


def _adamw_math(w, g, m, v):
    m = ADAM_B1 * m + (1.0 - ADAM_B1) * g
    v = ADAM_B2 * v + (1.0 - ADAM_B2) * (g * g)
    m_hat = m / (1.0 - ADAM_B1 ** ADAM_STEP)
    v_hat = v / (1.0 - ADAM_B2 ** ADAM_STEP)
    delta = -ADAM_LR * (m_hat / (jnp.sqrt(v_hat) + ADAM_EPS) + ADAM_WD * w)
    return delta, m, v


class _AdamStack:
    def __init__(self, w, m, v, name):
        self.w, self.m, self.v, self.name = w, m, v, name
        self.L, self.R, self.C = w.shape
        self.tr = _row_tile(self.R, self.C, 256 * 1024)
        self.outs = None

    def _layer(self, l, who, srcs, src_specs, make_grad):
        n_src = len(srcs)
        L, R, C, tr = self.L, self.R, self.C, self.tr
        wspec = pl.BlockSpec((None, tr, C), lambda i, who_ref: (l, i, 0))

        def body(who_ref, *refs):
            src_refs = refs[:n_src]
            w_ref, m_ref, v_ref = refs[n_src:n_src + 3]
            g_ref, d_ref, m2_ref, v2_ref = refs[-4:]
            g = make_grad(who_ref[0], *src_refs)
            delta, m2, v2 = _adamw_math(w_ref[...], g, m_ref[...], v_ref[...])
            g_ref[...] = g
            d_ref[...] = delta
            m2_ref[...] = m2
            v2_ref[...] = v2

        prev = [] if self.outs is None else list(self.outs)
        aliases = {} if self.outs is None else {1 + n_src + 3 + t: t for t in range(4)}
        self.outs = pl.pallas_call(
            body, name=f"{self.name}_l{l}",
            grid_spec=pltpu.PrefetchScalarGridSpec(
                num_scalar_prefetch=1, grid=(R // tr,),
                in_specs=list(src_specs) + [wspec] * 3 + [_ANY] * len(prev), out_specs=[wspec] * 4),
            out_shape=[_sds((L, R, C), F32)] * 4,
            input_output_aliases=aliases,
            compiler_params=_params(("parallel",)),
        )(who, *srcs, self.w, self.m, self.v, *prev)

    def from_parts(self, l, who, landed, sent):
        tr, C = self.tr, self.C

        def make_grad(me, p_ref, own_ref):
            mine = own_ref[...].astype(F32)
            g = jnp.where(me == 0, mine, p_ref[0].astype(F32))
            for s in range(1, N_DEV):
                g = g + jnp.where(me == s, mine, p_ref[s].astype(F32))
            return g

        self._layer(l, who, [landed, sent],
                    [pl.BlockSpec((N_DEV, tr, C), lambda i, who_ref: (0, i, 0)),
                     pl.BlockSpec((None, tr, C), lambda i, who_ref: (who_ref[0], i, 0))], make_grad)

    def from_outer(self, l, who, sc_t, dmod):
        tr, C = self.tr, self.C

        def make_grad(me, s_ref, d_ref):
            g = s_ref[:, 0:1] * d_ref[0:1, :]
            for b in range(1, N_DEV):
                g = g + s_ref[:, b:b + 1] * d_ref[b:b + 1, :]
            return g

        self._layer(l, who, [sc_t, dmod], [pl.BlockSpec((tr, N_DEV), lambda i, who_ref: (i, 0)),
                                           pl.BlockSpec((N_DEV, C), lambda i, who_ref: (0, 0))], make_grad)


def _adamw_flat(w, g, m, v, name):
    R, C = w.shape
    tr = _row_tile(R, C, 128 * 1024)

    def body(w_ref, g_ref, m_ref, v_ref, d_ref, m2_ref, v2_ref):
        delta, m2, v2 = _adamw_math(w_ref[...], g_ref[...], m_ref[...], v_ref[...])
        d_ref[...] = delta
        m2_ref[...] = m2
        v2_ref[...] = v2

    spec = pl.BlockSpec((tr, C), lambda i: (i, 0))
    return pl.pallas_call(
        body, name=name, grid=(R // tr,), in_specs=[spec] * 4, out_specs=[spec] * 3,
        out_shape=[_sds((R, C), F32)] * 3, compiler_params=_params(("parallel",)),
    )(w, g, m, v)


def _pack(arrays):
    flat = jnp.concatenate([a.reshape(-1).astype(F32) for a in arrays])
    pad = (-flat.shape[0]) % (64 * BLK)
    if pad:
        flat = jnp.concatenate([flat, jnp.zeros((pad,), F32)])
    return flat.reshape(-1, BLK)


def _unpack(buf, shapes, lead=()):
    sizes = [int(math.prod(s)) for s in shapes]
    out, off = [], 0
    if all(n % BLK == 0 for n in sizes):
        for s, n in zip(shapes, sizes):
            out.append(buf[..., off // BLK:(off + n) // BLK, :].reshape(tuple(lead) + tuple(s)))
            off += n
        return out
    flat = buf.reshape(tuple(lead) + (-1,))
    for s, n in zip(shapes, sizes):
        out.append(flat[..., off:off + n].reshape(tuple(lead) + tuple(s)))
        off += n
    return out


def _row(vec):
    return vec.reshape(1, -1)


def _shard_of(full, axis, me, size):
    return lax.dynamic_slice_in_dim(full, me * size, size, axis=axis)


def kernel(x, c, ada_w, ada_b, norm_g, mlp_w1, mlp_w2, gmlp_w_in, gmlp_ln_g, gmlp_ln_b, gmlp_ws, gmlp_bs, gmlp_w_out, kv_norm_g, kv_ada_w, kv_ada_b, w_kv, k_norm_g, w_f, b_f, attn_wq, q_norm_g, attn_wo, loss_target, m_ada_w, m_ada_b, m_norm_g, m_mlp_w1, m_mlp_w2, m_gmlp_w_in, m_gmlp_ln_g, m_gmlp_ln_b, m_gmlp_ws, m_gmlp_bs, m_gmlp_w_out, m_kv_norm_g, m_kv_ada_w, m_kv_ada_b, m_w_kv, m_k_norm_g, m_w_f, m_b_f, m_attn_wq, m_q_norm_g, m_attn_wo, v_ada_w, v_ada_b, v_norm_g, v_mlp_w1, v_mlp_w2, v_gmlp_w_in, v_gmlp_ln_g, v_gmlp_ln_b, v_gmlp_ws, v_gmlp_bs, v_gmlp_w_out, v_kv_norm_g, v_kv_ada_w, v_kv_ada_b, v_w_kv, v_k_norm_g, v_w_f, v_b_f, v_attn_wq, v_q_norm_g, v_attn_wo):
    given = dict(locals())
    weights = {n: given[n] for n in WEIGHT_NAMES}
    mom_m = {n: given["m_" + n] for n in WEIGHT_NAMES}
    mom_v = {n: given["v_" + n] for n in WEIGHT_NAMES}

    me = _my_index()
    T, D = x.shape[1], x.shape[2]
    depth = ada_w.shape[0]
    n_a = gmlp_w_in.shape[0]
    n_heads = b_f.shape[0]
    G = gmlp_ws.shape[1]
    Hh = gmlp_ln_g.shape[1] * N_DEV
    mod_cols = ada_w.shape[2]
    kv_cols = kv_ada_w.shape[1]
    x0 = x.reshape(T, D)
    target = loss_target.reshape(T, D)

    small_in = [c, norm_g, gmlp_ln_g, gmlp_ln_b, w_f]
    small_shapes = [a.shape for a in small_in]
    got = _all_to_all(_pack(small_in), "gather_small_inputs", bcast=True)
    c_all, norm_g_sh, ln_g_sh, ln_b_sh, w_f_sh = _unpack(got, small_shapes, lead=(N_DEV,))
    c_all = c_all.reshape(N_DEV, D)
    norm_g_full = jnp.moveaxis(norm_g_sh, 0, 2).reshape(depth, 2, D)
    ln_g_full = jnp.moveaxis(ln_g_sh, 0, 1).reshape(n_a, Hh)
    ln_b_full = jnp.moveaxis(ln_b_sh, 0, 1).reshape(n_a, Hh)
    w_f_full = w_f_sh.reshape(D, n_heads)
    w_f_pad = jnp.pad(w_f_full, ((0, 0), (0, BLK - n_heads))).astype(BF16)
    b_f_pad = jnp.pad(b_f, (0, BLK - n_heads)).reshape(1, BLK)

    mod_parts = []
    for l in range(depth):
        bias = _shard_of(ada_b[l], 0, me, mod_cols).reshape(1, mod_cols)
        mod_parts.append(_mods(c_all, ada_w, l, bias, f"mods_l{l}"))
    kv_bias = _shard_of(kv_ada_b, 0, me, kv_cols).reshape(1, kv_cols)
    mod_parts.append(_mods(c_all, kv_ada_w.reshape(1, D, kv_cols), 0, kv_bias, "mods_kv"))
    mods_mine = jnp.concatenate(mod_parts, axis=1)
    mod_width = mods_mine.shape[1]
    mods_pack = jnp.pad(mods_mine, ((0, 0), (0, (-mod_width) % (8 * BLK)))).reshape(N_DEV, -1, BLK)
    mods_landed = _all_to_all(mods_pack, "exchange_mods")
    mods_got = mods_landed.reshape(N_DEV, -1)[:, :mod_width]
    mods = []
    for l in range(depth):
        mods.append(mods_got[:, l * mod_cols:(l + 1) * mod_cols].reshape(N_MOD, D))
    kv_mod = mods_got[:, depth * mod_cols:].reshape(2, D)
    silu_all = c_all / (1.0 + jnp.exp(-c_all))

    assert 1 <= n_a < depth
    who = me.astype(jnp.int32).reshape(1)
    big = {"mlp_w1": mlp_w1, "mlp_w2": mlp_w2, "gmlp_w_in": gmlp_w_in, "gmlp_w_out": gmlp_w_out,
           "w_kv": w_kv.reshape((1,) + w_kv.shape), "attn_wq": attn_wq, "attn_wo": attn_wo}
    groups = [[("gmlp_w_in", 0)], [("gmlp_w_out", 0), ("mlp_w1", 0), ("mlp_w2", 0)]]
    for l in range(1, depth):
        if l < n_a:
            groups.append([("gmlp_w_in", l), ("gmlp_w_out", l), ("mlp_w1", l), ("mlp_w2", l)])
        else:
            first = [("w_kv", 0)] if l == n_a else []
            groups.append(first + [("attn_wq", l - n_a), ("attn_wo", l - n_a), ("mlp_w1", l), ("mlp_w2", l)])
    tokens = []

    def behind_starts():
        out = tuple(tokens)
        tokens.clear()
        return out

    over_ici, placed = {}, {}

    def gather_begin(gi, deps=()):
        over_ici[gi] = _gather_start(placed[gi], 1, f"gather_g{gi}_ici_start", deps=deps)
        tokens.append(over_ici[gi][3])

    for gi, grp in enumerate(groups):
        behind = (mods_landed,) if gi == 0 else (over_ici[0][3],)
        placed[gi] = [_place_shard(big[n], l, who, f"place_{n}_l{l}", deps=behind) for n, l in grp]
        gather_begin(gi, deps=tuple(tokens[-1:]))
    to_sibling = {}
    gw = {}

    def gather_forward(gi, after):
        landed = _gather_wait(over_ici[gi], 1, after, f"gather_g{gi}_ici_wait")
        to_sibling[gi] = _gather_start(landed, 2, f"gather_g{gi}_d2d_start")
        tokens.append(to_sibling[gi][3])

    def gather_finish(gi, after):
        for key, arr in zip(groups[gi], _gather_wait(to_sibling[gi], 2, after, f"gather_g{gi}_d2d_wait")):
            gw[key] = arr

    tkk = _attn_tile(T)

    saved = []
    xs = x0
    pending = None
    kv = None
    for l in range(depth):
        sh1, sc1, g1, sh2, sc2, g2 = [_row(mods[l][t]) for t in range(N_MOD)]
        ng1, ng2 = _row(norm_g_full[l, 0]), _row(norm_g_full[l, 1])
        st = dict(sc1=sc1, g1=g1, sc2=sc2, g2=g2, ng1=ng1, ng2=ng2)
        if pending is None:
            h1 = _norm_mod(xs, ng1, sc1, sh1, f"norm1_l{l}")
            gather_forward(0, h1)
            gather_finish(0, h1)
        else:
            gather_finish(l + 1, pending[0])
            xs, h1 = _res_norm_mod(xs, pending[0], pending[1], ng1, sc1, sh1, f"res_norm1_l{l}")
        st["x_in"], st["h1"] = xs, h1
        if l < n_a:
            a = l
            zpre = _mm(h1, gw["gmlp_w_in", a], bmode="col", out_dtypes=(BF16,), name=f"gmlp_in_l{l}",
                       deps=behind_starts())
            if l == 0:
                gather_forward(1, zpre)
            bs_t = gmlp_bs[a].T
            p = _gmlp_mid_fwd(zpre, _row(ln_g_full[a]), _row(ln_b_full[a]), gmlp_ws[a], bs_t, f"gmlp_mid_l{l}")
            if l == 0:
                gather_finish(1, p)
            y = _mm(p, gw["gmlp_w_out", a], bmode="row", name=f"gmlp_out_l{l}", deps=behind_starts())
            st.update(zpre=zpre, p=p)
        else:
            if kv is None:
                kv_ng, kv_sh, kv_sc = _row(kv_norm_g), _row(kv_mod[0]), _row(kv_mod[1])
                hkv = _norm_mod(xs, kv_ng, kv_sc, kv_sh, "norm_kv")
                kvp = _mm(hkv, gw["w_kv", 0], bmode="col", name="kv_proj", deps=behind_starts())
                kk, vv = _head_norm(kvp, _row(k_norm_g), n_heads, "k_norm", tail=True)
                fl = _mm(hkv, w_f_pad, name="gate_logits")
                fcum = _fcum_fwd(fl, b_f_pad, "fcum")
                fk = fcum[:, :n_heads].T.reshape(n_heads, T // tkk, 1, tkk)
                kv = dict(x=xs, hkv=hkv, kvp=kvp, k=kk, v=vv, fl=fl, fcum=fcum, fk=fk, ng=kv_ng, sc=kv_sc)
            bl = l - n_a
            qp = _mm(h1, gw["attn_wq", bl], bmode="row", name=f"q_proj_l{l}", deps=behind_starts())
            q = _head_norm(qp, _row(q_norm_g[bl]), n_heads, f"q_norm_l{l}")[0]
            o, o32, lse = _attn_fwd(q, kv["k"], kv["v"], kv["fk"], n_heads, f"attn_l{l}")
            y = _mm(o, gw["attn_wo", bl], bmode="row", name=f"attn_out_l{l}")
            st.update(qp=qp, q=q, o=o, o32=o32, lse=lse)
        xs, h2 = _res_norm_mod(xs, y, g1, ng2, sc2, sh2, f"res_norm2_l{l}")
        a_pre, s_act = _mm(h2, gw["mlp_w1", l], bmode="col", out_dtypes=(BF16, BF16), epilogue=_relu2_epilogue,
                           name=f"mlp_up_l{l}")
        if l + 1 < depth:
            gather_forward(l + 2, a_pre)
        mo = _mm(s_act, gw["mlp_w2", l], bmode="row", name=f"mlp_down_l{l}", deps=behind_starts())
        st.update(y=y, x_mid=xs, h2=h2, a_pre=a_pre, s=s_act, m=mo)
        saved.append(st)
        pending = (mo, g2)

    dx, dm, loss_row, dg2 = _res_loss(xs, pending[0], pending[1], target, "loss")
    loss = lax.psum(loss_row[0, 0], ("x", "y", "c"))

    started = {}

    def scatter(dw_slabs, key, idx):
        started[(key, idx)] = _a2a_start(dw_slabs, f"scatter_{key}_l{idx}_start")
        tokens.append(started[(key, idx)][4])

    d_mod = [None] * depth
    d_norm_g = [None] * depth
    d_ln_g, d_ln_b, d_ws, d_bs = [None] * n_a, [None] * n_a, [None] * n_a, [None] * n_a
    d_qg = [None] * (depth - n_a)
    dk_list, dv_list, dfk_list = [], [], []
    small = {}

    for l in reversed(range(depth)):
        st = saved[l]
        da = _mm(dm, gw["mlp_w2", l], tb=True, bmode="row", out_dtypes=(BF16,), epilogue=_relu2_bwd_epilogue,
                 extra=(st["a_pre"],), name=f"mlp_down_dx_l{l}")
        dw2 = _mm(st["s"], dm, ta=True, out_dtypes=(BF16,), name=f"mlp_down_dw_l{l}", deps=behind_starts())
        scatter(dw2.reshape(N_DEV, -1, D), "mlp_w2", l)
        dw1 = _mm(st["h2"], da, ta=True, out_mode="col", out_dtypes=(BF16,), name=f"mlp_up_dw_l{l}",
                  deps=behind_starts())
        scatter(dw1, "mlp_w1", l)
        dh2 = _mm(da, gw["mlp_w1", l], tb=True, bmode="col", out_dtypes=(BF16,), name=f"mlp_up_dx_l{l}",
                  deps=behind_starts())
        dx, dy, sums2, dg1 = _norm_mod_bwd(st["x_mid"], [dh2], dx, st["ng2"], st["sc2"], f"norm2_bwd_l{l}",
                                           gated=(st["y"], st["g1"]))
        if l < n_a:
            a = l
            dwo = _mm(st["p"], dy, ta=True, out_dtypes=(BF16,), name=f"gmlp_out_dw_l{l}", deps=behind_starts())
            scatter(dwo.reshape(N_DEV, -1, D), "gmlp_w_out", a)
            dp = _mm(dy, gw["gmlp_w_out", a], tb=True, bmode="row", out_dtypes=(BF16,), name=f"gmlp_out_dx_l{l}",
                     deps=behind_starts())
            dz, d_ws[a], dbs_t, dlg, dlb = _gmlp_mid_bwd(
                st["zpre"], dp, _row(ln_g_full[a]), _row(ln_b_full[a]), gmlp_ws[a],
                jnp.swapaxes(gmlp_ws[a], 1, 2), gmlp_bs[a].T, f"gmlp_mid_bwd_l{l}")
            d_bs[a], d_ln_g[a], d_ln_b[a] = dbs_t[:, :G].T, dlg[0], dlb[0]
            dwi = _mm(st["h1"], dz, ta=True, out_mode="col", out_dtypes=(BF16,), name=f"gmlp_in_dw_l{l}",
                      deps=behind_starts())
            scatter(dwi, "gmlp_w_in", a)
            dh1s = [_mm(dz, gw["gmlp_w_in", a], tb=True, bmode="col", out_dtypes=(BF16,), name=f"gmlp_in_dx_l{l}",
                        deps=behind_starts())]
        else:
            bl = l - n_a
            dwo = _mm(st["o"], dy, ta=True, out_dtypes=(BF16,), name=f"attn_out_dw_l{l}", deps=behind_starts())
            scatter(dwo.reshape(N_DEV, -1, D), "attn_wo", bl)
            do = _mm(dy, gw["attn_wo", bl], tb=True, bmode="row", out_dtypes=(BF16,), name=f"attn_out_dx_l{l}",
                     deps=behind_starts())
            dq, dk, dv, dfq, dfk = _attn_bwd(st["q"], kv["k"], kv["v"], st["o32"], do, st["lse"], kv["fk"],
                                             n_heads, f"attn_bwd_l{l}")
            dfk_list += [dfq, dfk]
            dk_list.append(dk)
            dv_list.append(dv)
            dqp, dqg = _head_norm_bwd(st["qp"], [dq], _row(q_norm_g[bl]), n_heads, f"q_norm_bwd_l{l}")
            d_qg[bl] = dqg[0]
            dwq = _mm(st["h1"], dqp, ta=True, out_dtypes=(BF16,), name=f"q_proj_dw_l{l}", deps=behind_starts())
            scatter(dwq.reshape(N_DEV, -1, D), "attn_wq", bl)
            dh1s = [_mm(dqp, gw["attn_wq", bl], tb=True, bmode="row", out_dtypes=(BF16,), name=f"q_proj_dx_l{l}",
                        deps=behind_starts())]
        below = (saved[l - 1]["m"], saved[l - 1]["g2"]) if l > 0 else None
        if below is None or l == n_a:
            dx, sums1 = _norm_mod_bwd(st["x_in"], dh1s, dx, st["ng1"], st["sc1"], f"norm1_bwd_l{l}")
        else:
            dx, dm_below, sums1, dg2_below = _norm_mod_bwd(st["x_in"], dh1s, dx, st["ng1"], st["sc1"],
                                                           f"norm1_bwd_l{l}", gated=below)
        d_mod[l] = jnp.stack([sums1[0], sums1[1], dg1[0], sums2[0], sums2[1], dg2[0]])
        d_norm_g[l] = jnp.stack([sums1[2], sums2[2]])
        if l == n_a:
            dkvp, dkg = _head_norm_bwd(kv["kvp"], dk_list, _row(k_norm_g), n_heads, "k_norm_bwd", tails=dv_list)
            dfc = [jnp.pad(d.reshape(n_heads, T).T, ((0, 0), (0, BLK - n_heads))) for d in dfk_list]
            dfl, dbf = _fcum_bwd(dfc, kv["fl"], b_f_pad, "fcum_bwd")
            dwkv = _mm(kv["hkv"], dkvp, ta=True, out_mode="col", out_dtypes=(BF16,), name="kv_proj_dw",
                       deps=behind_starts())
            scatter(dwkv, "w_kv", 0)
            dwf = _mm(kv["hkv"], dfl, ta=True, name="gate_logits_dw", deps=behind_starts())
            dh_a = _mm(dkvp, gw["w_kv", 0], tb=True, bmode="col", out_dtypes=(BF16,), name="kv_proj_dx")
            dh_b = _mm(dfl, w_f_pad, tb=True, out_dtypes=(BF16,), name="gate_logits_dx")
            dx, dm_below, sums_kv, dg2_below = _norm_mod_bwd(kv["x"], [dh_a, dh_b], dx, kv["ng"], kv["sc"],
                                                             "norm_kv_bwd", gated=below)
            small.update(d_kv_mod=jnp.stack([sums_kv[0], sums_kv[1]]), d_kv_norm_g=sums_kv[2], d_k_norm_g=dkg[0],
                         d_w_f=dwf[:, :n_heads], d_b_f=dbf[0, :n_heads])
        if l > 0:
            dm, dg2 = dm_below, dg2_below
        if l == 1:
            hi_contrib = [jnp.stack(d_mod[1:]).reshape(depth - 1, N_MOD * D), small["d_kv_mod"].reshape(-1)]
            hi_started = _a2a_start(_pack(hi_contrib), "gather_mod_grads_hi_start", bcast=True)
            tokens.append(hi_started[4])

    grad_x = dx.reshape(x.shape)

    lo_contrib = [d_mod[0].reshape(1, N_MOD * D)]
    rest_contrib = [jnp.stack(d_norm_g), jnp.stack(d_ln_g), jnp.stack(d_ln_b), jnp.stack(d_ws), jnp.stack(d_bs),
                    small["d_kv_norm_g"], small["d_k_norm_g"], small["d_w_f"], small["d_b_f"], jnp.stack(d_qg)]
    hi_shapes, lo_shapes = [a.shape for a in hi_contrib], [a.shape for a in lo_contrib]
    rest_shapes = [a.shape for a in rest_contrib]
    lo_started = _a2a_start(_pack(lo_contrib), "gather_mod_grads_lo_start", bcast=True, deps=behind_starts())
    rest_started = _a2a_start(_pack(rest_contrib), "gather_small_grads_start", bcast=True, deps=(lo_started[4],))
    grads, deltas, new_m, new_v = {}, {}, {}, {}

    def stack_of(n):
        lead = () if weights[n].ndim == 3 else (1,)
        return _AdamStack(*[a.reshape(lead + a.shape) for a in (weights[n], mom_m[n], mom_v[n])], f"adamw_{n}")

    def results_of(n, stack):
        grads[n], deltas[n], new_m[n], new_v[n] = [a.reshape(weights[n].shape) for a in stack.outs]

    sc_t = silu_all.T
    ada_stack, kv_ada_stack = stack_of("ada_w"), stack_of("kv_ada_w")
    is_me = (jnp.arange(N_DEV) == me).reshape(N_DEV, 1, 1)

    def landed_mods(started_pack, shapes, after, name):
        sent, land = _a2a_wait(started_pack, after, f"gather_{name}_wait", bcast=True)
        everyone = _unpack(jnp.where(is_me, sent[None], land), shapes, lead=(N_DEV,))
        return everyone, _unpack(_sum_slabs(land, sent, who, f"sum_{name}"), shapes)

    stacks = {n: stack_of(n) for n in ("mlp_w1", "mlp_w2", "gmlp_w_in", "gmlp_w_out", "attn_wq", "attn_wo", "w_kv")}
    after = rest_started[4]
    for l in reversed(range(depth)):
        if l == n_a - 1:
            (dmod_hi, dkvmod_all), (g_ada_b_hi, g_kv_ada_b) = landed_mods(hi_started, hi_shapes, after, "mod_grads_hi")
            for up in range(1, depth):
                ada_stack.from_outer(up, who, sc_t, _shard_of(dmod_hi[:, up - 1], 1, me, mod_cols))
            kv_ada_stack.from_outer(0, who, sc_t, _shard_of(dkvmod_all, 1, me, kv_cols))
            after = ada_stack.outs[0]
        if l == 0:
            (dmod_lo,), (g_ada_b_lo,) = landed_mods(lo_started, lo_shapes, after, "mod_grads_lo")
            ada_stack.from_outer(0, who, sc_t, _shard_of(dmod_lo[:, 0], 1, me, mod_cols))
            after = ada_stack.outs[0]
        keys = [("mlp_w2", l), ("mlp_w1", l)]
        keys += [("gmlp_w_out", l), ("gmlp_w_in", l)] if l < n_a else [("attn_wo", l - n_a), ("attn_wq", l - n_a)]
        keys += [("w_kv", 0)] if l == 0 else []
        landed = {}
        for key in keys:
            sent, land = _a2a_wait(started[key], after, f"scatter_{key[0]}_l{key[1]}_wait")
            landed[key] = (land, sent)
        for key in keys:
            stacks[key[0]].from_parts(key[1], who, *landed[key])
            after = stacks[key[0]].outs[0]
    for n, stack in stacks.items():
        results_of(n, stack)
    results_of("ada_w", ada_stack)
    results_of("kv_ada_w", kv_ada_stack)
    g_ada_b = jnp.concatenate([g_ada_b_lo, g_ada_b_hi], axis=0)

    rest_sent, rest_land = _a2a_wait(rest_started, after, "gather_small_grads_wait", bcast=True)
    (g_norm_g_full, g_ln_g_full, g_ln_b_full, g_ws, g_bs, g_kv_norm_g, g_k_norm_g, g_w_f_full, g_b_f,
     g_q_norm_g) = _unpack(_sum_slabs(rest_land, rest_sent, who, "sum_small_grads"), rest_shapes)
    small_grads = {
        "ada_b": g_ada_b, "kv_ada_b": g_kv_ada_b.reshape(kv_ada_b.shape),
        "norm_g": _shard_of(g_norm_g_full, 2, me, norm_g.shape[2]),
        "gmlp_ln_g": _shard_of(g_ln_g_full, 1, me, gmlp_ln_g.shape[1]),
        "gmlp_ln_b": _shard_of(g_ln_b_full, 1, me, gmlp_ln_b.shape[1]),
        "gmlp_ws": g_ws, "gmlp_bs": g_bs, "kv_norm_g": g_kv_norm_g, "k_norm_g": g_k_norm_g,
        "w_f": _shard_of(g_w_f_full, 0, me, w_f.shape[0]), "b_f": g_b_f, "q_norm_g": g_q_norm_g,
    }
    small_names = list(small_grads)
    small_w_shapes = [weights[n].shape for n in small_names]
    d_pack, m_pack, v_pack = _adamw_flat(_pack([weights[n] for n in small_names]),
                                         _pack([small_grads[n] for n in small_names]),
                                         _pack([mom_m[n] for n in small_names]), _pack([mom_v[n] for n in small_names]),
                                         "adamw_small")
    grads.update(small_grads)
    deltas.update(zip(small_names, _unpack(d_pack, small_w_shapes)))
    new_m.update(zip(small_names, _unpack(m_pack, small_w_shapes)))
    new_v.update(zip(small_names, _unpack(v_pack, small_w_shapes)))

    return (loss, grad_x, *[grads[n] for n in WEIGHT_NAMES], *[deltas[n] for n in WEIGHT_NAMES],
            *[new_m[n] for n in WEIGHT_NAMES], *[new_v[n] for n in WEIGHT_NAMES])
```

```python
import functools
import math

import jax
import jax.numpy as jnp
from jax import lax
from jax.experimental import pallas as pl
from jax.experimental.pallas import tpu as pltpu

F32 = jnp.float32
BF16 = jnp.bfloat16
N_DEV = 8
EPS = 1e-6
CHUNK = 64
BLK = 128
N_MOD = 6
ADAM_LR = 0.001
ADAM_B1 = 0.9
ADAM_B2 = 0.999
ADAM_EPS = 1e-08
ADAM_WD = 0.01
ADAM_STEP = 10
VMEM_LIMIT_BYTES = 56 * 2 ** 20
NEG_BIG = -1e30
WEIGHT_NAMES = ['ada_w', 'ada_b', 'norm_g', 'mlp_w1', 'mlp_w2', 'gmlp_w_in', 'gmlp_ln_g', 'gmlp_ln_b', 'gmlp_ws',
                'gmlp_bs', 'gmlp_w_out', 'kv_norm_g', 'kv_ada_w', 'kv_ada_b', 'w_kv', 'k_norm_g', 'w_f', 'b_f',
                'attn_wq', 'q_norm_g', 'attn_wo']
MESH = pl.DeviceIdType.MESH


def _params(sem):
    return pltpu.CompilerParams(dimension_semantics=sem, vmem_limit_bytes=VMEM_LIMIT_BYTES)


def _tile(n, cap, unit=128):
    if n <= cap:
        return n
    t = (cap // unit) * unit
    while t > unit and n % t:
        t -= unit
    assert n % t == 0, (n, cap, unit)
    return t


def _my_index():
    return 4 * lax.axis_index("x") + 2 * lax.axis_index("y") + lax.axis_index("c")


def _all_to_all(x, name, bcast=False):
    slab = x.shape if bcast else x.shape[1:]

    def body(x_ref, o_ref, send_sems, recv_sems, local_sem):
        me = _my_index()

        def src(j):
            return x_ref if bcast else x_ref.at[j]

        mine = pltpu.make_async_copy(src(me), o_ref.at[me], local_sem)
        mine.start()
        sends = []
        for d in range(1, N_DEV):
            peer = (me + d) % N_DEV
            cp = pltpu.make_async_remote_copy(
                src_ref=src(peer), dst_ref=o_ref.at[me],
                send_sem=send_sems.at[d - 1], recv_sem=recv_sems.at[d - 1],
                device_id=(peer // 4, (peer // 2) % 2, peer % 2), device_id_type=MESH)
            cp.start()
            sends.append(cp)
        for d in range(1, N_DEV):
            frm = (me + N_DEV - d) % N_DEV
            pltpu.make_async_remote_copy(
                src_ref=src(frm), dst_ref=o_ref.at[frm],
                send_sem=send_sems.at[d - 1], recv_sem=recv_sems.at[d - 1],
                device_id=(frm // 4, (frm // 2) % 2, frm % 2), device_id_type=MESH).wait_recv()
        for cp in sends:
            cp.wait_send()
        mine.wait()

    return pl.pallas_call(
        body, name=name,
        out_shape=jax.ShapeDtypeStruct((N_DEV,) + tuple(slab), x.dtype),
        in_specs=[pl.BlockSpec(memory_space=pl.ANY)],
        out_specs=pl.BlockSpec(memory_space=pl.ANY),
        scratch_shapes=[pltpu.SemaphoreType.DMA((N_DEV - 1,)), pltpu.SemaphoreType.DMA((N_DEV - 1,)),
                        pltpu.SemaphoreType.DMA],
        compiler_params=pltpu.CompilerParams(has_side_effects=True),
    )(x)


_HBM = pl.BlockSpec(memory_space=pltpu.HBM)
_SEM = pl.BlockSpec(memory_space=pltpu.SEMAPHORE)
_ANY = pl.BlockSpec(memory_space=pl.ANY)
_DATAFLOW = pltpu.SideEffectType.DATAFLOW_SIDE_EFFECTING


def _a2a_peer_copy(x_ref, land_ref, send_sems, recv_sems, d, me, incoming, bcast):
    peer = (me + N_DEV - d) % N_DEV if incoming else (me + d) % N_DEV
    return pltpu.make_async_remote_copy(
        src_ref=x_ref if bcast else x_ref.at[peer], dst_ref=land_ref.at[peer if incoming else me],
        send_sem=send_sems.at[d - 1], recv_sem=recv_sems.at[d - 1],
        device_id=(peer // 4, (peer // 2) % 2, peer % 2), device_id_type=MESH)


def _a2a_start(x, name, bcast=False, deps=()):
    land_shape = ((N_DEV,) + tuple(x.shape)) if bcast else x.shape
    n_dep = len(deps)

    def body(x_ref, land_ref, *rest):
        send_sems, recv_sems, token = rest[n_dep], rest[n_dep + 1], rest[-1]
        me = _my_index()
        for d in range(1, N_DEV):
            _a2a_peer_copy(x_ref, land_ref, send_sems, recv_sems, d, me, False, bcast).start()
        token[...] = jnp.zeros_like(token)

    return pl.pallas_call(
        body, name=name,
        out_shape=(pltpu.SemaphoreType.DMA((N_DEV - 1,)), pltpu.SemaphoreType.DMA((N_DEV - 1,)),
                   pltpu.HBM(x.shape, x.dtype), pltpu.HBM(land_shape, x.dtype), jax.ShapeDtypeStruct((8, BLK), F32)),
        in_specs=(_HBM, _HBM, *[_ANY for _ in deps]),
        out_specs=(_SEM, _SEM, _HBM, _HBM, pl.BlockSpec(memory_space=pltpu.VMEM)),
        input_output_aliases={0: 2, 1: 3},
        compiler_params=pltpu.CompilerParams(has_side_effects=_DATAFLOW),
    )(pltpu.with_memory_space_constraint(x, pltpu.HBM),
      pltpu.with_memory_space_constraint(lax.empty(land_shape, x.dtype), pltpu.HBM), *deps)


def _a2a_wait(started, after, name, bcast=False):
    send_sems, recv_sems, x_thru, land_thru, _ = started

    def body(x_ref, land_ref, send_sems, recv_sems, after_ref, x_dead, land_out):
        me = _my_index()
        for d in range(1, N_DEV):
            _a2a_peer_copy(x_ref, land_ref, send_sems, recv_sems, d, me, False, bcast).wait_send()
        for d in range(1, N_DEV):
            _a2a_peer_copy(x_ref, land_ref, send_sems, recv_sems, d, me, True, bcast).wait_recv()

    return pl.pallas_call(
        body, name=name,
        out_shape=(pltpu.HBM(x_thru.shape, x_thru.dtype), pltpu.HBM(land_thru.shape, land_thru.dtype)),
        in_specs=(_HBM, _HBM, _SEM, _SEM, _ANY), out_specs=(_HBM, _HBM),
        input_output_aliases={0: 0, 1: 1},
        compiler_params=pltpu.CompilerParams(has_side_effects=_DATAFLOW),
    )(x_thru, land_thru, send_sems, recv_sems, after)


def _place_shard(w, layer, who, name, deps=()):
    _, R, C = w.shape
    tr = _row_tile(R, C)

    def body(who_ref, w_ref, *rest):
        rest[-1][...] = w_ref[...].astype(BF16)

    return pl.pallas_call(
        body, name=name,
        grid_spec=pltpu.PrefetchScalarGridSpec(
            num_scalar_prefetch=1, grid=(R // tr,),
            in_specs=[pl.BlockSpec((None, tr, C), lambda i, who_ref: (layer, i, 0))] + [_ANY for _ in deps],
            out_specs=pl.BlockSpec((None, None, tr, C), lambda i, who_ref: (who_ref[0], 0, i, 0))),
        out_shape=_sds((N_DEV, 1, R, C), BF16),
        compiler_params=_params(("parallel",)),
    )(who, w, *deps)


def _gather_copies(land_ref, send_sems, recv_sems, base, phase, incoming):
    cx, cy, cc = lax.axis_index("x"), lax.axis_index("y"), lax.axis_index("c")
    sibling = (cx, cy, 1 - cc)
    chips = [(1 - cx, cy), (cx, 1 - cy), (1 - cx, 1 - cy)]
    if phase == 1:
        out = [((cx, cy, cc), sibling)] + [((cx, cy, cc), (*chip, cc)) for chip in chips]
        inc = [(sibling, sibling)] + [((*chip, cc), (*chip, cc)) for chip in chips]
    else:
        out = [((*chip, cc), sibling) for chip in chips]
        inc = [((*chip, 1 - cc), sibling) for chip in chips]
    copies = []
    for k, (block, peer) in enumerate(inc if incoming else out):
        slab = land_ref.at[4 * block[0] + 2 * block[1] + block[2]]
        copies.append(pltpu.make_async_remote_copy(
            src_ref=slab, dst_ref=slab, send_sem=send_sems.at[base + k], recv_sem=recv_sems.at[base + k],
            device_id=peer, device_id_type=MESH))
    return copies


def _gather_start(lands, phase, name, deps=()):
    n, per, n_dep = len(lands), (4 if phase == 1 else 3), len(deps)

    def body(*refs):
        land_refs, send_sems, recv_sems, token = refs[:n], refs[n + n_dep], refs[n + n_dep + 1], refs[-1]
        for a, land_ref in enumerate(land_refs):
            for cp in _gather_copies(land_ref, send_sems, recv_sems, a * per, phase, False):
                cp.start()
        token[...] = jnp.zeros_like(token)

    outs = pl.pallas_call(
        body, name=name,
        out_shape=(pltpu.SemaphoreType.DMA((n * per,)), pltpu.SemaphoreType.DMA((n * per,)),
                   *[pltpu.HBM(x.shape, x.dtype) for x in lands], jax.ShapeDtypeStruct((8, BLK), F32)),
        in_specs=(*[_HBM for _ in lands], *[_ANY for _ in deps]),
        out_specs=(_SEM, _SEM, *[_HBM for _ in lands], pl.BlockSpec(memory_space=pltpu.VMEM)),
        input_output_aliases={a: 2 + a for a in range(n)},
        compiler_params=pltpu.CompilerParams(has_side_effects=_DATAFLOW),
    )(*[pltpu.with_memory_space_constraint(x, pltpu.HBM) for x in lands], *deps)
    return outs[0], outs[1], list(outs[2:2 + n]), outs[-1]


def _gather_wait(started, phase, after, name):
    send_sems, recv_sems, lands, _ = started
    n, per = len(lands), (4 if phase == 1 else 3)

    def body(*refs):
        land_refs, send_sems, recv_sems = refs[:n], refs[n], refs[n + 1]
        for a, land_ref in enumerate(land_refs):
            for cp in _gather_copies(land_ref, send_sems, recv_sems, a * per, phase, False):
                cp.wait_send()
            for cp in _gather_copies(land_ref, send_sems, recv_sems, a * per, phase, True):
                cp.wait_recv()

    outs = pl.pallas_call(
        body, name=name,
        out_shape=tuple(pltpu.HBM(x.shape, x.dtype) for x in lands),
        in_specs=(*[_HBM for _ in lands], _SEM, _SEM, _ANY), out_specs=tuple(_HBM for _ in lands),
        input_output_aliases={a: a for a in range(n)},
        compiler_params=pltpu.CompilerParams(has_side_effects=_DATAFLOW),
    )(*lands, send_sems, recv_sems, after)
    return list(outs)


def _mm(a, b, *, name, ta=False, tb=False, bmode="plain", layer=0, out_mode="plain", out_dtypes=(F32,),
        epilogue=None, extra=(), caps=(1024, 1024, 2048), deps=()):
    if ta:
        K, M = a.shape
    else:
        M, K = a.shape
    n_unit = k_unit = None
    if bmode == "plain":
        N, Kb = (b.shape if tb else b.shape[::-1])
    elif bmode == "col":
        _, _, Kw, Ns = b.shape
        if tb:
            N, Kb, k_unit = Kw, N_DEV * Ns, Ns
        else:
            N, Kb, n_unit = N_DEV * Ns, Kw, Ns
    else:
        _, _, Ks, Nw = b.shape
        if tb:
            N, Kb, n_unit = N_DEV * Ks, Nw, Ks
        else:
            N, Kb, k_unit = Nw, N_DEV * Ks, Ks
    assert K == Kb, (name, a.shape, b.shape)
    if out_mode == "col":
        assert n_unit is None
        n_unit = N // N_DEV
    tm = _tile(M, caps[0])
    tn = _tile(n_unit or N, caps[1])
    span = 1
    if k_unit and 2 * k_unit <= caps[2]:
        while 2 * span * k_unit <= caps[2] and N_DEV % (2 * span) == 0:
            span *= 2
        tk = span * k_unit
    else:
        tk = _tile(k_unit or K, caps[2])
    nk = K // tk
    npb = (n_unit // tn) if n_unit else None
    kpb = (k_unit // tk) if (k_unit and span == 1) else None
    grid = (M // tm, N // tn, nk)

    a_spec = pl.BlockSpec((tk, tm), lambda i, j, k: (k, i)) if ta else pl.BlockSpec((tm, tk), lambda i, j, k: (i, k))
    if bmode == "plain":
        b_spec = (pl.BlockSpec((tn, tk), lambda i, j, k: (j, k)) if tb
                  else pl.BlockSpec((tk, tn), lambda i, j, k: (k, j)))
    elif bmode == "col":
        if tb and span > 1:
            b_spec = pl.BlockSpec((span, None, tn, k_unit), lambda i, j, k: (k, layer, j, 0))
        elif tb:
            b_spec = pl.BlockSpec((None, None, tn, tk), lambda i, j, k: (k // kpb, layer, j, k % kpb))
        else:
            b_spec = pl.BlockSpec((None, None, tk, tn), lambda i, j, k: (j // npb, layer, k, j % npb))
    else:
        if tb:
            b_spec = pl.BlockSpec((None, None, tn, tk), lambda i, j, k: (j // npb, layer, j % npb, k))
        elif span > 1:
            b_spec = pl.BlockSpec((span, None, k_unit, tn), lambda i, j, k: (k, layer, 0, j))
        else:
            b_spec = pl.BlockSpec((None, None, tk, tn), lambda i, j, k: (k // kpb, layer, k % kpb, j))
    mn_spec = pl.BlockSpec((tm, tn), lambda i, j, k: (i, j))
    if out_mode == "col":
        o_specs = [pl.BlockSpec((None, tm, tn), lambda i, j, k: (j // npb, i, j % npb))]
        o_shapes = [jax.ShapeDtypeStruct((N_DEV, M, N // N_DEV), out_dtypes[0])]
    else:
        o_specs = [mn_spec for _ in out_dtypes]
        o_shapes = [jax.ShapeDtypeStruct((M, N), dt) for dt in out_dtypes]
    dims = (((0 if ta else 1,), (1 if tb else 0,)), ((), ()))
    n_extra, n_out, n_dep = len(extra), len(out_dtypes), len(deps)

    def body(a_ref, b_ref, *rest):
        extra_refs, out_refs = rest[:n_extra], rest[n_extra + n_dep:n_extra + n_dep + n_out]
        k = pl.program_id(2)

        def product():
            if span == 1:
                return lax.dot_general(a_ref[...].astype(BF16), b_ref[...].astype(BF16), dims,
                                       preferred_element_type=F32)
            if not tb:
                return lax.dot_general(a_ref[...].astype(BF16), b_ref[...].reshape(tk, tn).astype(BF16), dims,
                                       preferred_element_type=F32)
            out = None
            for s in range(span):
                part = lax.dot_general(a_ref[:, s * k_unit:(s + 1) * k_unit].astype(BF16), b_ref[s].astype(BF16),
                                       dims, preferred_element_type=F32)
                out = part if out is None else out + part
            return out

        def finish(acc):
            outs = (acc,) if epilogue is None else epilogue(acc, *[r[...] for r in extra_refs])
            for o_ref, val in zip(out_refs, outs):
                o_ref[...] = val.astype(o_ref.dtype)

        if nk == 1:
            finish(product())
            return
        acc_ref = rest[-1]

        @pl.when(k == 0)
        def _():
            acc_ref[...] = product()

        if nk > 2:
            @pl.when(jnp.logical_and(k > 0, k < nk - 1))
            def _():
                acc_ref[...] += product()

        @pl.when(k == nk - 1)
        def _():
            finish(acc_ref[...] + product())

    outs = pl.pallas_call(
        body, name=name, grid=grid,
        in_specs=[a_spec, b_spec] + [mn_spec for _ in extra] + [_ANY for _ in deps],
        out_specs=o_specs, out_shape=o_shapes,
        scratch_shapes=[pltpu.VMEM((tm, tn), F32)] if nk > 1 else [],
        compiler_params=_params(("parallel", "parallel", "arbitrary")),
    )(a, b, *extra, *deps)
    return outs[0] if n_out == 1 else outs


def _relu2_epilogue(acc):
    r = jnp.maximum(acc, 0.0)
    return acc, r * r


def _relu2_bwd_epilogue(acc, a_pre):
    return (acc * (2.0 * jnp.maximum(a_pre.astype(F32), 0.0)),)


def _rowcall(body, *, name, tr, row_ins, full_ins=(), row_outs=(), acc_outs=(), scratch=(), reverse=False):
    T = row_ins[0].shape[0]
    nb = T // tr
    rmap = (lambda i: (nb - 1 - i, 0)) if reverse else (lambda i: (i, 0))

    def full_spec(shape):
        nd = len(shape)
        return pl.BlockSpec(tuple(shape), lambda i: (0,) * nd)

    in_specs = [pl.BlockSpec((tr, a.shape[1]), rmap) for a in row_ins] + [full_spec(a.shape) for a in full_ins]
    out_specs = [pl.BlockSpec((tr, s.shape[1]), rmap) for s in row_outs] + [full_spec(s.shape) for s in acc_outs]
    outs = pl.pallas_call(
        body, name=name, grid=(nb,), in_specs=in_specs, out_specs=out_specs,
        out_shape=list(row_outs) + list(acc_outs), scratch_shapes=list(scratch),
        compiler_params=_params(("arbitrary",)),
    )(*row_ins, *full_ins)
    return outs


def _sds(shape, dtype):
    return jax.ShapeDtypeStruct(tuple(shape), dtype)


def _row_tile(T, C, elems=512 * 1024):
    t = max(8, min(T, elems // C))
    p = 8
    while p * 2 <= t and T % (p * 2) == 0:
        p *= 2
    return p


def _norm_mod(x, ng, sc, sh, name):
    T, D = x.shape

    def body(x_ref, ng_ref, sc_ref, sh_ref, h_ref):
        xv = x_ref[...]
        r = lax.rsqrt(jnp.mean(xv * xv, axis=-1, keepdims=True) + EPS)
        h_ref[...] = (((xv * r) * ng_ref[...]) * (1.0 + sc_ref[...]) + sh_ref[...]).astype(BF16)

    return _rowcall(body, name=name, tr=_row_tile(T, D), row_ins=[x], full_ins=[ng, sc, sh],
                    row_outs=[_sds((T, D), BF16)])[0]


def _res_norm_mod(x, y, gate, ng, sc, sh, name):
    T, D = x.shape

    def body(x_ref, y_ref, g_ref, ng_ref, sc_ref, sh_ref, x2_ref, h_ref):
        xv = x_ref[...] + g_ref[...] * y_ref[...]
        x2_ref[...] = xv
        r = lax.rsqrt(jnp.mean(xv * xv, axis=-1, keepdims=True) + EPS)
        h_ref[...] = (((xv * r) * ng_ref[...]) * (1.0 + sc_ref[...]) + sh_ref[...]).astype(BF16)

    return _rowcall(body, name=name, tr=_row_tile(T, D, 256 * 1024), row_ins=[x, y], full_ins=[gate, ng, sc, sh],
                    row_outs=[_sds((T, D), F32), _sds((T, D), BF16)])


def _res_loss(x, y, gate, target, name):
    T, D = x.shape

    def body(x_ref, y_ref, t_ref, g_ref, dout_ref, dy_ref, loss_ref, dg_ref):
        @pl.when(pl.program_id(0) == 0)
        def _():
            loss_ref[...] = jnp.zeros_like(loss_ref)
            dg_ref[...] = jnp.zeros_like(dg_ref)

        yv, gv = y_ref[...], g_ref[...]
        diff = x_ref[...] + gv * yv - t_ref[...]
        dout = diff * (1.0 / D)
        dout_ref[...] = dout
        dy_ref[...] = (dout * gv).astype(BF16)
        loss_ref[...] += jnp.sum(diff * diff) * (0.5 / D)
        dg_ref[...] += jnp.sum(dout * yv, axis=0, keepdims=True)

    return _rowcall(body, name=name, tr=_row_tile(T, D, 256 * 1024), row_ins=[x, y, target], full_ins=[gate],
                    row_outs=[_sds((T, D), F32), _sds((T, D), BF16)],
                    acc_outs=[_sds((1, BLK), F32), _sds((1, D), F32)])


def _norm_mod_bwd(x, dhs, dres, ng, sc, name, gated=None):
    T, D = x.shape
    n_dh = len(dhs)
    n_row = 2 + n_dh + (1 if gated else 0)

    def body(*refs):
        x_ref, dh_refs, dres_ref = refs[0], refs[1:1 + n_dh], refs[1 + n_dh]
        ng_ref, sc_ref = refs[n_row:n_row + 2]
        outs = refs[n_row + (3 if gated else 2):]
        dx_ref, sums_ref = (outs[0], outs[2]) if gated else (outs[0], outs[1])

        @pl.when(pl.program_id(0) == 0)
        def _():
            sums_ref[...] = jnp.zeros_like(sums_ref)
            if gated:
                outs[3][...] = jnp.zeros_like(outs[3])

        xv = x_ref[...]
        dh = dh_refs[0][...].astype(F32)
        for r_ in dh_refs[1:]:
            dh = dh + r_[...].astype(F32)
        r = lax.rsqrt(jnp.mean(xv * xv, axis=-1, keepdims=True) + EPS)
        n = xv * r
        ngv, scale1 = ng_ref[...], 1.0 + sc_ref[...]
        dn = dh * (ngv * scale1)
        dx = dres_ref[...] + r * (dn - n * jnp.mean(dn * n, axis=-1, keepdims=True))
        dx_ref[...] = dx
        dhn = dh * n
        sums_ref[0:1, :] += jnp.sum(dh, axis=0, keepdims=True)
        sums_ref[1:2, :] += jnp.sum(dhn * ngv, axis=0, keepdims=True)
        sums_ref[2:3, :] += jnp.sum(dhn * scale1, axis=0, keepdims=True)
        if gated:
            y_ref, g_ref = refs[2 + n_dh], refs[n_row + 2]
            outs[1][...] = (dx * g_ref[...]).astype(BF16)
            outs[3][...] += jnp.sum(dx * y_ref[...], axis=0, keepdims=True)

    return _rowcall(body, name=name, tr=_row_tile(T, D, 256 * 1024),
                    row_ins=[x, *dhs, dres] + ([gated[0]] if gated else []),
                    full_ins=[ng, sc] + ([gated[1]] if gated else []),
                    row_outs=[_sds((T, D), F32)] + ([_sds((T, D), BF16)] if gated else []),
                    acc_outs=[_sds((8, D), F32)] + ([_sds((1, D), F32)] if gated else []))


def _head_norm(x, g, n_heads, name, tail=False):
    T = x.shape[0]
    D = n_heads * BLK
    W = x.shape[1] if tail else D

    def body(x_ref, g_ref, o_ref, *tail_ref):
        for h in range(n_heads):
            xv = x_ref[:, h * BLK:(h + 1) * BLK]
            r = lax.rsqrt(jnp.mean(xv * xv, axis=-1, keepdims=True) + EPS)
            o_ref[:, h * BLK:(h + 1) * BLK] = ((xv * r) * g_ref[...]).astype(BF16)
        if tail:
            tail_ref[0][...] = x_ref[:, D:2 * D].astype(BF16)

    tr = _row_tile(T, x.shape[1])
    o_spec = pl.BlockSpec((tr, D), lambda i: (i, 0))
    return pl.pallas_call(
        body, name=name, grid=(T // tr,),
        in_specs=[pl.BlockSpec((tr, W), lambda i: (i, 0)), pl.BlockSpec((1, BLK), lambda i: (0, 0))],
        out_specs=[o_spec] * (2 if tail else 1), out_shape=[_sds((T, D), BF16)] * (2 if tail else 1),
        compiler_params=_params(("parallel",)),
    )(x, g)


def _head_norm_bwd(x, dys, g, n_heads, name, tails=()):
    T = x.shape[0]
    D = n_heads * BLK
    n_dy, n_tail = len(dys), len(tails)
    W = 2 * D if tails else D

    def body(*refs):
        x_ref, dy_refs, tail_refs = refs[0], refs[1:1 + n_dy], refs[1 + n_dy:1 + n_dy + n_tail]
        g_ref, dx_ref, dg_ref = refs[1 + n_dy + n_tail:]

        @pl.when(pl.program_id(0) == 0)
        def _():
            dg_ref[...] = jnp.zeros_like(dg_ref)

        tot = jnp.zeros((1, BLK), F32)
        for h in range(n_heads):
            cols = slice(h * BLK, (h + 1) * BLK)
            xv = x_ref[:, cols]
            dyv = dy_refs[0][:, cols]
            for r_ in dy_refs[1:]:
                dyv = dyv + r_[:, cols]
            r = lax.rsqrt(jnp.mean(xv * xv, axis=-1, keepdims=True) + EPS)
            n = xv * r
            dn = dyv * g_ref[...]
            dx_ref[:, cols] = (r * (dn - n * jnp.mean(dn * n, axis=-1, keepdims=True))).astype(BF16)
            tot = tot + jnp.sum(dyv * n, axis=0, keepdims=True)
        dg_ref[0:1, :] += tot
        if n_tail:
            tv = tail_refs[0][...]
            for r_ in tail_refs[1:]:
                tv = tv + r_[...]
            dx_ref[:, D:] = tv.astype(BF16)

    tr = _row_tile(T, 2 * D, 256 * 1024)
    d_spec = pl.BlockSpec((tr, D), lambda i: (i, 0))
    return pl.pallas_call(
        body, name=name, grid=(T // tr,),
        in_specs=[d_spec] * (1 + n_dy + n_tail) + [pl.BlockSpec((1, BLK), lambda i: (0, 0))],
        out_specs=[pl.BlockSpec((tr, W), lambda i: (i, 0)), pl.BlockSpec((8, BLK), lambda i: (0, 0))],
        out_shape=[_sds((T, W), BF16), _sds((8, BLK), F32)],
        compiler_params=_params(("arbitrary",)),
    )(x, *dys, *tails, g)


def _fcum_fwd(fl, bf, name):
    T = fl.shape[0]

    def body(fl_ref, b_ref, o_ref, carry_ref):
        @pl.when(pl.program_id(0) == 0)
        def _():
            carry_ref[...] = jnp.zeros_like(carry_ref)

        z = fl_ref[...] + b_ref[...]
        logf = jnp.minimum(z, 0.0) - jnp.log(1.0 + jnp.exp(-jnp.abs(z)))
        row = lax.broadcasted_iota(jnp.int32, (BLK, BLK), 0)
        col = lax.broadcasted_iota(jnp.int32, (BLK, BLK), 1)
        tri = (col <= row).astype(F32)
        run = jnp.dot(tri, logf, preferred_element_type=F32, precision=lax.Precision.HIGHEST) + carry_ref[0:1, :]
        o_ref[...] = run
        carry_ref[0:1, :] = run[BLK - 1:BLK, :]

    return _rowcall(body, name=name, tr=BLK, row_ins=[fl], full_ins=[bf], row_outs=[_sds((T, BLK), F32)],
                    scratch=[pltpu.VMEM((8, BLK), F32)])[0]


def _fcum_bwd(dfs, fl, bf, name):
    T = fl.shape[0]
    n_df = len(dfs)

    def body(*refs):
        df_refs = refs[:n_df]
        fl_ref, b_ref, dfl_ref, dbias_ref, carry_ref = refs[n_df:]

        @pl.when(pl.program_id(0) == 0)
        def _():
            carry_ref[...] = jnp.zeros_like(carry_ref)
            dbias_ref[...] = jnp.zeros_like(dbias_ref)

        dfc = df_refs[0][...]
        for r_ in df_refs[1:]:
            dfc = dfc + r_[...]
        row = lax.broadcasted_iota(jnp.int32, (BLK, BLK), 0)
        col = lax.broadcasted_iota(jnp.int32, (BLK, BLK), 1)
        tri = (col >= row).astype(F32)
        suffix = jnp.dot(tri, dfc, preferred_element_type=F32, precision=lax.Precision.HIGHEST) + carry_ref[0:1, :]
        carry_ref[0:1, :] = suffix[0:1, :]
        z = fl_ref[...] + b_ref[...]
        dfl = suffix / (1.0 + jnp.exp(z))
        dfl_ref[...] = dfl.astype(BF16)
        dbias_ref[0:1, :] += jnp.sum(dfl, axis=0, keepdims=True)

    return _rowcall(body, name=name, tr=BLK, row_ins=[*dfs, fl], full_ins=[bf], reverse=True,
                    row_outs=[_sds((T, BLK), BF16)], acc_outs=[_sds((8, BLK), F32)],
                    scratch=[pltpu.VMEM((8, BLK), F32)])


def _attn_tile(T):
    return min(T, 512)


def _attn_fwd(q, k, v, fk, n_heads, name):
    T = q.shape[0]
    tq = tk = _attn_tile(T)
    nkb = T // tk
    inv_sqrt = 1.0 / float(math.sqrt(BLK))

    def body(q_ref, k_ref, v_ref, fk_ref, o_ref, o32_ref, lse_ref):
        i = pl.program_id(1)
        qv = q_ref[...]

        def block(j, carry, diagonal):
            m, l, acc = carry
            rows = pl.ds(pl.multiple_of(j * tk, tk), tk)
            kj, vj = k_ref[rows, :], v_ref[rows, :]
            s = lax.dot_general(qv, kj, (((1,), (1,)), ((), ())), preferred_element_type=F32) * inv_sqrt
            s = s - fk_ref[j]
            if diagonal:
                s = jnp.where(lax.broadcasted_iota(jnp.int32, (tq, tk), 1)
                              <= lax.broadcasted_iota(jnp.int32, (tq, tk), 0), s, NEG_BIG)
            m_new = jnp.maximum(m, jnp.max(s, axis=-1, keepdims=True))
            alpha = jnp.exp(m - m_new)
            p = jnp.exp(s - m_new)
            l = alpha * l + jnp.sum(p, axis=-1, keepdims=True)
            acc = alpha * acc + jnp.dot(p.astype(BF16), vj, preferred_element_type=F32)
            return m_new, l, acc

        init = (jnp.full((tq, 1), NEG_BIG, F32), jnp.zeros((tq, 1), F32), jnp.zeros((tq, BLK), F32))
        carry = lax.fori_loop(0, i, lambda j, c: block(j, c, False), init)
        m, l, acc = block(i, carry, True)
        out = acc / l
        o_ref[...] = out.astype(BF16)
        o32_ref[...] = out
        lse_ref[...] = m + jnp.log(l)

    return pl.pallas_call(
        body, name=name, grid=(n_heads, T // tq),
        in_specs=[pl.BlockSpec((tq, BLK), lambda h, i: (i, h)),
                  pl.BlockSpec((T, BLK), lambda h, i: (0, h)),
                  pl.BlockSpec((T, BLK), lambda h, i: (0, h)),
                  pl.BlockSpec((None, nkb, 1, tk), lambda h, i: (h, 0, 0, 0))],
        out_specs=[pl.BlockSpec((tq, BLK), lambda h, i: (i, h)),
                   pl.BlockSpec((tq, BLK), lambda h, i: (i, h)),
                   pl.BlockSpec((None, tq, 1), lambda h, i: (h, i, 0))],
        out_shape=[_sds((T, n_heads * BLK), BF16), _sds((T, n_heads * BLK), F32), _sds((n_heads, T, 1), F32)],
        compiler_params=_params(("parallel", "arbitrary")),
    )(q, k, v, fk)


def _attn_bwd(q, k, v, o, do, lse, fk, n_heads, name):
    T = q.shape[0]
    tq = tk = _attn_tile(T)
    nkb = T // tk
    nq = T // tq
    inv_sqrt = 1.0 / float(math.sqrt(BLK))
    tn_dims = (((0,), (0,)), ((), ()))
    nt_dims = (((1,), (1,)), ((), ()))

    def body(q_ref, k_ref, v_ref, o_ref, do_ref, lse_ref, fk_ref, dq_ref, dk_ref, dv_ref, dfq_ref, dfk_ref, delta_ref):
        j = pl.program_id(1)

        @pl.when(j == 0)
        def _():
            delta_ref[...] = jnp.sum(do_ref[...].astype(F32) * o_ref[...], axis=1, keepdims=True)
            dq_ref[...] = jnp.zeros_like(dq_ref)
            dfq_ref[...] = jnp.zeros_like(dfq_ref)

        kj, vj, fkv = k_ref[...], v_ref[...], fk_ref[...]

        def step(i, carry, diagonal):
            dk, dv, dfk = carry
            rows = pl.ds(pl.multiple_of(i * tq, tq), tq)
            qi, doi = q_ref[rows, :], do_ref[rows, :]
            s = lax.dot_general(qi, kj, nt_dims, preferred_element_type=F32) * inv_sqrt - fkv
            if diagonal:
                s = jnp.where(lax.broadcasted_iota(jnp.int32, (tq, tk), 1)
                              <= lax.broadcasted_iota(jnp.int32, (tq, tk), 0), s, NEG_BIG)
            p = jnp.exp(s - lse_ref[rows, :])
            dv = dv + lax.dot_general(p.astype(BF16), doi, tn_dims, preferred_element_type=F32)
            dp = lax.dot_general(doi, vj, nt_dims, preferred_element_type=F32)
            ds = p * (dp - delta_ref[rows, :])
            dsb = ds.astype(BF16)
            dq_ref[rows, :] += jnp.dot(dsb, kj, preferred_element_type=F32) * inv_sqrt
            dk = dk + lax.dot_general(dsb, qi, tn_dims, preferred_element_type=F32)
            dfq_ref[rows, :] += jnp.sum(ds, axis=1, keepdims=True)
            dfk = dfk - jnp.sum(ds, axis=0, keepdims=True)
            return dk, dv, dfk

        init = (jnp.zeros((tk, BLK), F32), jnp.zeros((tk, BLK), F32), jnp.zeros((1, tk), F32))
        carry = step(j, init, True)
        dk, dv, dfk = lax.fori_loop(j + 1, nq, lambda i, c: step(i, c, False), carry)
        dk_ref[...] = dk * inv_sqrt
        dv_ref[...] = dv
        dfk_ref[...] = dfk

    head_col = lambda h, j: (0, h)
    return pl.pallas_call(
        body, name=name, grid=(n_heads, nkb),
        in_specs=[pl.BlockSpec((T, BLK), head_col),
                  pl.BlockSpec((tk, BLK), lambda h, j: (j, h)),
                  pl.BlockSpec((tk, BLK), lambda h, j: (j, h)),
                  pl.BlockSpec((T, BLK), head_col),
                  pl.BlockSpec((T, BLK), head_col),
                  pl.BlockSpec((None, T, 1), lambda h, j: (h, 0, 0)),
                  pl.BlockSpec((None, None, 1, tk), lambda h, j: (h, j, 0, 0))],
        out_specs=[pl.BlockSpec((T, BLK), head_col),
                   pl.BlockSpec((tk, BLK), lambda h, j: (j, h)),
                   pl.BlockSpec((tk, BLK), lambda h, j: (j, h)),
                   pl.BlockSpec((None, T, 1), lambda h, j: (h, 0, 0)),
                   pl.BlockSpec((None, None, 1, tk), lambda h, j: (h, j, 0, 0))],
        out_shape=[_sds((T, n_heads * BLK), F32), _sds((T, n_heads * BLK), F32), _sds((T, n_heads * BLK), F32),
                   _sds((n_heads, T, 1), F32), _sds((n_heads, nkb, 1, tk), F32)],
        scratch_shapes=[pltpu.VMEM((T, 1), F32)],
        compiler_params=_params(("parallel", "arbitrary")),
    )(q, k, v, o, do, lse, fk)


_INV_SQRT2 = 1.0 / math.sqrt(2.0)
_INV_SQRT_2PI = 1.0 / math.sqrt(2.0 * math.pi)


def _gelu_parts(z):
    cdf = 0.5 * (1.0 + lax.erf(z * _INV_SQRT2))
    return cdf, z * cdf


def _mix_mask(transposed):
    row = lax.broadcasted_iota(jnp.int32, (BLK, BLK), 0) // CHUNK
    col = lax.broadcasted_iota(jnp.int32, (BLK, BLK), 1) // CHUNK
    return (row <= col) if transposed else (col <= row)


def _gmlp_mid_fwd(zpre, ln_g, ln_b, ws, bs_t, name):
    T, two_h = zpre.shape
    Hh = two_h // 2
    G = ws.shape[0]
    gd = Hh // G

    def body(z_ref, lg_ref, lb_ref, ws_ref, bs_ref, p_ref):
        _, zg = _gelu_parts(z_ref[...].astype(F32))
        u, v = zg[:, :Hh], zg[:, Hh:]
        mu = jnp.mean(v, axis=-1, keepdims=True)
        vc = v - mu
        rstd = lax.rsqrt(jnp.mean(vc * vc, axis=-1, keepdims=True) + EPS)
        vn = ((vc * rstd) * lg_ref[...] + lb_ref[...]).astype(BF16)
        mask = _mix_mask(False)
        for g in range(G):
            wm = jnp.where(mask, ws_ref[g], 0.0).astype(BF16)
            sv = jnp.dot(wm, vn[:, g * gd:(g + 1) * gd], preferred_element_type=F32) + bs_ref[:, g:g + 1]
            p_ref[:, g * gd:(g + 1) * gd] = (u[:, g * gd:(g + 1) * gd] * sv).astype(BF16)

    return _rowcall(body, name=name, tr=BLK, row_ins=[zpre], full_ins=[ln_g, ln_b, ws, bs_t],
                    row_outs=[_sds((T, Hh), BF16)])[0]


def _gmlp_mid_bwd(zpre, dp, ln_g, ln_b, ws, ws_t, bs_t, name):
    T, two_h = zpre.shape
    Hh = two_h // 2
    G = ws.shape[0]
    gd = Hh // G
    nt_dims = (((1,), (1,)), ((), ()))

    def body(z_ref, dp_ref, lg_ref, lb_ref, ws_ref, wst_ref, bs_ref, dz_ref, dws_ref, dbs_ref, dlg_ref, dlb_ref,
             dvn_ref):
        @pl.when(pl.program_id(0) == 0)
        def _():
            dws_ref[...] = jnp.zeros_like(dws_ref)
            dbs_ref[...] = jnp.zeros_like(dbs_ref)
            dlg_ref[...] = jnp.zeros_like(dlg_ref)
            dlb_ref[...] = jnp.zeros_like(dlb_ref)

        z = z_ref[...].astype(F32)
        cdf, zg = _gelu_parts(z)
        dgelu = cdf + z * (jnp.exp(-0.5 * z * z) * _INV_SQRT_2PI)
        u, v = zg[:, :Hh], zg[:, Hh:]
        mu = jnp.mean(v, axis=-1, keepdims=True)
        vc = v - mu
        rstd = lax.rsqrt(jnp.mean(vc * vc, axis=-1, keepdims=True) + EPS)
        vhat = vc * rstd
        vn = (vhat * lg_ref[...] + lb_ref[...]).astype(BF16)
        mask, mask_t = _mix_mask(False), _mix_mask(True)
        lane = lax.broadcasted_iota(jnp.int32, (BLK, BLK), 1)
        dbs = jnp.zeros((BLK, BLK), F32)
        for g in range(G):
            cols = slice(g * gd, (g + 1) * gd)
            wm = jnp.where(mask, ws_ref[g], 0.0).astype(BF16)
            wm_t = jnp.where(mask_t, wst_ref[g], 0.0).astype(BF16)
            vn_g = vn[:, cols]
            sv = jnp.dot(wm, vn_g, preferred_element_type=F32) + bs_ref[:, g:g + 1]
            dp_g = dp_ref[:, cols].astype(F32)
            dz_ref[:, cols] = ((dp_g * sv) * dgelu[:, cols]).astype(BF16)
            dsv = dp_g * u[:, cols]
            dsv_b = dsv.astype(BF16)
            dbs = dbs + jnp.where(lane == g, jnp.sum(dsv, axis=1, keepdims=True), 0.0)
            dws_ref[g] += jnp.where(mask, lax.dot_general(dsv_b, vn_g, nt_dims, preferred_element_type=F32), 0.0)
            dvn_ref[:, cols] = jnp.dot(wm_t, dsv_b, preferred_element_type=F32)
        dbs_ref[...] += dbs
        dvn = dvn_ref[...]
        dlg_ref[0:1, :] += jnp.sum(dvn * vhat, axis=0, keepdims=True)
        dlb_ref[0:1, :] += jnp.sum(dvn, axis=0, keepdims=True)
        dvh = dvn * lg_ref[...]
        dv = rstd * (dvh - jnp.mean(dvh, axis=-1, keepdims=True) - vhat * jnp.mean(dvh * vhat, axis=-1, keepdims=True))
        dz_ref[:, Hh:] = (dv * dgelu[:, Hh:]).astype(BF16)

    return _rowcall(body, name=name, tr=BLK, row_ins=[zpre, dp], full_ins=[ln_g, ln_b, ws, ws_t, bs_t],
                    row_outs=[_sds((T, two_h), BF16)],
                    acc_outs=[_sds((G, BLK, BLK), F32), _sds((BLK, BLK), F32), _sds((8, Hh), F32), _sds((8, Hh), F32)],
                    scratch=[pltpu.VMEM((BLK, Hh), F32)])


def _mods(c_all, w, layer, bias, name):
    nb, K = c_all.shape
    N = w.shape[-1]
    tn = _tile(N, 512)

    def body(c_ref, w_ref, b_ref, o_ref):
        cv = c_ref[...]
        sc = cv / (1.0 + jnp.exp(-cv))
        o_ref[...] = jnp.dot(sc, w_ref[...], preferred_element_type=F32, precision=lax.Precision.HIGHEST) + b_ref[...]

    return pl.pallas_call(
        body, name=name, grid=(N // tn,),
        in_specs=[pl.BlockSpec((nb, K), lambda j: (0, 0)),
                  pl.BlockSpec((None, K, tn), lambda j: (layer, 0, j)),
                  pl.BlockSpec((1, tn), lambda j: (0, j))],
        out_specs=pl.BlockSpec((nb, tn), lambda j: (0, j)), out_shape=_sds((nb, N), F32),
        compiler_params=_params(("parallel",)),
    )(c_all, w, bias)


def _sum_slabs(landed, own, who, name):
    _, R, C = landed.shape
    tr = _row_tile(R, C * N_DEV)

    def body(who_ref, x_ref, own_ref, o_ref):
        me, mine = who_ref[0], own_ref[...]
        acc = jnp.where(me == 0, mine, x_ref[0])
        for s in range(1, N_DEV):
            acc = acc + jnp.where(me == s, mine, x_ref[s])
        o_ref[...] = acc

    return pl.pallas_call(
        body, name=name,
        grid_spec=pltpu.PrefetchScalarGridSpec(
            num_scalar_prefetch=1, grid=(R // tr,),
            in_specs=[pl.BlockSpec((N_DEV, tr, C), lambda i, who_ref: (0, i, 0)),
                      pl.BlockSpec((tr, C), lambda i, who_ref: (i, 0))],
            out_specs=pl.BlockSpec((tr, C), lambda i, who_ref: (i, 0))),
        out_shape=_sds((R, C), F32),
        compiler_params=_params(("parallel",)),
    )(who, landed, own)


def _adamw_math(w, g, m, v):
    m = ADAM_B1 * m + (1.0 - ADAM_B1) * g
    v = ADAM_B2 * v + (1.0 - ADAM_B2) * (g * g)
    m_hat = m / (1.0 - ADAM_B1 ** ADAM_STEP)
    v_hat = v / (1.0 - ADAM_B2 ** ADAM_STEP)
    delta = -ADAM_LR * (m_hat / (jnp.sqrt(v_hat) + ADAM_EPS) + ADAM_WD * w)
    return delta, m, v


class _AdamStack:
    def __init__(self, w, m, v, name):
        self.w, self.m, self.v, self.name = w, m, v, name
        self.L, self.R, self.C = w.shape
        self.tr = _row_tile(self.R, self.C, 256 * 1024)
        self.outs = None

    def _layer(self, l, who, srcs, src_specs, make_grad):
        n_src = len(srcs)
        L, R, C, tr = self.L, self.R, self.C, self.tr
        wspec = pl.BlockSpec((None, tr, C), lambda i, who_ref: (l, i, 0))

        def body(who_ref, *refs):
            src_refs = refs[:n_src]
            w_ref, m_ref, v_ref = refs[n_src:n_src + 3]
            g_ref, d_ref, m2_ref, v2_ref = refs[-4:]
            g = make_grad(who_ref[0], *src_refs)
            delta, m2, v2 = _adamw_math(w_ref[...], g, m_ref[...], v_ref[...])
            g_ref[...] = g
            d_ref[...] = delta
            m2_ref[...] = m2
            v2_ref[...] = v2

        prev = [] if self.outs is None else list(self.outs)
        aliases = {} if self.outs is None else {1 + n_src + 3 + t: t for t in range(4)}
        self.outs = pl.pallas_call(
            body, name=f"{self.name}_l{l}",
            grid_spec=pltpu.PrefetchScalarGridSpec(
                num_scalar_prefetch=1, grid=(R // tr,),
                in_specs=list(src_specs) + [wspec] * 3 + [_ANY] * len(prev), out_specs=[wspec] * 4),
            out_shape=[_sds((L, R, C), F32)] * 4,
            input_output_aliases=aliases,
            compiler_params=_params(("parallel",)),
        )(who, *srcs, self.w, self.m, self.v, *prev)

    def from_parts(self, l, who, landed, sent):
        tr, C = self.tr, self.C

        def make_grad(me, p_ref, own_ref):
            mine = own_ref[...].astype(F32)
            g = jnp.where(me == 0, mine, p_ref[0].astype(F32))
            for s in range(1, N_DEV):
                g = g + jnp.where(me == s, mine, p_ref[s].astype(F32))
            return g

        self._layer(l, who, [landed, sent],
                    [pl.BlockSpec((N_DEV, tr, C), lambda i, who_ref: (0, i, 0)),
                     pl.BlockSpec((None, tr, C), lambda i, who_ref: (who_ref[0], i, 0))], make_grad)

    def from_outer(self, l, who, sc_t, dmod):
        tr, C = self.tr, self.C

        def make_grad(me, s_ref, d_ref):
            g = s_ref[:, 0:1] * d_ref[0:1, :]
            for b in range(1, N_DEV):
                g = g + s_ref[:, b:b + 1] * d_ref[b:b + 1, :]
            return g

        self._layer(l, who, [sc_t, dmod], [pl.BlockSpec((tr, N_DEV), lambda i, who_ref: (i, 0)),
                                           pl.BlockSpec((N_DEV, C), lambda i, who_ref: (0, 0))], make_grad)


def _adamw_flat(w, g, m, v, name):
    R, C = w.shape
    tr = _row_tile(R, C, 128 * 1024)

    def body(w_ref, g_ref, m_ref, v_ref, d_ref, m2_ref, v2_ref):
        delta, m2, v2 = _adamw_math(w_ref[...], g_ref[...], m_ref[...], v_ref[...])
        d_ref[...] = delta
        m2_ref[...] = m2
        v2_ref[...] = v2

    spec = pl.BlockSpec((tr, C), lambda i: (i, 0))
    return pl.pallas_call(
        body, name=name, grid=(R // tr,), in_specs=[spec] * 4, out_specs=[spec] * 3,
        out_shape=[_sds((R, C), F32)] * 3, compiler_params=_params(("parallel",)),
    )(w, g, m, v)


def _pack(arrays):
    flat = jnp.concatenate([a.reshape(-1).astype(F32) for a in arrays])
    pad = (-flat.shape[0]) % (64 * BLK)
    if pad:
        flat = jnp.concatenate([flat, jnp.zeros((pad,), F32)])
    return flat.reshape(-1, BLK)


def _unpack(buf, shapes, lead=()):
    sizes = [int(math.prod(s)) for s in shapes]
    out, off = [], 0
    if all(n % BLK == 0 for n in sizes):
        for s, n in zip(shapes, sizes):
            out.append(buf[..., off // BLK:(off + n) // BLK, :].reshape(tuple(lead) + tuple(s)))
            off += n
        return out
    flat = buf.reshape(tuple(lead) + (-1,))
    for s, n in zip(shapes, sizes):
        out.append(flat[..., off:off + n].reshape(tuple(lead) + tuple(s)))
        off += n
    return out


def _row(vec):
    return vec.reshape(1, -1)


def _shard_of(full, axis, me, size):
    return lax.dynamic_slice_in_dim(full, me * size, size, axis=axis)


def kernel(x, c, ada_w, ada_b, norm_g, mlp_w1, mlp_w2, gmlp_w_in, gmlp_ln_g, gmlp_ln_b, gmlp_ws, gmlp_bs, gmlp_w_out, kv_norm_g, kv_ada_w, kv_ada_b, w_kv, k_norm_g, w_f, b_f, attn_wq, q_norm_g, attn_wo, loss_target, m_ada_w, m_ada_b, m_norm_g, m_mlp_w1, m_mlp_w2, m_gmlp_w_in, m_gmlp_ln_g, m_gmlp_ln_b, m_gmlp_ws, m_gmlp_bs, m_gmlp_w_out, m_kv_norm_g, m_kv_ada_w, m_kv_ada_b, m_w_kv, m_k_norm_g, m_w_f, m_b_f, m_attn_wq, m_q_norm_g, m_attn_wo, v_ada_w, v_ada_b, v_norm_g, v_mlp_w1, v_mlp_w2, v_gmlp_w_in, v_gmlp_ln_g, v_gmlp_ln_b, v_gmlp_ws, v_gmlp_bs, v_gmlp_w_out, v_kv_norm_g, v_kv_ada_w, v_kv_ada_b, v_w_kv, v_k_norm_g, v_w_f, v_b_f, v_attn_wq, v_q_norm_g, v_attn_wo):
    given = dict(locals())
    weights = {n: given[n] for n in WEIGHT_NAMES}
    mom_m = {n: given["m_" + n] for n in WEIGHT_NAMES}
    mom_v = {n: given["v_" + n] for n in WEIGHT_NAMES}

    me = _my_index()
    T, D = x.shape[1], x.shape[2]
    depth = ada_w.shape[0]
    n_a = gmlp_w_in.shape[0]
    n_heads = b_f.shape[0]
    G = gmlp_ws.shape[1]
    Hh = gmlp_ln_g.shape[1] * N_DEV
    mod_cols = ada_w.shape[2]
    kv_cols = kv_ada_w.shape[1]
    x0 = x.reshape(T, D)
    target = loss_target.reshape(T, D)

    small_in = [c, norm_g, gmlp_ln_g, gmlp_ln_b, w_f]
    small_shapes = [a.shape for a in small_in]
    got = _all_to_all(_pack(small_in), "gather_small_inputs", bcast=True)
    c_all, norm_g_sh, ln_g_sh, ln_b_sh, w_f_sh = _unpack(got, small_shapes, lead=(N_DEV,))
    c_all = c_all.reshape(N_DEV, D)
    norm_g_full = jnp.moveaxis(norm_g_sh, 0, 2).reshape(depth, 2, D)
    ln_g_full = jnp.moveaxis(ln_g_sh, 0, 1).reshape(n_a, Hh)
    ln_b_full = jnp.moveaxis(ln_b_sh, 0, 1).reshape(n_a, Hh)
    w_f_full = w_f_sh.reshape(D, n_heads)
    w_f_pad = jnp.pad(w_f_full, ((0, 0), (0, BLK - n_heads))).astype(BF16)
    b_f_pad = jnp.pad(b_f, (0, BLK - n_heads)).reshape(1, BLK)

    mod_parts = []
    for l in range(depth):
        bias = _shard_of(ada_b[l], 0, me, mod_cols).reshape(1, mod_cols)
        mod_parts.append(_mods(c_all, ada_w, l, bias, f"mods_l{l}"))
    kv_bias = _shard_of(kv_ada_b, 0, me, kv_cols).reshape(1, kv_cols)
    mod_parts.append(_mods(c_all, kv_ada_w.reshape(1, D, kv_cols), 0, kv_bias, "mods_kv"))
    mods_mine = jnp.concatenate(mod_parts, axis=1)
    mod_width = mods_mine.shape[1]
    mods_pack = jnp.pad(mods_mine, ((0, 0), (0, (-mod_width) % (8 * BLK)))).reshape(N_DEV, -1, BLK)
    mods_landed = _all_to_all(mods_pack, "exchange_mods")
    mods_got = mods_landed.reshape(N_DEV, -1)[:, :mod_width]
    mods = []
    for l in range(depth):
        mods.append(mods_got[:, l * mod_cols:(l + 1) * mod_cols].reshape(N_MOD, D))
    kv_mod = mods_got[:, depth * mod_cols:].reshape(2, D)
    silu_all = c_all / (1.0 + jnp.exp(-c_all))

    assert 1 <= n_a < depth
    who = me.astype(jnp.int32).reshape(1)
    big = {"mlp_w1": mlp_w1, "mlp_w2": mlp_w2, "gmlp_w_in": gmlp_w_in, "gmlp_w_out": gmlp_w_out,
           "w_kv": w_kv.reshape((1,) + w_kv.shape), "attn_wq": attn_wq, "attn_wo": attn_wo}
    groups, first_group = [], {}
    for l in range(depth):
        first_group[l] = len(groups)
        if l < n_a:
            groups.append([("gmlp_w_in", l)])
            groups.append([("gmlp_w_out", l), ("mlp_w1", l), ("mlp_w2", l)])
        else:
            first = [("w_kv", 0)] if l == n_a else []
            groups.append(first + [("attn_wq", l - n_a), ("attn_wo", l - n_a), ("mlp_w1", l), ("mlp_w2", l)])
    tokens = []

    def behind_starts():
        out = tuple(tokens)
        tokens.clear()
        return out

    over_ici, placed = {}, {}

    def gather_begin(gi, deps=()):
        over_ici[gi] = _gather_start(placed[gi], 1, f"gather_g{gi}_ici_start", deps=deps)
        tokens.append(over_ici[gi][3])

    for gi, grp in enumerate(groups):
        behind = (mods_landed,) if gi == 0 else (over_ici[0][3],)
        placed[gi] = [_place_shard(big[n], l, who, f"place_{n}_l{l}", deps=behind) for n, l in grp]
        gather_begin(gi, deps=tuple(tokens[-1:]))
    to_sibling = {}
    gw = {}

    def gather_forward(gi, after):
        landed = _gather_wait(over_ici[gi], 1, after, f"gather_g{gi}_ici_wait")
        to_sibling[gi] = _gather_start(landed, 2, f"gather_g{gi}_d2d_start")
        tokens.append(to_sibling[gi][3])

    def gather_finish(gi, after):
        for key, arr in zip(groups[gi], _gather_wait(to_sibling[gi], 2, after, f"gather_g{gi}_d2d_wait")):
            gw[key] = arr

    tkk = _attn_tile(T)

    saved = []
    xs = x0
    pending = None
    kv = None
    for l in range(depth):
        sh1, sc1, g1, sh2, sc2, g2 = [_row(mods[l][t]) for t in range(N_MOD)]
        ng1, ng2 = _row(norm_g_full[l, 0]), _row(norm_g_full[l, 1])
        st = dict(sc1=sc1, g1=g1, sc2=sc2, g2=g2, ng1=ng1, ng2=ng2)
        if pending is None:
            h1 = _norm_mod(xs, ng1, sc1, sh1, f"norm1_l{l}")
            gather_forward(0, h1)
            gather_finish(0, h1)
        else:
            gather_finish(first_group[l], pending[0])
            xs, h1 = _res_norm_mod(xs, pending[0], pending[1], ng1, sc1, sh1, f"res_norm1_l{l}")
        st["x_in"], st["h1"] = xs, h1
        if l < n_a:
            a = l
            zpre = _mm(h1, gw["gmlp_w_in", a], bmode="col", out_dtypes=(BF16,), name=f"gmlp_in_l{l}",
                       deps=behind_starts())
            gather_forward(first_group[l] + 1, zpre)
            bs_t = gmlp_bs[a].T
            p = _gmlp_mid_fwd(zpre, _row(ln_g_full[a]), _row(ln_b_full[a]), gmlp_ws[a], bs_t, f"gmlp_mid_l{l}")
            gather_finish(first_group[l] + 1, p)
            y = _mm(p, gw["gmlp_w_out", a], bmode="row", name=f"gmlp_out_l{l}", deps=behind_starts())
            st.update(zpre=zpre, p=p)
        else:
            if kv is None:
                kv_ng, kv_sh, kv_sc = _row(kv_norm_g), _row(kv_mod[0]), _row(kv_mod[1])
                hkv = _norm_mod(xs, kv_ng, kv_sc, kv_sh, "norm_kv")
                kvp = _mm(hkv, gw["w_kv", 0], bmode="col", name="kv_proj", deps=behind_starts())
                kk, vv = _head_norm(kvp, _row(k_norm_g), n_heads, "k_norm", tail=True)
                fl = _mm(hkv, w_f_pad, name="gate_logits")
                fcum = _fcum_fwd(fl, b_f_pad, "fcum")
                fk = fcum[:, :n_heads].T.reshape(n_heads, T // tkk, 1, tkk)
                kv = dict(x=xs, hkv=hkv, kvp=kvp, k=kk, v=vv, fl=fl, fcum=fcum, fk=fk, ng=kv_ng, sc=kv_sc)
            bl = l - n_a
            qp = _mm(h1, gw["attn_wq", bl], bmode="row", name=f"q_proj_l{l}", deps=behind_starts())
            q = _head_norm(qp, _row(q_norm_g[bl]), n_heads, f"q_norm_l{l}")[0]
            o, o32, lse = _attn_fwd(q, kv["k"], kv["v"], kv["fk"], n_heads, f"attn_l{l}")
            y = _mm(o, gw["attn_wo", bl], bmode="row", name=f"attn_out_l{l}")
            st.update(qp=qp, q=q, o=o, o32=o32, lse=lse)
        xs, h2 = _res_norm_mod(xs, y, g1, ng2, sc2, sh2, f"res_norm2_l{l}")
        a_pre, s_act = _mm(h2, gw["mlp_w1", l], bmode="col", out_dtypes=(BF16, BF16), epilogue=_relu2_epilogue,
                           name=f"mlp_up_l{l}")
        if l + 1 < depth:
            gather_forward(first_group[l + 1], a_pre)
        mo = _mm(s_act, gw["mlp_w2", l], bmode="row", name=f"mlp_down_l{l}", deps=behind_starts())
        st.update(y=y, x_mid=xs, h2=h2, a_pre=a_pre, s=s_act, m=mo)
        saved.append(st)
        pending = (mo, g2)

    dx, dm, loss_row, dg2 = _res_loss(xs, pending[0], pending[1], target, "loss")
    loss = lax.psum(loss_row[0, 0], ("x", "y", "c"))

    started = {}

    def scatter(dw_slabs, key, idx):
        started[(key, idx)] = _a2a_start(dw_slabs, f"scatter_{key}_l{idx}_start")
        tokens.append(started[(key, idx)][4])

    d_mod = [None] * depth
    d_norm_g = [None] * depth
    d_ln_g, d_ln_b, d_ws, d_bs = [None] * n_a, [None] * n_a, [None] * n_a, [None] * n_a
    d_qg = [None] * (depth - n_a)
    dk_list, dv_list, dfk_list = [], [], []
    small = {}

    for l in reversed(range(depth)):
        st = saved[l]
        da = _mm(dm, gw["mlp_w2", l], tb=True, bmode="row", out_dtypes=(BF16,), epilogue=_relu2_bwd_epilogue,
                 extra=(st["a_pre"],), name=f"mlp_down_dx_l{l}")
        dw2 = _mm(st["s"], dm, ta=True, out_dtypes=(BF16,), name=f"mlp_down_dw_l{l}", deps=behind_starts())
        scatter(dw2.reshape(N_DEV, -1, D), "mlp_w2", l)
        dw1 = _mm(st["h2"], da, ta=True, out_mode="col", out_dtypes=(BF16,), name=f"mlp_up_dw_l{l}",
                  deps=behind_starts())
        scatter(dw1, "mlp_w1", l)
        dh2 = _mm(da, gw["mlp_w1", l], tb=True, bmode="col", out_dtypes=(BF16,), name=f"mlp_up_dx_l{l}",
                  deps=behind_starts())
        dx, dy, sums2, dg1 = _norm_mod_bwd(st["x_mid"], [dh2], dx, st["ng2"], st["sc2"], f"norm2_bwd_l{l}",
                                           gated=(st["y"], st["g1"]))
        if l < n_a:
            a = l
            dwo = _mm(st["p"], dy, ta=True, out_dtypes=(BF16,), name=f"gmlp_out_dw_l{l}", deps=behind_starts())
            scatter(dwo.reshape(N_DEV, -1, D), "gmlp_w_out", a)
            dp = _mm(dy, gw["gmlp_w_out", a], tb=True, bmode="row", out_dtypes=(BF16,), name=f"gmlp_out_dx_l{l}",
                     deps=behind_starts())
            dz, d_ws[a], dbs_t, dlg, dlb = _gmlp_mid_bwd(
                st["zpre"], dp, _row(ln_g_full[a]), _row(ln_b_full[a]), gmlp_ws[a],
                jnp.swapaxes(gmlp_ws[a], 1, 2), gmlp_bs[a].T, f"gmlp_mid_bwd_l{l}")
            d_bs[a], d_ln_g[a], d_ln_b[a] = dbs_t[:, :G].T, dlg[0], dlb[0]
            dwi = _mm(st["h1"], dz, ta=True, out_mode="col", out_dtypes=(BF16,), name=f"gmlp_in_dw_l{l}",
                      deps=behind_starts())
            scatter(dwi, "gmlp_w_in", a)
            dh1s = [_mm(dz, gw["gmlp_w_in", a], tb=True, bmode="col", out_dtypes=(BF16,), name=f"gmlp_in_dx_l{l}",
                        deps=behind_starts())]
        else:
            bl = l - n_a
            dwo = _mm(st["o"], dy, ta=True, out_dtypes=(BF16,), name=f"attn_out_dw_l{l}", deps=behind_starts())
            scatter(dwo.reshape(N_DEV, -1, D), "attn_wo", bl)
            do = _mm(dy, gw["attn_wo", bl], tb=True, bmode="row", out_dtypes=(BF16,), name=f"attn_out_dx_l{l}",
                     deps=behind_starts())
            dq, dk, dv, dfq, dfk = _attn_bwd(st["q"], kv["k"], kv["v"], st["o32"], do, st["lse"], kv["fk"],
                                             n_heads, f"attn_bwd_l{l}")
            dfk_list += [dfq, dfk]
            dk_list.append(dk)
            dv_list.append(dv)
            dqp, dqg = _head_norm_bwd(st["qp"], [dq], _row(q_norm_g[bl]), n_heads, f"q_norm_bwd_l{l}")
            d_qg[bl] = dqg[0]
            dwq = _mm(st["h1"], dqp, ta=True, out_dtypes=(BF16,), name=f"q_proj_dw_l{l}", deps=behind_starts())
            scatter(dwq.reshape(N_DEV, -1, D), "attn_wq", bl)
            dh1s = [_mm(dqp, gw["attn_wq", bl], tb=True, bmode="row", out_dtypes=(BF16,), name=f"q_proj_dx_l{l}",
                        deps=behind_starts())]
        below = (saved[l - 1]["m"], saved[l - 1]["g2"]) if l > 0 else None
        if below is None or l == n_a:
            dx, sums1 = _norm_mod_bwd(st["x_in"], dh1s, dx, st["ng1"], st["sc1"], f"norm1_bwd_l{l}")
        else:
            dx, dm_below, sums1, dg2_below = _norm_mod_bwd(st["x_in"], dh1s, dx, st["ng1"], st["sc1"],
                                                           f"norm1_bwd_l{l}", gated=below)
        d_mod[l] = jnp.stack([sums1[0], sums1[1], dg1[0], sums2[0], sums2[1], dg2[0]])
        d_norm_g[l] = jnp.stack([sums1[2], sums2[2]])
        if l == n_a:
            dkvp, dkg = _head_norm_bwd(kv["kvp"], dk_list, _row(k_norm_g), n_heads, "k_norm_bwd", tails=dv_list)
            dfc = [jnp.pad(d.reshape(n_heads, T).T, ((0, 0), (0, BLK - n_heads))) for d in dfk_list]
            dfl, dbf = _fcum_bwd(dfc, kv["fl"], b_f_pad, "fcum_bwd")
            dwkv = _mm(kv["hkv"], dkvp, ta=True, out_mode="col", out_dtypes=(BF16,), name="kv_proj_dw",
                       deps=behind_starts())
            scatter(dwkv, "w_kv", 0)
            dwf = _mm(kv["hkv"], dfl, ta=True, name="gate_logits_dw", deps=behind_starts())
            dh_a = _mm(dkvp, gw["w_kv", 0], tb=True, bmode="col", out_dtypes=(BF16,), name="kv_proj_dx")
            dh_b = _mm(dfl, w_f_pad, tb=True, out_dtypes=(BF16,), name="gate_logits_dx")
            dx, dm_below, sums_kv, dg2_below = _norm_mod_bwd(kv["x"], [dh_a, dh_b], dx, kv["ng"], kv["sc"],
                                                             "norm_kv_bwd", gated=below)
            small.update(d_kv_mod=jnp.stack([sums_kv[0], sums_kv[1]]), d_kv_norm_g=sums_kv[2], d_k_norm_g=dkg[0],
                         d_w_f=dwf[:, :n_heads], d_b_f=dbf[0, :n_heads])
        if l > 0:
            dm, dg2 = dm_below, dg2_below
        if l == 1:
            hi_contrib = [jnp.stack(d_mod[1:]).reshape(depth - 1, N_MOD * D), small["d_kv_mod"].reshape(-1)]
            hi_started = _a2a_start(_pack(hi_contrib), "gather_mod_grads_hi_start", bcast=True)
            tokens.append(hi_started[4])

    grad_x = dx.reshape(x.shape)

    lo_contrib = [d_mod[0].reshape(1, N_MOD * D)]
    rest_contrib = [jnp.stack(d_norm_g), jnp.stack(d_ln_g), jnp.stack(d_ln_b), jnp.stack(d_ws), jnp.stack(d_bs),
                    small["d_kv_norm_g"], small["d_k_norm_g"], small["d_w_f"], small["d_b_f"], jnp.stack(d_qg)]
    hi_shapes, lo_shapes = [a.shape for a in hi_contrib], [a.shape for a in lo_contrib]
    rest_shapes = [a.shape for a in rest_contrib]
    lo_started = _a2a_start(_pack(lo_contrib), "gather_mod_grads_lo_start", bcast=True, deps=behind_starts())
    rest_started = _a2a_start(_pack(rest_contrib), "gather_small_grads_start", bcast=True, deps=(lo_started[4],))
    grads, deltas, new_m, new_v = {}, {}, {}, {}

    def stack_of(n):
        lead = () if weights[n].ndim == 3 else (1,)
        return _AdamStack(*[a.reshape(lead + a.shape) for a in (weights[n], mom_m[n], mom_v[n])], f"adamw_{n}")

    def results_of(n, stack):
        grads[n], deltas[n], new_m[n], new_v[n] = [a.reshape(weights[n].shape) for a in stack.outs]

    sc_t = silu_all.T
    ada_stack, kv_ada_stack = stack_of("ada_w"), stack_of("kv_ada_w")
    is_me = (jnp.arange(N_DEV) == me).reshape(N_DEV, 1, 1)

    def landed_mods(started_pack, shapes, after, name):
        sent, land = _a2a_wait(started_pack, after, f"gather_{name}_wait", bcast=True)
        everyone = _unpack(jnp.where(is_me, sent[None], land), shapes, lead=(N_DEV,))
        return everyone, _unpack(_sum_slabs(land, sent, who, f"sum_{name}"), shapes)

    stacks = {n: stack_of(n) for n in ("mlp_w1", "mlp_w2", "gmlp_w_in", "gmlp_w_out", "attn_wq", "attn_wo", "w_kv")}
    after = rest_started[4]
    for l in reversed(range(depth)):
        if l == n_a - 1:
            (dmod_hi, dkvmod_all), (g_ada_b_hi, g_kv_ada_b) = landed_mods(hi_started, hi_shapes, after, "mod_grads_hi")
            for up in range(1, depth):
                ada_stack.from_outer(up, who, sc_t, _shard_of(dmod_hi[:, up - 1], 1, me, mod_cols))
            kv_ada_stack.from_outer(0, who, sc_t, _shard_of(dkvmod_all, 1, me, kv_cols))
            after = ada_stack.outs[0]
        if l == 0:
            (dmod_lo,), (g_ada_b_lo,) = landed_mods(lo_started, lo_shapes, after, "mod_grads_lo")
            ada_stack.from_outer(0, who, sc_t, _shard_of(dmod_lo[:, 0], 1, me, mod_cols))
            after = ada_stack.outs[0]
        keys = [("mlp_w2", l), ("mlp_w1", l)]
        keys += [("gmlp_w_out", l), ("gmlp_w_in", l)] if l < n_a else [("attn_wo", l - n_a), ("attn_wq", l - n_a)]
        keys += [("w_kv", 0)] if l == 0 else []
        landed = {}
        for key in keys:
            sent, land = _a2a_wait(started[key], after, f"scatter_{key[0]}_l{key[1]}_wait")
            landed[key] = (land, sent)
        for key in keys:
            stacks[key[0]].from_parts(key[1], who, *landed[key])
            after = stacks[key[0]].outs[0]
    for n, stack in stacks.items():
        results_of(n, stack)
    results_of("ada_w", ada_stack)
    results_of("kv_ada_w", kv_ada_stack)
    g_ada_b = jnp.concatenate([g_ada_b_lo, g_ada_b_hi], axis=0)

    rest_sent, rest_land = _a2a_wait(rest_started, after, "gather_small_grads_wait", bcast=True)
    (g_norm_g_full, g_ln_g_full, g_ln_b_full, g_ws, g_bs, g_kv_norm_g, g_k_norm_g, g_w_f_full, g_b_f,
     g_q_norm_g) = _unpack(_sum_slabs(rest_land, rest_sent, who, "sum_small_grads"), rest_shapes)
    small_grads = {
        "ada_b": g_ada_b, "kv_ada_b": g_kv_ada_b.reshape(kv_ada_b.shape),
        "norm_g": _shard_of(g_norm_g_full, 2, me, norm_g.shape[2]),
        "gmlp_ln_g": _shard_of(g_ln_g_full, 1, me, gmlp_ln_g.shape[1]),
        "gmlp_ln_b": _shard_of(g_ln_b_full, 1, me, gmlp_ln_b.shape[1]),
        "gmlp_ws": g_ws, "gmlp_bs": g_bs, "kv_norm_g": g_kv_norm_g, "k_norm_g": g_k_norm_g,
        "w_f": _shard_of(g_w_f_full, 0, me, w_f.shape[0]), "b_f": g_b_f, "q_norm_g": g_q_norm_g,
    }
    small_names = list(small_grads)
    small_w_shapes = [weights[n].shape for n in small_names]
    d_pack, m_pack, v_pack = _adamw_flat(_pack([weights[n] for n in small_names]),
                                         _pack([small_grads[n] for n in small_names]),
                                         _pack([mom_m[n] for n in small_names]), _pack([mom_v[n] for n in small_names]),
                                         "adamw_small")
    grads.update(small_grads)
    deltas.update(zip(small_names, _unpack(d_pack, small_w_shapes)))
    new_m.update(zip(small_names, _unpack(m_pack, small_w_shapes)))
    new_v.update(zip(small_names, _unpack(v_pack, small_w_shapes)))

    return (loss, grad_x, *[grads[n] for n in WEIGHT_NAMES], *[deltas[n] for n in WEIGHT_NAMES],
            *[new_m[n] for n in WEIGHT_NAMES], *[new_v[n] for n in WEIGHT_NAMES])
```
